```python
import jax, jax.numpy as jnp
from jax import lax
import numpy as np

D_MODEL = 1024
BATCH = 8
SEQ = 4096
DEPTH = 1
DEC_BATCH = 32
DEC_SEQ = 8
PAST_LEN = 16384
PAGE_SIZE = 128

PLE_DIM = 256
NORM_EPS = 1e-6
GLA_HEADS = 4
GLA_DK = D_MODEL // 2 // GLA_HEADS
GLA_DV = D_MODEL // GLA_HEADS
GLA_KEY = GLA_HEADS * GLA_DK
GLA_VAL = GLA_HEADS * GLA_DV
GLA_LOWRANK = 16
GLA_TAU = 16.0
GLA_CHUNK = 64
DIL_GROUPS = ((128, 1), (512, 4), (2048, 16))
DIL_HEADS = 4
DIL_HD = 128
DIL_WIDTH = DIL_HEADS * DIL_HD
DIL_QKV = len(DIL_GROUPS) * DIL_WIDTH
IN_SPLITS = (GLA_KEY, GLA_KEY, GLA_VAL, GLA_VAL, GLA_LOWRANK,
             DIL_QKV, DIL_QKV, DIL_QKV, DIL_WIDTH, D_MODEL, D_MODEL)
IN_WIDTH = sum(IN_SPLITS)

kernel_name = "gla_dilated_window_gated_hybrid_step"


def _rmsnorm(x, g):
    xf = x.astype(jnp.float32)
    y = xf * lax.rsqrt(jnp.mean(xf * xf, axis=-1, keepdims=True) + NORM_EPS)
    return (y * g.astype(jnp.float32)).astype(x.dtype)


def _gla(q, k, v, log_a, s0):
    f32 = jnp.float32
    B, L, H, DK = q.shape
    DV = v.shape[-1]
    C = min(GLA_CHUNK, L)
    n = -(-L // C)
    pad = n * C - L

    def chunks(t):
        t = jnp.pad(t.astype(f32), ((0, 0), (0, pad), (0, 0), (0, 0)))
        return jnp.moveaxis(t.reshape(B, n, C, H, t.shape[-1]), 1, 0)

    qc, kc, vc, ac = chunks(q), chunks(k), chunks(v), chunks(log_a)
    causal = jnp.tril(jnp.ones((C, C), dtype=bool))

    def step(s, inp):
        q_, k_, v_, a_ = inp
        b = jnp.cumsum(a_, axis=1)
        b_end = b[:, -1]
        q_in = q_ * jnp.exp(b)
        att = jnp.einsum('bihd,bjhd->bhij', q_in, k_ * jnp.exp(-b))
        att = jnp.where(causal, att, 0.0)
        o = (jnp.einsum('bhij,bjhe->bihe', att, v_)
             + jnp.einsum('bihd,bhde->bihe', q_in, s))
        k_end = k_ * jnp.exp(b_end[:, None] - b)
        s = jnp.exp(b_end)[..., None] * s + jnp.einsum('bjhd,bjhe->bhde', k_end, v_)
        return s, o

    s_fin, o = lax.scan(step, s0.astype(f32), (qc, kc, vc, ac))
    o = jnp.moveaxis(o, 0, 1).reshape(B, n * C, H, DV)[:, :L]
    return o, s_fin


def _dilated_prompt(q, k, v, dil, span):
    f32 = jnp.float32
    B, S, H, D = q.shape
    L = S // dil
    nb = -(-L // span)

    def split(t):
        t = t.astype(f32).reshape(B, L, dil, H, D)
        t = jnp.pad(t, ((0, 0), (0, nb * span - L), (0, 0), (0, 0), (0, 0)))
        return t.reshape(B, nb, span, dil, H, D)

    qs, ks, vs = split(q), split(k), split(v)

    def with_prev(t):
        prev = jnp.pad(t, ((0, 0), (1, 0), (0, 0), (0, 0), (0, 0), (0, 0)))[:, :-1]
        return jnp.concatenate([prev, t], axis=2)

    kk, vv = with_prev(ks), with_prev(vs)
    i = span + jnp.arange(span)[:, None]
    j = jnp.arange(2 * span)[None, :]
    band = (i - j >= 0) & (i - j <= span)
    first = (jnp.arange(nb)[:, None, None] == 0) & (j < span)[None]
    valid = (band[None] & ~first)[None, :, None, None]
    s = jnp.einsum('bnirhd,bnjrhd->bnrhij', qs, kk) * (D ** -0.5)
    s = jnp.where(valid, s, -jnp.inf)
    m = jnp.max(s, axis=-1, keepdims=True)
    p = jnp.exp(s - m)
    l = jnp.sum(p, axis=-1, keepdims=True)
    o = jnp.einsum('bnrhij,bnjrhd->bnirhd', p / l, vv)
    lse = jnp.moveaxis((m + jnp.log(l))[..., 0], -1, 2)
    o = o.reshape(B, nb * span, dil, H, D)[:, :L].reshape(B, S, H, D)
    lse = lse.reshape(B, nb * span, dil, H)[:, :L].reshape(B, S, H)
    return o, lse


def _dilated_sample(q, k, v, buf, dil, span):
    f32 = jnp.float32
    B, T, H, D = q.shape
    Lb = buf.shape[1]
    kk = jnp.concatenate([buf[:, :, 0].astype(f32), k.astype(f32)], axis=1)
    vv = jnp.concatenate([buf[:, :, 1].astype(f32), v.astype(f32)], axis=1)
    idx = Lb + jnp.arange(T)[:, None] - dil * jnp.arange(span + 1)[None, :]
    valid = (idx >= 0)[None, :, None, :]
    idx = jnp.maximum(idx, 0)
    kg = jnp.take(kk, idx, axis=1)
    vg = jnp.take(vv, idx, axis=1)
    s = jnp.einsum('bthd,btkhd->bthk', q.astype(f32), kg) * (D ** -0.5)
    s = jnp.where(valid, s, -jnp.inf)
    m = jnp.max(s, axis=-1, keepdims=True)
    p = jnp.exp(s - m)
    l = jnp.sum(p, axis=-1, keepdims=True)
    o = jnp.einsum('bthk,btkhd->bthd', p / l, vg)
    lse = (m + jnp.log(l))[..., 0]
    keep = min(dil * span, Lb + T)
    new_buf = jnp.stack([kk, vv], axis=2)[:, Lb + T - keep:].astype(buf.dtype)
    return o, lse, new_buf


def _layer(x, pe, gla_s0, bufs, g_pre, g_post, w_in, w_gla_lr, b_gla_lr, g_gla_norm,
           w_gla_branch, w_dil_branch, w_out, w_ple_proj, w_ple_gate):
    B, L, _ = x.shape
    xn = _rmsnorm(x, g_pre)
    u = xn @ w_in
    cuts = np.cumsum(IN_SPLITS)[:-1].tolist()
    gq, gk, gv, gg, glr, dq, dk, dv, dg, ga, gb = jnp.split(u, cuts, axis=-1)

    q = gq.reshape(B, L, GLA_HEADS, GLA_DK) * (GLA_DK ** -0.5)
    k = gk.reshape(B, L, GLA_HEADS, GLA_DK)
    v = gv.reshape(B, L, GLA_HEADS, GLA_DV)
    z = (glr @ w_gla_lr + b_gla_lr).astype(jnp.float32)
    log_a = (jax.nn.log_sigmoid(z) / GLA_TAU).reshape(B, L, GLA_HEADS, GLA_DK)
    o_a, s_new = _gla(q, k, v, log_a, gla_s0)
    o_a = _rmsnorm(o_a.astype(x.dtype), g_gla_norm.reshape(GLA_HEADS, GLA_DV)).reshape(B, L, GLA_VAL)
    y_a = (o_a * jax.nn.silu(gg)) @ w_gla_branch

    ng = len(DIL_GROUPS)
    dq = dq.reshape(B, L, ng, DIL_HEADS, DIL_HD)
    dk = dk.reshape(B, L, ng, DIL_HEADS, DIL_HD)
    dv = dv.reshape(B, L, ng, DIL_HEADS, DIL_HD)
    outs, lses, new_bufs = [], [], []
    for g, (win, dil) in enumerate(DIL_GROUPS):
        span = win // dil
        qg, kg, vg = dq[:, :, g], dk[:, :, g], dv[:, :, g]
        if bufs is None:
            o_g, lse_g = _dilated_prompt(qg, kg, vg, dil, span)
            keep = min(win, L)
            nbuf = jnp.stack([kg, vg], axis=2)[:, L - keep:]
        else:
            o_g, lse_g, nbuf = _dilated_sample(qg, kg, vg, bufs[g], dil, span)
        outs.append(o_g)
        lses.append(lse_g)
        new_bufs.append(nbuf)
    wts = jax.nn.softmax(jnp.stack(lses), axis=0)
    o_b = jnp.einsum('gblh,gblhd->blhd', wts, jnp.stack(outs)).reshape(B, L, DIL_WIDTH)
    y_b = (o_b.astype(x.dtype) * jax.nn.silu(dg)) @ w_dil_branch

    merged = jax.nn.sigmoid(ga) * y_a + jax.nn.sigmoid(gb) * y_b
    h = x + _rmsnorm(merged @ w_out, g_post)
    h = h + (pe @ w_ple_proj) * jax.nn.sigmoid(h @ w_ple_gate)
    return h, s_new.astype(x.dtype), new_bufs


def setup_inputs(seed: int = 0) -> dict:
    key = jax.random.key(seed)
    ks = jax.random.split(key, 24)
    f32 = jnp.float32
    nrm = lambda k, shape, sc: jax.random.normal(k, shape, f32) * sc
    buf_len = [min(w, PAST_LEN) for (w, _) in DIL_GROUPS]
    return {
        "x_prompt": nrm(ks[0], (BATCH, SEQ, D_MODEL), 1.0),
        "x_sample": nrm(ks[1], (DEC_BATCH, DEC_SEQ, D_MODEL), 1.0),
        "state_gla": nrm(ks[2], (DEPTH, DEC_BATCH, GLA_HEADS, GLA_DK, GLA_DV), 1.0),
        "cache_kv_w128": nrm(ks[3], (DEPTH, DEC_BATCH, buf_len[0], 2, DIL_HEADS, DIL_HD), 1.0),
        "cache_kv_w512": nrm(ks[4], (DEPTH, DEC_BATCH, buf_len[1], 2, DIL_HEADS, DIL_HD), 1.0),
        "cache_kv_w2048": nrm(ks[5], (DEPTH, DEC_BATCH, buf_len[2], 2, DIL_HEADS, DIL_HD), 1.0),
        "p_prompt": nrm(ks[6], (DEPTH, BATCH, SEQ, PLE_DIM), 1.0),
        "p_sample": nrm(ks[7], (DEPTH, DEC_BATCH, DEC_SEQ, PLE_DIM), 1.0),
        "g_pre": 1.0 + nrm(ks[8], (DEPTH, D_MODEL), 0.02),
        "g_post": 1.0 + nrm(ks[9], (DEPTH, D_MODEL), 0.02),
        "w_in": nrm(ks[10], (DEPTH, D_MODEL, IN_WIDTH), D_MODEL ** -0.5),
        "w_gla_lr": nrm(ks[11], (DEPTH, GLA_LOWRANK, GLA_KEY), GLA_LOWRANK ** -0.5),
        "b_gla_lr": nrm(ks[12], (DEPTH, GLA_KEY), 0.02),
        "g_gla_norm": 1.0 + nrm(ks[13], (DEPTH, GLA_VAL), 0.02),
        "w_gla_branch": nrm(ks[14], (DEPTH, GLA_VAL, D_MODEL), GLA_VAL ** -0.5),
        "w_dil_branch": nrm(ks[15], (DEPTH, DIL_WIDTH, D_MODEL), DIL_WIDTH ** -0.5),
        "w_out": nrm(ks[16], (DEPTH, D_MODEL, D_MODEL), D_MODEL ** -0.5),
        "w_ple_proj": nrm(ks[17], (DEPTH, PLE_DIM, D_MODEL), PLE_DIM ** -0.5),
        "w_ple_gate": nrm(ks[18], (DEPTH, D_MODEL, D_MODEL), D_MODEL ** -0.5),
    }


def reference(x_prompt, x_sample, state_gla, cache_kv_w128, cache_kv_w512, cache_kv_w2048,
              p_prompt, p_sample, g_pre, g_post, w_in, w_gla_lr, b_gla_lr, g_gla_norm,
              w_gla_branch, w_dil_branch, w_out, w_ple_proj, w_ple_gate):
    hp, hs = x_prompt, x_sample
    gla_p, gla_s = [], []
    kvp = [[], [], []]
    kvs = [[], [], []]
    for i in range(DEPTH):
        lw = (g_pre[i], g_post[i], w_in[i], w_gla_lr[i], b_gla_lr[i], g_gla_norm[i],
              w_gla_branch[i], w_dil_branch[i], w_out[i], w_ple_proj[i], w_ple_gate[i])
        s0 = jnp.zeros((x_prompt.shape[0], GLA_HEADS, GLA_DK, GLA_DV), x_prompt.dtype)
        hp, sp_new, bp_new = _layer(hp, p_prompt[i], s0, None, *lw)
        hs, ss_new, bs_new = _layer(hs, p_sample[i], state_gla[i],
                                    (cache_kv_w128[i], cache_kv_w512[i], cache_kv_w2048[i]), *lw)
        gla_p.append(sp_new)
        gla_s.append(ss_new)
        for g in range(len(DIL_GROUPS)):
            kvp[g].append(bp_new[g])
            kvs[g].append(bs_new[g])
    state_gla_prompt = jnp.stack(gla_p)
    state_gla_sample = jnp.stack(gla_s)
    kv_w128_prompt, kv_w512_prompt, kv_w2048_prompt = (jnp.stack(kvp[0]), jnp.stack(kvp[1]), jnp.stack(kvp[2]))
    kv_w128_sample, kv_w512_sample, kv_w2048_sample = (jnp.stack(kvs[0]), jnp.stack(kvs[1]), jnp.stack(kvs[2]))
    return (hp, hs, state_gla_prompt, state_gla_sample,
            kv_w128_prompt, kv_w512_prompt, kv_w2048_prompt,
            kv_w128_sample, kv_w512_sample, kv_w2048_sample)
```

```python
import functools

import jax
import jax.numpy as jnp
from jax import lax
from jax.experimental import pallas as pl
from jax.experimental.pallas import tpu as pltpu

F32 = jnp.float32
BF16 = jnp.bfloat16

NORM_EPS = 1e-6
GLA_HEADS = 4
GLA_DK = 128
GLA_DV = 256
GLA_KEY = GLA_HEADS * GLA_DK
GLA_VAL = GLA_HEADS * GLA_DV
GLA_LOWRANK = 16
GLA_TAU = 16.0
GLA_CHUNK = 64
DIL_GROUPS = ((128, 1), (512, 4), (2048, 16))
DIL_HEADS = 4
DIL_HD = 128
DIL_WIDTH = DIL_HEADS * DIL_HD
DIL_SPAN = 128
N_GROUPS = len(DIL_GROUPS)
GLA_COLS = 2 * GLA_KEY + 2 * GLA_VAL
QKV_COLS = 3 * DIL_WIDTH
KV_COLS = 2 * DIL_WIDTH

LANES = 128
LSE_LANES_PER_HEAD = LANES // DIL_HEADS
_LOG2_LSE_LANES = LSE_LANES_PER_HEAD.bit_length() - 1
_LOG2_DIL_HD = DIL_HD.bit_length() - 1
assert 1 << _LOG2_LSE_LANES == LSE_LANES_PER_HEAD and 1 << _LOG2_DIL_HD == DIL_HD
VMEM_LIMIT_BYTES = 56 * 1024 * 1024

_NT = (((1,), (1,)), ((), ()))
_TN = (((0,), (0,)), ((), ()))


def _dot(a, b, dims=None, precision=None):
    if dims is None:
        return jnp.dot(a, b, preferred_element_type=F32, precision=precision)
    return lax.dot_general(a, b, dims, preferred_element_type=F32, precision=precision)


def _rms(xf, g):
    return xf * lax.rsqrt(jnp.mean(xf * xf, axis=-1, keepdims=True) + NORM_EPS) * g


def _sigmoid(x):
    return 1.0 / (1.0 + jnp.exp(-x))


def _const_spec(shape):
    nd = len(shape)
    return pl.BlockSpec(shape, lambda *_: (0,) * nd, pipeline_mode=pl.Buffered(1))


def _params(*sem):
    return pltpu.CompilerParams(dimension_semantics=sem, vmem_limit_bytes=VMEM_LIMIT_BYTES)


def _proj_gla_kernel(x_ref, g_ref, w_ref, wlr_ref, wlr2_ref, blr_ref, qkvg_ref, la_ref, *, col_chunk):
    xn = _rms(x_ref[...], g_ref[...]).astype(BF16)
    for c0 in range(0, GLA_COLS, col_chunk):
        qkvg_ref[:, c0:c0 + col_chunk] = _dot(xn, w_ref[:, c0:c0 + col_chunk]).astype(qkvg_ref.dtype)
    glr = _dot(xn, wlr_ref[...])
    z = _dot(glr.astype(BF16), wlr2_ref[...]) + blr_ref[...]
    log_sig = jnp.minimum(z, 0.0) - jnp.log1p(jnp.exp(-jnp.abs(z)))
    la_ref[...] = log_sig * (1.0 / GLA_TAU)


def _proj_gla(x2d, g_pre, w_a, w_lr, w_lr2, b_lr, *, tm, out_dtype):
    n, d = x2d.shape
    return pl.pallas_call(
        functools.partial(_proj_gla_kernel, col_chunk=512),
        grid=(n // tm,),
        in_specs=[
            pl.BlockSpec((tm, d), lambda i: (i, 0)),
            _const_spec(g_pre.shape),
            _const_spec(w_a.shape),
            _const_spec(w_lr.shape),
            _const_spec(w_lr2.shape),
            _const_spec(b_lr.shape),
        ],
        out_specs=[
            pl.BlockSpec((tm, GLA_COLS), lambda i: (i, 0)),
            pl.BlockSpec((tm, GLA_KEY), lambda i: (i, 0)),
        ],
        out_shape=[
            jax.ShapeDtypeStruct((n, GLA_COLS), out_dtype),
            jax.ShapeDtypeStruct((n, GLA_KEY), F32),
        ],
        compiler_params=_params("parallel"),
        name="proj_gla",
    )(x2d, g_pre, w_a, w_lr, w_lr2, b_lr)


def _proj_rest_kernel(x_ref, g_ref, w_ref, *out_refs, tm, tiles_per_seq, keeps):
    a_refs = out_refs[:N_GROUPS]
    gates_ref = out_refs[N_GROUPS]
    kv_refs = out_refs[N_GROUPS + 1:]
    xn = _rms(x_ref[...], g_ref[...]).astype(BF16)
    t = pl.program_id(0) % tiles_per_seq if kv_refs else None
    for g in range(N_GROUPS):
        base = g * QKV_COLS
        for part in range(3):
            c0 = part * DIL_WIDTH
            r = _dot(xn, w_ref[:, base + c0:base + c0 + DIL_WIDTH])
            a_refs[g][:, c0:c0 + DIL_WIDTH] = r.astype(a_refs[g].dtype)
            if kv_refs and part > 0:
                keep = keeps[g]
                kc0 = c0 - DIL_WIDTH
                if keep >= tm:
                    @pl.when(t >= tiles_per_seq - keep // tm)
                    def _(r=r, g=g, kc0=kc0):
                        kv_refs[g][:, kc0:kc0 + DIL_WIDTH] = r
                else:
                    @pl.when(t == tiles_per_seq - 1)
                    def _(r=r, g=g, kc0=kc0, keep=keep):
                        kv_refs[g][:, kc0:kc0 + DIL_WIDTH] = r[tm - keep:, :]
    gbase = N_GROUPS * QKV_COLS
    gcols = gates_ref.shape[1]
    for c0 in range(0, gcols, DIL_WIDTH):
        gates_ref[:, c0:c0 + DIL_WIDTH] = _dot(xn, w_ref[:, gbase + c0:gbase + c0 + DIL_WIDTH]).astype(gates_ref.dtype)


def _proj_rest(x2d, g_pre, w_r, *, tm, seq, out_dtype, with_cache):
    n, d = x2d.shape
    tiles_per_seq = seq // tm if with_cache else None
    gate_cols = w_r.shape[1] - N_GROUPS * QKV_COLS
    keeps = tuple(min(w, seq) for (w, _) in DIL_GROUPS)
    out_specs = [pl.BlockSpec((tm, QKV_COLS), lambda i: (i, 0)) for _ in range(N_GROUPS)]
    out_shape = [jax.ShapeDtypeStruct((n, QKV_COLS), out_dtype) for _ in range(N_GROUPS)]
    out_specs.append(pl.BlockSpec((tm, gate_cols), lambda i: (i, 0)))
    out_shape.append(jax.ShapeDtypeStruct((n, gate_cols), out_dtype))
    if with_cache:
        nb = n // seq
        for keep in keeps:
            if keep >= tm:
                assert keep % tm == 0
                kt = keep // tm

                def idx(i, kt=kt):
                    b, t = i // tiles_per_seq, i % tiles_per_seq
                    return (b * kt + jnp.maximum(t - (tiles_per_seq - kt), 0), 0)

                out_specs.append(pl.BlockSpec((tm, KV_COLS), idx))
            else:
                out_specs.append(pl.BlockSpec((keep, KV_COLS), lambda i: (i // tiles_per_seq, 0)))
            out_shape.append(jax.ShapeDtypeStruct((nb * keep, KV_COLS), F32))
    return pl.pallas_call(
        functools.partial(_proj_rest_kernel, tm=tm, tiles_per_seq=tiles_per_seq, keeps=keeps),
        grid=(n // tm,),
        in_specs=[
            pl.BlockSpec((tm, d), lambda i: (i, 0)),
            _const_spec(g_pre.shape),
            _const_spec(w_r.shape),
        ],
        out_specs=out_specs,
        out_shape=out_shape,
        compiler_params=_params("arbitrary"),
        name="proj_rest",
    )(x2d, g_pre, w_r)


def _gla_kernel(qkvg_ref, la_ref, s0_ref, gn_ref, ya_ref, sfin_ref, st_ref, *, chunk, nchunk, mm_dtype, precision):
    j = pl.program_id(1)

    @pl.when(j == 0)
    def _():
        for h in range(GLA_HEADS):
            st_ref[h] = s0_ref[0, h].T

    row = lax.broadcasted_iota(jnp.int32, (chunk, chunk), 0)
    col = lax.broadcasted_iota(jnp.int32, (chunk, chunk), 1)
    causal = row >= col
    tril = causal.astype(F32)
    qscale = GLA_DK ** -0.5

    def body(c, carry):
        r0 = pl.multiple_of(c * chunk, chunk)
        rows = pl.ds(r0, chunk)
        a = la_ref[rows, :]
        b = _dot(tril, a, precision=lax.Precision.HIGHEST)
        b_end = b[chunk - 1:chunk, :]
        e_b = jnp.exp(b)
        e_nb = jnp.exp(-b)
        e_rest = jnp.exp(b_end - b)
        e_end = jnp.exp(b_end)
        for h in range(GLA_HEADS):
            ks = slice(h * GLA_DK, (h + 1) * GLA_DK)
            vs = slice(h * GLA_DV, (h + 1) * GLA_DV)
            q = qkvg_ref[rows, ks].astype(F32) * qscale
            k = qkvg_ref[rows, GLA_KEY + h * GLA_DK:GLA_KEY + (h + 1) * GLA_DK].astype(F32)
            v = qkvg_ref[rows, 2 * GLA_KEY + h * GLA_DV:2 * GLA_KEY + (h + 1) * GLA_DV].astype(mm_dtype)
            gg = qkvg_ref[rows, 2 * GLA_KEY + GLA_VAL + h * GLA_DV:2 * GLA_KEY + GLA_VAL + (h + 1) * GLA_DV].astype(F32)
            q_in = (q * e_b[:, ks]).astype(mm_dtype)
            k_dec = (k * e_nb[:, ks]).astype(mm_dtype)
            k_end = (k * e_rest[:, ks]).astype(mm_dtype)
            att = _dot(q_in, k_dec, _NT, precision)
            att = jnp.where(causal, att, 0.0).astype(mm_dtype)
            st = st_ref[h]
            o = _dot(att, v, None, precision) + _dot(q_in, st.astype(mm_dtype), _NT, precision)
            st_ref[h] = st * e_end[:, ks] + _dot(v, k_end, _TN, precision)
            on = _rms(o, gn_ref[:, vs])
            ya_ref[rows, vs] = (on * (gg * _sigmoid(gg))).astype(ya_ref.dtype)
        return carry

    lax.fori_loop(0, nchunk, body, 0)

    @pl.when(j == pl.num_programs(1) - 1)
    def _():
        for h in range(GLA_HEADS):
            sfin_ref[0, h] = st_ref[h].T


def _gla(qkvg, la, s0, gn, *, nb, seq, chunk, rows_per_step, out_dtype, mm_dtype, precision):
    steps = seq // rows_per_step
    return pl.pallas_call(
        functools.partial(_gla_kernel, chunk=chunk, nchunk=rows_per_step // chunk, mm_dtype=mm_dtype, precision=precision),
        grid=(nb, steps),
        in_specs=[
            pl.BlockSpec((rows_per_step, GLA_COLS), lambda b, j: (b * steps + j, 0)),
            pl.BlockSpec((rows_per_step, GLA_KEY), lambda b, j: (b * steps + j, 0)),
            pl.BlockSpec((1, GLA_HEADS, GLA_DK, GLA_DV), lambda b, j: (b, 0, 0, 0)),
            pl.BlockSpec((1, GLA_VAL), lambda b, j: (0, 0)),
        ],
        out_specs=[
            pl.BlockSpec((rows_per_step, GLA_VAL), lambda b, j: (b * steps + j, 0)),
            pl.BlockSpec((1, GLA_HEADS, GLA_DK, GLA_DV), lambda b, j: (b, 0, 0, 0)),
        ],
        out_shape=[
            jax.ShapeDtypeStruct((nb * seq, GLA_VAL), out_dtype),
            jax.ShapeDtypeStruct((nb, GLA_HEADS, GLA_DK, GLA_DV), F32),
        ],
        scratch_shapes=[pltpu.VMEM((GLA_HEADS, GLA_DV, GLA_DK), F32)],
        compiler_params=_params("parallel", "arbitrary"),
        name="gla",
    )(qkvg, la, s0, gn)


def _pack_lse(lses):
    rows = lses[0].shape[0]
    lane_head = lax.broadcasted_iota(jnp.int32, (rows, LANES), 1) >> _LOG2_LSE_LANES
    packed = jnp.broadcast_to(lses[0], (rows, LANES))
    for h in range(1, DIL_HEADS):
        packed = jnp.where(lane_head == h, lses[h], packed)
    return packed


def _dil_prompt_kernel(a_ref, o_ref, lse_ref, kvp_ref, *, nblk):
    first = pl.program_id(2) == 0

    @pl.when(first)
    def _():
        kvp_ref[...] = jnp.zeros_like(kvp_ref)

    i_idx = lax.broadcasted_iota(jnp.int32, (DIL_SPAN, DIL_SPAN), 0)
    j_idx = lax.broadcasted_iota(jnp.int32, (DIL_SPAN, DIL_SPAN), 1)
    mask_cur = j_idx <= i_idx
    mask_prev_full = j_idx >= i_idx
    mask_prev_first = (j_idx - i_idx) >= jnp.where(first, DIL_SPAN, 0)
    scale = DIL_HD ** -0.5
    neg_inf = -jnp.inf
    for jb in range(nblk):
        rows = slice(jb * DIL_SPAN, (jb + 1) * DIL_SPAN)
        prows = slice((jb - 1) * DIL_SPAN, jb * DIL_SPAN)
        mask_prev = mask_prev_first if jb == 0 else mask_prev_full
        lses = []
        for h in range(DIL_HEADS):
            hs = slice(h * DIL_HD, (h + 1) * DIL_HD)
            ks = slice(DIL_WIDTH + h * DIL_HD, DIL_WIDTH + (h + 1) * DIL_HD)
            vs = slice(2 * DIL_WIDTH + h * DIL_HD, 2 * DIL_WIDTH + (h + 1) * DIL_HD)
            q = a_ref[rows, hs]
            k_cur = a_ref[rows, ks]
            v_cur = a_ref[rows, vs]
            if jb == 0:
                k_prev = kvp_ref[:, hs]
                v_prev = kvp_ref[:, DIL_WIDTH + h * DIL_HD:DIL_WIDTH + (h + 1) * DIL_HD]
            else:
                k_prev = a_ref[prows, ks]
                v_prev = a_ref[prows, vs]
            s_prev = jnp.where(mask_prev, _dot(q, k_prev, _NT) * scale, neg_inf)
            s_cur = jnp.where(mask_cur, _dot(q, k_cur, _NT) * scale, neg_inf)
            m = jnp.maximum(jnp.max(s_prev, axis=-1, keepdims=True), jnp.max(s_cur, axis=-1, keepdims=True))
            p_prev = jnp.exp(s_prev - m)
            p_cur = jnp.exp(s_cur - m)
            l = jnp.sum(p_prev, axis=-1, keepdims=True) + jnp.sum(p_cur, axis=-1, keepdims=True)
            acc = _dot(p_prev.astype(BF16), v_prev) + _dot(p_cur.astype(BF16), v_cur)
            o_ref[rows, hs] = acc * (1.0 / l)
            lses.append(m + jnp.log(l))
        lse_ref[rows, :] = _pack_lse(lses)
    kvp_ref[...] = a_ref[(nblk - 1) * DIL_SPAN:nblk * DIL_SPAN, DIL_WIDTH:QKV_COLS]


def _dil_prompt(a2d, *, nb, seq, dil, max_rows=512):
    length = seq // dil
    rows = min(max_rows, length)
    a3 = a2d.reshape(nb, length, dil * QKV_COLS)
    o, lse = pl.pallas_call(
        functools.partial(_dil_prompt_kernel, nblk=rows // DIL_SPAN),
        grid=(nb, dil, length // rows),
        in_specs=[pl.BlockSpec((None, rows, QKV_COLS), lambda b, r, n: (b, n, r))],
        out_specs=[
            pl.BlockSpec((None, rows, DIL_WIDTH), lambda b, r, n: (b, n, r)),
            pl.BlockSpec((None, rows, LANES), lambda b, r, n: (b, n, r)),
        ],
        out_shape=[
            jax.ShapeDtypeStruct((nb, length, dil * DIL_WIDTH), F32),
            jax.ShapeDtypeStruct((nb, length, dil * LANES), F32),
        ],
        scratch_shapes=[pltpu.VMEM((DIL_SPAN, KV_COLS), BF16)],
        compiler_params=_params("parallel", "parallel", "arbitrary"),
        name=f"dil_prompt_d{dil}",
    )(a3)
    return o.reshape(nb * seq, DIL_WIDTH), lse.reshape(nb * seq, LANES)


def _dil_sample_kernel(a_ref, cache_ref, o_ref, lse_ref, new_ref, *, dil, t_new):
    lb = cache_ref.shape[0]
    tq = lax.broadcasted_iota(jnp.int32, (t_new, lb), 0)
    row = lax.broadcasted_iota(jnp.int32, (t_new, lb), 1)
    valid_c = jnp.logical_and(((row - tq) & (dil - 1)) == 0, row >= tq)
    tq_n = lax.broadcasted_iota(jnp.int32, (t_new, t_new), 0)
    j_n = lax.broadcasted_iota(jnp.int32, (t_new, t_new), 1)
    valid_n = jnp.logical_and(((tq_n - j_n) & (dil - 1)) == 0, j_n <= tq_n)
    scale = DIL_HD ** -0.5
    neg_inf = -jnp.inf
    lses = []
    for h in range(DIL_HEADS):
        hs = slice(h * DIL_HD, (h + 1) * DIL_HD)
        vs = slice(DIL_WIDTH + h * DIL_HD, DIL_WIDTH + (h + 1) * DIL_HD)
        q = a_ref[:, hs].astype(BF16)
        k_new = a_ref[:, DIL_WIDTH + h * DIL_HD:DIL_WIDTH + (h + 1) * DIL_HD].astype(BF16)
        v_new = a_ref[:, 2 * DIL_WIDTH + h * DIL_HD:2 * DIL_WIDTH + (h + 1) * DIL_HD].astype(BF16)
        k_c = cache_ref[:, hs].astype(BF16)
        v_c = cache_ref[:, vs].astype(BF16)
        s_c = jnp.where(valid_c, _dot(q, k_c, _NT) * scale, neg_inf)
        s_n = jnp.where(valid_n, _dot(q, k_new, _NT) * scale, neg_inf)
        m = jnp.maximum(jnp.max(s_c, axis=-1, keepdims=True), jnp.max(s_n, axis=-1, keepdims=True))
        p_c = jnp.exp(s_c - m)
        p_n = jnp.exp(s_n - m)
        l = jnp.sum(p_c, axis=-1, keepdims=True) + jnp.sum(p_n, axis=-1, keepdims=True)
        acc = _dot(p_c.astype(BF16), v_c) + _dot(p_n.astype(BF16), v_new)
        o_ref[:, hs] = acc * (1.0 / l)
        lses.append(m + jnp.log(l))
    lse_ref[...] = _pack_lse(lses)
    new_ref[0:lb - t_new, :] = cache_ref[t_new:lb, :]
    new_ref[lb - t_new:lb, :] = a_ref[:, DIL_WIDTH:QKV_COLS]


def _dil_sample(a2d, cache, *, dil, t_new):
    nb, lb, _ = cache.shape
    assert lb == dil * DIL_SPAN and a2d.dtype == F32
    return pl.pallas_call(
        functools.partial(_dil_sample_kernel, dil=dil, t_new=t_new),
        grid=(nb,),
        in_specs=[
            pl.BlockSpec((t_new, QKV_COLS), lambda b: (b, 0)),
            pl.BlockSpec((None, lb, KV_COLS), lambda b: (b, 0, 0)),
        ],
        out_specs=[
            pl.BlockSpec((t_new, DIL_WIDTH), lambda b: (b, 0)),
            pl.BlockSpec((t_new, LANES), lambda b: (b, 0)),
            pl.BlockSpec((None, lb, KV_COLS), lambda b: (b, 0, 0)),
        ],
        out_shape=[
            jax.ShapeDtypeStruct((nb * t_new, DIL_WIDTH), F32),
            jax.ShapeDtypeStruct((nb * t_new, LANES), F32),
            jax.ShapeDtypeStruct((nb, lb, KV_COLS), F32),
        ],
        compiler_params=_params("parallel"),
        name=f"dil_sample_d{dil}",
    )(a2d, cache)


def _final_kernel(ya_ref, o0_ref, o1_ref, o2_ref, l0_ref, l1_ref, l2_ref, gates_ref, x_ref, pe_ref,
                  wa_ref, wb_ref, wo_ref, wpp_ref, wpg_ref, gpost_ref, y_ref):
    lse = [l0_ref[...], l1_ref[...], l2_ref[...]]
    m = jnp.maximum(jnp.maximum(lse[0], lse[1]), lse[2])
    e = [jnp.exp(x - m) for x in lse]
    inv = 1.0 / (e[0] + e[1] + e[2])
    src = lax.broadcasted_iota(jnp.int32, (LANES, DIL_WIDTH), 0)
    dst_head = lax.broadcasted_iota(jnp.int32, (LANES, DIL_WIDTH), 1) >> _LOG2_DIL_HD
    expand = (src == dst_head * LSE_LANES_PER_HEAD).astype(BF16)
    o_b = None
    for eg, o_ref in zip(e, (o0_ref, o1_ref, o2_ref)):
        w = eg * inv
        w_hi = w.astype(BF16)
        w_lo = (w - w_hi.astype(F32)).astype(BF16)
        w_full = _dot(w_hi, expand) + _dot(w_lo, expand)
        term = w_full * o_ref[...]
        o_b = term if o_b is None else o_b + term
    dg = gates_ref[:, 0:DIL_WIDTH].astype(F32)
    d = x_ref.shape[1]
    ga = gates_ref[:, DIL_WIDTH:DIL_WIDTH + d].astype(F32)
    gb = gates_ref[:, DIL_WIDTH + d:DIL_WIDTH + 2 * d].astype(F32)
    y_a = _dot(ya_ref[...].astype(BF16), wa_ref[...])
    y_b = _dot((o_b * (dg * _sigmoid(dg))).astype(BF16), wb_ref[...])
    merged = _sigmoid(ga) * y_a + _sigmoid(gb) * y_b
    h = x_ref[...] + _rms(_dot(merged.astype(BF16), wo_ref[...]), gpost_ref[...])
    ple = _dot(pe_ref[...].astype(BF16), wpp_ref[...])
    y_ref[...] = h + ple * _sigmoid(_dot(h.astype(BF16), wpg_ref[...]))


def _final(ya, os_, lses, gates, x2d, pe2d, wa, wb, wo, wpp, wpg, g_post, *, tm):
    n, d = x2d.shape

    def rows(cols):
        return pl.BlockSpec((tm, cols), lambda i: (i, 0))

    return pl.pallas_call(
        _final_kernel,
        grid=(n // tm,),
        in_specs=[rows(GLA_VAL)] + [rows(DIL_WIDTH)] * 3 + [rows(LANES)] * 3
        + [rows(gates.shape[1]), rows(d), rows(pe2d.shape[1])]
        + [_const_spec(w.shape) for w in (wa, wb, wo, wpp, wpg, g_post)],
        out_specs=rows(d),
        out_shape=jax.ShapeDtypeStruct((n, d), F32),
        compiler_params=_params("parallel"),
        name="final",
    )(ya, *os_, *lses, gates, x2d, pe2d, wa, wb, wo, wpp, wpg, g_post)


def _prep_weights(g_pre, g_post, w_in, w_gla_lr, b_gla_lr, g_gla_norm, w_gla_branch, w_dil_branch, w_out,
                  w_ple_proj, w_ple_gate):
    d = w_in.shape[0]
    wb16 = w_in.astype(BF16)
    c_lr = GLA_COLS
    c_q = c_lr + GLA_LOWRANK
    c_k = c_q + N_GROUPS * DIL_WIDTH
    c_v = c_k + N_GROUPS * DIL_WIDTH
    c_g = c_v + N_GROUPS * DIL_WIDTH
    w_a = wb16[:, :c_lr]
    w_lr = jnp.pad(wb16[:, c_lr:c_q], ((0, 0), (0, LANES - GLA_LOWRANK)))
    w_lr2 = jnp.pad(w_gla_lr.astype(BF16), ((0, LANES - GLA_LOWRANK), (0, 0)))
    parts = []
    for g in range(N_GROUPS):
        for c in (c_q, c_k, c_v):
            parts.append(wb16[:, c + g * DIL_WIDTH:c + (g + 1) * DIL_WIDTH])
    parts.append(wb16[:, c_g:])
    w_r = jnp.concatenate(parts, axis=1)
    return dict(
        g_pre=g_pre.reshape(1, d), g_post=g_post.reshape(1, d), w_a=w_a, w_lr=w_lr, w_lr2=w_lr2, w_r=w_r,
        b_lr=b_gla_lr.reshape(1, GLA_KEY), gn=g_gla_norm.reshape(1, GLA_VAL),
        wa=w_gla_branch.astype(BF16), wb=w_dil_branch.astype(BF16), wo=w_out.astype(BF16),
        wpp=w_ple_proj.astype(BF16), wpg=w_ple_gate.astype(BF16),
    )


def _layer(x, pe, s0, caches, w):
    nb, seq, d = x.shape
    n = nb * seq
    x2d = x.reshape(n, d)
    pe2d = pe.reshape(n, pe.shape[-1])
    prompt = caches is None
    if prompt:
        tm = 512
        assert seq % tm == 0 and seq % (DIL_GROUPS[-1][0]) == 0
        inter = BF16
    else:
        tm = n
        inter = F32
    qkvg, la = _proj_gla(x2d, w["g_pre"], w["w_a"], w["w_lr"], w["w_lr2"], w["b_lr"], tm=tm, out_dtype=inter)
    rest = _proj_rest(x2d, w["g_pre"], w["w_r"], tm=tm, seq=seq, out_dtype=inter, with_cache=prompt)
    a_g, gates = rest[:N_GROUPS], rest[N_GROUPS]
    if prompt:
        ya, s_new = _gla(qkvg, la, s0, w["gn"], nb=nb, seq=seq, chunk=GLA_CHUNK, rows_per_step=512,
                         out_dtype=BF16, mm_dtype=BF16, precision=None)
        new_bufs = [kv.reshape(nb, -1, 2, DIL_HEADS, DIL_HD) for kv in rest[N_GROUPS + 1:]]
        os_, lses = [], []
        for g, (_, dil) in enumerate(DIL_GROUPS):
            o_g, lse_g = _dil_prompt(a_g[g], nb=nb, seq=seq, dil=dil)
            os_.append(o_g)
            lses.append(lse_g)
    else:
        chunk = min(GLA_CHUNK, seq)
        assert seq % chunk == 0
        ya, s_new = _gla(qkvg, la, s0, w["gn"], nb=nb, seq=seq, chunk=chunk, rows_per_step=seq,
                         out_dtype=F32, mm_dtype=F32, precision=lax.Precision.HIGHEST)
        os_, lses, new_bufs = [], [], []
        for g, (_, dil) in enumerate(DIL_GROUPS):
            cache = caches[g]
            lb = cache.shape[1]
            o_g, lse_g, nc = _dil_sample(a_g[g], cache.reshape(nb, lb, KV_COLS), dil=dil, t_new=seq)
            os_.append(o_g)
            lses.append(lse_g)
            new_bufs.append(nc.reshape(cache.shape))
    y = _final(ya, os_, lses, gates, x2d, pe2d, w["wa"], w["wb"], w["wo"], w["wpp"], w["wpg"], w["g_post"], tm=tm)
    return y.reshape(nb, seq, d), s_new, new_bufs


def kernel(x_prompt, x_sample, state_gla, cache_kv_w128, cache_kv_w512, cache_kv_w2048, p_prompt, p_sample, g_pre, g_post, w_in, w_gla_lr, b_gla_lr, g_gla_norm, w_gla_branch, w_dil_branch, w_out, w_ple_proj, w_ple_gate):
    depth = w_in.shape[0]
    hp, hs = x_prompt, x_sample
    gla_p, gla_s = [], []
    kvp = [[] for _ in DIL_GROUPS]
    kvs = [[] for _ in DIL_GROUPS]
    for i in range(depth):
        w = _prep_weights(g_pre[i], g_post[i], w_in[i], w_gla_lr[i], b_gla_lr[i], g_gla_norm[i], w_gla_branch[i],
                          w_dil_branch[i], w_out[i], w_ple_proj[i], w_ple_gate[i])
        s0 = jnp.zeros((x_prompt.shape[0], GLA_HEADS, GLA_DK, GLA_DV), F32)
        hp, sp_new, bp_new = _layer(hp, p_prompt[i], s0, None, w)
        hs, ss_new, bs_new = _layer(hs, p_sample[i], state_gla[i],
                                    (cache_kv_w128[i], cache_kv_w512[i], cache_kv_w2048[i]), w)
        gla_p.append(sp_new)
        gla_s.append(ss_new)
        for g in range(N_GROUPS):
            kvp[g].append(bp_new[g])
            kvs[g].append(bs_new[g])
    return (hp, hs, jnp.stack(gla_p), jnp.stack(gla_s),
            jnp.stack(kvp[0]), jnp.stack(kvp[1]), jnp.stack(kvp[2]),
            jnp.stack(kvs[0]), jnp.stack(kvs[1]), jnp.stack(kvs[2]))
```

```python
import functools

import jax
import jax.numpy as jnp
from jax import lax
from jax.experimental import pallas as pl
from jax.experimental.pallas import tpu as pltpu

F32 = jnp.float32
BF16 = jnp.bfloat16

NORM_EPS = 1e-6
GLA_HEADS = 4
GLA_DK = 128
GLA_DV = 256
GLA_KEY = GLA_HEADS * GLA_DK
GLA_VAL = GLA_HEADS * GLA_DV
GLA_LOWRANK = 16
GLA_TAU = 16.0
GLA_CHUNK = 64
DIL_GROUPS = ((128, 1), (512, 4), (2048, 16))
DIL_HEADS = 4
DIL_HD = 128
DIL_WIDTH = DIL_HEADS * DIL_HD
DIL_SPAN = 128
N_GROUPS = len(DIL_GROUPS)
GLA_COLS = 2 * GLA_KEY + 2 * GLA_VAL
QKV_COLS = 3 * DIL_WIDTH
KV_COLS = 2 * DIL_WIDTH

LANES = 128
LSE_LANES_PER_HEAD = LANES // DIL_HEADS
_LOG2_LSE_LANES = LSE_LANES_PER_HEAD.bit_length() - 1
_LOG2_DIL_HD = DIL_HD.bit_length() - 1
assert 1 << _LOG2_LSE_LANES == LSE_LANES_PER_HEAD and 1 << _LOG2_DIL_HD == DIL_HD
VMEM_LIMIT_BYTES = 56 * 1024 * 1024

_NT = (((1,), (1,)), ((), ()))
_TN = (((0,), (0,)), ((), ()))


def _dot(a, b, dims=None, precision=None):
    if dims is None:
        return jnp.dot(a, b, preferred_element_type=F32, precision=precision)
    return lax.dot_general(a, b, dims, preferred_element_type=F32, precision=precision)


def _rms(xf, g):
    return xf * lax.rsqrt(jnp.mean(xf * xf, axis=-1, keepdims=True) + NORM_EPS) * g


def _sigmoid(x):
    return 1.0 / (1.0 + jnp.exp(-x))


def _const_spec(shape):
    nd = len(shape)
    return pl.BlockSpec(shape, lambda *_: (0,) * nd, pipeline_mode=pl.Buffered(1))


def _params(*sem):
    return pltpu.CompilerParams(dimension_semantics=sem, vmem_limit_bytes=VMEM_LIMIT_BYTES)


def _proj_gla_kernel(x_ref, g_ref, w_ref, wlr_ref, wlr2_ref, blr_ref, qkvg_ref, la_ref, *, col_chunk):
    xn = _rms(x_ref[...], g_ref[...]).astype(BF16)
    for c0 in range(0, GLA_COLS, col_chunk):
        qkvg_ref[:, c0:c0 + col_chunk] = _dot(xn, w_ref[:, c0:c0 + col_chunk]).astype(qkvg_ref.dtype)
    glr = _dot(xn, wlr_ref[...])
    z = _dot(glr.astype(BF16), wlr2_ref[...]) + blr_ref[...]
    log_sig = jnp.minimum(z, 0.0) - jnp.log1p(jnp.exp(-jnp.abs(z)))
    la_ref[...] = log_sig * (1.0 / GLA_TAU)


def _proj_gla(x2d, g_pre, w_a, w_lr, w_lr2, b_lr, *, tm, out_dtype):
    n, d = x2d.shape
    return pl.pallas_call(
        functools.partial(_proj_gla_kernel, col_chunk=512),
        grid=(n // tm,),
        in_specs=[
            pl.BlockSpec((tm, d), lambda i: (i, 0)),
            _const_spec(g_pre.shape),
            _const_spec(w_a.shape),
            _const_spec(w_lr.shape),
            _const_spec(w_lr2.shape),
            _const_spec(b_lr.shape),
        ],
        out_specs=[
            pl.BlockSpec((tm, GLA_COLS), lambda i: (i, 0)),
            pl.BlockSpec((tm, GLA_KEY), lambda i: (i, 0)),
        ],
        out_shape=[
            jax.ShapeDtypeStruct((n, GLA_COLS), out_dtype),
            jax.ShapeDtypeStruct((n, GLA_KEY), F32),
        ],
        compiler_params=_params("parallel"),
        name="proj_gla",
    )(x2d, g_pre, w_a, w_lr, w_lr2, b_lr)


def _proj_rest_kernel(x_ref, g_ref, w_ref, *refs, tm, tiles_per_seq, keeps, prompt):
    a_refs = refs[:N_GROUPS]
    gates_ref = refs[N_GROUPS]
    kv_refs = refs[N_GROUPS + 1:2 * N_GROUPS + 1] if prompt else ()
    stage_refs = refs[2 * N_GROUPS + 1:] if prompt else ()
    xn = _rms(x_ref[...], g_ref[...]).astype(BF16)
    t = pl.program_id(0) % tiles_per_seq if prompt else None
    n_staged = 0
    for g in range(N_GROUPS):
        base = g * QKV_COLS
        dil = DIL_GROUPS[g][1]
        for part in range(3):
            c0 = part * DIL_WIDTH
            r = _dot(xn, w_ref[:, base + c0:base + c0 + DIL_WIDTH])
            if not prompt:
                a_refs[g][:, c0:c0 + DIL_WIDTH] = r
                continue
            if dil == 1:
                a_refs[g][0, :, c0:c0 + DIL_WIDTH] = r.astype(BF16)
            else:
                stage = stage_refs[n_staged % len(stage_refs)]
                n_staged += 1
                for s in range(DIL_WIDTH // LANES):
                    stage[s] = r[:, s * LANES:(s + 1) * LANES]
                for res in range(dil):
                    for s in range(DIL_WIDTH // LANES):
                        a_refs[g][res, :, c0 + s * LANES:c0 + (s + 1) * LANES] = (
                            stage[s, pl.ds(res, tm // dil, stride=dil), :].astype(BF16))
            if part > 0:
                keep = keeps[g]
                rows = min(keep, tm)
                cond = (t >= tiles_per_seq - keep // tm) if keep >= tm else (t == tiles_per_seq - 1)

                @pl.when(cond)
                def _(r=r, g=g, part=part, rows=rows):
                    for h in range(DIL_HEADS):
                        kv_refs[g][pl.ds((part - 1) * DIL_HEADS + h, rows, stride=2 * DIL_HEADS), :] = (
                            r[tm - rows:, h * DIL_HD:(h + 1) * DIL_HD])
    gbase = N_GROUPS * QKV_COLS
    gcols = gates_ref.shape[1]
    for c0 in range(0, gcols, DIL_WIDTH):
        gates_ref[:, c0:c0 + DIL_WIDTH] = _dot(xn, w_ref[:, gbase + c0:gbase + c0 + DIL_WIDTH]).astype(gates_ref.dtype)


def _proj_rest(x2d, g_pre, w_r, *, tm, seq, prompt):
    n, d = x2d.shape
    nb = n // seq
    tiles_per_seq = seq // tm if prompt else None
    gate_cols = w_r.shape[1] - N_GROUPS * QKV_COLS
    keeps = tuple(min(w, seq) for (w, _) in DIL_GROUPS)
    out_specs, out_shape, scratch = [], [], []
    if prompt:
        for _, dil in DIL_GROUPS:
            out_specs.append(pl.BlockSpec((None, dil, tm // dil, QKV_COLS),
                                          lambda i: (i // tiles_per_seq, 0, i % tiles_per_seq, 0)))
            out_shape.append(jax.ShapeDtypeStruct((nb, dil, seq // dil, QKV_COLS), BF16))
    else:
        out_specs += [pl.BlockSpec((tm, QKV_COLS), lambda i: (i, 0)) for _ in range(N_GROUPS)]
        out_shape += [jax.ShapeDtypeStruct((n, QKV_COLS), F32) for _ in range(N_GROUPS)]
    out_specs.append(pl.BlockSpec((tm, gate_cols), lambda i: (i, 0)))
    out_shape.append(jax.ShapeDtypeStruct((n, gate_cols), BF16 if prompt else F32))
    if prompt:
        rows_per_token = 2 * DIL_HEADS
        for keep in keeps:
            if keep >= tm:
                assert keep % tm == 0
                kt = keep // tm

                def idx(i, kt=kt):
                    b, t = i // tiles_per_seq, i % tiles_per_seq
                    return (b * kt + jnp.maximum(t - (tiles_per_seq - kt), 0), 0)

                out_specs.append(pl.BlockSpec((tm * rows_per_token, DIL_HD), idx))
            else:
                out_specs.append(pl.BlockSpec((keep * rows_per_token, DIL_HD), lambda i: (i // tiles_per_seq, 0)))
            out_shape.append(jax.ShapeDtypeStruct((nb * keep * rows_per_token, DIL_HD), F32))
        scratch = [pltpu.VMEM((DIL_WIDTH // LANES, tm, LANES), F32)] * 2
    return pl.pallas_call(
        functools.partial(_proj_rest_kernel, tm=tm, tiles_per_seq=tiles_per_seq, keeps=keeps, prompt=prompt),
        grid=(n // tm,),
        in_specs=[
            pl.BlockSpec((tm, d), lambda i: (i, 0)),
            _const_spec(g_pre.shape),
            _const_spec(w_r.shape),
        ],
        out_specs=out_specs,
        out_shape=out_shape,
        scratch_shapes=scratch,
        compiler_params=_params("arbitrary"),
        name="proj_rest",
    )(x2d, g_pre, w_r)


def _gla_kernel(qkvg_ref, la_ref, s0_ref, gn_ref, ya_ref, sfin_ref, st_ref, *, chunk, nchunk, mm_dtype, precision):
    j = pl.program_id(1)

    @pl.when(j == 0)
    def _():
        for h in range(GLA_HEADS):
            st_ref[h] = s0_ref[0, h].T

    row = lax.broadcasted_iota(jnp.int32, (chunk, chunk), 0)
    col = lax.broadcasted_iota(jnp.int32, (chunk, chunk), 1)
    causal = row >= col
    tril = causal.astype(F32)
    qscale = GLA_DK ** -0.5

    def body(c, carry):
        r0 = pl.multiple_of(c * chunk, chunk)
        rows = pl.ds(r0, chunk)
        a = la_ref[rows, :]
        b = _dot(tril, a, precision=lax.Precision.HIGHEST)
        b_end = b[chunk - 1:chunk, :]
        e_b = jnp.exp(b)
        e_nb = jnp.exp(-b)
        e_rest = jnp.exp(b_end - b)
        e_end = jnp.exp(b_end)
        for h in range(GLA_HEADS):
            ks = slice(h * GLA_DK, (h + 1) * GLA_DK)
            vs = slice(h * GLA_DV, (h + 1) * GLA_DV)
            q = qkvg_ref[rows, ks].astype(F32) * qscale
            k = qkvg_ref[rows, GLA_KEY + h * GLA_DK:GLA_KEY + (h + 1) * GLA_DK].astype(F32)
            v = qkvg_ref[rows, 2 * GLA_KEY + h * GLA_DV:2 * GLA_KEY + (h + 1) * GLA_DV].astype(mm_dtype)
            gg = qkvg_ref[rows, 2 * GLA_KEY + GLA_VAL + h * GLA_DV:2 * GLA_KEY + GLA_VAL + (h + 1) * GLA_DV].astype(F32)
            q_in = (q * e_b[:, ks]).astype(mm_dtype)
            k_dec = (k * e_nb[:, ks]).astype(mm_dtype)
            k_end = (k * e_rest[:, ks]).astype(mm_dtype)
            att = _dot(q_in, k_dec, _NT, precision)
            att = jnp.where(causal, att, 0.0).astype(mm_dtype)
            st = st_ref[h]
            o = _dot(att, v, None, precision) + _dot(q_in, st.astype(mm_dtype), _NT, precision)
            st_ref[h] = st * e_end[:, ks] + _dot(v, k_end, _TN, precision)
            on = _rms(o, gn_ref[:, vs])
            ya_ref[rows, vs] = (on * (gg * _sigmoid(gg))).astype(ya_ref.dtype)
        return carry

    lax.fori_loop(0, nchunk, body, 0)

    @pl.when(j == pl.num_programs(1) - 1)
    def _():
        for h in range(GLA_HEADS):
            sfin_ref[0, h] = st_ref[h].T


def _gla(qkvg, la, s0, gn, *, nb, seq, chunk, rows_per_step, out_dtype, mm_dtype, precision):
    steps = seq // rows_per_step
    return pl.pallas_call(
        functools.partial(_gla_kernel, chunk=chunk, nchunk=rows_per_step // chunk, mm_dtype=mm_dtype, precision=precision),
        grid=(nb, steps),
        in_specs=[
            pl.BlockSpec((rows_per_step, GLA_COLS), lambda b, j: (b * steps + j, 0)),
            pl.BlockSpec((rows_per_step, GLA_KEY), lambda b, j: (b * steps + j, 0)),
            pl.BlockSpec((1, GLA_HEADS, GLA_DK, GLA_DV), lambda b, j: (b, 0, 0, 0)),
            pl.BlockSpec((1, GLA_VAL), lambda b, j: (0, 0)),
        ],
        out_specs=[
            pl.BlockSpec((rows_per_step, GLA_VAL), lambda b, j: (b * steps + j, 0)),
            pl.BlockSpec((1, GLA_HEADS, GLA_DK, GLA_DV), lambda b, j: (b, 0, 0, 0)),
        ],
        out_shape=[
            jax.ShapeDtypeStruct((nb * seq, GLA_VAL), out_dtype),
            jax.ShapeDtypeStruct((nb, GLA_HEADS, GLA_DK, GLA_DV), F32),
        ],
        scratch_shapes=[pltpu.VMEM((GLA_HEADS, GLA_DV, GLA_DK), F32)],
        compiler_params=_params("parallel", "arbitrary"),
        name="gla",
    )(qkvg, la, s0, gn)


def _pack_lse(lses):
    rows = lses[0].shape[0]
    lane_head = lax.broadcasted_iota(jnp.int32, (rows, LANES), 1) >> _LOG2_LSE_LANES
    packed = jnp.broadcast_to(lses[0], (rows, LANES))
    for h in range(1, DIL_HEADS):
        packed = jnp.where(lane_head == h, lses[h], packed)
    return packed


def _dil_prompt_kernel(a_ref, o_ref, lse_ref, kvp_ref, *, nblk):
    first = pl.program_id(2) == 0

    @pl.when(first)
    def _():
        kvp_ref[...] = jnp.zeros_like(kvp_ref)

    i_idx = lax.broadcasted_iota(jnp.int32, (DIL_SPAN, DIL_SPAN), 0)
    j_idx = lax.broadcasted_iota(jnp.int32, (DIL_SPAN, DIL_SPAN), 1)
    mask_cur = j_idx <= i_idx
    mask_prev_full = j_idx >= i_idx
    mask_prev_first = (j_idx - i_idx) >= jnp.where(first, DIL_SPAN, 0)
    scale = DIL_HD ** -0.5
    neg_inf = -jnp.inf
    for jb in range(nblk):
        rows = slice(jb * DIL_SPAN, (jb + 1) * DIL_SPAN)
        prows = slice((jb - 1) * DIL_SPAN, jb * DIL_SPAN)
        mask_prev = mask_prev_first if jb == 0 else mask_prev_full
        lses = []
        for h in range(DIL_HEADS):
            hs = slice(h * DIL_HD, (h + 1) * DIL_HD)
            ks = slice(DIL_WIDTH + h * DIL_HD, DIL_WIDTH + (h + 1) * DIL_HD)
            vs = slice(2 * DIL_WIDTH + h * DIL_HD, 2 * DIL_WIDTH + (h + 1) * DIL_HD)
            q = a_ref[rows, hs]
            k_cur = a_ref[rows, ks]
            v_cur = a_ref[rows, vs]
            if jb == 0:
                k_prev = kvp_ref[:, hs]
                v_prev = kvp_ref[:, DIL_WIDTH + h * DIL_HD:DIL_WIDTH + (h + 1) * DIL_HD]
            else:
                k_prev = a_ref[prows, ks]
                v_prev = a_ref[prows, vs]
            s_prev = jnp.where(mask_prev, _dot(q, k_prev, _NT) * scale, neg_inf)
            s_cur = jnp.where(mask_cur, _dot(q, k_cur, _NT) * scale, neg_inf)
            m = jnp.max(jnp.maximum(s_prev, s_cur), axis=-1, keepdims=True)
            p_prev = jnp.exp(s_prev - m)
            p_cur = jnp.exp(s_cur - m)
            l = jnp.sum(p_prev + p_cur, axis=-1, keepdims=True)
            acc = _dot(p_prev.astype(BF16), v_prev) + _dot(p_cur.astype(BF16), v_cur)
            o_ref[rows, hs] = acc * (1.0 / l)
            lses.append(m + jnp.log(l))
        lse_ref[rows, :] = _pack_lse(lses)
    kvp_ref[...] = a_ref[(nblk - 1) * DIL_SPAN:nblk * DIL_SPAN, DIL_WIDTH:QKV_COLS]


def _dil_prompt(a4, *, max_rows=512):
    nb, dil, length, _ = a4.shape
    rows = min(max_rows, length)

    def spec(cols):
        return pl.BlockSpec((None, None, rows, cols), lambda b, r, n: (b, r, n, 0))

    return pl.pallas_call(
        functools.partial(_dil_prompt_kernel, nblk=rows // DIL_SPAN),
        grid=(nb, dil, length // rows),
        in_specs=[spec(QKV_COLS)],
        out_specs=[spec(DIL_WIDTH), spec(LANES)],
        out_shape=[
            jax.ShapeDtypeStruct((nb, dil, length, DIL_WIDTH), F32),
            jax.ShapeDtypeStruct((nb, dil, length, LANES), F32),
        ],
        scratch_shapes=[pltpu.VMEM((DIL_SPAN, KV_COLS), BF16)],
        compiler_params=_params("parallel", "parallel", "arbitrary"),
        name=f"dil_prompt_d{dil}",
    )(a4)


def _dil_sample_kernel(a_ref, cache_ref, o_ref, lse_ref, new_ref, *, dil, t_new):
    rpt = 2 * DIL_HEADS
    lb = cache_ref.shape[0] // rpt
    tq = lax.broadcasted_iota(jnp.int32, (t_new, lb), 0)
    row = lax.broadcasted_iota(jnp.int32, (t_new, lb), 1)
    valid_c = jnp.logical_and(((row - tq) & (dil - 1)) == 0, row >= tq)
    tq_n = lax.broadcasted_iota(jnp.int32, (t_new, t_new), 0)
    j_n = lax.broadcasted_iota(jnp.int32, (t_new, t_new), 1)
    valid_n = jnp.logical_and(((tq_n - j_n) & (dil - 1)) == 0, j_n <= tq_n)
    scale = DIL_HD ** -0.5
    neg_inf = -jnp.inf
    lses = []
    for h in range(DIL_HEADS):
        hs = slice(h * DIL_HD, (h + 1) * DIL_HD)
        vs = slice(DIL_WIDTH + h * DIL_HD, DIL_WIDTH + (h + 1) * DIL_HD)
        q = a_ref[:, hs].astype(BF16)
        k_new = a_ref[:, DIL_WIDTH + h * DIL_HD:DIL_WIDTH + (h + 1) * DIL_HD].astype(BF16)
        v_new = a_ref[:, 2 * DIL_WIDTH + h * DIL_HD:2 * DIL_WIDTH + (h + 1) * DIL_HD].astype(BF16)
        k_c = cache_ref[pl.ds(h, lb, stride=rpt), :].astype(BF16)
        v_c = cache_ref[pl.ds(DIL_HEADS + h, lb, stride=rpt), :].astype(BF16)
        s_c = jnp.where(valid_c, _dot(q, k_c, _NT) * scale, neg_inf)
        s_n = jnp.where(valid_n, _dot(q, k_new, _NT) * scale, neg_inf)
        m = jnp.maximum(jnp.max(s_c, axis=-1, keepdims=True), jnp.max(s_n, axis=-1, keepdims=True))
        p_c = jnp.exp(s_c - m)
        p_n = jnp.exp(s_n - m)
        l = jnp.sum(p_c, axis=-1, keepdims=True) + jnp.sum(p_n, axis=-1, keepdims=True)
        acc = _dot(p_c.astype(BF16), v_c) + _dot(p_n.astype(BF16), v_new)
        o_ref[:, hs] = acc * (1.0 / l)
        lses.append(m + jnp.log(l))
    lse_ref[...] = _pack_lse(lses)
    kept = (lb - t_new) * rpt
    new_ref[0:kept, :] = cache_ref[t_new * rpt:lb * rpt, :]
    for c in range(rpt):
        new_ref[pl.ds(kept + c, t_new, stride=rpt), :] = a_ref[:, DIL_WIDTH + c * DIL_HD:DIL_WIDTH + (c + 1) * DIL_HD]


def _dil_sample(a2d, cache, *, dil, t_new):
    nb, lb = cache.shape[:2]
    assert lb == dil * DIL_SPAN and a2d.dtype == F32
    rows = lb * 2 * DIL_HEADS
    o, lse, new = pl.pallas_call(
        functools.partial(_dil_sample_kernel, dil=dil, t_new=t_new),
        grid=(nb,),
        in_specs=[
            pl.BlockSpec((t_new, QKV_COLS), lambda b: (b, 0)),
            pl.BlockSpec((None, rows, DIL_HD), lambda b: (b, 0, 0)),
        ],
        out_specs=[
            pl.BlockSpec((t_new, DIL_WIDTH), lambda b: (b, 0)),
            pl.BlockSpec((t_new, LANES), lambda b: (b, 0)),
            pl.BlockSpec((None, rows, DIL_HD), lambda b: (b, 0, 0)),
        ],
        out_shape=[
            jax.ShapeDtypeStruct((nb * t_new, DIL_WIDTH), F32),
            jax.ShapeDtypeStruct((nb * t_new, LANES), F32),
            jax.ShapeDtypeStruct((nb, rows, DIL_HD), F32),
        ],
        compiler_params=_params("parallel"),
        name=f"dil_sample_d{dil}",
    )(a2d, cache.reshape(nb, rows, DIL_HD))
    return o, lse, new.reshape(cache.shape)


def _token_order(ref, scratch_refs):
    if len(ref.shape) == 2:
        return ref[...]
    dil, rows_per_class, _ = ref.shape
    if dil == 1:
        return ref[0]
    scratch = scratch_refs.pop()
    n_slabs = scratch.shape[0]
    for res in range(dil):
        for s in range(n_slabs):
            scratch[s, pl.ds(res, rows_per_class, stride=dil), :] = ref[res, :, s * LANES:(s + 1) * LANES]
    return jnp.concatenate([scratch[s] for s in range(n_slabs)], axis=1) if n_slabs > 1 else scratch[0]


def _final_kernel(ya_ref, o0_ref, o1_ref, o2_ref, l0_ref, l1_ref, l2_ref, gates_ref, x_ref, pe_ref,
                  wa_ref, wb_ref, wo_ref, wpp_ref, wpg_ref, gpost_ref, y_ref, *scratch_refs):
    o_scratch = [s for s in scratch_refs if s.shape[0] == DIL_WIDTH // LANES]
    l_scratch = [s for s in scratch_refs if s.shape[0] == 1]
    lse = [_token_order(r, l_scratch) for r in (l0_ref, l1_ref, l2_ref)]
    o_groups = [_token_order(r, o_scratch) for r in (o0_ref, o1_ref, o2_ref)]
    m = jnp.maximum(jnp.maximum(lse[0], lse[1]), lse[2])
    e = [jnp.exp(x - m) for x in lse]
    inv = 1.0 / (e[0] + e[1] + e[2])
    src = lax.broadcasted_iota(jnp.int32, (LANES, DIL_WIDTH), 0)
    dst_head = lax.broadcasted_iota(jnp.int32, (LANES, DIL_WIDTH), 1) >> _LOG2_DIL_HD
    expand = (src == dst_head * LSE_LANES_PER_HEAD).astype(BF16)
    o_b = None
    for eg, o_g in zip(e, o_groups):
        w = eg * inv
        w_hi = w.astype(BF16)
        w_lo = (w - w_hi.astype(F32)).astype(BF16)
        w_full = _dot(w_hi, expand) + _dot(w_lo, expand)
        term = w_full * o_g
        o_b = term if o_b is None else o_b + term
    dg = gates_ref[:, 0:DIL_WIDTH].astype(F32)
    d = x_ref.shape[1]
    ga = gates_ref[:, DIL_WIDTH:DIL_WIDTH + d].astype(F32)
    gb = gates_ref[:, DIL_WIDTH + d:DIL_WIDTH + 2 * d].astype(F32)
    y_a = _dot(ya_ref[...].astype(BF16), wa_ref[...])
    y_b = _dot((o_b * (dg * _sigmoid(dg))).astype(BF16), wb_ref[...])
    merged = _sigmoid(ga) * y_a + _sigmoid(gb) * y_b
    h = x_ref[...] + _rms(_dot(merged.astype(BF16), wo_ref[...]), gpost_ref[...])
    ple = _dot(pe_ref[...].astype(BF16), wpp_ref[...])
    y_ref[...] = h + ple * _sigmoid(_dot(h.astype(BF16), wpg_ref[...]))


def _final(ya, os_, lses, gates, x2d, pe2d, wa, wb, wo, wpp, wpg, g_post, *, tm):
    n, d = x2d.shape

    def rows(cols):
        return pl.BlockSpec((tm, cols), lambda i: (i, 0))

    scratch = []

    def group_spec(arr):
        if arr.ndim == 2:
            return rows(arr.shape[1])
        _, dil, length, cols = arr.shape
        tiles_per_seq = dil * length // tm
        if dil > 1:
            scratch.append(pltpu.VMEM((cols // LANES, tm, LANES), F32))
        return pl.BlockSpec((None, dil, tm // dil, cols), lambda i: (i // tiles_per_seq, 0, i % tiles_per_seq, 0))

    return pl.pallas_call(
        _final_kernel,
        grid=(n // tm,),
        in_specs=[rows(GLA_VAL)] + [group_spec(a) for a in os_] + [group_spec(a) for a in lses]
        + [rows(gates.shape[1]), rows(d), rows(pe2d.shape[1])]
        + [_const_spec(w.shape) for w in (wa, wb, wo, wpp, wpg, g_post)],
        out_specs=rows(d),
        out_shape=jax.ShapeDtypeStruct((n, d), F32),
        scratch_shapes=scratch,
        compiler_params=_params("parallel"),
        name="final",
    )(ya, *os_, *lses, gates, x2d, pe2d, wa, wb, wo, wpp, wpg, g_post)


def _prep_weights(g_pre, g_post, w_in, w_gla_lr, b_gla_lr, g_gla_norm, w_gla_branch, w_dil_branch, w_out,
                  w_ple_proj, w_ple_gate):
    d = w_in.shape[0]
    wb16 = w_in.astype(BF16)
    c_lr = GLA_COLS
    c_q = c_lr + GLA_LOWRANK
    c_k = c_q + N_GROUPS * DIL_WIDTH
    c_v = c_k + N_GROUPS * DIL_WIDTH
    c_g = c_v + N_GROUPS * DIL_WIDTH
    w_a = wb16[:, :c_lr]
    w_lr = jnp.pad(wb16[:, c_lr:c_q], ((0, 0), (0, LANES - GLA_LOWRANK)))
    w_lr2 = jnp.pad(w_gla_lr.astype(BF16), ((0, LANES - GLA_LOWRANK), (0, 0)))
    parts = []
    for g in range(N_GROUPS):
        for c in (c_q, c_k, c_v):
            parts.append(wb16[:, c + g * DIL_WIDTH:c + (g + 1) * DIL_WIDTH])
    parts.append(wb16[:, c_g:])
    w_r = jnp.concatenate(parts, axis=1)
    return dict(
        g_pre=g_pre.reshape(1, d), g_post=g_post.reshape(1, d), w_a=w_a, w_lr=w_lr, w_lr2=w_lr2, w_r=w_r,
        b_lr=b_gla_lr.reshape(1, GLA_KEY), gn=g_gla_norm.reshape(1, GLA_VAL),
        wa=w_gla_branch.astype(BF16), wb=w_dil_branch.astype(BF16), wo=w_out.astype(BF16),
        wpp=w_ple_proj.astype(BF16), wpg=w_ple_gate.astype(BF16),
    )


def _layer(x, pe, s0, caches, w):
    nb, seq, d = x.shape
    n = nb * seq
    x2d = x.reshape(n, d)
    pe2d = pe.reshape(n, pe.shape[-1])
    prompt = caches is None
    if prompt:
        tm = 512
        assert seq % tm == 0 and seq % (DIL_GROUPS[-1][0]) == 0
        inter = BF16
    else:
        tm = n
        inter = F32
    qkvg, la = _proj_gla(x2d, w["g_pre"], w["w_a"], w["w_lr"], w["w_lr2"], w["b_lr"], tm=tm, out_dtype=inter)
    rest = _proj_rest(x2d, w["g_pre"], w["w_r"], tm=tm, seq=seq, prompt=prompt)
    a_g, gates = rest[:N_GROUPS], rest[N_GROUPS]
    if prompt:
        ya, s_new = _gla(qkvg, la, s0, w["gn"], nb=nb, seq=seq, chunk=GLA_CHUNK, rows_per_step=512,
                         out_dtype=BF16, mm_dtype=BF16, precision=None)
        new_bufs = [kv.reshape(nb, -1, 2, DIL_HEADS, DIL_HD) for kv in rest[N_GROUPS + 1:]]
        os_, lses = [], []
        for g in range(N_GROUPS):
            o_g, lse_g = _dil_prompt(a_g[g])
            os_.append(o_g)
            lses.append(lse_g)
    else:
        chunk = min(GLA_CHUNK, seq)
        assert seq % chunk == 0
        ya, s_new = _gla(qkvg, la, s0, w["gn"], nb=nb, seq=seq, chunk=chunk, rows_per_step=seq,
                         out_dtype=F32, mm_dtype=F32, precision=lax.Precision.HIGHEST)
        os_, lses, new_bufs = [], [], []
        for g, (_, dil) in enumerate(DIL_GROUPS):
            o_g, lse_g, nc = _dil_sample(a_g[g], caches[g], dil=dil, t_new=seq)
            os_.append(o_g)
            lses.append(lse_g)
            new_bufs.append(nc)
    y = _final(ya, os_, lses, gates, x2d, pe2d, w["wa"], w["wb"], w["wo"], w["wpp"], w["wpg"], w["g_post"], tm=tm)
    return y.reshape(nb, seq, d), s_new, new_bufs


def kernel(x_prompt, x_sample, state_gla, cache_kv_w128, cache_kv_w512, cache_kv_w2048, p_prompt, p_sample, g_pre, g_post, w_in, w_gla_lr, b_gla_lr, g_gla_norm, w_gla_branch, w_dil_branch, w_out, w_ple_proj, w_ple_gate):
    depth = w_in.shape[0]
    hp, hs = x_prompt, x_sample
    gla_p, gla_s = [], []
    kvp = [[] for _ in DIL_GROUPS]
    kvs = [[] for _ in DIL_GROUPS]
    for i in range(depth):
        w = _prep_weights(g_pre[i], g_post[i], w_in[i], w_gla_lr[i], b_gla_lr[i], g_gla_norm[i], w_gla_branch[i],
                          w_dil_branch[i], w_out[i], w_ple_proj[i], w_ple_gate[i])
        s0 = jnp.zeros((x_prompt.shape[0], GLA_HEADS, GLA_DK, GLA_DV), F32)
        hp, sp_new, bp_new = _layer(hp, p_prompt[i], s0, None, w)
        hs, ss_new, bs_new = _layer(hs, p_sample[i], state_gla[i],
                                    (cache_kv_w128[i], cache_kv_w512[i], cache_kv_w2048[i]), w)
        gla_p.append(sp_new)
        gla_s.append(ss_new)
        for g in range(N_GROUPS):
            kvp[g].append(bp_new[g])
            kvs[g].append(bs_new[g])
    return (hp, hs, jnp.stack(gla_p), jnp.stack(gla_s),
            jnp.stack(kvp[0]), jnp.stack(kvp[1]), jnp.stack(kvp[2]),
            jnp.stack(kvs[0]), jnp.stack(kvs[1]), jnp.stack(kvs[2]))
```

```python
import functools

import jax
import jax.numpy as jnp
from jax import lax
from jax.experimental import pallas as pl
from jax.experimental.pallas import tpu as pltpu

F32 = jnp.float32
BF16 = jnp.bfloat16

NORM_EPS = 1e-6
GLA_HEADS = 4
GLA_DK = 128
GLA_DV = 256
GLA_KEY = GLA_HEADS * GLA_DK
GLA_VAL = GLA_HEADS * GLA_DV
GLA_LOWRANK = 16
GLA_TAU = 16.0
GLA_CHUNK = 64
DIL_GROUPS = ((128, 1), (512, 4), (2048, 16))
DIL_HEADS = 4
DIL_HD = 128
DIL_WIDTH = DIL_HEADS * DIL_HD
DIL_SPAN = 128
N_GROUPS = len(DIL_GROUPS)
GLA_COLS = 2 * GLA_KEY + 2 * GLA_VAL
QKV_COLS = 3 * DIL_WIDTH
KV_COLS = 2 * DIL_WIDTH

LANES = 128
LSE_LANES_PER_HEAD = LANES // DIL_HEADS
_LOG2_LSE_LANES = LSE_LANES_PER_HEAD.bit_length() - 1
_LOG2_DIL_HD = DIL_HD.bit_length() - 1
assert 1 << _LOG2_LSE_LANES == LSE_LANES_PER_HEAD and 1 << _LOG2_DIL_HD == DIL_HD
VMEM_LIMIT_BYTES = 56 * 1024 * 1024

_NT = (((1,), (1,)), ((), ()))
_TN = (((0,), (0,)), ((), ()))


def _dot(a, b, dims=None, precision=None):
    if dims is None:
        return jnp.dot(a, b, preferred_element_type=F32, precision=precision)
    return lax.dot_general(a, b, dims, preferred_element_type=F32, precision=precision)


def _rms(xf, g):
    return xf * lax.rsqrt(jnp.mean(xf * xf, axis=-1, keepdims=True) + NORM_EPS) * g


def _sigmoid(x):
    return 1.0 / (1.0 + jnp.exp(-x))


def _const_spec(shape):
    nd = len(shape)
    return pl.BlockSpec(shape, lambda *_: (0,) * nd, pipeline_mode=pl.Buffered(1))


def _params(*sem):
    return pltpu.CompilerParams(dimension_semantics=sem, vmem_limit_bytes=VMEM_LIMIT_BYTES)


def _proj_gla_kernel(x_ref, g_ref, w_ref, wlr_ref, wlr2_ref, blr_ref, qkvg_ref, la_ref, *, col_chunk):
    xn = _rms(x_ref[...], g_ref[...]).astype(BF16)
    for c0 in range(0, GLA_COLS, col_chunk):
        qkvg_ref[:, c0:c0 + col_chunk] = _dot(xn, w_ref[:, c0:c0 + col_chunk]).astype(qkvg_ref.dtype)
    glr = _dot(xn, wlr_ref[...])
    z = _dot(glr.astype(BF16), wlr2_ref[...]) + blr_ref[...]
    log_sig = jnp.minimum(z, 0.0) - jnp.log1p(jnp.exp(-jnp.abs(z)))
    la_ref[...] = log_sig * (1.0 / GLA_TAU)


def _proj_gla(x2d, g_pre, w_a, w_lr, w_lr2, b_lr, *, tm, out_dtype):
    n, d = x2d.shape
    return pl.pallas_call(
        functools.partial(_proj_gla_kernel, col_chunk=512),
        grid=(n // tm,),
        in_specs=[
            pl.BlockSpec((tm, d), lambda i: (i, 0)),
            _const_spec(g_pre.shape),
            _const_spec(w_a.shape),
            _const_spec(w_lr.shape),
            _const_spec(w_lr2.shape),
            _const_spec(b_lr.shape),
        ],
        out_specs=[
            pl.BlockSpec((tm, GLA_COLS), lambda i: (i, 0)),
            pl.BlockSpec((tm, GLA_KEY), lambda i: (i, 0)),
        ],
        out_shape=[
            jax.ShapeDtypeStruct((n, GLA_COLS), out_dtype),
            jax.ShapeDtypeStruct((n, GLA_KEY), F32),
        ],
        compiler_params=_params("parallel"),
        name="proj_gla",
    )(x2d, g_pre, w_a, w_lr, w_lr2, b_lr)


def _proj_rest_kernel(x_ref, g_ref, w_ref, *refs, tm, tiles_per_seq, keeps, prompt):
    a_refs = refs[:N_GROUPS]
    gates_ref = refs[N_GROUPS]
    kv_refs = refs[N_GROUPS + 1:2 * N_GROUPS + 1] if prompt else ()
    stage_refs = refs[2 * N_GROUPS + 1:] if prompt else ()
    xn = _rms(x_ref[...], g_ref[...]).astype(BF16)
    t = pl.program_id(0) % tiles_per_seq if prompt else None
    n_staged = 0
    for g in range(N_GROUPS):
        base = g * QKV_COLS
        dil = DIL_GROUPS[g][1]
        for part in range(3):
            c0 = part * DIL_WIDTH
            r = _dot(xn, w_ref[:, base + c0:base + c0 + DIL_WIDTH])
            if not prompt:
                a_refs[g][:, c0:c0 + DIL_WIDTH] = r
                continue
            if dil == 1:
                a_refs[g][0, :, c0:c0 + DIL_WIDTH] = r.astype(BF16)
            else:
                stage = stage_refs[n_staged % len(stage_refs)]
                n_staged += 1
                for s in range(DIL_WIDTH // LANES):
                    stage[s] = r[:, s * LANES:(s + 1) * LANES]
                for res in range(dil):
                    for s in range(DIL_WIDTH // LANES):
                        a_refs[g][res, :, c0 + s * LANES:c0 + (s + 1) * LANES] = (
                            stage[s, pl.ds(res, tm // dil, stride=dil), :].astype(BF16))
            if part > 0:
                rows = min(keeps[g], tm)
                for h in range(DIL_HEADS):
                    kv_refs[g][pl.ds((part - 1) * DIL_HEADS + h, rows, stride=2 * DIL_HEADS), :] = (
                        r[tm - rows:, h * DIL_HD:(h + 1) * DIL_HD])
    gbase = N_GROUPS * QKV_COLS
    gcols = gates_ref.shape[1]
    for c0 in range(0, gcols, DIL_WIDTH):
        gates_ref[:, c0:c0 + DIL_WIDTH] = _dot(xn, w_ref[:, gbase + c0:gbase + c0 + DIL_WIDTH]).astype(gates_ref.dtype)


def _proj_rest(x2d, g_pre, w_r, *, tm, seq, prompt):
    n, d = x2d.shape
    nb = n // seq
    tiles_per_seq = seq // tm if prompt else None
    gate_cols = w_r.shape[1] - N_GROUPS * QKV_COLS
    keeps = tuple(min(w, seq) for (w, _) in DIL_GROUPS)
    out_specs, out_shape, scratch = [], [], []
    if prompt:
        for _, dil in DIL_GROUPS:
            out_specs.append(pl.BlockSpec((None, dil, tm // dil, QKV_COLS),
                                          lambda i: (i // tiles_per_seq, 0, i % tiles_per_seq, 0)))
            out_shape.append(jax.ShapeDtypeStruct((nb, dil, seq // dil, QKV_COLS), BF16))
    else:
        out_specs += [pl.BlockSpec((tm, QKV_COLS), lambda i: (i, 0)) for _ in range(N_GROUPS)]
        out_shape += [jax.ShapeDtypeStruct((n, QKV_COLS), F32) for _ in range(N_GROUPS)]
    out_specs.append(pl.BlockSpec((tm, gate_cols), lambda i: (i, 0)))
    out_shape.append(jax.ShapeDtypeStruct((n, gate_cols), BF16 if prompt else F32))
    if prompt:
        rows_per_token = 2 * DIL_HEADS
        for keep in keeps:
            if keep >= tm:
                assert keep % tm == 0
                kt = keep // tm

                def idx(i, kt=kt):
                    b, t = i // tiles_per_seq, i % tiles_per_seq
                    return (b * kt + jnp.maximum(t - (tiles_per_seq - kt), 0), 0)

                out_specs.append(pl.BlockSpec((tm * rows_per_token, DIL_HD), idx))
            else:
                out_specs.append(pl.BlockSpec((keep * rows_per_token, DIL_HD), lambda i: (i // tiles_per_seq, 0)))
            out_shape.append(jax.ShapeDtypeStruct((nb * keep * rows_per_token, DIL_HD), F32))
        scratch = [pltpu.VMEM((DIL_WIDTH // LANES, tm, LANES), F32)] * 2
    return pl.pallas_call(
        functools.partial(_proj_rest_kernel, tm=tm, tiles_per_seq=tiles_per_seq, keeps=keeps, prompt=prompt),
        grid=(n // tm,),
        in_specs=[
            pl.BlockSpec((tm, d), lambda i: (i, 0)),
            _const_spec(g_pre.shape),
            _const_spec(w_r.shape),
        ],
        out_specs=out_specs,
        out_shape=out_shape,
        scratch_shapes=scratch,
        compiler_params=_params("arbitrary"),
        name="proj_rest",
    )(x2d, g_pre, w_r)


def _gla_kernel(qkvg_ref, la_ref, s0_ref, gn_ref, ya_ref, sfin_ref, st_ref, *, chunk, nchunk, nseq, mm_dtype, precision):
    j = pl.program_id(1)

    @pl.when(j == 0)
    def _():
        for s in range(nseq):
            for h in range(GLA_HEADS):
                st_ref[s * GLA_HEADS + h] = s0_ref[s, h].T

    row = lax.broadcasted_iota(jnp.int32, (chunk, chunk), 0)
    col = lax.broadcasted_iota(jnp.int32, (chunk, chunk), 1)
    causal = row >= col
    tril = causal.astype(F32)
    qscale = GLA_DK ** -0.5

    segs = [(s, c) for s in range(nseq) for c in range(nchunk)]
    units = [(s, c, h) for s, c in segs for h in range(GLA_HEADS)]

    def rows_of(s, c):
        r0 = (s * nchunk + c) * chunk
        return slice(r0, r0 + chunk)

    def key_cols(h):
        return slice(h * GLA_DK, (h + 1) * GLA_DK)

    def v_of(s, c, h):
        return qkvg_ref[rows_of(s, c), 2 * GLA_KEY + h * GLA_DV:2 * GLA_KEY + (h + 1) * GLA_DV].astype(mm_dtype)

    cum = {sc: _dot(tril, la_ref[rows_of(*sc), :], precision=lax.Precision.HIGHEST) for sc in segs}
    b_end = {sc: cum[sc][chunk - 1:chunk, :] for sc in segs}
    e_b = {sc: jnp.exp(cum[sc]) for sc in segs}
    e_nb = {sc: jnp.exp(-cum[sc]) for sc in segs}
    e_rest = {sc: jnp.exp(b_end[sc] - cum[sc]) for sc in segs}
    e_end = {sc: jnp.exp(b_end[sc]) for sc in segs}
    q_in, k_dec, k_end = {}, {}, {}
    for s, c, h in units:
        q = qkvg_ref[rows_of(s, c), key_cols(h)].astype(F32) * qscale
        k = qkvg_ref[rows_of(s, c), GLA_KEY + h * GLA_DK:GLA_KEY + (h + 1) * GLA_DK].astype(F32)
        q_in[s, c, h] = (q * e_b[s, c][:, key_cols(h)]).astype(mm_dtype)
        k_dec[s, c, h] = (k * e_nb[s, c][:, key_cols(h)]).astype(mm_dtype)
        k_end[s, c, h] = (k * e_rest[s, c][:, key_cols(h)]).astype(mm_dtype)
    att = {u: jnp.where(causal, _dot(q_in[u], k_dec[u], _NT, precision), 0.0).astype(mm_dtype) for u in units}
    o_intra = {u: _dot(att[u], v_of(*u), None, precision) for u in units}
    d_state = {u: _dot(v_of(*u), k_end[u], _TN, precision) for u in units}
    o_inter = {}
    for s in range(nseq):
        for h in range(GLA_HEADS):
            st = st_ref[s * GLA_HEADS + h]
            for c in range(nchunk):
                o_inter[s, c, h] = _dot(q_in[s, c, h], st.astype(mm_dtype), _NT, precision)
                st = st * e_end[s, c][:, key_cols(h)] + d_state[s, c, h]
            st_ref[s * GLA_HEADS + h] = st
    for s, c, h in units:
        vs = slice(h * GLA_DV, (h + 1) * GLA_DV)
        gg = qkvg_ref[rows_of(s, c), 2 * GLA_KEY + GLA_VAL + h * GLA_DV:2 * GLA_KEY + GLA_VAL + (h + 1) * GLA_DV].astype(F32)
        on = _rms(o_intra[s, c, h] + o_inter[s, c, h], gn_ref[:, vs])
        ya_ref[rows_of(s, c), vs] = (on * (gg * _sigmoid(gg))).astype(ya_ref.dtype)

    @pl.when(j == pl.num_programs(1) - 1)
    def _():
        for s in range(nseq):
            for h in range(GLA_HEADS):
                sfin_ref[s, h] = st_ref[s * GLA_HEADS + h].T


def _gla(qkvg, la, s0, gn, *, nb, seq, chunk, rows_per_step, nseq, out_dtype, mm_dtype, precision):
    steps = seq // rows_per_step
    assert nb % nseq == 0 and (nseq == 1 or steps == 1)
    rows = nseq * rows_per_step
    return pl.pallas_call(
        functools.partial(_gla_kernel, chunk=chunk, nchunk=rows_per_step // chunk, nseq=nseq, mm_dtype=mm_dtype,
                          precision=precision),
        grid=(nb // nseq, steps),
        in_specs=[
            pl.BlockSpec((rows, GLA_COLS), lambda b, j: (b * steps + j, 0)),
            pl.BlockSpec((rows, GLA_KEY), lambda b, j: (b * steps + j, 0)),
            pl.BlockSpec((nseq, GLA_HEADS, GLA_DK, GLA_DV), lambda b, j: (b, 0, 0, 0)),
            pl.BlockSpec((1, GLA_VAL), lambda b, j: (0, 0)),
        ],
        out_specs=[
            pl.BlockSpec((rows, GLA_VAL), lambda b, j: (b * steps + j, 0)),
            pl.BlockSpec((nseq, GLA_HEADS, GLA_DK, GLA_DV), lambda b, j: (b, 0, 0, 0)),
        ],
        out_shape=[
            jax.ShapeDtypeStruct((nb * seq, GLA_VAL), out_dtype),
            jax.ShapeDtypeStruct((nb, GLA_HEADS, GLA_DK, GLA_DV), F32),
        ],
        scratch_shapes=[pltpu.VMEM((nseq * GLA_HEADS, GLA_DV, GLA_DK), F32)],
        compiler_params=_params("parallel", "arbitrary"),
        name="gla",
    )(qkvg, la, s0, gn)


def _pack_lse(lses):
    rows = lses[0].shape[0]
    lane_head = lax.broadcasted_iota(jnp.int32, (rows, LANES), 1) >> _LOG2_LSE_LANES
    packed = jnp.broadcast_to(lses[0], (rows, LANES))
    for h in range(1, DIL_HEADS):
        packed = jnp.where(lane_head == h, lses[h], packed)
    return packed


def _dil_prompt_kernel(a_ref, o_ref, lse_ref, kvp_ref, *, nblk):
    first = pl.program_id(2) == 0

    @pl.when(first)
    def _():
        kvp_ref[...] = jnp.zeros_like(kvp_ref)

    i_idx = lax.broadcasted_iota(jnp.int32, (DIL_SPAN, DIL_SPAN), 0)
    j_idx = lax.broadcasted_iota(jnp.int32, (DIL_SPAN, DIL_SPAN), 1)
    mask_cur = j_idx <= i_idx
    mask_prev_full = j_idx >= i_idx
    mask_prev_first = (j_idx - i_idx) >= jnp.where(first, DIL_SPAN, 0)
    scale = DIL_HD ** -0.5
    neg_inf = -jnp.inf
    units = [(jb, h) for jb in range(nblk) for h in range(DIL_HEADS)]

    def kv_prev(jb, h):
        if jb == 0:
            return (kvp_ref[:, h * DIL_HD:(h + 1) * DIL_HD],
                    kvp_ref[:, DIL_WIDTH + h * DIL_HD:DIL_WIDTH + (h + 1) * DIL_HD])
        prows = slice((jb - 1) * DIL_SPAN, jb * DIL_SPAN)
        return (a_ref[prows, DIL_WIDTH + h * DIL_HD:DIL_WIDTH + (h + 1) * DIL_HD],
                a_ref[prows, 2 * DIL_WIDTH + h * DIL_HD:2 * DIL_WIDTH + (h + 1) * DIL_HD])

    scores = []
    for jb, h in units:
        rows = slice(jb * DIL_SPAN, (jb + 1) * DIL_SPAN)
        q = a_ref[rows, h * DIL_HD:(h + 1) * DIL_HD]
        k_cur = a_ref[rows, DIL_WIDTH + h * DIL_HD:DIL_WIDTH + (h + 1) * DIL_HD]
        k_prev, _ = kv_prev(jb, h)
        mask_prev = mask_prev_first if jb == 0 else mask_prev_full
        s_prev = jnp.where(mask_prev, _dot(q, k_prev, _NT) * scale, neg_inf)
        s_cur = jnp.where(mask_cur, _dot(q, k_cur, _NT) * scale, neg_inf)
        scores.append((s_prev, s_cur))
    maxes = [jnp.max(jnp.maximum(sp, sc), axis=-1, keepdims=True) for sp, sc in scores]
    probs = [(jnp.exp(sp - m), jnp.exp(sc - m)) for (sp, sc), m in zip(scores, maxes)]
    sums = [jnp.sum(pp + pc, axis=-1, keepdims=True) for pp, pc in probs]
    lses = {}
    for (jb, h), (pp, pc), m, l in zip(units, probs, maxes, sums):
        rows = slice(jb * DIL_SPAN, (jb + 1) * DIL_SPAN)
        v_cur = a_ref[rows, 2 * DIL_WIDTH + h * DIL_HD:2 * DIL_WIDTH + (h + 1) * DIL_HD]
        _, v_prev = kv_prev(jb, h)
        acc = _dot(pp.astype(BF16), v_prev) + _dot(pc.astype(BF16), v_cur)
        o_ref[rows, h * DIL_HD:(h + 1) * DIL_HD] = acc * (1.0 / l)
        lses[jb, h] = m + jnp.log(l)
    for jb in range(nblk):
        lse_ref[jb * DIL_SPAN:(jb + 1) * DIL_SPAN, :] = _pack_lse([lses[jb, h] for h in range(DIL_HEADS)])
    kvp_ref[...] = a_ref[(nblk - 1) * DIL_SPAN:nblk * DIL_SPAN, DIL_WIDTH:QKV_COLS]


def _dil_prompt(a4, *, max_rows=512):
    nb, dil, length, _ = a4.shape
    rows = min(max_rows, length)

    def spec(cols):
        return pl.BlockSpec((None, None, rows, cols), lambda b, r, n: (b, r, n, 0))

    return pl.pallas_call(
        functools.partial(_dil_prompt_kernel, nblk=rows // DIL_SPAN),
        grid=(nb, dil, length // rows),
        in_specs=[spec(QKV_COLS)],
        out_specs=[spec(DIL_WIDTH), spec(LANES)],
        out_shape=[
            jax.ShapeDtypeStruct((nb, dil, length, DIL_WIDTH), F32),
            jax.ShapeDtypeStruct((nb, dil, length, LANES), F32),
        ],
        scratch_shapes=[pltpu.VMEM((DIL_SPAN, KV_COLS), BF16)],
        compiler_params=_params("parallel", "parallel", "arbitrary"),
        name=f"dil_prompt_d{dil}",
    )(a4)


def _dil_sample_kernel(a_ref, cache_ref, o_ref, lse_ref, new_ref, *, dil, t_new):
    rpt = 2 * DIL_HEADS
    lb = cache_ref.shape[0] // rpt
    tq = lax.broadcasted_iota(jnp.int32, (t_new, lb), 0)
    row = lax.broadcasted_iota(jnp.int32, (t_new, lb), 1)
    valid_c = jnp.logical_and(((row - tq) & (dil - 1)) == 0, row >= tq)
    tq_n = lax.broadcasted_iota(jnp.int32, (t_new, t_new), 0)
    j_n = lax.broadcasted_iota(jnp.int32, (t_new, t_new), 1)
    valid_n = jnp.logical_and(((tq_n - j_n) & (dil - 1)) == 0, j_n <= tq_n)
    scale = DIL_HD ** -0.5
    neg_inf = -jnp.inf
    lses = []
    for h in range(DIL_HEADS):
        hs = slice(h * DIL_HD, (h + 1) * DIL_HD)
        vs = slice(DIL_WIDTH + h * DIL_HD, DIL_WIDTH + (h + 1) * DIL_HD)
        q = a_ref[:, hs].astype(BF16)
        k_new = a_ref[:, DIL_WIDTH + h * DIL_HD:DIL_WIDTH + (h + 1) * DIL_HD].astype(BF16)
        v_new = a_ref[:, 2 * DIL_WIDTH + h * DIL_HD:2 * DIL_WIDTH + (h + 1) * DIL_HD].astype(BF16)
        k_c = cache_ref[pl.ds(h, lb, stride=rpt), :].astype(BF16)
        v_c = cache_ref[pl.ds(DIL_HEADS + h, lb, stride=rpt), :].astype(BF16)
        s_c = jnp.where(valid_c, _dot(q, k_c, _NT) * scale, neg_inf)
        s_n = jnp.where(valid_n, _dot(q, k_new, _NT) * scale, neg_inf)
        m = jnp.maximum(jnp.max(s_c, axis=-1, keepdims=True), jnp.max(s_n, axis=-1, keepdims=True))
        p_c = jnp.exp(s_c - m)
        p_n = jnp.exp(s_n - m)
        l = jnp.sum(p_c, axis=-1, keepdims=True) + jnp.sum(p_n, axis=-1, keepdims=True)
        acc = _dot(p_c.astype(BF16), v_c) + _dot(p_n.astype(BF16), v_new)
        o_ref[:, hs] = acc * (1.0 / l)
        lses.append(m + jnp.log(l))
    lse_ref[...] = _pack_lse(lses)
    kept = (lb - t_new) * rpt
    new_ref[0:kept, :] = cache_ref[t_new * rpt:lb * rpt, :]
    for c in range(rpt):
        new_ref[pl.ds(kept + c, t_new, stride=rpt), :] = a_ref[:, DIL_WIDTH + c * DIL_HD:DIL_WIDTH + (c + 1) * DIL_HD]


def _dil_sample(a2d, cache, *, dil, t_new):
    nb, lb = cache.shape[:2]
    assert lb == dil * DIL_SPAN and a2d.dtype == F32
    rows = lb * 2 * DIL_HEADS
    o, lse, new = pl.pallas_call(
        functools.partial(_dil_sample_kernel, dil=dil, t_new=t_new),
        grid=(nb,),
        in_specs=[
            pl.BlockSpec((t_new, QKV_COLS), lambda b: (b, 0)),
            pl.BlockSpec((None, rows, DIL_HD), lambda b: (b, 0, 0)),
        ],
        out_specs=[
            pl.BlockSpec((t_new, DIL_WIDTH), lambda b: (b, 0)),
            pl.BlockSpec((t_new, LANES), lambda b: (b, 0)),
            pl.BlockSpec((None, rows, DIL_HD), lambda b: (b, 0, 0)),
        ],
        out_shape=[
            jax.ShapeDtypeStruct((nb * t_new, DIL_WIDTH), F32),
            jax.ShapeDtypeStruct((nb * t_new, LANES), F32),
            jax.ShapeDtypeStruct((nb, rows, DIL_HD), F32),
        ],
        compiler_params=_params("parallel"),
        name=f"dil_sample_d{dil}",
    )(a2d, cache.reshape(nb, rows, DIL_HD))
    return o, lse, new.reshape(cache.shape)


def _token_order(ref, scratch_refs):
    if len(ref.shape) == 2:
        return ref[...]
    dil, rows_per_class, _ = ref.shape
    if dil == 1:
        return ref[0]
    scratch = scratch_refs.pop()
    n_slabs = scratch.shape[0]
    for res in range(dil):
        for s in range(n_slabs):
            scratch[s, pl.ds(res, rows_per_class, stride=dil), :] = ref[res, :, s * LANES:(s + 1) * LANES]
    return jnp.concatenate([scratch[s] for s in range(n_slabs)], axis=1) if n_slabs > 1 else scratch[0]


def _final_kernel(ya_ref, o0_ref, o1_ref, o2_ref, l0_ref, l1_ref, l2_ref, gates_ref, x_ref, pe_ref,
                  wa_ref, wb_ref, wo_ref, wpp_ref, wpg_ref, gpost_ref, y_ref, *scratch_refs):
    o_scratch = [s for s in scratch_refs if s.shape[0] == DIL_WIDTH // LANES]
    l_scratch = [s for s in scratch_refs if s.shape[0] == 1]
    lse = [_token_order(r, l_scratch) for r in (l0_ref, l1_ref, l2_ref)]
    o_groups = [_token_order(r, o_scratch) for r in (o0_ref, o1_ref, o2_ref)]
    m = jnp.maximum(jnp.maximum(lse[0], lse[1]), lse[2])
    e = [jnp.exp(x - m) for x in lse]
    inv = 1.0 / (e[0] + e[1] + e[2])
    src = lax.broadcasted_iota(jnp.int32, (LANES, DIL_WIDTH), 0)
    dst_head = lax.broadcasted_iota(jnp.int32, (LANES, DIL_WIDTH), 1) >> _LOG2_DIL_HD
    expand = (src == dst_head * LSE_LANES_PER_HEAD).astype(BF16)
    o_b = None
    for eg, o_g in zip(e, o_groups):
        w = eg * inv
        w_hi = w.astype(BF16)
        w_lo = (w - w_hi.astype(F32)).astype(BF16)
        w_full = _dot(w_hi, expand) + _dot(w_lo, expand)
        term = w_full * o_g
        o_b = term if o_b is None else o_b + term
    dg = gates_ref[:, 0:DIL_WIDTH].astype(F32)
    d = x_ref.shape[1]
    ga = gates_ref[:, DIL_WIDTH:DIL_WIDTH + d].astype(F32)
    gb = gates_ref[:, DIL_WIDTH + d:DIL_WIDTH + 2 * d].astype(F32)
    y_a = _dot(ya_ref[...].astype(BF16), wa_ref[...])
    y_b = _dot((o_b * (dg * _sigmoid(dg))).astype(BF16), wb_ref[...])
    merged = _sigmoid(ga) * y_a + _sigmoid(gb) * y_b
    h = x_ref[...] + _rms(_dot(merged.astype(BF16), wo_ref[...]), gpost_ref[...])
    ple = _dot(pe_ref[...].astype(BF16), wpp_ref[...])
    y_ref[...] = h + ple * _sigmoid(_dot(h.astype(BF16), wpg_ref[...]))


def _final(ya, os_, lses, gates, x2d, pe2d, wa, wb, wo, wpp, wpg, g_post, *, tm):
    n, d = x2d.shape

    def rows(cols):
        return pl.BlockSpec((tm, cols), lambda i: (i, 0))

    scratch = []

    def group_spec(arr):
        if arr.ndim == 2:
            return rows(arr.shape[1])
        _, dil, length, cols = arr.shape
        tiles_per_seq = dil * length // tm
        if dil > 1:
            scratch.append(pltpu.VMEM((cols // LANES, tm, LANES), F32))
        return pl.BlockSpec((None, dil, tm // dil, cols), lambda i: (i // tiles_per_seq, 0, i % tiles_per_seq, 0))

    return pl.pallas_call(
        _final_kernel,
        grid=(n // tm,),
        in_specs=[rows(GLA_VAL)] + [group_spec(a) for a in os_] + [group_spec(a) for a in lses]
        + [rows(gates.shape[1]), rows(d), rows(pe2d.shape[1])]
        + [_const_spec(w.shape) for w in (wa, wb, wo, wpp, wpg, g_post)],
        out_specs=rows(d),
        out_shape=jax.ShapeDtypeStruct((n, d), F32),
        scratch_shapes=scratch,
        compiler_params=_params("parallel"),
        name="final",
    )(ya, *os_, *lses, gates, x2d, pe2d, wa, wb, wo, wpp, wpg, g_post)


def _prep_weights(g_pre, g_post, w_in, w_gla_lr, b_gla_lr, g_gla_norm, w_gla_branch, w_dil_branch, w_out,
                  w_ple_proj, w_ple_gate):
    d = w_in.shape[0]
    c_lr = GLA_COLS
    c_q = c_lr + GLA_LOWRANK
    c_k = c_q + N_GROUPS * DIL_WIDTH
    c_v = c_k + N_GROUPS * DIL_WIDTH
    c_g = c_v + N_GROUPS * DIL_WIDTH
    w_a = w_in[:, :c_lr].astype(BF16)
    w_lr = jnp.pad(w_in[:, c_lr:c_q].astype(BF16), ((0, 0), (0, LANES - GLA_LOWRANK)))
    w_lr2 = jnp.pad(w_gla_lr.astype(BF16), ((0, LANES - GLA_LOWRANK), (0, 0)))
    parts = []
    for g in range(N_GROUPS):
        for c in (c_q, c_k, c_v):
            parts.append(w_in[:, c + g * DIL_WIDTH:c + (g + 1) * DIL_WIDTH])
    parts.append(w_in[:, c_g:])
    w_r = jnp.concatenate(parts, axis=1).astype(BF16)
    return dict(
        g_pre=g_pre.reshape(1, d), g_post=g_post.reshape(1, d), w_a=w_a, w_lr=w_lr, w_lr2=w_lr2, w_r=w_r,
        b_lr=b_gla_lr.reshape(1, GLA_KEY), gn=g_gla_norm.reshape(1, GLA_VAL),
        wa=w_gla_branch.astype(BF16), wb=w_dil_branch.astype(BF16), wo=w_out.astype(BF16),
        wpp=w_ple_proj.astype(BF16), wpg=w_ple_gate.astype(BF16),
    )


def _layer(x, pe, s0, caches, w):
    nb, seq, d = x.shape
    n = nb * seq
    x2d = x.reshape(n, d)
    pe2d = pe.reshape(n, pe.shape[-1])
    prompt = caches is None
    if prompt:
        tm = 512
        assert seq % tm == 0 and seq % (DIL_GROUPS[-1][0]) == 0
        inter = BF16
    else:
        tm = n
        inter = F32
    qkvg, la = _proj_gla(x2d, w["g_pre"], w["w_a"], w["w_lr"], w["w_lr2"], w["b_lr"], tm=tm, out_dtype=inter)
    rest = _proj_rest(x2d, w["g_pre"], w["w_r"], tm=tm, seq=seq, prompt=prompt)
    a_g, gates = rest[:N_GROUPS], rest[N_GROUPS]
    if prompt:
        ya, s_new = _gla(qkvg, la, s0, w["gn"], nb=nb, seq=seq, chunk=GLA_CHUNK, rows_per_step=512, nseq=1,
                         out_dtype=BF16, mm_dtype=BF16, precision=None)
        new_bufs = [kv.reshape(nb, -1, 2, DIL_HEADS, DIL_HD) for kv in rest[N_GROUPS + 1:]]
        os_, lses = [], []
        for g in range(N_GROUPS):
            o_g, lse_g = _dil_prompt(a_g[g])
            os_.append(o_g)
            lses.append(lse_g)
    else:
        chunk = min(GLA_CHUNK, seq)
        assert seq % chunk == 0
        nseq = 4 if nb % 4 == 0 else 1
        ya, s_new = _gla(qkvg, la, s0, w["gn"], nb=nb, seq=seq, chunk=chunk, rows_per_step=seq, nseq=nseq,
                         out_dtype=F32, mm_dtype=F32, precision=lax.Precision.HIGHEST)
        os_, lses, new_bufs = [], [], []
        for g, (_, dil) in enumerate(DIL_GROUPS):
            o_g, lse_g, nc = _dil_sample(a_g[g], caches[g], dil=dil, t_new=seq)
            os_.append(o_g)
            lses.append(lse_g)
            new_bufs.append(nc)
    y = _final(ya, os_, lses, gates, x2d, pe2d, w["wa"], w["wb"], w["wo"], w["wpp"], w["wpg"], w["g_post"], tm=tm)
    return y.reshape(nb, seq, d), s_new, new_bufs


def kernel(x_prompt, x_sample, state_gla, cache_kv_w128, cache_kv_w512, cache_kv_w2048, p_prompt, p_sample, g_pre, g_post, w_in, w_gla_lr, b_gla_lr, g_gla_norm, w_gla_branch, w_dil_branch, w_out, w_ple_proj, w_ple_gate):
    depth = w_in.shape[0]
    hp, hs = x_prompt, x_sample
    gla_p, gla_s = [], []
    kvp = [[] for _ in DIL_GROUPS]
    kvs = [[] for _ in DIL_GROUPS]
    for i in range(depth):
        w = _prep_weights(g_pre[i], g_post[i], w_in[i], w_gla_lr[i], b_gla_lr[i], g_gla_norm[i], w_gla_branch[i],
                          w_dil_branch[i], w_out[i], w_ple_proj[i], w_ple_gate[i])
        s0 = jnp.zeros((x_prompt.shape[0], GLA_HEADS, GLA_DK, GLA_DV), F32)
        hp, sp_new, bp_new = _layer(hp, p_prompt[i], s0, None, w)
        hs, ss_new, bs_new = _layer(hs, p_sample[i], state_gla[i],
                                    (cache_kv_w128[i], cache_kv_w512[i], cache_kv_w2048[i]), w)
        gla_p.append(sp_new)
        gla_s.append(ss_new)
        for g in range(N_GROUPS):
            kvp[g].append(bp_new[g])
            kvs[g].append(bs_new[g])
    return (hp, hs, jnp.stack(gla_p), jnp.stack(gla_s),
            jnp.stack(kvp[0]), jnp.stack(kvp[1]), jnp.stack(kvp[2]),
            jnp.stack(kvs[0]), jnp.stack(kvs[1]), jnp.stack(kvs[2]))
```

```python
import functools

import jax
import jax.numpy as jnp
from jax import lax
from jax.experimental import pallas as pl
from jax.experimental.pallas import tpu as pltpu

F32 = jnp.float32
BF16 = jnp.bfloat16

NORM_EPS = 1e-6
GLA_HEADS = 4
GLA_DK = 128
GLA_DV = 256
GLA_KEY = GLA_HEADS * GLA_DK
GLA_VAL = GLA_HEADS * GLA_DV
GLA_LOWRANK = 16
GLA_TAU = 16.0
GLA_CHUNK = 64
DIL_GROUPS = ((128, 1), (512, 4), (2048, 16))
DIL_HEADS = 4
DIL_HD = 128
DIL_WIDTH = DIL_HEADS * DIL_HD
DIL_SPAN = 128
N_GROUPS = len(DIL_GROUPS)
GLA_COLS = 2 * GLA_KEY + 2 * GLA_VAL
QKV_COLS = 3 * DIL_WIDTH
KV_COLS = 2 * DIL_WIDTH

LANES = 128
LSE_LANES_PER_HEAD = LANES // DIL_HEADS
_LOG2_LSE_LANES = LSE_LANES_PER_HEAD.bit_length() - 1
_LOG2_DIL_HD = DIL_HD.bit_length() - 1
assert 1 << _LOG2_LSE_LANES == LSE_LANES_PER_HEAD and 1 << _LOG2_DIL_HD == DIL_HD
VMEM_LIMIT_BYTES = 56 * 1024 * 1024

_NT = (((1,), (1,)), ((), ()))
_TN = (((0,), (0,)), ((), ()))


def _dot(a, b, dims=None, precision=None):
    if dims is None:
        return jnp.dot(a, b, preferred_element_type=F32, precision=precision)
    return lax.dot_general(a, b, dims, preferred_element_type=F32, precision=precision)


def _rms(xf, g):
    return xf * lax.rsqrt(jnp.mean(xf * xf, axis=-1, keepdims=True) + NORM_EPS) * g


def _sigmoid(x):
    return 1.0 / (1.0 + jnp.exp(-x))


def _const_spec(shape):
    nd = len(shape)
    return pl.BlockSpec(shape, lambda *_: (0,) * nd, pipeline_mode=pl.Buffered(1))


def _params(*sem):
    return pltpu.CompilerParams(dimension_semantics=sem, vmem_limit_bytes=VMEM_LIMIT_BYTES)


def _proj_gla_kernel(x_ref, g_ref, w_ref, wlr_ref, wlr2_ref, blr_ref, qkvg_ref, la_ref, *, col_chunk):
    xn = _rms(x_ref[...], g_ref[...]).astype(BF16)
    for c0 in range(0, GLA_COLS, col_chunk):
        qkvg_ref[:, c0:c0 + col_chunk] = _dot(xn, w_ref[:, c0:c0 + col_chunk]).astype(qkvg_ref.dtype)
    glr = _dot(xn, wlr_ref[...])
    z = _dot(glr.astype(BF16), wlr2_ref[...]) + blr_ref[...]
    log_sig = jnp.minimum(z, 0.0) - jnp.log1p(jnp.exp(-jnp.abs(z)))
    la_ref[...] = log_sig * (1.0 / GLA_TAU)


def _proj_gla(x2d, g_pre, w_a, w_lr, w_lr2, b_lr, *, tm, out_dtype):
    n, d = x2d.shape
    return pl.pallas_call(
        functools.partial(_proj_gla_kernel, col_chunk=512),
        grid=(n // tm,),
        in_specs=[
            pl.BlockSpec((tm, d), lambda i: (i, 0)),
            _const_spec(g_pre.shape),
            _const_spec(w_a.shape),
            _const_spec(w_lr.shape),
            _const_spec(w_lr2.shape),
            _const_spec(b_lr.shape),
        ],
        out_specs=[
            pl.BlockSpec((tm, GLA_COLS), lambda i: (i, 0)),
            pl.BlockSpec((tm, GLA_KEY), lambda i: (i, 0)),
        ],
        out_shape=[
            jax.ShapeDtypeStruct((n, GLA_COLS), out_dtype),
            jax.ShapeDtypeStruct((n, GLA_KEY), F32),
        ],
        compiler_params=_params("parallel"),
        name="proj_gla",
    )(x2d, g_pre, w_a, w_lr, w_lr2, b_lr)


def _proj_rest_kernel(x_ref, g_ref, w_ref, *refs, tm, tiles_per_seq, keeps, prompt):
    a_refs = refs[:N_GROUPS]
    gates_ref = refs[N_GROUPS]
    kv_refs = refs[N_GROUPS + 1:2 * N_GROUPS + 1] if prompt else ()
    stage_refs = refs[2 * N_GROUPS + 1:] if prompt else ()

    def body(with_cache_rows):
        xn = _rms(x_ref[...], g_ref[...]).astype(BF16)
        n_staged = 0
        for g in range(N_GROUPS):
            base = g * QKV_COLS
            dil = DIL_GROUPS[g][1]
            for part in range(3):
                c0 = part * DIL_WIDTH
                r = _dot(xn, w_ref[:, base + c0:base + c0 + DIL_WIDTH])
                if not prompt:
                    a_refs[g][:, c0:c0 + DIL_WIDTH] = r
                    continue
                if dil == 1:
                    a_refs[g][0, :, c0:c0 + DIL_WIDTH] = r.astype(BF16)
                else:
                    stage = stage_refs[n_staged % len(stage_refs)]
                    n_staged += 1
                    for s in range(DIL_WIDTH // LANES):
                        stage[s] = r[:, s * LANES:(s + 1) * LANES]
                    for res in range(dil):
                        for s in range(DIL_WIDTH // LANES):
                            a_refs[g][res, :, c0 + s * LANES:c0 + (s + 1) * LANES] = (
                                stage[s, pl.ds(res, tm // dil, stride=dil), :].astype(BF16))
                if part > 0 and with_cache_rows:
                    rows = min(keeps[g], tm)
                    for h in range(DIL_HEADS):
                        kv_refs[g][pl.ds((part - 1) * DIL_HEADS + h, rows, stride=2 * DIL_HEADS), :] = (
                            r[tm - rows:, h * DIL_HD:(h + 1) * DIL_HD])
        gbase = N_GROUPS * QKV_COLS
        gcols = gates_ref.shape[1]
        for c0 in range(0, gcols, DIL_WIDTH):
            gates_ref[:, c0:c0 + DIL_WIDTH] = _dot(xn, w_ref[:, gbase + c0:gbase + c0 + DIL_WIDTH]).astype(gates_ref.dtype)

    if not prompt:
        body(False)
        return
    t = pl.program_id(0) % tiles_per_seq
    in_kept_range = t >= tiles_per_seq - max(1, max(keeps) // tm)
    pl.when(in_kept_range)(lambda: body(True))
    pl.when(jnp.logical_not(in_kept_range))(lambda: body(False))


def _proj_rest(x2d, g_pre, w_r, *, tm, seq, prompt):
    n, d = x2d.shape
    nb = n // seq
    tiles_per_seq = seq // tm if prompt else None
    gate_cols = w_r.shape[1] - N_GROUPS * QKV_COLS
    keeps = tuple(min(w, seq) for (w, _) in DIL_GROUPS)
    out_specs, out_shape, scratch = [], [], []
    if prompt:
        for _, dil in DIL_GROUPS:
            out_specs.append(pl.BlockSpec((None, dil, tm // dil, QKV_COLS),
                                          lambda i: (i // tiles_per_seq, 0, i % tiles_per_seq, 0)))
            out_shape.append(jax.ShapeDtypeStruct((nb, dil, seq // dil, QKV_COLS), BF16))
    else:
        out_specs += [pl.BlockSpec((tm, QKV_COLS), lambda i: (i, 0)) for _ in range(N_GROUPS)]
        out_shape += [jax.ShapeDtypeStruct((n, QKV_COLS), F32) for _ in range(N_GROUPS)]
    out_specs.append(pl.BlockSpec((tm, gate_cols), lambda i: (i, 0)))
    out_shape.append(jax.ShapeDtypeStruct((n, gate_cols), BF16 if prompt else F32))
    if prompt:
        rows_per_token = 2 * DIL_HEADS
        for keep in keeps:
            if keep >= tm:
                assert keep % tm == 0
                kt = keep // tm

                def idx(i, kt=kt):
                    b, t = i // tiles_per_seq, i % tiles_per_seq
                    return (b * kt + jnp.maximum(t - (tiles_per_seq - kt), 0), 0)

                out_specs.append(pl.BlockSpec((tm * rows_per_token, DIL_HD), idx))
            else:
                out_specs.append(pl.BlockSpec((keep * rows_per_token, DIL_HD), lambda i: (i // tiles_per_seq, 0)))
            out_shape.append(jax.ShapeDtypeStruct((nb * keep * rows_per_token, DIL_HD), F32))
        n_stage = 3 * sum(1 for _, dil in DIL_GROUPS if dil > 1)
        scratch = [pltpu.VMEM((DIL_WIDTH // LANES, tm, LANES), F32)] * n_stage
    return pl.pallas_call(
        functools.partial(_proj_rest_kernel, tm=tm, tiles_per_seq=tiles_per_seq, keeps=keeps, prompt=prompt),
        grid=(n // tm,),
        in_specs=[
            pl.BlockSpec((tm, d), lambda i: (i, 0)),
            _const_spec(g_pre.shape),
            _const_spec(w_r.shape),
        ],
        out_specs=out_specs,
        out_shape=out_shape,
        scratch_shapes=scratch,
        compiler_params=_params("arbitrary"),
        name="proj_rest",
    )(x2d, g_pre, w_r)


def _gla_kernel(qkvg_ref, la_ref, s0_ref, gn_ref, ya_ref, sfin_ref, st_ref, *, chunk, nchunk, nseq, mm_dtype, precision):
    j = pl.program_id(1)

    @pl.when(j == 0)
    def _():
        for s in range(nseq):
            for h in range(GLA_HEADS):
                st_ref[s * GLA_HEADS + h] = s0_ref[s, h].T

    _gla_compute(lambda r, c: qkvg_ref[r, c], lambda r: la_ref[r, :], st_ref, gn_ref, ya_ref,
                 chunk=chunk, nchunk=nchunk, nseq=nseq, mm_dtype=mm_dtype, precision=precision)

    @pl.when(j == pl.num_programs(1) - 1)
    def _():
        for s in range(nseq):
            for h in range(GLA_HEADS):
                sfin_ref[s, h] = st_ref[s * GLA_HEADS + h].T


def _gla_compute(*args, **kwargs):
    for _ in _gla_phases(*args, **kwargs):
        pass


def _interleave(*generators):
    live = list(generators)
    while live:
        for gen in list(live):
            if next(gen, StopIteration) is StopIteration:
                live.remove(gen)


def _gla_phases(load_qkvg, load_la, st_ref, gn_ref, ya_ref, *, chunk, nchunk, nseq, mm_dtype, precision):
    row = lax.broadcasted_iota(jnp.int32, (chunk, chunk), 0)
    col = lax.broadcasted_iota(jnp.int32, (chunk, chunk), 1)
    causal = row >= col
    tril = causal.astype(F32)
    qscale = GLA_DK ** -0.5

    segs = [(s, c) for s in range(nseq) for c in range(nchunk)]
    units = [(s, c, h) for s, c in segs for h in range(GLA_HEADS)]

    def rows_of(s, c):
        r0 = (s * nchunk + c) * chunk
        return slice(r0, r0 + chunk)

    def key_cols(h):
        return slice(h * GLA_DK, (h + 1) * GLA_DK)

    def v_of(s, c, h):
        return load_qkvg(rows_of(s, c), slice(2 * GLA_KEY + h * GLA_DV, 2 * GLA_KEY + (h + 1) * GLA_DV)).astype(mm_dtype)

    def cumsum(a):
        if mm_dtype != BF16:
            return _dot(tril, a, precision=lax.Precision.HIGHEST)
        hi = a.astype(BF16)
        rest = a - hi.astype(F32)
        mid = rest.astype(BF16)
        lo = (rest - mid.astype(F32)).astype(BF16)
        parts = _dot(tril.astype(BF16), jnp.concatenate([hi, mid, lo], axis=1))
        return parts[:, :GLA_KEY] + parts[:, GLA_KEY:2 * GLA_KEY] + parts[:, 2 * GLA_KEY:]

    cum = {sc: cumsum(load_la(rows_of(*sc))) for sc in segs}
    b_end = {sc: cum[sc][chunk - 1:chunk, :] for sc in segs}
    e_b = {sc: jnp.exp(cum[sc]) for sc in segs}
    e_nb = {sc: jnp.exp(-cum[sc]) for sc in segs}
    e_rest = {sc: jnp.exp(b_end[sc] - cum[sc]) for sc in segs}
    e_end = {sc: jnp.exp(b_end[sc]) for sc in segs}
    yield
    q_in, k_dec, k_end = {}, {}, {}
    for s, c, h in units:
        q = load_qkvg(rows_of(s, c), key_cols(h)).astype(F32) * qscale
        k = load_qkvg(rows_of(s, c), slice(GLA_KEY + h * GLA_DK, GLA_KEY + (h + 1) * GLA_DK)).astype(F32)
        q_in[s, c, h] = (q * e_b[s, c][:, key_cols(h)]).astype(mm_dtype)
        k_dec[s, c, h] = (k * e_nb[s, c][:, key_cols(h)]).astype(mm_dtype)
        k_end[s, c, h] = (k * e_rest[s, c][:, key_cols(h)]).astype(mm_dtype)
    yield
    att = {u: jnp.where(causal, _dot(q_in[u], k_dec[u], _NT, precision), 0.0).astype(mm_dtype) for u in units}
    yield
    o_intra = {u: _dot(att[u], v_of(*u), None, precision) for u in units}
    yield
    d_state = {u: _dot(v_of(*u), k_end[u], _TN, precision) for u in units}
    yield
    o_inter = {}
    for s in range(nseq):
        for h in range(GLA_HEADS):
            st = st_ref[s * GLA_HEADS + h]
            for c in range(nchunk):
                o_inter[s, c, h] = _dot(q_in[s, c, h], st.astype(mm_dtype), _NT, precision)
                st = st * e_end[s, c][:, key_cols(h)] + d_state[s, c, h]
            st_ref[s * GLA_HEADS + h] = st
    yield
    for i, (s, c, h) in enumerate(units):
        vs = slice(h * GLA_DV, (h + 1) * GLA_DV)
        gg = load_qkvg(rows_of(s, c), slice(2 * GLA_KEY + GLA_VAL + h * GLA_DV,
                                            2 * GLA_KEY + GLA_VAL + (h + 1) * GLA_DV)).astype(F32)
        on = _rms(o_intra[s, c, h] + o_inter[s, c, h], gn_ref[:, vs])
        ya_ref[rows_of(s, c), vs] = (on * (gg * _sigmoid(gg))).astype(ya_ref.dtype)
        if i % 8 == 7:
            yield


def _proj_gla_fused_kernel(x_ref, g_ref, w_ref, wlr_ref, wlr2_ref, blr_ref, s0_ref, gn_ref, ya_ref, sfin_ref,
                           qkvg_s0, qkvg_s1, la_s0, la_s1, st_ref, *, steps, chunk, col_chunk):
    j = pl.program_id(0)
    prev = j - 1

    @pl.when(j == 0)
    def _():
        qkvg_s1[...] = jnp.zeros_like(qkvg_s1)
        la_s1[...] = jnp.zeros_like(la_s1)

    @pl.when(jnp.logical_or(j == 0, prev % steps == 0))
    def _():
        for h in range(GLA_HEADS):
            st_ref[h] = s0_ref[0, h].T

    def project(qkvg_w, la_w):
        xn = _rms(x_ref[...], g_ref[...]).astype(BF16)
        yield
        for c0 in range(0, GLA_COLS, col_chunk):
            qkvg_w[:, c0:c0 + col_chunk] = _dot(xn, w_ref[:, c0:c0 + col_chunk]).astype(qkvg_w.dtype)
            yield
        glr = _dot(xn, wlr_ref[...])
        z = _dot(glr.astype(BF16), wlr2_ref[...]) + blr_ref[...]
        log_sig = jnp.minimum(z, 0.0) - jnp.log1p(jnp.exp(-jnp.abs(z)))
        la_w[...] = log_sig * (1.0 / GLA_TAU)

    def step(qkvg_w, la_w, qkvg_r, la_r):
        _interleave(
            project(qkvg_w, la_w),
            _gla_phases(lambda r, c: qkvg_r[r, c], lambda r: la_r[r, :], st_ref, gn_ref, ya_ref,
                        chunk=chunk, nchunk=x_ref.shape[0] // chunk, nseq=1, mm_dtype=BF16, precision=None))

    @pl.when(j % 2 == 0)
    def _():
        step(qkvg_s0, la_s0, qkvg_s1, la_s1)

    @pl.when(j % 2 == 1)
    def _():
        step(qkvg_s1, la_s1, qkvg_s0, la_s0)

    @pl.when(jnp.logical_and(j >= 1, prev % steps == steps - 1))
    def _():
        for h in range(GLA_HEADS):
            sfin_ref[0, h] = st_ref[h].T


def _proj_gla_fused(x2d, g_pre, w_a, w_lr, w_lr2, b_lr, s0, gn, *, tm, seq, chunk):
    n, d = x2d.shape
    nb = n // seq
    steps = seq // tm
    n_tiles = n // tm

    def prev_tile(j):
        return jnp.maximum(j - 1, 0)

    return pl.pallas_call(
        functools.partial(_proj_gla_fused_kernel, steps=steps, chunk=chunk, col_chunk=512),
        grid=(n_tiles + 1,),
        in_specs=[
            pl.BlockSpec((tm, d), lambda j: (jnp.minimum(j, n_tiles - 1), 0)),
            _const_spec(g_pre.shape),
            _const_spec(w_a.shape),
            _const_spec(w_lr.shape),
            _const_spec(w_lr2.shape),
            _const_spec(b_lr.shape),
            pl.BlockSpec((1, GLA_HEADS, GLA_DK, GLA_DV), lambda j: (prev_tile(j) // steps, 0, 0, 0)),
            _const_spec(gn.shape),
        ],
        out_specs=[
            pl.BlockSpec((tm, GLA_VAL), lambda j: (prev_tile(j), 0)),
            pl.BlockSpec((1, GLA_HEADS, GLA_DK, GLA_DV), lambda j: (prev_tile(j) // steps, 0, 0, 0)),
        ],
        out_shape=[
            jax.ShapeDtypeStruct((n, GLA_VAL), BF16),
            jax.ShapeDtypeStruct((nb, GLA_HEADS, GLA_DK, GLA_DV), F32),
        ],
        scratch_shapes=[
            pltpu.VMEM((tm, GLA_COLS), BF16),
            pltpu.VMEM((tm, GLA_COLS), BF16),
            pltpu.VMEM((tm, GLA_KEY), F32),
            pltpu.VMEM((tm, GLA_KEY), F32),
            pltpu.VMEM((GLA_HEADS, GLA_DV, GLA_DK), F32),
        ],
        compiler_params=_params("arbitrary"),
        name="proj_gla_fused",
    )(x2d, g_pre, w_a, w_lr, w_lr2, b_lr, s0, gn)


def _gla(qkvg, la, s0, gn, *, nb, seq, chunk, rows_per_step, nseq, out_dtype, mm_dtype, precision):
    steps = seq // rows_per_step
    assert nb % nseq == 0 and (nseq == 1 or steps == 1)
    rows = nseq * rows_per_step
    return pl.pallas_call(
        functools.partial(_gla_kernel, chunk=chunk, nchunk=rows_per_step // chunk, nseq=nseq, mm_dtype=mm_dtype,
                          precision=precision),
        grid=(nb // nseq, steps),
        in_specs=[
            pl.BlockSpec((rows, GLA_COLS), lambda b, j: (b * steps + j, 0)),
            pl.BlockSpec((rows, GLA_KEY), lambda b, j: (b * steps + j, 0)),
            pl.BlockSpec((nseq, GLA_HEADS, GLA_DK, GLA_DV), lambda b, j: (b, 0, 0, 0)),
            pl.BlockSpec((1, GLA_VAL), lambda b, j: (0, 0)),
        ],
        out_specs=[
            pl.BlockSpec((rows, GLA_VAL), lambda b, j: (b * steps + j, 0)),
            pl.BlockSpec((nseq, GLA_HEADS, GLA_DK, GLA_DV), lambda b, j: (b, 0, 0, 0)),
        ],
        out_shape=[
            jax.ShapeDtypeStruct((nb * seq, GLA_VAL), out_dtype),
            jax.ShapeDtypeStruct((nb, GLA_HEADS, GLA_DK, GLA_DV), F32),
        ],
        scratch_shapes=[pltpu.VMEM((nseq * GLA_HEADS, GLA_DV, GLA_DK), F32)],
        compiler_params=_params("parallel", "arbitrary"),
        name="gla",
    )(qkvg, la, s0, gn)


def _pack_lse(lses):
    rows = lses[0].shape[0]
    lane_head = lax.broadcasted_iota(jnp.int32, (rows, LANES), 1) >> _LOG2_LSE_LANES
    packed = jnp.broadcast_to(lses[0], (rows, LANES))
    for h in range(1, DIL_HEADS):
        packed = jnp.where(lane_head == h, lses[h], packed)
    return packed


def _dil_prompt_kernel(a_ref, o_ref, lse_ref, kvp_ref, *, nres, nblk):
    first = pl.program_id(2) == 0

    @pl.when(first)
    def _():
        kvp_ref[...] = jnp.zeros_like(kvp_ref)

    i_idx = lax.broadcasted_iota(jnp.int32, (DIL_SPAN, 2 * DIL_SPAN), 0)
    j_idx = lax.broadcasted_iota(jnp.int32, (DIL_SPAN, 2 * DIL_SPAN), 1)
    diff = j_idx - i_idx
    band = jnp.logical_and(diff >= 0, diff <= DIL_SPAN)
    band_first = jnp.logical_and(diff >= jnp.where(first, DIL_SPAN - i_idx, 0), diff <= DIL_SPAN)
    scale = DIL_HD ** -0.5
    neg_inf = -jnp.inf
    units = [(res, jb, h) for res in range(nres) for jb in range(nblk) for h in range(DIL_HEADS)]

    def window(res, jb, c0):
        if jb == 0:
            prev = kvp_ref[res, :, c0 - DIL_WIDTH:c0 - DIL_WIDTH + DIL_HD]
            return jnp.concatenate([prev, a_ref[res, 0:DIL_SPAN, c0:c0 + DIL_HD]], axis=0)
        return a_ref[res, (jb - 1) * DIL_SPAN:(jb + 1) * DIL_SPAN, c0:c0 + DIL_HD]

    scores = []
    for res, jb, h in units:
        q = a_ref[res, jb * DIL_SPAN:(jb + 1) * DIL_SPAN, h * DIL_HD:(h + 1) * DIL_HD]
        s = _dot(q, window(res, jb, DIL_WIDTH + h * DIL_HD), _NT) * scale
        scores.append(jnp.where(band_first if jb == 0 else band, s, neg_inf))
    maxes = [jnp.max(s, axis=-1, keepdims=True) for s in scores]
    probs = [jnp.exp(s - m) for s, m in zip(scores, maxes)]
    sums = [jnp.sum(p, axis=-1, keepdims=True) for p in probs]
    lses = {}
    for (res, jb, h), p, m, l in zip(units, probs, maxes, sums):
        acc = _dot(p.astype(BF16), window(res, jb, 2 * DIL_WIDTH + h * DIL_HD))
        o_ref[res, jb * DIL_SPAN:(jb + 1) * DIL_SPAN, h * DIL_HD:(h + 1) * DIL_HD] = acc * (1.0 / l)
        lses[res, jb, h] = m + jnp.log(l)
    for res in range(nres):
        for jb in range(nblk):
            lse_ref[res, jb * DIL_SPAN:(jb + 1) * DIL_SPAN, :] = _pack_lse([lses[res, jb, h] for h in range(DIL_HEADS)])
        kvp_ref[res] = a_ref[res, (nblk - 1) * DIL_SPAN:nblk * DIL_SPAN, DIL_WIDTH:QKV_COLS]


def _dil_prompt(a4, *, rows_per_step=1024):
    nb, dil, length, _ = a4.shape
    rows = min(rows_per_step, length)
    nres = min(dil, rows_per_step // rows)

    def spec(cols):
        return pl.BlockSpec((None, nres, rows, cols), lambda b, r, n: (b, r, n, 0))

    return pl.pallas_call(
        functools.partial(_dil_prompt_kernel, nres=nres, nblk=rows // DIL_SPAN),
        grid=(nb, dil // nres, length // rows),
        in_specs=[spec(QKV_COLS)],
        out_specs=[spec(DIL_WIDTH), spec(LANES)],
        out_shape=[
            jax.ShapeDtypeStruct((nb, dil, length, DIL_WIDTH), F32),
            jax.ShapeDtypeStruct((nb, dil, length, LANES), F32),
        ],
        scratch_shapes=[pltpu.VMEM((nres, DIL_SPAN, KV_COLS), BF16)],
        compiler_params=_params("parallel", "parallel", "arbitrary"),
        name=f"dil_prompt_d{dil}",
    )(a4)


def _dil_sample_kernel(a_ref, cache_ref, o_ref, lse_ref, new_ref, *, dil, t_new):
    rpt = 2 * DIL_HEADS
    lb = cache_ref.shape[0] // rpt
    tq = lax.broadcasted_iota(jnp.int32, (t_new, lb), 0)
    row = lax.broadcasted_iota(jnp.int32, (t_new, lb), 1)
    valid_c = jnp.logical_and(((row - tq) & (dil - 1)) == 0, row >= tq)
    tq_n = lax.broadcasted_iota(jnp.int32, (t_new, t_new), 0)
    j_n = lax.broadcasted_iota(jnp.int32, (t_new, t_new), 1)
    valid_n = jnp.logical_and(((tq_n - j_n) & (dil - 1)) == 0, j_n <= tq_n)
    scale = DIL_HD ** -0.5
    neg_inf = -jnp.inf
    lses = []
    for h in range(DIL_HEADS):
        hs = slice(h * DIL_HD, (h + 1) * DIL_HD)
        vs = slice(DIL_WIDTH + h * DIL_HD, DIL_WIDTH + (h + 1) * DIL_HD)
        q = a_ref[:, hs].astype(BF16)
        k_new = a_ref[:, DIL_WIDTH + h * DIL_HD:DIL_WIDTH + (h + 1) * DIL_HD].astype(BF16)
        v_new = a_ref[:, 2 * DIL_WIDTH + h * DIL_HD:2 * DIL_WIDTH + (h + 1) * DIL_HD].astype(BF16)
        k_c = cache_ref[pl.ds(h, lb, stride=rpt), :].astype(BF16)
        v_c = cache_ref[pl.ds(DIL_HEADS + h, lb, stride=rpt), :].astype(BF16)
        s_c = jnp.where(valid_c, _dot(q, k_c, _NT) * scale, neg_inf)
        s_n = jnp.where(valid_n, _dot(q, k_new, _NT) * scale, neg_inf)
        m = jnp.maximum(jnp.max(s_c, axis=-1, keepdims=True), jnp.max(s_n, axis=-1, keepdims=True))
        p_c = jnp.exp(s_c - m)
        p_n = jnp.exp(s_n - m)
        l = jnp.sum(p_c, axis=-1, keepdims=True) + jnp.sum(p_n, axis=-1, keepdims=True)
        acc = _dot(p_c.astype(BF16), v_c) + _dot(p_n.astype(BF16), v_new)
        o_ref[:, hs] = acc * (1.0 / l)
        lses.append(m + jnp.log(l))
    lse_ref[...] = _pack_lse(lses)
    kept = (lb - t_new) * rpt
    new_ref[0:kept, :] = cache_ref[t_new * rpt:lb * rpt, :]
    for c in range(rpt):
        new_ref[pl.ds(kept + c, t_new, stride=rpt), :] = a_ref[:, DIL_WIDTH + c * DIL_HD:DIL_WIDTH + (c + 1) * DIL_HD]


def _dil_sample(a2d, cache, *, dil, t_new):
    nb, lb = cache.shape[:2]
    assert lb == dil * DIL_SPAN and a2d.dtype == F32
    rows = lb * 2 * DIL_HEADS
    o, lse, new = pl.pallas_call(
        functools.partial(_dil_sample_kernel, dil=dil, t_new=t_new),
        grid=(nb,),
        in_specs=[
            pl.BlockSpec((t_new, QKV_COLS), lambda b: (b, 0)),
            pl.BlockSpec((None, rows, DIL_HD), lambda b: (b, 0, 0)),
        ],
        out_specs=[
            pl.BlockSpec((t_new, DIL_WIDTH), lambda b: (b, 0)),
            pl.BlockSpec((t_new, LANES), lambda b: (b, 0)),
            pl.BlockSpec((None, rows, DIL_HD), lambda b: (b, 0, 0)),
        ],
        out_shape=[
            jax.ShapeDtypeStruct((nb * t_new, DIL_WIDTH), F32),
            jax.ShapeDtypeStruct((nb * t_new, LANES), F32),
            jax.ShapeDtypeStruct((nb, rows, DIL_HD), F32),
        ],
        compiler_params=_params("parallel"),
        name=f"dil_sample_d{dil}",
    )(a2d, cache.reshape(nb, rows, DIL_HD))
    return o, lse, new.reshape(cache.shape)


def _token_order(ref, scratch_refs):
    if len(ref.shape) == 2:
        return ref[...]
    dil, rows_per_class, _ = ref.shape
    if dil == 1:
        return ref[0]
    scratch = scratch_refs.pop()
    n_slabs = scratch.shape[0]
    for res in range(dil):
        for s in range(n_slabs):
            scratch[s, pl.ds(res, rows_per_class, stride=dil), :] = ref[res, :, s * LANES:(s + 1) * LANES]
    return jnp.concatenate([scratch[s] for s in range(n_slabs)], axis=1) if n_slabs > 1 else scratch[0]


def _final_kernel(ya_ref, o0_ref, o1_ref, o2_ref, l0_ref, l1_ref, l2_ref, gates_ref, x_ref, pe_ref,
                  wa_ref, wb_ref, wo_ref, wpp_ref, wpg_ref, gpost_ref, y_ref, *scratch_refs):
    o_scratch = [s for s in scratch_refs if s.shape[0] == DIL_WIDTH // LANES]
    l_scratch = [s for s in scratch_refs if s.shape[0] == 1]
    lse = [_token_order(r, l_scratch) for r in (l0_ref, l1_ref, l2_ref)]
    o_groups = [_token_order(r, o_scratch) for r in (o0_ref, o1_ref, o2_ref)]
    m = jnp.maximum(jnp.maximum(lse[0], lse[1]), lse[2])
    e = [jnp.exp(x - m) for x in lse]
    inv = 1.0 / (e[0] + e[1] + e[2])
    src = lax.broadcasted_iota(jnp.int32, (LANES, DIL_WIDTH), 0)
    dst_head = lax.broadcasted_iota(jnp.int32, (LANES, DIL_WIDTH), 1) >> _LOG2_DIL_HD
    expand = (src == dst_head * LSE_LANES_PER_HEAD).astype(BF16)
    o_b = None
    for eg, o_g in zip(e, o_groups):
        w = eg * inv
        w_hi = w.astype(BF16)
        w_lo = (w - w_hi.astype(F32)).astype(BF16)
        w_full = _dot(w_hi, expand) + _dot(w_lo, expand)
        term = w_full * o_g
        o_b = term if o_b is None else o_b + term
    dg = gates_ref[:, 0:DIL_WIDTH].astype(F32)
    d = x_ref.shape[1]
    ga = gates_ref[:, DIL_WIDTH:DIL_WIDTH + d].astype(F32)
    gb = gates_ref[:, DIL_WIDTH + d:DIL_WIDTH + 2 * d].astype(F32)
    y_a = _dot(ya_ref[...].astype(BF16), wa_ref[...])
    y_b = _dot((o_b * (dg * _sigmoid(dg))).astype(BF16), wb_ref[...])
    merged = _sigmoid(ga) * y_a + _sigmoid(gb) * y_b
    h = x_ref[...] + _rms(_dot(merged.astype(BF16), wo_ref[...]), gpost_ref[...])
    ple = _dot(pe_ref[...].astype(BF16), wpp_ref[...])
    y_ref[...] = h + ple * _sigmoid(_dot(h.astype(BF16), wpg_ref[...]))


def _final(ya, os_, lses, gates, x2d, pe2d, wa, wb, wo, wpp, wpg, g_post, *, tm):
    n, d = x2d.shape

    def rows(cols):
        return pl.BlockSpec((tm, cols), lambda i: (i, 0))

    scratch = []

    def group_spec(arr):
        if arr.ndim == 2:
            return rows(arr.shape[1])
        _, dil, length, cols = arr.shape
        tiles_per_seq = dil * length // tm
        if dil > 1:
            scratch.append(pltpu.VMEM((cols // LANES, tm, LANES), F32))
        return pl.BlockSpec((None, dil, tm // dil, cols), lambda i: (i // tiles_per_seq, 0, i % tiles_per_seq, 0))

    return pl.pallas_call(
        _final_kernel,
        grid=(n // tm,),
        in_specs=[rows(GLA_VAL)] + [group_spec(a) for a in os_] + [group_spec(a) for a in lses]
        + [rows(gates.shape[1]), rows(d), rows(pe2d.shape[1])]
        + [_const_spec(w.shape) for w in (wa, wb, wo, wpp, wpg, g_post)],
        out_specs=rows(d),
        out_shape=jax.ShapeDtypeStruct((n, d), F32),
        scratch_shapes=scratch,
        compiler_params=_params("parallel"),
        name="final",
    )(ya, *os_, *lses, gates, x2d, pe2d, wa, wb, wo, wpp, wpg, g_post)


def _prep_weights(g_pre, g_post, w_in, w_gla_lr, b_gla_lr, g_gla_norm, w_gla_branch, w_dil_branch, w_out,
                  w_ple_proj, w_ple_gate):
    d = w_in.shape[0]
    c_lr = GLA_COLS
    c_q = c_lr + GLA_LOWRANK
    c_k = c_q + N_GROUPS * DIL_WIDTH
    c_v = c_k + N_GROUPS * DIL_WIDTH
    c_g = c_v + N_GROUPS * DIL_WIDTH
    w_a = w_in[:, :c_lr].astype(BF16)
    w_lr = jnp.pad(w_in[:, c_lr:c_q].astype(BF16), ((0, 0), (0, LANES - GLA_LOWRANK)))
    w_lr2 = jnp.pad(w_gla_lr.astype(BF16), ((0, LANES - GLA_LOWRANK), (0, 0)))
    parts = []
    for g in range(N_GROUPS):
        for c in (c_q, c_k, c_v):
            parts.append(w_in[:, c + g * DIL_WIDTH:c + (g + 1) * DIL_WIDTH])
    parts.append(w_in[:, c_g:])
    w_r = jnp.concatenate(parts, axis=1).astype(BF16)
    return dict(
        g_pre=g_pre.reshape(1, d), g_post=g_post.reshape(1, d), w_a=w_a, w_lr=w_lr, w_lr2=w_lr2, w_r=w_r,
        b_lr=b_gla_lr.reshape(1, GLA_KEY), gn=g_gla_norm.reshape(1, GLA_VAL),
        wa=w_gla_branch.astype(BF16), wb=w_dil_branch.astype(BF16), wo=w_out.astype(BF16),
        wpp=w_ple_proj.astype(BF16), wpg=w_ple_gate.astype(BF16),
    )


def _layer(x, pe, s0, caches, w):
    nb, seq, d = x.shape
    n = nb * seq
    x2d = x.reshape(n, d)
    pe2d = pe.reshape(n, pe.shape[-1])
    prompt = caches is None
    if prompt:
        tm = 512
        assert seq % tm == 0 and seq % (DIL_GROUPS[-1][0]) == 0
    else:
        tm = n
    rest = _proj_rest(x2d, w["g_pre"], w["w_r"], tm=tm, seq=seq, prompt=prompt)
    a_g, gates = rest[:N_GROUPS], rest[N_GROUPS]
    if prompt:
        ya, s_new = _proj_gla_fused(x2d, w["g_pre"], w["w_a"], w["w_lr"], w["w_lr2"], w["b_lr"], s0, w["gn"],
                                    tm=tm, seq=seq, chunk=GLA_CHUNK)
        new_bufs = [kv.reshape(nb, -1, 2, DIL_HEADS, DIL_HD) for kv in rest[N_GROUPS + 1:]]
        os_, lses = [], []
        for g in range(N_GROUPS):
            o_g, lse_g = _dil_prompt(a_g[g])
            os_.append(o_g)
            lses.append(lse_g)
    else:
        chunk = min(GLA_CHUNK, seq)
        assert seq % chunk == 0
        nseq = 4 if nb % 4 == 0 else 1
        qkvg, la = _proj_gla(x2d, w["g_pre"], w["w_a"], w["w_lr"], w["w_lr2"], w["b_lr"], tm=tm, out_dtype=F32)
        ya, s_new = _gla(qkvg, la, s0, w["gn"], nb=nb, seq=seq, chunk=chunk, rows_per_step=seq, nseq=nseq,
                         out_dtype=F32, mm_dtype=F32, precision=lax.Precision.HIGHEST)
        os_, lses, new_bufs = [], [], []
        for g, (_, dil) in enumerate(DIL_GROUPS):
            o_g, lse_g, nc = _dil_sample(a_g[g], caches[g], dil=dil, t_new=seq)
            os_.append(o_g)
            lses.append(lse_g)
            new_bufs.append(nc)
    y = _final(ya, os_, lses, gates, x2d, pe2d, w["wa"], w["wb"], w["wo"], w["wpp"], w["wpg"], w["g_post"], tm=tm)
    return y.reshape(nb, seq, d), s_new, new_bufs


def kernel(x_prompt, x_sample, state_gla, cache_kv_w128, cache_kv_w512, cache_kv_w2048, p_prompt, p_sample, g_pre, g_post, w_in, w_gla_lr, b_gla_lr, g_gla_norm, w_gla_branch, w_dil_branch, w_out, w_ple_proj, w_ple_gate):
    depth = w_in.shape[0]
    hp, hs = x_prompt, x_sample
    gla_p, gla_s = [], []
    kvp = [[] for _ in DIL_GROUPS]
    kvs = [[] for _ in DIL_GROUPS]
    for i in range(depth):
        w = _prep_weights(g_pre[i], g_post[i], w_in[i], w_gla_lr[i], b_gla_lr[i], g_gla_norm[i], w_gla_branch[i],
                          w_dil_branch[i], w_out[i], w_ple_proj[i], w_ple_gate[i])
        s0 = jnp.zeros((x_prompt.shape[0], GLA_HEADS, GLA_DK, GLA_DV), F32)
        hp, sp_new, bp_new = _layer(hp, p_prompt[i], s0, None, w)
        hs, ss_new, bs_new = _layer(hs, p_sample[i], state_gla[i],
                                    (cache_kv_w128[i], cache_kv_w512[i], cache_kv_w2048[i]), w)
        gla_p.append(sp_new)
        gla_s.append(ss_new)
        for g in range(N_GROUPS):
            kvp[g].append(bp_new[g])
            kvs[g].append(bs_new[g])
    return (hp, hs, jnp.stack(gla_p), jnp.stack(gla_s),
            jnp.stack(kvp[0]), jnp.stack(kvp[1]), jnp.stack(kvp[2]),
            jnp.stack(kvs[0]), jnp.stack(kvs[1]), jnp.stack(kvs[2]))
```

```python
import functools

import jax
import jax.numpy as jnp
from jax import lax
from jax.experimental import pallas as pl
from jax.experimental.pallas import tpu as pltpu

F32 = jnp.float32
BF16 = jnp.bfloat16

NORM_EPS = 1e-6
GLA_HEADS = 4
GLA_DK = 128
GLA_DV = 256
GLA_KEY = GLA_HEADS * GLA_DK
GLA_VAL = GLA_HEADS * GLA_DV
GLA_LOWRANK = 16
GLA_TAU = 16.0
GLA_CHUNK = 64
DIL_GROUPS = ((128, 1), (512, 4), (2048, 16))
DIL_HEADS = 4
DIL_HD = 128
DIL_WIDTH = DIL_HEADS * DIL_HD
DIL_SPAN = 128
N_GROUPS = len(DIL_GROUPS)
GLA_COLS = 2 * GLA_KEY + 2 * GLA_VAL
QKV_COLS = 3 * DIL_WIDTH
KV_COLS = 2 * DIL_WIDTH

LANES = 128
LSE_LANES_PER_HEAD = LANES // DIL_HEADS
_LOG2_LSE_LANES = LSE_LANES_PER_HEAD.bit_length() - 1
_LOG2_DIL_HD = DIL_HD.bit_length() - 1
assert 1 << _LOG2_LSE_LANES == LSE_LANES_PER_HEAD and 1 << _LOG2_DIL_HD == DIL_HD
VMEM_LIMIT_BYTES = 56 * 1024 * 1024

_NT = (((1,), (1,)), ((), ()))
_TN = (((0,), (0,)), ((), ()))


def _dot(a, b, dims=None, precision=None):
    if dims is None:
        return jnp.dot(a, b, preferred_element_type=F32, precision=precision)
    return lax.dot_general(a, b, dims, preferred_element_type=F32, precision=precision)


def _rms(xf, g):
    return xf * lax.rsqrt(jnp.mean(xf * xf, axis=-1, keepdims=True) + NORM_EPS) * g


def _sigmoid(x):
    return 1.0 / (1.0 + jnp.exp(-x))


def _const_spec(shape):
    nd = len(shape)
    return pl.BlockSpec(shape, lambda *_: (0,) * nd, pipeline_mode=pl.Buffered(1))


def _params(*sem):
    return pltpu.CompilerParams(dimension_semantics=sem, vmem_limit_bytes=VMEM_LIMIT_BYTES)


def _proj_gla_kernel(x_ref, g_ref, w_ref, wlr_ref, wlr2_ref, blr_ref, qkvg_ref, la_ref, *, col_chunk):
    xn = _rms(x_ref[...], g_ref[...]).astype(BF16)
    for c0 in range(0, GLA_COLS, col_chunk):
        qkvg_ref[:, c0:c0 + col_chunk] = _dot(xn, w_ref[:, c0:c0 + col_chunk]).astype(qkvg_ref.dtype)
    glr = _dot(xn, wlr_ref[...])
    z = _dot(glr.astype(BF16), wlr2_ref[...]) + blr_ref[...]
    log_sig = jnp.minimum(z, 0.0) - jnp.log1p(jnp.exp(-jnp.abs(z)))
    la_ref[...] = log_sig * (1.0 / GLA_TAU)


def _proj_gla(x2d, g_pre, w_a, w_lr, w_lr2, b_lr, *, tm, out_dtype):
    n, d = x2d.shape
    return pl.pallas_call(
        functools.partial(_proj_gla_kernel, col_chunk=512),
        grid=(n // tm,),
        in_specs=[
            pl.BlockSpec((tm, d), lambda i: (i, 0)),
            _const_spec(g_pre.shape),
            _const_spec(w_a.shape),
            _const_spec(w_lr.shape),
            _const_spec(w_lr2.shape),
            _const_spec(b_lr.shape),
        ],
        out_specs=[
            pl.BlockSpec((tm, GLA_COLS), lambda i: (i, 0)),
            pl.BlockSpec((tm, GLA_KEY), lambda i: (i, 0)),
        ],
        out_shape=[
            jax.ShapeDtypeStruct((n, GLA_COLS), out_dtype),
            jax.ShapeDtypeStruct((n, GLA_KEY), F32),
        ],
        compiler_params=_params("parallel"),
        name="proj_gla",
    )(x2d, g_pre, w_a, w_lr, w_lr2, b_lr)


def _proj_rest_kernel(x_ref, g_ref, w_ref, *refs, tm, tiles_per_seq, keeps, prompt):
    a_refs = refs[:N_GROUPS]
    gates_ref = refs[N_GROUPS]
    kv_refs = refs[N_GROUPS + 1:2 * N_GROUPS + 1]
    stage_refs = refs[2 * N_GROUPS + 1:]

    def body(with_cache_rows):
        xn = _rms(x_ref[...], g_ref[...]).astype(BF16)
        n_staged = 0
        for g in range(N_GROUPS):
            base = g * QKV_COLS
            dil = DIL_GROUPS[g][1]
            for part in range(3):
                c0 = part * DIL_WIDTH
                r = _dot(xn, w_ref[:, base + c0:base + c0 + DIL_WIDTH])
                if not prompt:
                    a_refs[g][:, c0:c0 + DIL_WIDTH] = r
                elif dil == 1:
                    a_refs[g][0, :, c0:c0 + DIL_WIDTH] = r.astype(BF16)
                else:
                    stage = stage_refs[n_staged % len(stage_refs)]
                    n_staged += 1
                    for s in range(DIL_WIDTH // LANES):
                        stage[s] = r[:, s * LANES:(s + 1) * LANES]
                    for res in range(dil):
                        for s in range(DIL_WIDTH // LANES):
                            a_refs[g][res, :, c0 + s * LANES:c0 + (s + 1) * LANES] = (
                                stage[s, pl.ds(res, tm // dil, stride=dil), :].astype(BF16))
                if part > 0 and with_cache_rows:
                    rows = min(keeps[g], tm)
                    for h in range(DIL_HEADS):
                        kv_refs[g][pl.ds((part - 1) * DIL_HEADS + h, rows, stride=2 * DIL_HEADS), :] = (
                            r[tm - rows:, h * DIL_HD:(h + 1) * DIL_HD])
        gbase = N_GROUPS * QKV_COLS
        gcols = gates_ref.shape[1]
        for c0 in range(0, gcols, DIL_WIDTH):
            gates_ref[:, c0:c0 + DIL_WIDTH] = _dot(xn, w_ref[:, gbase + c0:gbase + c0 + DIL_WIDTH]).astype(gates_ref.dtype)

    if not prompt:
        body(True)
        return
    t = pl.program_id(0) % tiles_per_seq
    in_kept_range = t >= tiles_per_seq - max(1, max(keeps) // tm)
    pl.when(in_kept_range)(lambda: body(True))
    pl.when(jnp.logical_not(in_kept_range))(lambda: body(False))


def _proj_rest(x2d, g_pre, w_r, *, tm, seq, prompt):
    n, d = x2d.shape
    nb = n // seq
    tiles_per_seq = seq // tm if prompt else None
    gate_cols = w_r.shape[1] - N_GROUPS * QKV_COLS
    keeps = tuple(min(w, seq) for (w, _) in DIL_GROUPS) if prompt else (tm,) * N_GROUPS
    rows_per_token = 2 * DIL_HEADS
    out_specs, out_shape, scratch = [], [], []
    if prompt:
        for _, dil in DIL_GROUPS:
            out_specs.append(pl.BlockSpec((None, dil, tm // dil, QKV_COLS),
                                          lambda i: (i // tiles_per_seq, 0, i % tiles_per_seq, 0)))
            out_shape.append(jax.ShapeDtypeStruct((nb, dil, seq // dil, QKV_COLS), BF16))
    else:
        out_specs += [pl.BlockSpec((tm, QKV_COLS), lambda i: (i, 0)) for _ in range(N_GROUPS)]
        out_shape += [jax.ShapeDtypeStruct((n, QKV_COLS), F32) for _ in range(N_GROUPS)]
    out_specs.append(pl.BlockSpec((tm, gate_cols), lambda i: (i, 0)))
    out_shape.append(jax.ShapeDtypeStruct((n, gate_cols), BF16 if prompt else F32))
    if not prompt:
        for _ in range(N_GROUPS):
            out_specs.append(pl.BlockSpec((tm * rows_per_token, DIL_HD), lambda i: (i, 0)))
            out_shape.append(jax.ShapeDtypeStruct((n * rows_per_token, DIL_HD), F32))
    else:
        for keep in keeps:
            if keep >= tm:
                assert keep % tm == 0
                kt = keep // tm

                def idx(i, kt=kt):
                    b, t = i // tiles_per_seq, i % tiles_per_seq
                    return (b * kt + jnp.maximum(t - (tiles_per_seq - kt), 0), 0)

                out_specs.append(pl.BlockSpec((tm * rows_per_token, DIL_HD), idx))
            else:
                out_specs.append(pl.BlockSpec((keep * rows_per_token, DIL_HD), lambda i: (i // tiles_per_seq, 0)))
            out_shape.append(jax.ShapeDtypeStruct((nb * keep * rows_per_token, DIL_HD), F32))
        n_stage = 3 * sum(1 for _, dil in DIL_GROUPS if dil > 1)
        scratch = [pltpu.VMEM((DIL_WIDTH // LANES, tm, LANES), F32)] * n_stage
    return pl.pallas_call(
        functools.partial(_proj_rest_kernel, tm=tm, tiles_per_seq=tiles_per_seq, keeps=keeps, prompt=prompt),
        grid=(n // tm,),
        in_specs=[
            pl.BlockSpec((tm, d), lambda i: (i, 0)),
            _const_spec(g_pre.shape),
            _const_spec(w_r.shape),
        ],
        out_specs=out_specs,
        out_shape=out_shape,
        scratch_shapes=scratch,
        compiler_params=_params("arbitrary"),
        name="proj_rest",
    )(x2d, g_pre, w_r)


def _gla_kernel(qkvg_ref, la_ref, s0_ref, gn_ref, ya_ref, sfin_ref, st_ref, *, chunk, nchunk, nseq, mm_dtype, precision):
    j = pl.program_id(1)

    @pl.when(j == 0)
    def _():
        for s in range(nseq):
            for h in range(GLA_HEADS):
                st_ref[s * GLA_HEADS + h] = s0_ref[s, h].T

    _gla_compute(lambda r, c: qkvg_ref[r, c], lambda r: la_ref[r, :], st_ref, gn_ref, ya_ref,
                 chunk=chunk, nchunk=nchunk, nseq=nseq, mm_dtype=mm_dtype, precision=precision)

    @pl.when(j == pl.num_programs(1) - 1)
    def _():
        for s in range(nseq):
            for h in range(GLA_HEADS):
                sfin_ref[s, h] = st_ref[s * GLA_HEADS + h].T


def _gla_compute(*args, **kwargs):
    for _ in _gla_phases(*args, **kwargs):
        pass


def _interleave(*generators):
    live = list(generators)
    while live:
        for gen in list(live):
            if next(gen, StopIteration) is StopIteration:
                live.remove(gen)


def _gla_phases(load_qkvg, load_la, st_ref, gn_ref, ya_ref, *, chunk, nchunk, nseq, mm_dtype, precision):
    row = lax.broadcasted_iota(jnp.int32, (chunk, chunk), 0)
    col = lax.broadcasted_iota(jnp.int32, (chunk, chunk), 1)
    causal = row >= col
    tril = causal.astype(F32)
    qscale = GLA_DK ** -0.5

    segs = [(s, c) for s in range(nseq) for c in range(nchunk)]
    units = [(s, c, h) for s, c in segs for h in range(GLA_HEADS)]

    def rows_of(s, c):
        r0 = (s * nchunk + c) * chunk
        return slice(r0, r0 + chunk)

    def key_cols(h):
        return slice(h * GLA_DK, (h + 1) * GLA_DK)

    def v_of(s, c, h):
        return load_qkvg(rows_of(s, c), slice(2 * GLA_KEY + h * GLA_DV, 2 * GLA_KEY + (h + 1) * GLA_DV)).astype(mm_dtype)

    def cumsum(a):
        if mm_dtype != BF16:
            return _dot(tril, a, precision=lax.Precision.HIGHEST)
        hi = a.astype(BF16)
        rest = a - hi.astype(F32)
        mid = rest.astype(BF16)
        lo = (rest - mid.astype(F32)).astype(BF16)
        parts = _dot(tril.astype(BF16), jnp.concatenate([hi, mid, lo], axis=1))
        return parts[:, :GLA_KEY] + parts[:, GLA_KEY:2 * GLA_KEY] + parts[:, 2 * GLA_KEY:]

    cum = {sc: cumsum(load_la(rows_of(*sc))) for sc in segs}
    b_end = {sc: cum[sc][chunk - 1:chunk, :] for sc in segs}
    e_b = {sc: jnp.exp(cum[sc]) for sc in segs}
    e_nb = {sc: jnp.exp(-cum[sc]) for sc in segs}
    e_rest = {sc: jnp.exp(b_end[sc] - cum[sc]) for sc in segs}
    e_end = {sc: jnp.exp(b_end[sc]) for sc in segs}
    yield
    q_in, k_dec, k_end = {}, {}, {}
    for s, c, h in units:
        q = load_qkvg(rows_of(s, c), key_cols(h)).astype(F32) * qscale
        k = load_qkvg(rows_of(s, c), slice(GLA_KEY + h * GLA_DK, GLA_KEY + (h + 1) * GLA_DK)).astype(F32)
        q_in[s, c, h] = (q * e_b[s, c][:, key_cols(h)]).astype(mm_dtype)
        k_dec[s, c, h] = (k * e_nb[s, c][:, key_cols(h)]).astype(mm_dtype)
        k_end[s, c, h] = (k * e_rest[s, c][:, key_cols(h)]).astype(mm_dtype)
    yield
    att = {u: jnp.where(causal, _dot(q_in[u], k_dec[u], _NT, precision), 0.0).astype(mm_dtype) for u in units}
    yield
    o_intra = {u: _dot(att[u], v_of(*u), None, precision) for u in units}
    yield
    d_state = {u: _dot(v_of(*u), k_end[u], _TN, precision) for u in units}
    yield
    o_inter = {}
    for s in range(nseq):
        for h in range(GLA_HEADS):
            st = st_ref[s * GLA_HEADS + h]
            for c in range(nchunk):
                o_inter[s, c, h] = _dot(q_in[s, c, h], st.astype(mm_dtype), _NT, precision)
                st = st * e_end[s, c][:, key_cols(h)] + d_state[s, c, h]
            st_ref[s * GLA_HEADS + h] = st
    yield
    for i, (s, c, h) in enumerate(units):
        vs = slice(h * GLA_DV, (h + 1) * GLA_DV)
        gg = load_qkvg(rows_of(s, c), slice(2 * GLA_KEY + GLA_VAL + h * GLA_DV,
                                            2 * GLA_KEY + GLA_VAL + (h + 1) * GLA_DV)).astype(F32)
        on = _rms(o_intra[s, c, h] + o_inter[s, c, h], gn_ref[:, vs])
        ya_ref[rows_of(s, c), vs] = (on * (gg * _sigmoid(gg))).astype(ya_ref.dtype)
        if i % 8 == 7:
            yield


def _proj_gla_fused_kernel(x_ref, g_ref, w_ref, wlr_ref, wlr2_ref, blr_ref, s0_ref, gn_ref, *refs,
                           steps, chunk, col_chunk, shifts):
    n_shift = len(shifts)
    old_refs = refs[:n_shift]
    tail_refs = refs[n_shift:2 * n_shift]
    ya_ref, sfin_ref = refs[2 * n_shift:2 * n_shift + 2]
    new_refs = refs[2 * n_shift + 2:3 * n_shift + 2]
    qkvg_s0, qkvg_s1, la_s0, la_s1, st_ref = refs[3 * n_shift + 2:3 * n_shift + 7]
    j = pl.program_id(0)
    prev = j - 1

    if n_shift:
        copy_sem = refs[3 * n_shift + 7]
        pieces = shifts[0][2]
        n_chunks = old_refs[0].shape[0] * pieces

        def chunk_copies(c):
            entry, piece = c // pieces, c % pieces
            out = []
            for i, (skip, kept, _) in enumerate(shifts):
                rows, tail = kept // pieces, skip // pieces
                out.append(pltpu.make_async_copy(
                    old_refs[i].at[entry, pl.ds(pl.multiple_of(skip + piece * rows, 8), rows), :],
                    new_refs[i].at[entry, pl.ds(pl.multiple_of(piece * rows, 8), rows), :],
                    copy_sem.at[2 * i]))
                out.append(pltpu.make_async_copy(
                    tail_refs[i].at[entry, pl.ds(pl.multiple_of(piece * tail, 8), tail), :],
                    new_refs[i].at[entry, pl.ds(pl.multiple_of(kept + piece * tail, 8), tail), :],
                    copy_sem.at[2 * i + 1]))
            return out

        @pl.when(jnp.logical_and(j >= 1, j <= n_chunks))
        def _():
            for cp in chunk_copies(j - 1):
                cp.wait()

        @pl.when(j < n_chunks)
        def _():
            for cp in chunk_copies(j):
                cp.start()

    @pl.when(j == 0)
    def _():
        qkvg_s1[...] = jnp.zeros_like(qkvg_s1)
        la_s1[...] = jnp.zeros_like(la_s1)

    @pl.when(jnp.logical_or(j == 0, prev % steps == 0))
    def _():
        for h in range(GLA_HEADS):
            st_ref[h] = s0_ref[0, h].T

    def project(qkvg_w, la_w):
        xn = _rms(x_ref[...], g_ref[...]).astype(BF16)
        yield
        for c0 in range(0, GLA_COLS, col_chunk):
            qkvg_w[:, c0:c0 + col_chunk] = _dot(xn, w_ref[:, c0:c0 + col_chunk]).astype(qkvg_w.dtype)
            yield
        glr = _dot(xn, wlr_ref[...])
        z = _dot(glr.astype(BF16), wlr2_ref[...]) + blr_ref[...]
        log_sig = jnp.minimum(z, 0.0) - jnp.log1p(jnp.exp(-jnp.abs(z)))
        la_w[...] = log_sig * (1.0 / GLA_TAU)

    def step(qkvg_w, la_w, qkvg_r, la_r):
        _interleave(
            project(qkvg_w, la_w),
            _gla_phases(lambda r, c: qkvg_r[r, c], lambda r: la_r[r, :], st_ref, gn_ref, ya_ref,
                        chunk=chunk, nchunk=x_ref.shape[0] // chunk, nseq=1, mm_dtype=BF16, precision=None))

    @pl.when(j % 2 == 0)
    def _():
        step(qkvg_s0, la_s0, qkvg_s1, la_s1)

    @pl.when(j % 2 == 1)
    def _():
        step(qkvg_s1, la_s1, qkvg_s0, la_s0)

    @pl.when(jnp.logical_and(j >= 1, prev % steps == steps - 1))
    def _():
        for h in range(GLA_HEADS):
            sfin_ref[0, h] = st_ref[h].T


def _proj_gla_fused(x2d, g_pre, w_a, w_lr, w_lr2, b_lr, s0, gn, *, tm, seq, chunk, shift_src=(), shift_tail=()):
    n, d = x2d.shape
    nb = n // seq
    steps = seq // tm
    n_tiles = n // tm
    shifts = []
    if shift_src:
        entries = shift_src[0].shape[0]
        pieces = max(1, n_tiles // entries)
        assert entries * pieces <= n_tiles
        for src, tail in zip(shift_src, shift_tail):
            skip = tail.shape[1]
            kept = src.shape[1] - skip
            assert src.shape[0] == tail.shape[0] == entries and src.shape[2] == tail.shape[2] == LANES
            assert kept % (8 * pieces) == 0 and skip % (8 * pieces) == 0
            shifts.append((skip, kept, pieces))

    def prev_tile(j):
        return jnp.maximum(j - 1, 0)

    any_spec = pl.BlockSpec(memory_space=pl.ANY)
    outs = pl.pallas_call(
        functools.partial(_proj_gla_fused_kernel, steps=steps, chunk=chunk, col_chunk=512, shifts=tuple(shifts)),
        grid=(n_tiles + 1,),
        in_specs=[
            pl.BlockSpec((tm, d), lambda j: (jnp.minimum(j, n_tiles - 1), 0)),
            _const_spec(g_pre.shape),
            _const_spec(w_a.shape),
            _const_spec(w_lr.shape),
            _const_spec(w_lr2.shape),
            _const_spec(b_lr.shape),
            pl.BlockSpec((1, GLA_HEADS, GLA_DK, GLA_DV), lambda j: (prev_tile(j) // steps, 0, 0, 0)),
            _const_spec(gn.shape),
        ] + [any_spec] * (2 * len(shifts)),
        out_specs=[
            pl.BlockSpec((tm, GLA_VAL), lambda j: (prev_tile(j), 0)),
            pl.BlockSpec((1, GLA_HEADS, GLA_DK, GLA_DV), lambda j: (prev_tile(j) // steps, 0, 0, 0)),
        ] + [any_spec] * len(shifts),
        out_shape=[
            jax.ShapeDtypeStruct((n, GLA_VAL), BF16),
            jax.ShapeDtypeStruct((nb, GLA_HEADS, GLA_DK, GLA_DV), F32),
        ] + [jax.ShapeDtypeStruct(src.shape, src.dtype) for src in shift_src],
        scratch_shapes=[
            pltpu.VMEM((tm, GLA_COLS), BF16),
            pltpu.VMEM((tm, GLA_COLS), BF16),
            pltpu.VMEM((tm, GLA_KEY), F32),
            pltpu.VMEM((tm, GLA_KEY), F32),
            pltpu.VMEM((GLA_HEADS, GLA_DV, GLA_DK), F32),
        ] + ([pltpu.SemaphoreType.DMA((2 * len(shifts),))] if shifts else []),
        compiler_params=_params("arbitrary"),
        name="proj_gla_fused",
    )(x2d, g_pre, w_a, w_lr, w_lr2, b_lr, s0, gn, *shift_src, *shift_tail)
    return outs[0], outs[1], list(outs[2:])


def _gla(qkvg, la, s0, gn, *, nb, seq, chunk, rows_per_step, nseq, out_dtype, mm_dtype, precision):
    steps = seq // rows_per_step
    assert nb % nseq == 0 and (nseq == 1 or steps == 1)
    rows = nseq * rows_per_step
    return pl.pallas_call(
        functools.partial(_gla_kernel, chunk=chunk, nchunk=rows_per_step // chunk, nseq=nseq, mm_dtype=mm_dtype,
                          precision=precision),
        grid=(nb // nseq, steps),
        in_specs=[
            pl.BlockSpec((rows, GLA_COLS), lambda b, j: (b * steps + j, 0)),
            pl.BlockSpec((rows, GLA_KEY), lambda b, j: (b * steps + j, 0)),
            pl.BlockSpec((nseq, GLA_HEADS, GLA_DK, GLA_DV), lambda b, j: (b, 0, 0, 0)),
            pl.BlockSpec((1, GLA_VAL), lambda b, j: (0, 0)),
        ],
        out_specs=[
            pl.BlockSpec((rows, GLA_VAL), lambda b, j: (b * steps + j, 0)),
            pl.BlockSpec((nseq, GLA_HEADS, GLA_DK, GLA_DV), lambda b, j: (b, 0, 0, 0)),
        ],
        out_shape=[
            jax.ShapeDtypeStruct((nb * seq, GLA_VAL), out_dtype),
            jax.ShapeDtypeStruct((nb, GLA_HEADS, GLA_DK, GLA_DV), F32),
        ],
        scratch_shapes=[pltpu.VMEM((nseq * GLA_HEADS, GLA_DV, GLA_DK), F32)],
        compiler_params=_params("parallel", "arbitrary"),
        name="gla",
    )(qkvg, la, s0, gn)


def _pack_lse(lses):
    rows = lses[0].shape[0]
    lane_head = lax.broadcasted_iota(jnp.int32, (rows, LANES), 1) >> _LOG2_LSE_LANES
    packed = jnp.broadcast_to(lses[0], (rows, LANES))
    for h in range(1, DIL_HEADS):
        packed = jnp.where(lane_head == h, lses[h], packed)
    return packed


def _dil_prompt_kernel(a_ref, o_ref, lse_ref, kvp_ref, *, nres, nblk):
    first = pl.program_id(2) == 0

    @pl.when(first)
    def _():
        kvp_ref[...] = jnp.zeros_like(kvp_ref)

    i_idx = lax.broadcasted_iota(jnp.int32, (DIL_SPAN, 2 * DIL_SPAN), 0)
    j_idx = lax.broadcasted_iota(jnp.int32, (DIL_SPAN, 2 * DIL_SPAN), 1)
    diff = j_idx - i_idx
    band = jnp.logical_and(diff >= 0, diff <= DIL_SPAN)
    band_first = jnp.logical_and(diff >= jnp.where(first, DIL_SPAN - i_idx, 0), diff <= DIL_SPAN)
    scale = DIL_HD ** -0.5
    neg_inf = -jnp.inf
    units = [(res, jb, h) for res in range(nres) for jb in range(nblk) for h in range(DIL_HEADS)]

    def window(res, jb, c0):
        if jb == 0:
            prev = kvp_ref[res, :, c0 - DIL_WIDTH:c0 - DIL_WIDTH + DIL_HD]
            return jnp.concatenate([prev, a_ref[res, 0:DIL_SPAN, c0:c0 + DIL_HD]], axis=0)
        return a_ref[res, (jb - 1) * DIL_SPAN:(jb + 1) * DIL_SPAN, c0:c0 + DIL_HD]

    scores = []
    for res, jb, h in units:
        q = a_ref[res, jb * DIL_SPAN:(jb + 1) * DIL_SPAN, h * DIL_HD:(h + 1) * DIL_HD]
        s = _dot(q, window(res, jb, DIL_WIDTH + h * DIL_HD), _NT) * scale
        scores.append(jnp.where(band_first if jb == 0 else band, s, neg_inf))
    maxes = [jnp.max(s, axis=-1, keepdims=True) for s in scores]
    probs = [jnp.exp(s - m) for s, m in zip(scores, maxes)]
    sums = [jnp.sum(p, axis=-1, keepdims=True) for p in probs]
    lses = {}
    for (res, jb, h), p, m, l in zip(units, probs, maxes, sums):
        acc = _dot(p.astype(BF16), window(res, jb, 2 * DIL_WIDTH + h * DIL_HD))
        o_ref[res, jb * DIL_SPAN:(jb + 1) * DIL_SPAN, h * DIL_HD:(h + 1) * DIL_HD] = acc * (1.0 / l)
        lses[res, jb, h] = m + jnp.log(l)
    for res in range(nres):
        for jb in range(nblk):
            lse_ref[res, jb * DIL_SPAN:(jb + 1) * DIL_SPAN, :] = _pack_lse([lses[res, jb, h] for h in range(DIL_HEADS)])
        kvp_ref[res] = a_ref[res, (nblk - 1) * DIL_SPAN:nblk * DIL_SPAN, DIL_WIDTH:QKV_COLS]


def _dil_prompt(a4, *, rows_per_step=1024):
    nb, dil, length, _ = a4.shape
    rows = min(rows_per_step, length)
    nres = min(dil, rows_per_step // rows)

    def spec(cols):
        return pl.BlockSpec((None, nres, rows, cols), lambda b, r, n: (b, r, n, 0))

    return pl.pallas_call(
        functools.partial(_dil_prompt_kernel, nres=nres, nblk=rows // DIL_SPAN),
        grid=(nb, dil // nres, length // rows),
        in_specs=[spec(QKV_COLS)],
        out_specs=[spec(DIL_WIDTH), spec(LANES)],
        out_shape=[
            jax.ShapeDtypeStruct((nb, dil, length, DIL_WIDTH), F32),
            jax.ShapeDtypeStruct((nb, dil, length, LANES), F32),
        ],
        scratch_shapes=[pltpu.VMEM((nres, DIL_SPAN, KV_COLS), BF16)],
        compiler_params=_params("parallel", "parallel", "arbitrary"),
        name=f"dil_prompt_d{dil}",
    )(a4)


def _dil_sample_kernel(a_ref, cache_ref, o_ref, lse_ref, *, dil, t_new):
    rpt = 2 * DIL_HEADS
    lb = cache_ref.shape[0] // rpt
    tq = lax.broadcasted_iota(jnp.int32, (t_new, lb), 0)
    row = lax.broadcasted_iota(jnp.int32, (t_new, lb), 1)
    valid_c = jnp.logical_and(((row - tq) & (dil - 1)) == 0, row >= tq)
    tq_n = lax.broadcasted_iota(jnp.int32, (t_new, t_new), 0)
    j_n = lax.broadcasted_iota(jnp.int32, (t_new, t_new), 1)
    valid_n = jnp.logical_and(((tq_n - j_n) & (dil - 1)) == 0, j_n <= tq_n)
    scale = DIL_HD ** -0.5
    neg_inf = -jnp.inf
    lses = []
    for h in range(DIL_HEADS):
        hs = slice(h * DIL_HD, (h + 1) * DIL_HD)
        vs = slice(DIL_WIDTH + h * DIL_HD, DIL_WIDTH + (h + 1) * DIL_HD)
        q = a_ref[:, hs].astype(BF16)
        k_new = a_ref[:, DIL_WIDTH + h * DIL_HD:DIL_WIDTH + (h + 1) * DIL_HD].astype(BF16)
        v_new = a_ref[:, 2 * DIL_WIDTH + h * DIL_HD:2 * DIL_WIDTH + (h + 1) * DIL_HD].astype(BF16)
        k_c = cache_ref[pl.ds(h, lb, stride=rpt), :].astype(BF16)
        v_c = cache_ref[pl.ds(DIL_HEADS + h, lb, stride=rpt), :].astype(BF16)
        s_c = jnp.where(valid_c, _dot(q, k_c, _NT) * scale, neg_inf)
        s_n = jnp.where(valid_n, _dot(q, k_new, _NT) * scale, neg_inf)
        m = jnp.maximum(jnp.max(s_c, axis=-1, keepdims=True), jnp.max(s_n, axis=-1, keepdims=True))
        p_c = jnp.exp(s_c - m)
        p_n = jnp.exp(s_n - m)
        l = jnp.sum(p_c, axis=-1, keepdims=True) + jnp.sum(p_n, axis=-1, keepdims=True)
        acc = _dot(p_c.astype(BF16), v_c) + _dot(p_n.astype(BF16), v_new)
        o_ref[:, hs] = acc * (1.0 / l)
        lses.append(m + jnp.log(l))
    lse_ref[...] = _pack_lse(lses)


def _cache_rows(cache):
    nb, lb = cache.shape[:2]
    return cache.reshape(nb, lb * 2 * DIL_HEADS, DIL_HD)


def _dil_sample(a2d, cache, *, dil, t_new):
    nb, lb = cache.shape[:2]
    assert lb == dil * DIL_SPAN and a2d.dtype == F32
    rows = lb * 2 * DIL_HEADS
    return pl.pallas_call(
        functools.partial(_dil_sample_kernel, dil=dil, t_new=t_new),
        grid=(nb,),
        in_specs=[
            pl.BlockSpec((t_new, QKV_COLS), lambda b: (b, 0)),
            pl.BlockSpec((None, rows, DIL_HD), lambda b: (b, 0, 0)),
        ],
        out_specs=[
            pl.BlockSpec((t_new, DIL_WIDTH), lambda b: (b, 0)),
            pl.BlockSpec((t_new, LANES), lambda b: (b, 0)),
        ],
        out_shape=[
            jax.ShapeDtypeStruct((nb * t_new, DIL_WIDTH), F32),
            jax.ShapeDtypeStruct((nb * t_new, LANES), F32),
        ],
        compiler_params=_params("parallel"),
        name=f"dil_sample_d{dil}",
    )(a2d, _cache_rows(cache))


def _token_order(ref, scratch_refs):
    if len(ref.shape) == 2:
        return ref[...]
    dil, rows_per_class, _ = ref.shape
    if dil == 1:
        return ref[0]
    scratch = scratch_refs.pop()
    n_slabs = scratch.shape[0]
    for res in range(dil):
        for s in range(n_slabs):
            scratch[s, pl.ds(res, rows_per_class, stride=dil), :] = ref[res, :, s * LANES:(s + 1) * LANES]
    return jnp.concatenate([scratch[s] for s in range(n_slabs)], axis=1) if n_slabs > 1 else scratch[0]


def _final_kernel(ya_ref, o0_ref, o1_ref, o2_ref, l0_ref, l1_ref, l2_ref, gates_ref, x_ref, pe_ref,
                  wa_ref, wb_ref, wo_ref, wpp_ref, wpg_ref, gpost_ref, y_ref, *scratch_refs):
    o_scratch = [s for s in scratch_refs if s.shape[0] == DIL_WIDTH // LANES]
    l_scratch = [s for s in scratch_refs if s.shape[0] == 1]
    lse = [_token_order(r, l_scratch) for r in (l0_ref, l1_ref, l2_ref)]
    o_groups = [_token_order(r, o_scratch) for r in (o0_ref, o1_ref, o2_ref)]
    m = jnp.maximum(jnp.maximum(lse[0], lse[1]), lse[2])
    e = [jnp.exp(x - m) for x in lse]
    inv = 1.0 / (e[0] + e[1] + e[2])
    src = lax.broadcasted_iota(jnp.int32, (LANES, DIL_WIDTH), 0)
    dst_head = lax.broadcasted_iota(jnp.int32, (LANES, DIL_WIDTH), 1) >> _LOG2_DIL_HD
    expand = (src == dst_head * LSE_LANES_PER_HEAD).astype(BF16)
    o_b = None
    for eg, o_g in zip(e, o_groups):
        w = eg * inv
        w_hi = w.astype(BF16)
        w_lo = (w - w_hi.astype(F32)).astype(BF16)
        w_full = _dot(w_hi, expand) + _dot(w_lo, expand)
        term = w_full * o_g
        o_b = term if o_b is None else o_b + term
    dg = gates_ref[:, 0:DIL_WIDTH].astype(F32)
    d = x_ref.shape[1]
    ga = gates_ref[:, DIL_WIDTH:DIL_WIDTH + d].astype(F32)
    gb = gates_ref[:, DIL_WIDTH + d:DIL_WIDTH + 2 * d].astype(F32)
    y_a = _dot(ya_ref[...].astype(BF16), wa_ref[...])
    y_b = _dot((o_b * (dg * _sigmoid(dg))).astype(BF16), wb_ref[...])
    merged = _sigmoid(ga) * y_a + _sigmoid(gb) * y_b
    h = x_ref[...] + _rms(_dot(merged.astype(BF16), wo_ref[...]), gpost_ref[...])
    ple = _dot(pe_ref[...].astype(BF16), wpp_ref[...])
    y_ref[...] = h + ple * _sigmoid(_dot(h.astype(BF16), wpg_ref[...]))


def _final(ya, os_, lses, gates, x2d, pe2d, wa, wb, wo, wpp, wpg, g_post, *, tm):
    n, d = x2d.shape

    def rows(cols):
        return pl.BlockSpec((tm, cols), lambda i: (i, 0))

    scratch = []

    def group_spec(arr):
        if arr.ndim == 2:
            return rows(arr.shape[1])
        _, dil, length, cols = arr.shape
        tiles_per_seq = dil * length // tm
        if dil > 1:
            scratch.append(pltpu.VMEM((cols // LANES, tm, LANES), F32))
        return pl.BlockSpec((None, dil, tm // dil, cols), lambda i: (i // tiles_per_seq, 0, i % tiles_per_seq, 0))

    return pl.pallas_call(
        _final_kernel,
        grid=(n // tm,),
        in_specs=[rows(GLA_VAL)] + [group_spec(a) for a in os_] + [group_spec(a) for a in lses]
        + [rows(gates.shape[1]), rows(d), rows(pe2d.shape[1])]
        + [_const_spec(w.shape) for w in (wa, wb, wo, wpp, wpg, g_post)],
        out_specs=rows(d),
        out_shape=jax.ShapeDtypeStruct((n, d), F32),
        scratch_shapes=scratch,
        compiler_params=_params("parallel"),
        name="final",
    )(ya, *os_, *lses, gates, x2d, pe2d, wa, wb, wo, wpp, wpg, g_post)


def _prep_weights(g_pre, g_post, w_in, w_gla_lr, b_gla_lr, g_gla_norm, w_gla_branch, w_dil_branch, w_out,
                  w_ple_proj, w_ple_gate):
    d = w_in.shape[0]
    c_lr = GLA_COLS
    c_q = c_lr + GLA_LOWRANK
    c_k = c_q + N_GROUPS * DIL_WIDTH
    c_v = c_k + N_GROUPS * DIL_WIDTH
    c_g = c_v + N_GROUPS * DIL_WIDTH
    w_a = w_in[:, :c_lr].astype(BF16)
    w_lr = jnp.pad(w_in[:, c_lr:c_q].astype(BF16), ((0, 0), (0, LANES - GLA_LOWRANK)))
    w_lr2 = jnp.pad(w_gla_lr.astype(BF16), ((0, LANES - GLA_LOWRANK), (0, 0)))
    parts = []
    for g in range(N_GROUPS):
        for c in (c_q, c_k, c_v):
            parts.append(w_in[:, c + g * DIL_WIDTH:c + (g + 1) * DIL_WIDTH])
    parts.append(w_in[:, c_g:])
    w_r = jnp.concatenate(parts, axis=1).astype(BF16)
    return dict(
        g_pre=g_pre.reshape(1, d), g_post=g_post.reshape(1, d), w_a=w_a, w_lr=w_lr, w_lr2=w_lr2, w_r=w_r,
        b_lr=b_gla_lr.reshape(1, GLA_KEY), gn=g_gla_norm.reshape(1, GLA_VAL),
        wa=w_gla_branch.astype(BF16), wb=w_dil_branch.astype(BF16), wo=w_out.astype(BF16),
        wpp=w_ple_proj.astype(BF16), wpg=w_ple_gate.astype(BF16),
    )


PROMPT_TILE = 512


def _prompt_layer(x, pe, s0, w, sample_caches, sample_new_rows):
    nb, seq, d = x.shape
    n = nb * seq
    tm = PROMPT_TILE
    assert seq % tm == 0 and seq % DIL_GROUPS[-1][0] == 0
    x2d = x.reshape(n, d)
    rest = _proj_rest(x2d, w["g_pre"], w["w_r"], tm=tm, seq=seq, prompt=True)
    a_g, gates = rest[:N_GROUPS], rest[N_GROUPS]
    ns = sample_caches[0].shape[0]
    ya, s_new, next_caches = _proj_gla_fused(
        x2d, w["g_pre"], w["w_a"], w["w_lr"], w["w_lr2"], w["b_lr"], s0, w["gn"], tm=tm, seq=seq, chunk=GLA_CHUNK,
        shift_src=tuple(_cache_rows(c) for c in sample_caches),
        shift_tail=tuple(r.reshape(ns, -1, DIL_HD) for r in sample_new_rows))
    new_bufs = [kv.reshape(nb, -1, 2, DIL_HEADS, DIL_HD) for kv in rest[N_GROUPS + 1:]]
    attn = [_dil_prompt(a) for a in a_g]
    y = _final(ya, [o for o, _ in attn], [lse for _, lse in attn], gates, x2d, pe.reshape(n, -1),
               w["wa"], w["wb"], w["wo"], w["wpp"], w["wpg"], w["g_post"], tm=tm)
    next_caches = [c.reshape(old.shape) for c, old in zip(next_caches, sample_caches)]
    return y.reshape(nb, seq, d), s_new, new_bufs, next_caches


def _sample_proj(x, w):
    nb, seq, d = x.shape
    return _proj_rest(x.reshape(nb * seq, d), w["g_pre"], w["w_r"], tm=nb * seq, seq=seq, prompt=False)


def _sample_layer(x, pe, s0, caches, w, rest):
    nb, seq, d = x.shape
    n = nb * seq
    x2d = x.reshape(n, d)
    a_g, gates = rest[:N_GROUPS], rest[N_GROUPS]
    chunk = min(GLA_CHUNK, seq)
    assert seq % chunk == 0
    nseq = 4 if nb % 4 == 0 else 1
    qkvg, la = _proj_gla(x2d, w["g_pre"], w["w_a"], w["w_lr"], w["w_lr2"], w["b_lr"], tm=n, out_dtype=F32)
    ya, s_new = _gla(qkvg, la, s0, w["gn"], nb=nb, seq=seq, chunk=chunk, rows_per_step=seq, nseq=nseq,
                     out_dtype=F32, mm_dtype=F32, precision=lax.Precision.HIGHEST)
    attn = [_dil_sample(a_g[g], caches[g], dil=dil, t_new=seq) for g, (_, dil) in enumerate(DIL_GROUPS)]
    y = _final(ya, [o for o, _ in attn], [lse for _, lse in attn], gates, x2d, pe.reshape(n, -1),
               w["wa"], w["wb"], w["wo"], w["wpp"], w["wpg"], w["g_post"], tm=n)
    return y.reshape(nb, seq, d), s_new


def kernel(x_prompt, x_sample, state_gla, cache_kv_w128, cache_kv_w512, cache_kv_w2048, p_prompt, p_sample, g_pre, g_post, w_in, w_gla_lr, b_gla_lr, g_gla_norm, w_gla_branch, w_dil_branch, w_out, w_ple_proj, w_ple_gate):
    depth = w_in.shape[0]
    hp, hs = x_prompt, x_sample
    gla_p, gla_s = [], []
    kvp = [[] for _ in DIL_GROUPS]
    kvs = [[] for _ in DIL_GROUPS]
    for i in range(depth):
        w = _prep_weights(g_pre[i], g_post[i], w_in[i], w_gla_lr[i], b_gla_lr[i], g_gla_norm[i], w_gla_branch[i],
                          w_dil_branch[i], w_out[i], w_ple_proj[i], w_ple_gate[i])
        s0 = jnp.zeros((x_prompt.shape[0], GLA_HEADS, GLA_DK, GLA_DV), F32)
        caches = (cache_kv_w128[i], cache_kv_w512[i], cache_kv_w2048[i])
        rest_s = _sample_proj(hs, w)
        hp, sp_new, bp_new, bs_new = _prompt_layer(hp, p_prompt[i], s0, w, caches, rest_s[N_GROUPS + 1:])
        hs, ss_new = _sample_layer(hs, p_sample[i], state_gla[i], caches, w, rest_s)
        gla_p.append(sp_new)
        gla_s.append(ss_new)
        for g in range(N_GROUPS):
            kvp[g].append(bp_new[g])
            kvs[g].append(bs_new[g])
    return (hp, hs, jnp.stack(gla_p), jnp.stack(gla_s),
            jnp.stack(kvp[0]), jnp.stack(kvp[1]), jnp.stack(kvp[2]),
            jnp.stack(kvs[0]), jnp.stack(kvs[1]), jnp.stack(kvs[2]))
```

```python
import functools

import jax
import jax.numpy as jnp
from jax import lax
from jax.experimental import pallas as pl
from jax.experimental.pallas import tpu as pltpu

F32 = jnp.float32
BF16 = jnp.bfloat16

NORM_EPS = 1e-6
GLA_HEADS = 4
GLA_DK = 128
GLA_DV = 256
GLA_KEY = GLA_HEADS * GLA_DK
GLA_VAL = GLA_HEADS * GLA_DV
GLA_LOWRANK = 16
GLA_TAU = 16.0
GLA_CHUNK = 64
DIL_GROUPS = ((128, 1), (512, 4), (2048, 16))
DIL_HEADS = 4
DIL_HD = 128
DIL_WIDTH = DIL_HEADS * DIL_HD
DIL_SPAN = 128
N_GROUPS = len(DIL_GROUPS)
GLA_COLS = 2 * GLA_KEY + 2 * GLA_VAL
QKV_COLS = 3 * DIL_WIDTH
KV_COLS = 2 * DIL_WIDTH

LANES = 128
LSE_LANES_PER_HEAD = LANES // DIL_HEADS
_LOG2_LSE_LANES = LSE_LANES_PER_HEAD.bit_length() - 1
_LOG2_DIL_HD = DIL_HD.bit_length() - 1
assert 1 << _LOG2_LSE_LANES == LSE_LANES_PER_HEAD and 1 << _LOG2_DIL_HD == DIL_HD
VMEM_LIMIT_BYTES = 56 * 1024 * 1024

_NT = (((1,), (1,)), ((), ()))
_TN = (((0,), (0,)), ((), ()))


def _dot(a, b, dims=None, precision=None):
    if dims is None:
        return jnp.dot(a, b, preferred_element_type=F32, precision=precision)
    return lax.dot_general(a, b, dims, preferred_element_type=F32, precision=precision)


def _rms(xf, g):
    return xf * lax.rsqrt(jnp.mean(xf * xf, axis=-1, keepdims=True) + NORM_EPS) * g


def _sigmoid(x):
    return 1.0 / (1.0 + jnp.exp(-x))


def _const_spec(shape):
    nd = len(shape)
    return pl.BlockSpec(shape, lambda *_: (0,) * nd, pipeline_mode=pl.Buffered(1))


def _params(*sem):
    return pltpu.CompilerParams(dimension_semantics=sem, vmem_limit_bytes=VMEM_LIMIT_BYTES)


def _proj_gla_kernel(x_ref, g_ref, w_ref, wlr_ref, wlr2_ref, blr_ref, qkvg_ref, la_ref, *, col_chunk):
    xn = _rms(x_ref[...], g_ref[...]).astype(BF16)
    for c0 in range(0, GLA_COLS, col_chunk):
        qkvg_ref[:, c0:c0 + col_chunk] = _dot(xn, w_ref[:, c0:c0 + col_chunk]).astype(qkvg_ref.dtype)
    glr = _dot(xn, wlr_ref[...])
    z = _dot(glr.astype(BF16), wlr2_ref[...]) + blr_ref[...]
    log_sig = jnp.minimum(z, 0.0) - jnp.log1p(jnp.exp(-jnp.abs(z)))
    la_ref[...] = log_sig * (1.0 / GLA_TAU)


def _proj_gla(x2d, g_pre, w_a, w_lr, w_lr2, b_lr, *, tm, out_dtype):
    n, d = x2d.shape
    return pl.pallas_call(
        functools.partial(_proj_gla_kernel, col_chunk=512),
        grid=(n // tm,),
        in_specs=[
            pl.BlockSpec((tm, d), lambda i: (i, 0)),
            _const_spec(g_pre.shape),
            _const_spec(w_a.shape),
            _const_spec(w_lr.shape),
            _const_spec(w_lr2.shape),
            _const_spec(b_lr.shape),
        ],
        out_specs=[
            pl.BlockSpec((tm, GLA_COLS), lambda i: (i, 0)),
            pl.BlockSpec((tm, GLA_KEY), lambda i: (i, 0)),
        ],
        out_shape=[
            jax.ShapeDtypeStruct((n, GLA_COLS), out_dtype),
            jax.ShapeDtypeStruct((n, GLA_KEY), F32),
        ],
        compiler_params=_params("parallel"),
        name="proj_gla",
    )(x2d, g_pre, w_a, w_lr, w_lr2, b_lr)


def _proj_rest_kernel(x_ref, g_ref, w_ref, *refs, tm, tiles_per_seq, keeps, prompt):
    a_refs = refs[:N_GROUPS]
    gates_ref = refs[N_GROUPS]
    kv_refs = refs[N_GROUPS + 1:2 * N_GROUPS + 1]
    stage_refs = refs[2 * N_GROUPS + 1:]

    def body(with_cache_rows):
        xn = _rms(x_ref[...], g_ref[...]).astype(BF16)
        n_staged = 0
        for g in range(N_GROUPS):
            base = g * QKV_COLS
            dil = DIL_GROUPS[g][1]
            for part in range(3):
                c0 = part * DIL_WIDTH
                r = _dot(xn, w_ref[:, base + c0:base + c0 + DIL_WIDTH])
                if not prompt:
                    a_refs[g][:, c0:c0 + DIL_WIDTH] = r
                elif dil == 1:
                    a_refs[g][0, :, c0:c0 + DIL_WIDTH] = r.astype(BF16)
                else:
                    stage = stage_refs[n_staged % len(stage_refs)]
                    n_staged += 1
                    for s in range(DIL_WIDTH // LANES):
                        stage[s] = r[:, s * LANES:(s + 1) * LANES]
                    for res in range(dil):
                        for s in range(DIL_WIDTH // LANES):
                            a_refs[g][res, :, c0 + s * LANES:c0 + (s + 1) * LANES] = (
                                stage[s, pl.ds(res, tm // dil, stride=dil), :].astype(BF16))
                if part > 0 and with_cache_rows:
                    rows = min(keeps[g], tm)
                    for h in range(DIL_HEADS):
                        kv_refs[g][pl.ds((part - 1) * DIL_HEADS + h, rows, stride=2 * DIL_HEADS), :] = (
                            r[tm - rows:, h * DIL_HD:(h + 1) * DIL_HD])
        gbase = N_GROUPS * QKV_COLS
        gcols = gates_ref.shape[1]
        for c0 in range(0, gcols, DIL_WIDTH):
            gates_ref[:, c0:c0 + DIL_WIDTH] = _dot(xn, w_ref[:, gbase + c0:gbase + c0 + DIL_WIDTH]).astype(gates_ref.dtype)

    if not prompt:
        body(True)
        return
    t = pl.program_id(0) % tiles_per_seq
    in_kept_range = t >= tiles_per_seq - max(1, max(keeps) // tm)
    pl.when(in_kept_range)(lambda: body(True))
    pl.when(jnp.logical_not(in_kept_range))(lambda: body(False))


def _proj_rest(x2d, g_pre, w_r, *, tm, seq, prompt):
    n, d = x2d.shape
    nb = n // seq
    tiles_per_seq = seq // tm if prompt else None
    gate_cols = w_r.shape[1] - N_GROUPS * QKV_COLS
    keeps = tuple(min(w, seq) for (w, _) in DIL_GROUPS) if prompt else (tm,) * N_GROUPS
    rows_per_token = 2 * DIL_HEADS
    out_specs, out_shape, scratch = [], [], []
    if prompt:
        for _, dil in DIL_GROUPS:
            out_specs.append(pl.BlockSpec((None, dil, tm // dil, QKV_COLS),
                                          lambda i: (i // tiles_per_seq, 0, i % tiles_per_seq, 0)))
            out_shape.append(jax.ShapeDtypeStruct((nb, dil, seq // dil, QKV_COLS), BF16))
    else:
        out_specs += [pl.BlockSpec((tm, QKV_COLS), lambda i: (i, 0)) for _ in range(N_GROUPS)]
        out_shape += [jax.ShapeDtypeStruct((n, QKV_COLS), F32) for _ in range(N_GROUPS)]
    out_specs.append(pl.BlockSpec((tm, gate_cols), lambda i: (i, 0)))
    out_shape.append(jax.ShapeDtypeStruct((n, gate_cols), BF16 if prompt else F32))
    if not prompt:
        for _ in range(N_GROUPS):
            out_specs.append(pl.BlockSpec((tm * rows_per_token, DIL_HD), lambda i: (i, 0)))
            out_shape.append(jax.ShapeDtypeStruct((n * rows_per_token, DIL_HD), F32))
    else:
        for keep in keeps:
            if keep >= tm:
                assert keep % tm == 0
                kt = keep // tm

                def idx(i, kt=kt):
                    b, t = i // tiles_per_seq, i % tiles_per_seq
                    return (b * kt + jnp.maximum(t - (tiles_per_seq - kt), 0), 0)

                out_specs.append(pl.BlockSpec((tm * rows_per_token, DIL_HD), idx))
            else:
                out_specs.append(pl.BlockSpec((keep * rows_per_token, DIL_HD), lambda i: (i // tiles_per_seq, 0)))
            out_shape.append(jax.ShapeDtypeStruct((nb * keep * rows_per_token, DIL_HD), F32))
        n_stage = 3 * sum(1 for _, dil in DIL_GROUPS if dil > 1)
        scratch = [pltpu.VMEM((DIL_WIDTH // LANES, tm, LANES), F32)] * n_stage
    return pl.pallas_call(
        functools.partial(_proj_rest_kernel, tm=tm, tiles_per_seq=tiles_per_seq, keeps=keeps, prompt=prompt),
        grid=(n // tm,),
        in_specs=[
            pl.BlockSpec((tm, d), lambda i: (i, 0)),
            _const_spec(g_pre.shape),
            _const_spec(w_r.shape),
        ],
        out_specs=out_specs,
        out_shape=out_shape,
        scratch_shapes=scratch,
        compiler_params=_params("arbitrary"),
        name="proj_rest",
    )(x2d, g_pre, w_r)


def _gla_kernel(qkvg_ref, la_ref, s0_ref, gn_ref, ya_ref, sfin_ref, st_ref, *, chunk, nchunk, nseq, mm_dtype, precision):
    j = pl.program_id(1)

    @pl.when(j == 0)
    def _():
        for s in range(nseq):
            for h in range(GLA_HEADS):
                st_ref[s * GLA_HEADS + h] = s0_ref[s, h].T

    _gla_compute(lambda r, c: qkvg_ref[r, c], lambda r: la_ref[r, :], st_ref, gn_ref, ya_ref,
                 chunk=chunk, nchunk=nchunk, nseq=nseq, mm_dtype=mm_dtype, precision=precision)

    @pl.when(j == pl.num_programs(1) - 1)
    def _():
        for s in range(nseq):
            for h in range(GLA_HEADS):
                sfin_ref[s, h] = st_ref[s * GLA_HEADS + h].T


def _gla_compute(*args, **kwargs):
    for _ in _gla_phases(*args, **kwargs):
        pass


def _interleave(*generators):
    live = list(generators)
    while live:
        for gen in list(live):
            if next(gen, StopIteration) is StopIteration:
                live.remove(gen)


def _gla_phases(load_qkvg, load_la, st_ref, gn_ref, ya_ref, *, chunk, nchunk, nseq, mm_dtype, precision):
    row = lax.broadcasted_iota(jnp.int32, (chunk, chunk), 0)
    col = lax.broadcasted_iota(jnp.int32, (chunk, chunk), 1)
    causal = row >= col
    tril = causal.astype(F32)
    qscale = GLA_DK ** -0.5

    segs = [(s, c) for s in range(nseq) for c in range(nchunk)]
    units = [(s, c, h) for s, c in segs for h in range(GLA_HEADS)]

    def rows_of(s, c):
        r0 = (s * nchunk + c) * chunk
        return slice(r0, r0 + chunk)

    def key_cols(h):
        return slice(h * GLA_DK, (h + 1) * GLA_DK)

    def v_of(s, c, h):
        return load_qkvg(rows_of(s, c), slice(2 * GLA_KEY + h * GLA_DV, 2 * GLA_KEY + (h + 1) * GLA_DV)).astype(mm_dtype)

    def cumsum(a):
        if mm_dtype != BF16:
            return _dot(tril, a, precision=lax.Precision.HIGHEST)
        hi = a.astype(BF16)
        rest = a - hi.astype(F32)
        mid = rest.astype(BF16)
        lo = (rest - mid.astype(F32)).astype(BF16)
        parts = _dot(tril.astype(BF16), jnp.concatenate([hi, mid, lo], axis=1))
        return parts[:, :GLA_KEY] + parts[:, GLA_KEY:2 * GLA_KEY] + parts[:, 2 * GLA_KEY:]

    cum = {sc: cumsum(load_la(rows_of(*sc))) for sc in segs}
    b_end = {sc: cum[sc][chunk - 1:chunk, :] for sc in segs}
    e_b = {sc: jnp.exp(cum[sc]) for sc in segs}
    e_nb = {sc: jnp.exp(-cum[sc]) for sc in segs}
    e_rest = {sc: jnp.exp(b_end[sc] - cum[sc]) for sc in segs}
    e_end = {sc: jnp.exp(b_end[sc]) for sc in segs}
    yield
    q_in, k_dec, k_end = {}, {}, {}
    for s, c, h in units:
        q = load_qkvg(rows_of(s, c), key_cols(h)).astype(F32) * qscale
        k = load_qkvg(rows_of(s, c), slice(GLA_KEY + h * GLA_DK, GLA_KEY + (h + 1) * GLA_DK)).astype(F32)
        q_in[s, c, h] = (q * e_b[s, c][:, key_cols(h)]).astype(mm_dtype)
        k_dec[s, c, h] = (k * e_nb[s, c][:, key_cols(h)]).astype(mm_dtype)
        k_end[s, c, h] = (k * e_rest[s, c][:, key_cols(h)]).astype(mm_dtype)
    yield
    att = {u: jnp.where(causal, _dot(q_in[u], k_dec[u], _NT, precision), 0.0).astype(mm_dtype) for u in units}
    yield
    o_intra = {u: _dot(att[u], v_of(*u), None, precision) for u in units}
    yield
    d_state = {u: _dot(v_of(*u), k_end[u], _TN, precision) for u in units}
    yield
    o_inter = {}
    for s in range(nseq):
        for h in range(GLA_HEADS):
            st = st_ref[s * GLA_HEADS + h]
            for c in range(nchunk):
                o_inter[s, c, h] = _dot(q_in[s, c, h], st.astype(mm_dtype), _NT, precision)
                st = st * e_end[s, c][:, key_cols(h)] + d_state[s, c, h]
            st_ref[s * GLA_HEADS + h] = st
    yield
    for i, (s, c, h) in enumerate(units):
        vs = slice(h * GLA_DV, (h + 1) * GLA_DV)
        gg = load_qkvg(rows_of(s, c), slice(2 * GLA_KEY + GLA_VAL + h * GLA_DV,
                                            2 * GLA_KEY + GLA_VAL + (h + 1) * GLA_DV)).astype(F32)
        on = _rms(o_intra[s, c, h] + o_inter[s, c, h], gn_ref[:, vs])
        ya_ref[rows_of(s, c), vs] = (on * (gg * _sigmoid(gg))).astype(ya_ref.dtype)
        if i % 8 == 7:
            yield


def _proj_gla_fused_kernel(x_ref, g_ref, w_ref, wlr_ref, wlr2_ref, blr_ref, s0_ref, gn_ref, *refs,
                           steps, chunk, col_chunk, n_shift, pieces, n_chunks):
    old_refs = refs[:n_shift]
    next_refs = refs[n_shift:2 * n_shift]
    tail_refs = refs[2 * n_shift:3 * n_shift]
    ya_ref, sfin_ref = refs[3 * n_shift:3 * n_shift + 2]
    new_refs = refs[3 * n_shift + 2:4 * n_shift + 2]
    qkvg_s0, qkvg_s1, la_s0, la_s1, st_ref = refs[4 * n_shift + 2:]
    j = pl.program_id(0)
    prev = j - 1

    if n_shift:
        last_piece = jnp.minimum(j, n_chunks - 1) % pieces == pieces - 1
        for old, nxt, tail, new in zip(old_refs, next_refs, tail_refs, new_refs):
            p_rows, t_rows = old.shape[0], tail.shape[0]
            new[0:p_rows - t_rows, :] = old[t_rows:p_rows, :]
            new[p_rows - t_rows:p_rows, :] = jnp.where(last_piece, tail[...], nxt[...])

    @pl.when(j == 0)
    def _():
        qkvg_s1[...] = jnp.zeros_like(qkvg_s1)
        la_s1[...] = jnp.zeros_like(la_s1)

    @pl.when(jnp.logical_or(j == 0, prev % steps == 0))
    def _():
        for h in range(GLA_HEADS):
            st_ref[h] = s0_ref[0, h].T

    def project(qkvg_w, la_w):
        xn = _rms(x_ref[...], g_ref[...]).astype(BF16)
        yield
        for c0 in range(0, GLA_COLS, col_chunk):
            qkvg_w[:, c0:c0 + col_chunk] = _dot(xn, w_ref[:, c0:c0 + col_chunk]).astype(qkvg_w.dtype)
            yield
        glr = _dot(xn, wlr_ref[...])
        z = _dot(glr.astype(BF16), wlr2_ref[...]) + blr_ref[...]
        log_sig = jnp.minimum(z, 0.0) - jnp.log1p(jnp.exp(-jnp.abs(z)))
        la_w[...] = log_sig * (1.0 / GLA_TAU)

    def step(qkvg_w, la_w, qkvg_r, la_r):
        _interleave(
            project(qkvg_w, la_w),
            _gla_phases(lambda r, c: qkvg_r[r, c], lambda r: la_r[r, :], st_ref, gn_ref, ya_ref,
                        chunk=chunk, nchunk=x_ref.shape[0] // chunk, nseq=1, mm_dtype=BF16, precision=None))

    @pl.when(j % 2 == 0)
    def _():
        step(qkvg_s0, la_s0, qkvg_s1, la_s1)

    @pl.when(j % 2 == 1)
    def _():
        step(qkvg_s1, la_s1, qkvg_s0, la_s0)

    @pl.when(jnp.logical_and(j >= 1, prev % steps == steps - 1))
    def _():
        for h in range(GLA_HEADS):
            sfin_ref[0, h] = st_ref[h].T


def _proj_gla_fused(x2d, g_pre, w_a, w_lr, w_lr2, b_lr, s0, gn, *, tm, seq, chunk, shift_src=(), shift_tail=()):
    n, d = x2d.shape
    nb = n // seq
    steps = seq // tm
    n_tiles = n // tm
    n_shift = len(shift_src)
    pieces = n_chunks = 1
    copy_in, copy_out = [], []
    if n_shift:
        entries = shift_src[0].shape[0]
        pieces = max(1, n_tiles // entries)
        n_chunks = entries * pieces
        assert n_chunks <= n_tiles + 1

        def piece_of(j):
            c = jnp.minimum(j, n_chunks - 1)
            return c // pieces, c % pieces

        next_specs, tail_specs = [], []
        for src, tail in zip(shift_src, shift_tail):
            t_rows = tail.shape[1]
            p_rows = src.shape[1] // pieces
            assert src.shape[0] == tail.shape[0] == entries and src.shape[2] == tail.shape[2] == LANES
            assert src.shape[1] % pieces == 0 and p_rows % t_rows == 0 and t_rows % 8 == 0
            piece_spec = pl.BlockSpec((None, p_rows, LANES), lambda j: (*piece_of(j), 0))
            last_t_block = src.shape[1] // t_rows - 1

            def next_idx(j, per_piece=p_rows // t_rows, last=last_t_block):
                entry, piece = piece_of(j)
                return (entry, jnp.minimum((piece + 1) * per_piece, last), 0)

            copy_in.append(piece_spec)
            next_specs.append(pl.BlockSpec((None, t_rows, LANES), next_idx))
            tail_specs.append(pl.BlockSpec((None, t_rows, LANES), lambda j: (piece_of(j)[0], 0, 0)))
            copy_out.append(piece_spec)
        copy_in += next_specs + tail_specs

    def prev_tile(j):
        return jnp.maximum(j - 1, 0)

    outs = pl.pallas_call(
        functools.partial(_proj_gla_fused_kernel, steps=steps, chunk=chunk, col_chunk=512, n_shift=n_shift,
                          pieces=pieces, n_chunks=n_chunks),
        grid=(n_tiles + 1,),
        in_specs=[
            pl.BlockSpec((tm, d), lambda j: (jnp.minimum(j, n_tiles - 1), 0)),
            _const_spec(g_pre.shape),
            _const_spec(w_a.shape),
            _const_spec(w_lr.shape),
            _const_spec(w_lr2.shape),
            _const_spec(b_lr.shape),
            pl.BlockSpec((1, GLA_HEADS, GLA_DK, GLA_DV), lambda j: (prev_tile(j) // steps, 0, 0, 0)),
            _const_spec(gn.shape),
        ] + copy_in,
        out_specs=[
            pl.BlockSpec((tm, GLA_VAL), lambda j: (prev_tile(j), 0)),
            pl.BlockSpec((1, GLA_HEADS, GLA_DK, GLA_DV), lambda j: (prev_tile(j) // steps, 0, 0, 0)),
        ] + copy_out,
        out_shape=[
            jax.ShapeDtypeStruct((n, GLA_VAL), BF16),
            jax.ShapeDtypeStruct((nb, GLA_HEADS, GLA_DK, GLA_DV), F32),
        ] + [jax.ShapeDtypeStruct(src.shape, src.dtype) for src in shift_src],
        scratch_shapes=[
            pltpu.VMEM((tm, GLA_COLS), BF16),
            pltpu.VMEM((tm, GLA_COLS), BF16),
            pltpu.VMEM((tm, GLA_KEY), F32),
            pltpu.VMEM((tm, GLA_KEY), F32),
            pltpu.VMEM((GLA_HEADS, GLA_DV, GLA_DK), F32),
        ],
        compiler_params=_params("arbitrary"),
        name="proj_gla_fused",
    )(x2d, g_pre, w_a, w_lr, w_lr2, b_lr, s0, gn, *shift_src, *shift_src, *shift_tail)
    return outs[0], outs[1], list(outs[2:])


def _gla(qkvg, la, s0, gn, *, nb, seq, chunk, rows_per_step, nseq, out_dtype, mm_dtype, precision):
    steps = seq // rows_per_step
    assert nb % nseq == 0 and (nseq == 1 or steps == 1)
    rows = nseq * rows_per_step
    return pl.pallas_call(
        functools.partial(_gla_kernel, chunk=chunk, nchunk=rows_per_step // chunk, nseq=nseq, mm_dtype=mm_dtype,
                          precision=precision),
        grid=(nb // nseq, steps),
        in_specs=[
            pl.BlockSpec((rows, GLA_COLS), lambda b, j: (b * steps + j, 0)),
            pl.BlockSpec((rows, GLA_KEY), lambda b, j: (b * steps + j, 0)),
            pl.BlockSpec((nseq, GLA_HEADS, GLA_DK, GLA_DV), lambda b, j: (b, 0, 0, 0)),
            pl.BlockSpec((1, GLA_VAL), lambda b, j: (0, 0)),
        ],
        out_specs=[
            pl.BlockSpec((rows, GLA_VAL), lambda b, j: (b * steps + j, 0)),
            pl.BlockSpec((nseq, GLA_HEADS, GLA_DK, GLA_DV), lambda b, j: (b, 0, 0, 0)),
        ],
        out_shape=[
            jax.ShapeDtypeStruct((nb * seq, GLA_VAL), out_dtype),
            jax.ShapeDtypeStruct((nb, GLA_HEADS, GLA_DK, GLA_DV), F32),
        ],
        scratch_shapes=[pltpu.VMEM((nseq * GLA_HEADS, GLA_DV, GLA_DK), F32)],
        compiler_params=_params("parallel", "arbitrary"),
        name="gla",
    )(qkvg, la, s0, gn)


def _pack_lse(lses):
    rows = lses[0].shape[0]
    lane_head = lax.broadcasted_iota(jnp.int32, (rows, LANES), 1) >> _LOG2_LSE_LANES
    packed = jnp.broadcast_to(lses[0], (rows, LANES))
    for h in range(1, DIL_HEADS):
        packed = jnp.where(lane_head == h, lses[h], packed)
    return packed


def _dil_prompt_kernel(a_ref, o_ref, lse_ref, kvp_ref, *, nres, nblk):
    first = pl.program_id(2) == 0

    @pl.when(first)
    def _():
        kvp_ref[...] = jnp.zeros_like(kvp_ref)

    i_idx = lax.broadcasted_iota(jnp.int32, (DIL_SPAN, 2 * DIL_SPAN), 0)
    j_idx = lax.broadcasted_iota(jnp.int32, (DIL_SPAN, 2 * DIL_SPAN), 1)
    diff = j_idx - i_idx
    band = jnp.logical_and(diff >= 0, diff <= DIL_SPAN)
    band_first = jnp.logical_and(diff >= jnp.where(first, DIL_SPAN - i_idx, 0), diff <= DIL_SPAN)
    scale = DIL_HD ** -0.5
    neg_inf = -jnp.inf
    units = [(res, jb, h) for res in range(nres) for jb in range(nblk) for h in range(DIL_HEADS)]

    def window(res, jb, c0):
        if jb == 0:
            prev = kvp_ref[res, :, c0 - DIL_WIDTH:c0 - DIL_WIDTH + DIL_HD]
            return jnp.concatenate([prev, a_ref[res, 0:DIL_SPAN, c0:c0 + DIL_HD]], axis=0)
        return a_ref[res, (jb - 1) * DIL_SPAN:(jb + 1) * DIL_SPAN, c0:c0 + DIL_HD]

    scores = []
    for res, jb, h in units:
        q = a_ref[res, jb * DIL_SPAN:(jb + 1) * DIL_SPAN, h * DIL_HD:(h + 1) * DIL_HD]
        s = _dot(q, window(res, jb, DIL_WIDTH + h * DIL_HD), _NT) * scale
        scores.append(jnp.where(band_first if jb == 0 else band, s, neg_inf))
    maxes = [jnp.max(s, axis=-1, keepdims=True) for s in scores]
    probs = [jnp.exp(s - m) for s, m in zip(scores, maxes)]
    sums = [jnp.sum(p, axis=-1, keepdims=True) for p in probs]
    lses = {}
    for (res, jb, h), p, m, l in zip(units, probs, maxes, sums):
        acc = _dot(p.astype(BF16), window(res, jb, 2 * DIL_WIDTH + h * DIL_HD))
        o_ref[res, jb * DIL_SPAN:(jb + 1) * DIL_SPAN, h * DIL_HD:(h + 1) * DIL_HD] = acc * (1.0 / l)
        lses[res, jb, h] = m + jnp.log(l)
    for res in range(nres):
        for jb in range(nblk):
            lse_ref[res, jb * DIL_SPAN:(jb + 1) * DIL_SPAN, :] = _pack_lse([lses[res, jb, h] for h in range(DIL_HEADS)])
        kvp_ref[res] = a_ref[res, (nblk - 1) * DIL_SPAN:nblk * DIL_SPAN, DIL_WIDTH:QKV_COLS]


def _dil_prompt(a4, *, rows_per_step=1024):
    nb, dil, length, _ = a4.shape
    rows = min(rows_per_step, length)
    nres = min(dil, rows_per_step // rows)

    def spec(cols):
        return pl.BlockSpec((None, nres, rows, cols), lambda b, r, n: (b, r, n, 0))

    return pl.pallas_call(
        functools.partial(_dil_prompt_kernel, nres=nres, nblk=rows // DIL_SPAN),
        grid=(nb, dil // nres, length // rows),
        in_specs=[spec(QKV_COLS)],
        out_specs=[spec(DIL_WIDTH), spec(LANES)],
        out_shape=[
            jax.ShapeDtypeStruct((nb, dil, length, DIL_WIDTH), F32),
            jax.ShapeDtypeStruct((nb, dil, length, LANES), F32),
        ],
        scratch_shapes=[pltpu.VMEM((nres, DIL_SPAN, KV_COLS), BF16)],
        compiler_params=_params("parallel", "parallel", "arbitrary"),
        name=f"dil_prompt_d{dil}",
    )(a4)


def _dil_sample_kernel(a_ref, cache_ref, o_ref, lse_ref, *, dil, t_new):
    rpt = 2 * DIL_HEADS
    lb = cache_ref.shape[0] // rpt
    tq = lax.broadcasted_iota(jnp.int32, (t_new, lb), 0)
    row = lax.broadcasted_iota(jnp.int32, (t_new, lb), 1)
    valid_c = jnp.logical_and(((row - tq) & (dil - 1)) == 0, row >= tq)
    tq_n = lax.broadcasted_iota(jnp.int32, (t_new, t_new), 0)
    j_n = lax.broadcasted_iota(jnp.int32, (t_new, t_new), 1)
    valid_n = jnp.logical_and(((tq_n - j_n) & (dil - 1)) == 0, j_n <= tq_n)
    scale = DIL_HD ** -0.5
    neg_inf = -jnp.inf
    lses = []
    for h in range(DIL_HEADS):
        hs = slice(h * DIL_HD, (h + 1) * DIL_HD)
        vs = slice(DIL_WIDTH + h * DIL_HD, DIL_WIDTH + (h + 1) * DIL_HD)
        q = a_ref[:, hs].astype(BF16)
        k_new = a_ref[:, DIL_WIDTH + h * DIL_HD:DIL_WIDTH + (h + 1) * DIL_HD].astype(BF16)
        v_new = a_ref[:, 2 * DIL_WIDTH + h * DIL_HD:2 * DIL_WIDTH + (h + 1) * DIL_HD].astype(BF16)
        k_c = cache_ref[pl.ds(h, lb, stride=rpt), :].astype(BF16)
        v_c = cache_ref[pl.ds(DIL_HEADS + h, lb, stride=rpt), :].astype(BF16)
        s_c = jnp.where(valid_c, _dot(q, k_c, _NT) * scale, neg_inf)
        s_n = jnp.where(valid_n, _dot(q, k_new, _NT) * scale, neg_inf)
        m = jnp.maximum(jnp.max(s_c, axis=-1, keepdims=True), jnp.max(s_n, axis=-1, keepdims=True))
        p_c = jnp.exp(s_c - m)
        p_n = jnp.exp(s_n - m)
        l = jnp.sum(p_c, axis=-1, keepdims=True) + jnp.sum(p_n, axis=-1, keepdims=True)
        acc = _dot(p_c.astype(BF16), v_c) + _dot(p_n.astype(BF16), v_new)
        o_ref[:, hs] = acc * (1.0 / l)
        lses.append(m + jnp.log(l))
    lse_ref[...] = _pack_lse(lses)


def _cache_rows(cache):
    nb, lb = cache.shape[:2]
    return cache.reshape(nb, lb * 2 * DIL_HEADS, DIL_HD)


def _dil_sample(a2d, cache, *, dil, t_new):
    nb, lb = cache.shape[:2]
    assert lb == dil * DIL_SPAN and a2d.dtype == F32
    rows = lb * 2 * DIL_HEADS
    return pl.pallas_call(
        functools.partial(_dil_sample_kernel, dil=dil, t_new=t_new),
        grid=(nb,),
        in_specs=[
            pl.BlockSpec((t_new, QKV_COLS), lambda b: (b, 0)),
            pl.BlockSpec((None, rows, DIL_HD), lambda b: (b, 0, 0)),
        ],
        out_specs=[
            pl.BlockSpec((t_new, DIL_WIDTH), lambda b: (b, 0)),
            pl.BlockSpec((t_new, LANES), lambda b: (b, 0)),
        ],
        out_shape=[
            jax.ShapeDtypeStruct((nb * t_new, DIL_WIDTH), F32),
            jax.ShapeDtypeStruct((nb * t_new, LANES), F32),
        ],
        compiler_params=_params("parallel"),
        name=f"dil_sample_d{dil}",
    )(a2d, _cache_rows(cache))


def _token_order(ref, scratch_refs):
    if len(ref.shape) == 2:
        return ref[...]
    dil, rows_per_class, _ = ref.shape
    if dil == 1:
        return ref[0]
    scratch = scratch_refs.pop()
    n_slabs = scratch.shape[0]
    for res in range(dil):
        for s in range(n_slabs):
            scratch[s, pl.ds(res, rows_per_class, stride=dil), :] = ref[res, :, s * LANES:(s + 1) * LANES]
    return jnp.concatenate([scratch[s] for s in range(n_slabs)], axis=1) if n_slabs > 1 else scratch[0]


def _final_kernel(ya_ref, o0_ref, o1_ref, o2_ref, l0_ref, l1_ref, l2_ref, gates_ref, x_ref, pe_ref,
                  wa_ref, wb_ref, wo_ref, wpp_ref, wpg_ref, gpost_ref, y_ref, *scratch_refs):
    o_scratch = [s for s in scratch_refs if s.shape[0] == DIL_WIDTH // LANES]
    l_scratch = [s for s in scratch_refs if s.shape[0] == 1]
    lse = [_token_order(r, l_scratch) for r in (l0_ref, l1_ref, l2_ref)]
    o_groups = [_token_order(r, o_scratch) for r in (o0_ref, o1_ref, o2_ref)]
    m = jnp.maximum(jnp.maximum(lse[0], lse[1]), lse[2])
    e = [jnp.exp(x - m) for x in lse]
    inv = 1.0 / (e[0] + e[1] + e[2])
    src = lax.broadcasted_iota(jnp.int32, (LANES, DIL_WIDTH), 0)
    dst_head = lax.broadcasted_iota(jnp.int32, (LANES, DIL_WIDTH), 1) >> _LOG2_DIL_HD
    expand = (src == dst_head * LSE_LANES_PER_HEAD).astype(BF16)
    o_b = None
    for eg, o_g in zip(e, o_groups):
        w = eg * inv
        w_hi = w.astype(BF16)
        w_lo = (w - w_hi.astype(F32)).astype(BF16)
        w_full = _dot(w_hi, expand) + _dot(w_lo, expand)
        term = w_full * o_g
        o_b = term if o_b is None else o_b + term
    dg = gates_ref[:, 0:DIL_WIDTH].astype(F32)
    d = x_ref.shape[1]
    ga = gates_ref[:, DIL_WIDTH:DIL_WIDTH + d].astype(F32)
    gb = gates_ref[:, DIL_WIDTH + d:DIL_WIDTH + 2 * d].astype(F32)
    y_a = _dot(ya_ref[...].astype(BF16), wa_ref[...])
    y_b = _dot((o_b * (dg * _sigmoid(dg))).astype(BF16), wb_ref[...])
    merged = _sigmoid(ga) * y_a + _sigmoid(gb) * y_b
    h = x_ref[...] + _rms(_dot(merged.astype(BF16), wo_ref[...]), gpost_ref[...])
    ple = _dot(pe_ref[...].astype(BF16), wpp_ref[...])
    y_ref[...] = h + ple * _sigmoid(_dot(h.astype(BF16), wpg_ref[...]))


def _final(ya, os_, lses, gates, x2d, pe2d, wa, wb, wo, wpp, wpg, g_post, *, tm):
    n, d = x2d.shape

    def rows(cols):
        return pl.BlockSpec((tm, cols), lambda i: (i, 0))

    scratch = []

    def group_spec(arr):
        if arr.ndim == 2:
            return rows(arr.shape[1])
        _, dil, length, cols = arr.shape
        tiles_per_seq = dil * length // tm
        if dil > 1:
            scratch.append(pltpu.VMEM((cols // LANES, tm, LANES), F32))
        return pl.BlockSpec((None, dil, tm // dil, cols), lambda i: (i // tiles_per_seq, 0, i % tiles_per_seq, 0))

    return pl.pallas_call(
        _final_kernel,
        grid=(n // tm,),
        in_specs=[rows(GLA_VAL)] + [group_spec(a) for a in os_] + [group_spec(a) for a in lses]
        + [rows(gates.shape[1]), rows(d), rows(pe2d.shape[1])]
        + [_const_spec(w.shape) for w in (wa, wb, wo, wpp, wpg, g_post)],
        out_specs=rows(d),
        out_shape=jax.ShapeDtypeStruct((n, d), F32),
        scratch_shapes=scratch,
        compiler_params=_params("parallel"),
        name="final",
    )(ya, *os_, *lses, gates, x2d, pe2d, wa, wb, wo, wpp, wpg, g_post)


def _prep_weights(g_pre, g_post, w_in, w_gla_lr, b_gla_lr, g_gla_norm, w_gla_branch, w_dil_branch, w_out,
                  w_ple_proj, w_ple_gate):
    d = w_in.shape[0]
    c_lr = GLA_COLS
    c_q = c_lr + GLA_LOWRANK
    c_k = c_q + N_GROUPS * DIL_WIDTH
    c_v = c_k + N_GROUPS * DIL_WIDTH
    c_g = c_v + N_GROUPS * DIL_WIDTH
    w_a = w_in[:, :c_lr].astype(BF16)
    w_lr = jnp.pad(w_in[:, c_lr:c_q].astype(BF16), ((0, 0), (0, LANES - GLA_LOWRANK)))
    w_lr2 = jnp.pad(w_gla_lr.astype(BF16), ((0, LANES - GLA_LOWRANK), (0, 0)))
    parts = []
    for g in range(N_GROUPS):
        for c in (c_q, c_k, c_v):
            parts.append(w_in[:, c + g * DIL_WIDTH:c + (g + 1) * DIL_WIDTH])
    parts.append(w_in[:, c_g:])
    w_r = jnp.concatenate(parts, axis=1).astype(BF16)
    return dict(
        g_pre=g_pre.reshape(1, d), g_post=g_post.reshape(1, d), w_a=w_a, w_lr=w_lr, w_lr2=w_lr2, w_r=w_r,
        b_lr=b_gla_lr.reshape(1, GLA_KEY), gn=g_gla_norm.reshape(1, GLA_VAL),
        wa=w_gla_branch.astype(BF16), wb=w_dil_branch.astype(BF16), wo=w_out.astype(BF16),
        wpp=w_ple_proj.astype(BF16), wpg=w_ple_gate.astype(BF16),
    )


PROMPT_TILE = 512


def _prompt_layer(x, pe, s0, w, sample_caches, sample_new_rows):
    nb, seq, d = x.shape
    n = nb * seq
    tm = PROMPT_TILE
    assert seq % tm == 0 and seq % DIL_GROUPS[-1][0] == 0
    x2d = x.reshape(n, d)
    rest = _proj_rest(x2d, w["g_pre"], w["w_r"], tm=tm, seq=seq, prompt=True)
    a_g, gates = rest[:N_GROUPS], rest[N_GROUPS]
    ns = sample_caches[0].shape[0]
    ya, s_new, next_caches = _proj_gla_fused(
        x2d, w["g_pre"], w["w_a"], w["w_lr"], w["w_lr2"], w["b_lr"], s0, w["gn"], tm=tm, seq=seq, chunk=GLA_CHUNK,
        shift_src=tuple(_cache_rows(c) for c in sample_caches),
        shift_tail=tuple(r.reshape(ns, -1, DIL_HD) for r in sample_new_rows))
    new_bufs = [kv.reshape(nb, -1, 2, DIL_HEADS, DIL_HD) for kv in rest[N_GROUPS + 1:]]
    attn = [_dil_prompt(a) for a in a_g]
    y = _final(ya, [o for o, _ in attn], [lse for _, lse in attn], gates, x2d, pe.reshape(n, -1),
               w["wa"], w["wb"], w["wo"], w["wpp"], w["wpg"], w["g_post"], tm=tm)
    next_caches = [c.reshape(old.shape) for c, old in zip(next_caches, sample_caches)]
    return y.reshape(nb, seq, d), s_new, new_bufs, next_caches


def _sample_proj(x, w):
    nb, seq, d = x.shape
    return _proj_rest(x.reshape(nb * seq, d), w["g_pre"], w["w_r"], tm=nb * seq, seq=seq, prompt=False)


def _sample_layer(x, pe, s0, caches, w, rest):
    nb, seq, d = x.shape
    n = nb * seq
    x2d = x.reshape(n, d)
    a_g, gates = rest[:N_GROUPS], rest[N_GROUPS]
    chunk = min(GLA_CHUNK, seq)
    assert seq % chunk == 0
    nseq = 4 if nb % 4 == 0 else 1
    qkvg, la = _proj_gla(x2d, w["g_pre"], w["w_a"], w["w_lr"], w["w_lr2"], w["b_lr"], tm=n, out_dtype=F32)
    ya, s_new = _gla(qkvg, la, s0, w["gn"], nb=nb, seq=seq, chunk=chunk, rows_per_step=seq, nseq=nseq,
                     out_dtype=F32, mm_dtype=F32, precision=lax.Precision.HIGHEST)
    attn = [_dil_sample(a_g[g], caches[g], dil=dil, t_new=seq) for g, (_, dil) in enumerate(DIL_GROUPS)]
    y = _final(ya, [o for o, _ in attn], [lse for _, lse in attn], gates, x2d, pe.reshape(n, -1),
               w["wa"], w["wb"], w["wo"], w["wpp"], w["wpg"], w["g_post"], tm=n)
    return y.reshape(nb, seq, d), s_new


def kernel(x_prompt, x_sample, state_gla, cache_kv_w128, cache_kv_w512, cache_kv_w2048, p_prompt, p_sample, g_pre, g_post, w_in, w_gla_lr, b_gla_lr, g_gla_norm, w_gla_branch, w_dil_branch, w_out, w_ple_proj, w_ple_gate):
    depth = w_in.shape[0]
    hp, hs = x_prompt, x_sample
    gla_p, gla_s = [], []
    kvp = [[] for _ in DIL_GROUPS]
    kvs = [[] for _ in DIL_GROUPS]
    for i in range(depth):
        w = _prep_weights(g_pre[i], g_post[i], w_in[i], w_gla_lr[i], b_gla_lr[i], g_gla_norm[i], w_gla_branch[i],
                          w_dil_branch[i], w_out[i], w_ple_proj[i], w_ple_gate[i])
        s0 = jnp.zeros((x_prompt.shape[0], GLA_HEADS, GLA_DK, GLA_DV), F32)
        caches = (cache_kv_w128[i], cache_kv_w512[i], cache_kv_w2048[i])
        rest_s = _sample_proj(hs, w)
        hp, sp_new, bp_new, bs_new = _prompt_layer(hp, p_prompt[i], s0, w, caches, rest_s[N_GROUPS + 1:])
        hs, ss_new = _sample_layer(hs, p_sample[i], state_gla[i], caches, w, rest_s)
        gla_p.append(sp_new)
        gla_s.append(ss_new)
        for g in range(N_GROUPS):
            kvp[g].append(bp_new[g])
            kvs[g].append(bs_new[g])
    return (hp, hs, jnp.stack(gla_p), jnp.stack(gla_s),
            jnp.stack(kvp[0]), jnp.stack(kvp[1]), jnp.stack(kvp[2]),
            jnp.stack(kvs[0]), jnp.stack(kvs[1]), jnp.stack(kvs[2]))
```

```python
import functools

import jax
import jax.numpy as jnp
from jax import lax
from jax.experimental import pallas as pl
from jax.experimental.pallas import tpu as pltpu

F32 = jnp.float32
BF16 = jnp.bfloat16

NORM_EPS = 1e-6
GLA_HEADS = 4
GLA_DK = 128
GLA_DV = 256
GLA_KEY = GLA_HEADS * GLA_DK
GLA_VAL = GLA_HEADS * GLA_DV
GLA_LOWRANK = 16
GLA_TAU = 16.0
GLA_CHUNK = 64
DIL_GROUPS = ((128, 1), (512, 4), (2048, 16))
DIL_HEADS = 4
DIL_HD = 128
DIL_WIDTH = DIL_HEADS * DIL_HD
DIL_SPAN = 128
N_GROUPS = len(DIL_GROUPS)
GLA_COLS = 2 * GLA_KEY + 2 * GLA_VAL
QKV_COLS = 3 * DIL_WIDTH
KV_COLS = 2 * DIL_WIDTH

LANES = 128
LSE_LANES_PER_HEAD = LANES // DIL_HEADS
_LOG2_LSE_LANES = LSE_LANES_PER_HEAD.bit_length() - 1
_LOG2_DIL_HD = DIL_HD.bit_length() - 1
assert 1 << _LOG2_LSE_LANES == LSE_LANES_PER_HEAD and 1 << _LOG2_DIL_HD == DIL_HD
VMEM_LIMIT_BYTES = 56 * 1024 * 1024

_NT = (((1,), (1,)), ((), ()))
_TN = (((0,), (0,)), ((), ()))


def _dot(a, b, dims=None, precision=None):
    if dims is None:
        return jnp.dot(a, b, preferred_element_type=F32, precision=precision)
    return lax.dot_general(a, b, dims, preferred_element_type=F32, precision=precision)


def _rms(xf, g):
    return xf * lax.rsqrt(jnp.mean(xf * xf, axis=-1, keepdims=True) + NORM_EPS) * g


def _sigmoid(x):
    return 1.0 / (1.0 + jnp.exp(-x))


def _const_spec(shape):
    nd = len(shape)
    return pl.BlockSpec(shape, lambda *_: (0,) * nd, pipeline_mode=pl.Buffered(1))


def _params(*sem):
    return pltpu.CompilerParams(dimension_semantics=sem, vmem_limit_bytes=VMEM_LIMIT_BYTES)


def _proj_gla_kernel(x_ref, g_ref, w_ref, wlr_ref, wlr2_ref, blr_ref, qkvg_ref, la_ref, *, col_chunk):
    xn = _rms(x_ref[...], g_ref[...]).astype(BF16)
    for c0 in range(0, GLA_COLS, col_chunk):
        qkvg_ref[:, c0:c0 + col_chunk] = _dot(xn, w_ref[:, c0:c0 + col_chunk]).astype(qkvg_ref.dtype)
    glr = _dot(xn, wlr_ref[...])
    z = _dot(glr.astype(BF16), wlr2_ref[...]) + blr_ref[...]
    log_sig = jnp.minimum(z, 0.0) - jnp.log1p(jnp.exp(-jnp.abs(z)))
    la_ref[...] = log_sig * (1.0 / GLA_TAU)


def _proj_gla(x2d, g_pre, w_a, w_lr, w_lr2, b_lr, *, tm, out_dtype):
    n, d = x2d.shape
    return pl.pallas_call(
        functools.partial(_proj_gla_kernel, col_chunk=512),
        grid=(n // tm,),
        in_specs=[
            pl.BlockSpec((tm, d), lambda i: (i, 0)),
            _const_spec(g_pre.shape),
            _const_spec(w_a.shape),
            _const_spec(w_lr.shape),
            _const_spec(w_lr2.shape),
            _const_spec(b_lr.shape),
        ],
        out_specs=[
            pl.BlockSpec((tm, GLA_COLS), lambda i: (i, 0)),
            pl.BlockSpec((tm, GLA_KEY), lambda i: (i, 0)),
        ],
        out_shape=[
            jax.ShapeDtypeStruct((n, GLA_COLS), out_dtype),
            jax.ShapeDtypeStruct((n, GLA_KEY), F32),
        ],
        compiler_params=_params("parallel"),
        name="proj_gla",
    )(x2d, g_pre, w_a, w_lr, w_lr2, b_lr)


def _proj_rest_kernel(x_ref, g_ref, w_ref, *refs, tm, tiles_per_seq, keeps, prompt):
    a_refs = refs[:N_GROUPS]
    gates_ref = refs[N_GROUPS]
    kv_refs = refs[N_GROUPS + 1:2 * N_GROUPS + 1]
    stage_refs = refs[2 * N_GROUPS + 1:]

    def body(with_cache_rows):
        xn = _rms(x_ref[...], g_ref[...]).astype(BF16)
        n_staged = 0
        for g in range(N_GROUPS):
            base = g * QKV_COLS
            dil = DIL_GROUPS[g][1]
            for part in range(3):
                c0 = part * DIL_WIDTH
                r = _dot(xn, w_ref[:, base + c0:base + c0 + DIL_WIDTH])
                if not prompt:
                    a_refs[g][:, c0:c0 + DIL_WIDTH] = r
                elif dil == 1:
                    a_refs[g][0, :, c0:c0 + DIL_WIDTH] = r.astype(BF16)
                else:
                    stage = stage_refs[n_staged % len(stage_refs)]
                    n_staged += 1
                    for s in range(DIL_WIDTH // LANES):
                        stage[s] = r[:, s * LANES:(s + 1) * LANES]
                    for res in range(dil):
                        for s in range(DIL_WIDTH // LANES):
                            a_refs[g][res, :, c0 + s * LANES:c0 + (s + 1) * LANES] = (
                                stage[s, pl.ds(res, tm // dil, stride=dil), :].astype(BF16))
                if part > 0 and with_cache_rows:
                    rows = min(keeps[g], tm)
                    for h in range(DIL_HEADS):
                        kv_refs[g][pl.ds((part - 1) * DIL_HEADS + h, rows, stride=2 * DIL_HEADS), :] = (
                            r[tm - rows:, h * DIL_HD:(h + 1) * DIL_HD])
        gbase = N_GROUPS * QKV_COLS
        gcols = gates_ref.shape[1]
        for c0 in range(0, gcols, DIL_WIDTH):
            gates_ref[:, c0:c0 + DIL_WIDTH] = _dot(xn, w_ref[:, gbase + c0:gbase + c0 + DIL_WIDTH]).astype(gates_ref.dtype)

    if not prompt:
        body(True)
        return
    t = pl.program_id(0) % tiles_per_seq
    in_kept_range = t >= tiles_per_seq - max(1, max(keeps) // tm)
    pl.when(in_kept_range)(lambda: body(True))
    pl.when(jnp.logical_not(in_kept_range))(lambda: body(False))


def _proj_rest(x2d, g_pre, w_r, *, tm, seq, prompt):
    n, d = x2d.shape
    nb = n // seq
    tiles_per_seq = seq // tm if prompt else None
    gate_cols = w_r.shape[1] - N_GROUPS * QKV_COLS
    keeps = tuple(min(w, seq) for (w, _) in DIL_GROUPS) if prompt else (tm,) * N_GROUPS
    rows_per_token = 2 * DIL_HEADS
    out_specs, out_shape, scratch = [], [], []
    if prompt:
        for _, dil in DIL_GROUPS:
            out_specs.append(pl.BlockSpec((None, dil, tm // dil, QKV_COLS),
                                          lambda i: (i // tiles_per_seq, 0, i % tiles_per_seq, 0)))
            out_shape.append(jax.ShapeDtypeStruct((nb, dil, seq // dil, QKV_COLS), BF16))
    else:
        out_specs += [pl.BlockSpec((tm, QKV_COLS), lambda i: (i, 0)) for _ in range(N_GROUPS)]
        out_shape += [jax.ShapeDtypeStruct((n, QKV_COLS), F32) for _ in range(N_GROUPS)]
    out_specs.append(pl.BlockSpec((tm, gate_cols), lambda i: (i, 0)))
    out_shape.append(jax.ShapeDtypeStruct((n, gate_cols), BF16 if prompt else F32))
    if not prompt:
        for _ in range(N_GROUPS):
            out_specs.append(pl.BlockSpec((tm * rows_per_token, DIL_HD), lambda i: (i, 0)))
            out_shape.append(jax.ShapeDtypeStruct((n * rows_per_token, DIL_HD), F32))
    else:
        for keep in keeps:
            if keep >= tm:
                assert keep % tm == 0
                kt = keep // tm

                def idx(i, kt=kt):
                    b, t = i // tiles_per_seq, i % tiles_per_seq
                    return (b * kt + jnp.maximum(t - (tiles_per_seq - kt), 0), 0)

                out_specs.append(pl.BlockSpec((tm * rows_per_token, DIL_HD), idx))
            else:
                out_specs.append(pl.BlockSpec((keep * rows_per_token, DIL_HD), lambda i: (i // tiles_per_seq, 0)))
            out_shape.append(jax.ShapeDtypeStruct((nb * keep * rows_per_token, DIL_HD), F32))
        n_stage = 3 * sum(1 for _, dil in DIL_GROUPS if dil > 1)
        scratch = [pltpu.VMEM((DIL_WIDTH // LANES, tm, LANES), F32)] * n_stage
    return pl.pallas_call(
        functools.partial(_proj_rest_kernel, tm=tm, tiles_per_seq=tiles_per_seq, keeps=keeps, prompt=prompt),
        grid=(n // tm,),
        in_specs=[
            pl.BlockSpec((tm, d), lambda i: (i, 0)),
            _const_spec(g_pre.shape),
            _const_spec(w_r.shape),
        ],
        out_specs=out_specs,
        out_shape=out_shape,
        scratch_shapes=scratch,
        compiler_params=_params("arbitrary"),
        name="proj_rest",
    )(x2d, g_pre, w_r)


def _gla_kernel(qkvg_ref, la_ref, s0_ref, gn_ref, ya_ref, sfin_ref, st_ref, *, chunk, nchunk, nseq, mm_dtype, precision):
    j = pl.program_id(1)

    @pl.when(j == 0)
    def _():
        for s in range(nseq):
            for h in range(GLA_HEADS):
                st_ref[s * GLA_HEADS + h] = s0_ref[s, h].T

    _gla_compute(lambda r, c: qkvg_ref[r, c], lambda r: la_ref[r, :], st_ref, gn_ref, ya_ref,
                 chunk=chunk, nchunk=nchunk, nseq=nseq, mm_dtype=mm_dtype, precision=precision)

    @pl.when(j == pl.num_programs(1) - 1)
    def _():
        for s in range(nseq):
            for h in range(GLA_HEADS):
                sfin_ref[s, h] = st_ref[s * GLA_HEADS + h].T


def _gla_compute(*args, **kwargs):
    for _ in _gla_phases(*args, **kwargs):
        pass


def _interleave(*generators):
    live = list(generators)
    while live:
        for gen in list(live):
            if next(gen, StopIteration) is StopIteration:
                live.remove(gen)


def _gla_phases(load_qkvg, load_la, st_ref, gn_ref, ya_ref, *, chunk, nchunk, nseq, mm_dtype, precision):
    row = lax.broadcasted_iota(jnp.int32, (chunk, chunk), 0)
    col = lax.broadcasted_iota(jnp.int32, (chunk, chunk), 1)
    causal = row >= col
    tril = causal.astype(F32)
    qscale = GLA_DK ** -0.5

    segs = [(s, c) for s in range(nseq) for c in range(nchunk)]
    units = [(s, c, h) for s, c in segs for h in range(GLA_HEADS)]

    def rows_of(s, c):
        r0 = (s * nchunk + c) * chunk
        return slice(r0, r0 + chunk)

    def key_cols(h):
        return slice(h * GLA_DK, (h + 1) * GLA_DK)

    def v_of(s, c, h):
        return load_qkvg(rows_of(s, c), slice(2 * GLA_KEY + h * GLA_DV, 2 * GLA_KEY + (h + 1) * GLA_DV)).astype(mm_dtype)

    def cumsum(a):
        if mm_dtype != BF16:
            return _dot(tril, a, precision=lax.Precision.HIGHEST)
        hi = a.astype(BF16)
        rest = a - hi.astype(F32)
        mid = rest.astype(BF16)
        lo = (rest - mid.astype(F32)).astype(BF16)
        parts = _dot(tril.astype(BF16), jnp.concatenate([hi, mid, lo], axis=1))
        return parts[:, :GLA_KEY] + parts[:, GLA_KEY:2 * GLA_KEY] + parts[:, 2 * GLA_KEY:]

    cum = {sc: cumsum(load_la(rows_of(*sc))) for sc in segs}
    b_end = {sc: cum[sc][chunk - 1:chunk, :] for sc in segs}
    e_b = {sc: jnp.exp(cum[sc]) for sc in segs}
    e_nb = {sc: jnp.exp(-cum[sc]) for sc in segs}
    e_rest = {sc: jnp.exp(b_end[sc] - cum[sc]) for sc in segs}
    e_end = {sc: jnp.exp(b_end[sc]) for sc in segs}
    yield
    q_in, k_dec, k_end = {}, {}, {}
    for s, c, h in units:
        q = load_qkvg(rows_of(s, c), key_cols(h)).astype(F32) * qscale
        k = load_qkvg(rows_of(s, c), slice(GLA_KEY + h * GLA_DK, GLA_KEY + (h + 1) * GLA_DK)).astype(F32)
        q_in[s, c, h] = (q * e_b[s, c][:, key_cols(h)]).astype(mm_dtype)
        k_dec[s, c, h] = (k * e_nb[s, c][:, key_cols(h)]).astype(mm_dtype)
        k_end[s, c, h] = (k * e_rest[s, c][:, key_cols(h)]).astype(mm_dtype)
    yield
    att = {u: jnp.where(causal, _dot(q_in[u], k_dec[u], _NT, precision), 0.0).astype(mm_dtype) for u in units}
    yield
    o_intra = {u: _dot(att[u], v_of(*u), None, precision) for u in units}
    yield
    d_state = {u: _dot(v_of(*u), k_end[u], _TN, precision) for u in units}
    yield
    o_inter = {}
    for s in range(nseq):
        for h in range(GLA_HEADS):
            st = st_ref[s * GLA_HEADS + h]
            for c in range(nchunk):
                o_inter[s, c, h] = _dot(q_in[s, c, h], st.astype(mm_dtype), _NT, precision)
                st = st * e_end[s, c][:, key_cols(h)] + d_state[s, c, h]
            st_ref[s * GLA_HEADS + h] = st
    yield
    for i, (s, c, h) in enumerate(units):
        vs = slice(h * GLA_DV, (h + 1) * GLA_DV)
        gg = load_qkvg(rows_of(s, c), slice(2 * GLA_KEY + GLA_VAL + h * GLA_DV,
                                            2 * GLA_KEY + GLA_VAL + (h + 1) * GLA_DV)).astype(F32)
        on = _rms(o_intra[s, c, h] + o_inter[s, c, h], gn_ref[:, vs])
        ya_ref[rows_of(s, c), vs] = (on * (gg * _sigmoid(gg))).astype(ya_ref.dtype)
        if i % 8 == 7:
            yield


def _proj_gla_fused_kernel(x_ref, g_ref, w_ref, wlr_ref, wlr2_ref, blr_ref, s0_ref, gn_ref, *refs,
                           steps, chunk, col_chunk, n_shift, pieces, n_chunks):
    old_refs = refs[:n_shift]
    next_refs = refs[n_shift:2 * n_shift]
    tail_refs = refs[2 * n_shift:3 * n_shift]
    ya_ref, sfin_ref = refs[3 * n_shift:3 * n_shift + 2]
    new_refs = refs[3 * n_shift + 2:4 * n_shift + 2]
    qkvg_s0, qkvg_s1, la_s0, la_s1, st_ref = refs[4 * n_shift + 2:]
    j = pl.program_id(0)
    prev = j - 1

    if n_shift:
        last_piece = jnp.minimum(j, n_chunks - 1) % pieces == pieces - 1
        for old, nxt, tail, new in zip(old_refs, next_refs, tail_refs, new_refs):
            p_rows, t_rows = old.shape[0], tail.shape[0]
            new[0:p_rows - t_rows, :] = old[t_rows:p_rows, :]
            new[p_rows - t_rows:p_rows, :] = jnp.where(last_piece, tail[...], nxt[...])

    @pl.when(j == 0)
    def _():
        qkvg_s1[...] = jnp.zeros_like(qkvg_s1)
        la_s1[...] = jnp.zeros_like(la_s1)

    @pl.when(jnp.logical_or(j == 0, prev % steps == 0))
    def _():
        for h in range(GLA_HEADS):
            st_ref[h] = s0_ref[0, h].T

    def project(qkvg_w, la_w):
        xn = _rms(x_ref[...], g_ref[...]).astype(BF16)
        yield
        for c0 in range(0, GLA_COLS, col_chunk):
            qkvg_w[:, c0:c0 + col_chunk] = _dot(xn, w_ref[:, c0:c0 + col_chunk]).astype(qkvg_w.dtype)
            yield
        glr = _dot(xn, wlr_ref[...])
        z = _dot(glr.astype(BF16), wlr2_ref[...]) + blr_ref[...]
        log_sig = jnp.minimum(z, 0.0) - jnp.log1p(jnp.exp(-jnp.abs(z)))
        la_w[...] = log_sig * (1.0 / GLA_TAU)

    def step(qkvg_w, la_w, qkvg_r, la_r):
        _interleave(
            project(qkvg_w, la_w),
            _gla_phases(lambda r, c: qkvg_r[r, c], lambda r: la_r[r, :], st_ref, gn_ref, ya_ref,
                        chunk=chunk, nchunk=x_ref.shape[0] // chunk, nseq=1, mm_dtype=BF16, precision=None))

    @pl.when(j % 2 == 0)
    def _():
        step(qkvg_s0, la_s0, qkvg_s1, la_s1)

    @pl.when(j % 2 == 1)
    def _():
        step(qkvg_s1, la_s1, qkvg_s0, la_s0)

    @pl.when(jnp.logical_and(j >= 1, prev % steps == steps - 1))
    def _():
        for h in range(GLA_HEADS):
            sfin_ref[0, h] = st_ref[h].T


def _proj_gla_fused(x2d, g_pre, w_a, w_lr, w_lr2, b_lr, s0, gn, *, tm, seq, chunk, shift_src=(), shift_tail=()):
    n, d = x2d.shape
    nb = n // seq
    steps = seq // tm
    n_tiles = n // tm
    n_shift = len(shift_src)
    pieces = n_chunks = 1
    copy_in, copy_out = [], []
    if n_shift:
        entries = shift_src[0].shape[0]
        pieces = max(1, n_tiles // entries)
        n_chunks = entries * pieces
        assert n_chunks <= n_tiles + 1

        def piece_of(j):
            c = jnp.minimum(j, n_chunks - 1)
            return c // pieces, c % pieces

        next_specs, tail_specs = [], []
        for src, tail in zip(shift_src, shift_tail):
            t_rows = tail.shape[1]
            p_rows = src.shape[1] // pieces
            assert src.shape[0] == tail.shape[0] == entries and src.shape[2] == tail.shape[2] == LANES
            assert src.shape[1] % pieces == 0 and p_rows % t_rows == 0 and t_rows % 8 == 0
            piece_spec = pl.BlockSpec((None, p_rows, LANES), lambda j: (*piece_of(j), 0))
            last_t_block = src.shape[1] // t_rows - 1

            def next_idx(j, per_piece=p_rows // t_rows, last=last_t_block):
                entry, piece = piece_of(j)
                return (entry, jnp.minimum((piece + 1) * per_piece, last), 0)

            copy_in.append(piece_spec)
            next_specs.append(pl.BlockSpec((None, t_rows, LANES), next_idx))
            tail_specs.append(pl.BlockSpec((None, t_rows, LANES), lambda j: (piece_of(j)[0], 0, 0)))
            copy_out.append(piece_spec)
        copy_in += next_specs + tail_specs

    def prev_tile(j):
        return jnp.maximum(j - 1, 0)

    outs = pl.pallas_call(
        functools.partial(_proj_gla_fused_kernel, steps=steps, chunk=chunk, col_chunk=512, n_shift=n_shift,
                          pieces=pieces, n_chunks=n_chunks),
        grid=(n_tiles + 1,),
        in_specs=[
            pl.BlockSpec((tm, d), lambda j: (jnp.minimum(j, n_tiles - 1), 0)),
            _const_spec(g_pre.shape),
            _const_spec(w_a.shape),
            _const_spec(w_lr.shape),
            _const_spec(w_lr2.shape),
            _const_spec(b_lr.shape),
            pl.BlockSpec((1, GLA_HEADS, GLA_DK, GLA_DV), lambda j: (prev_tile(j) // steps, 0, 0, 0)),
            _const_spec(gn.shape),
        ] + copy_in,
        out_specs=[
            pl.BlockSpec((tm, GLA_VAL), lambda j: (prev_tile(j), 0)),
            pl.BlockSpec((1, GLA_HEADS, GLA_DK, GLA_DV), lambda j: (prev_tile(j) // steps, 0, 0, 0)),
        ] + copy_out,
        out_shape=[
            jax.ShapeDtypeStruct((n, GLA_VAL), BF16),
            jax.ShapeDtypeStruct((nb, GLA_HEADS, GLA_DK, GLA_DV), F32),
        ] + [jax.ShapeDtypeStruct(src.shape, src.dtype) for src in shift_src],
        scratch_shapes=[
            pltpu.VMEM((tm, GLA_COLS), BF16),
            pltpu.VMEM((tm, GLA_COLS), BF16),
            pltpu.VMEM((tm, GLA_KEY), F32),
            pltpu.VMEM((tm, GLA_KEY), F32),
            pltpu.VMEM((GLA_HEADS, GLA_DV, GLA_DK), F32),
        ],
        compiler_params=_params("arbitrary"),
        name="proj_gla_fused",
    )(x2d, g_pre, w_a, w_lr, w_lr2, b_lr, s0, gn, *shift_src, *shift_src, *shift_tail)
    return outs[0], outs[1], list(outs[2:])


def _gla(qkvg, la, s0, gn, *, nb, seq, chunk, rows_per_step, nseq, out_dtype, mm_dtype, precision):
    steps = seq // rows_per_step
    assert nb % nseq == 0 and (nseq == 1 or steps == 1)
    rows = nseq * rows_per_step
    return pl.pallas_call(
        functools.partial(_gla_kernel, chunk=chunk, nchunk=rows_per_step // chunk, nseq=nseq, mm_dtype=mm_dtype,
                          precision=precision),
        grid=(nb // nseq, steps),
        in_specs=[
            pl.BlockSpec((rows, GLA_COLS), lambda b, j: (b * steps + j, 0)),
            pl.BlockSpec((rows, GLA_KEY), lambda b, j: (b * steps + j, 0)),
            pl.BlockSpec((nseq, GLA_HEADS, GLA_DK, GLA_DV), lambda b, j: (b, 0, 0, 0)),
            pl.BlockSpec((1, GLA_VAL), lambda b, j: (0, 0)),
        ],
        out_specs=[
            pl.BlockSpec((rows, GLA_VAL), lambda b, j: (b * steps + j, 0)),
            pl.BlockSpec((nseq, GLA_HEADS, GLA_DK, GLA_DV), lambda b, j: (b, 0, 0, 0)),
        ],
        out_shape=[
            jax.ShapeDtypeStruct((nb * seq, GLA_VAL), out_dtype),
            jax.ShapeDtypeStruct((nb, GLA_HEADS, GLA_DK, GLA_DV), F32),
        ],
        scratch_shapes=[pltpu.VMEM((nseq * GLA_HEADS, GLA_DV, GLA_DK), F32)],
        compiler_params=_params("parallel", "arbitrary"),
        name="gla",
    )(qkvg, la, s0, gn)


def _pack_lse(lses):
    rows = lses[0].shape[0]
    lane_head = lax.broadcasted_iota(jnp.int32, (rows, LANES), 1) >> _LOG2_LSE_LANES
    packed = jnp.broadcast_to(lses[0], (rows, LANES))
    for h in range(1, DIL_HEADS):
        packed = jnp.where(lane_head == h, lses[h], packed)
    return packed


def _dil_prompt_kernel(a_ref, o_ref, lse_ref, kvp_ref, *, nres, nblk):
    first = pl.program_id(2) == 0

    @pl.when(first)
    def _():
        kvp_ref[...] = jnp.zeros_like(kvp_ref)

    i_idx = lax.broadcasted_iota(jnp.int32, (DIL_SPAN, 2 * DIL_SPAN), 0)
    j_idx = lax.broadcasted_iota(jnp.int32, (DIL_SPAN, 2 * DIL_SPAN), 1)
    diff = j_idx - i_idx
    band = jnp.logical_and(diff >= 0, diff <= DIL_SPAN)
    band_first = jnp.logical_and(diff >= jnp.where(first, DIL_SPAN - i_idx, 0), diff <= DIL_SPAN)
    scale = DIL_HD ** -0.5
    neg_inf = -jnp.inf
    units = [(res, jb, h) for res in range(nres) for jb in range(nblk) for h in range(DIL_HEADS)]

    def window(res, jb, c0):
        if jb == 0:
            prev = kvp_ref[res, :, c0 - DIL_WIDTH:c0 - DIL_WIDTH + DIL_HD]
            return jnp.concatenate([prev, a_ref[res, 0:DIL_SPAN, c0:c0 + DIL_HD]], axis=0)
        return a_ref[res, (jb - 1) * DIL_SPAN:(jb + 1) * DIL_SPAN, c0:c0 + DIL_HD]

    scores = []
    for res, jb, h in units:
        q = a_ref[res, jb * DIL_SPAN:(jb + 1) * DIL_SPAN, h * DIL_HD:(h + 1) * DIL_HD]
        s = _dot(q, window(res, jb, DIL_WIDTH + h * DIL_HD), _NT) * scale
        scores.append(jnp.where(band_first if jb == 0 else band, s, neg_inf))
    maxes = [jnp.max(s, axis=-1, keepdims=True) for s in scores]
    probs = [jnp.exp(s - m) for s, m in zip(scores, maxes)]
    sums = [jnp.sum(p, axis=-1, keepdims=True) for p in probs]
    lses = {}
    for (res, jb, h), p, m, l in zip(units, probs, maxes, sums):
        acc = _dot(p.astype(BF16), window(res, jb, 2 * DIL_WIDTH + h * DIL_HD))
        o_ref[res, jb * DIL_SPAN:(jb + 1) * DIL_SPAN, h * DIL_HD:(h + 1) * DIL_HD] = acc * (1.0 / l)
        lses[res, jb, h] = m + jnp.log(l)
    for res in range(nres):
        for jb in range(nblk):
            lse_ref[res, jb * DIL_SPAN:(jb + 1) * DIL_SPAN, :] = _pack_lse([lses[res, jb, h] for h in range(DIL_HEADS)])
        kvp_ref[res] = a_ref[res, (nblk - 1) * DIL_SPAN:nblk * DIL_SPAN, DIL_WIDTH:QKV_COLS]


def _dil_prompt(a4, *, rows_per_step=1024):
    nb, dil, length, _ = a4.shape
    rows = min(rows_per_step, length)
    nres = min(dil, rows_per_step // rows)

    def spec(cols):
        return pl.BlockSpec((None, nres, rows, cols), lambda b, r, n: (b, r, n, 0))

    return pl.pallas_call(
        functools.partial(_dil_prompt_kernel, nres=nres, nblk=rows // DIL_SPAN),
        grid=(nb, dil // nres, length // rows),
        in_specs=[spec(QKV_COLS)],
        out_specs=[spec(DIL_WIDTH), spec(LANES)],
        out_shape=[
            jax.ShapeDtypeStruct((nb, dil, length, DIL_WIDTH), F32),
            jax.ShapeDtypeStruct((nb, dil, length, LANES), F32),
        ],
        scratch_shapes=[pltpu.VMEM((nres, DIL_SPAN, KV_COLS), BF16)],
        compiler_params=_params("parallel", "parallel", "arbitrary"),
        name=f"dil_prompt_d{dil}",
    )(a4)


def _dil_sample_kernel(*refs, t_new):
    a_refs, cache_refs = refs[:N_GROUPS], refs[N_GROUPS:2 * N_GROUPS]
    o_refs, lse_refs = refs[2 * N_GROUPS:3 * N_GROUPS], refs[3 * N_GROUPS:]
    rpt = 2 * DIL_HEADS
    scale = DIL_HD ** -0.5
    neg_inf = -jnp.inf
    tq_n = lax.broadcasted_iota(jnp.int32, (t_new, t_new), 0)
    j_n = lax.broadcasted_iota(jnp.int32, (t_new, t_new), 1)

    def cache_head(g, c):
        ref = cache_refs[g]
        if len(ref.shape) == 2:
            return ref[pl.ds(c, ref.shape[0] // rpt, stride=rpt), :].astype(BF16)
        x = ref[:, pl.ds(c, t_new, stride=rpt), :]
        return x.reshape(ref.shape[0] * t_new, DIL_HD).astype(BF16)

    valid_c, valid_n = [], []
    for g, (_, dil) in enumerate(DIL_GROUPS):
        ref = cache_refs[g]
        n_keys = ref.shape[0] // rpt if len(ref.shape) == 2 else ref.shape[0] * t_new
        tq = lax.broadcasted_iota(jnp.int32, (t_new, n_keys), 0)
        key = lax.broadcasted_iota(jnp.int32, (t_new, n_keys), 1)
        if len(ref.shape) == 2:
            valid_c.append(jnp.logical_and(((key - tq) & (dil - 1)) == 0, key >= tq))
        else:
            valid_c.append((key & (t_new - 1)) == tq)
        valid_n.append(jnp.logical_and(((tq_n - j_n) & (dil - 1)) == 0, j_n <= tq_n))

    units = [(g, h) for g in range(N_GROUPS) for h in range(DIL_HEADS)]
    scores = []
    for g, h in units:
        q = a_refs[g][:, h * DIL_HD:(h + 1) * DIL_HD].astype(BF16)
        k_new = a_refs[g][:, DIL_WIDTH + h * DIL_HD:DIL_WIDTH + (h + 1) * DIL_HD].astype(BF16)
        s_c = jnp.where(valid_c[g], _dot(q, cache_head(g, h), _NT) * scale, neg_inf)
        s_n = jnp.where(valid_n[g], _dot(q, k_new, _NT) * scale, neg_inf)
        scores.append((s_c, s_n))
    maxes = [jnp.maximum(jnp.max(s_c, axis=-1, keepdims=True), jnp.max(s_n, axis=-1, keepdims=True))
             for s_c, s_n in scores]
    probs = [(jnp.exp(s_c - m), jnp.exp(s_n - m)) for (s_c, s_n), m in zip(scores, maxes)]
    sums = [jnp.sum(p_c, axis=-1, keepdims=True) + jnp.sum(p_n, axis=-1, keepdims=True) for p_c, p_n in probs]
    lses = {}
    for (g, h), (p_c, p_n), m, l in zip(units, probs, maxes, sums):
        v_new = a_refs[g][:, 2 * DIL_WIDTH + h * DIL_HD:2 * DIL_WIDTH + (h + 1) * DIL_HD].astype(BF16)
        acc = _dot(p_c.astype(BF16), cache_head(g, DIL_HEADS + h)) + _dot(p_n.astype(BF16), v_new)
        o_refs[g][:, h * DIL_HD:(h + 1) * DIL_HD] = acc * (1.0 / l)
        lses[g, h] = m + jnp.log(l)
    for g in range(N_GROUPS):
        lse_refs[g][...] = _pack_lse([lses[g, h] for h in range(DIL_HEADS)])


def _cache_rows(cache):
    nb, lb = cache.shape[:2]
    return cache.reshape(nb, lb * 2 * DIL_HEADS, DIL_HD)


def _dil_sample(a_g, caches, *, t_new):
    nb = caches[0].shape[0]
    rpt = 2 * DIL_HEADS
    assert t_new & (t_new - 1) == 0
    cache_args, cache_specs = [], []
    for cache, (_, dil) in zip(caches, DIL_GROUPS):
        lb = cache.shape[1]
        assert lb == dil * DIL_SPAN
        if dil > t_new:
            assert dil % t_new == 0
            cache_args.append(cache.reshape(nb, lb // dil, dil * rpt, DIL_HD))
            cache_specs.append(pl.BlockSpec((None, lb // dil, t_new * rpt, DIL_HD), lambda b: (b, 0, 0, 0)))
        else:
            cache_args.append(_cache_rows(cache))
            cache_specs.append(pl.BlockSpec((None, lb * rpt, DIL_HD), lambda b: (b, 0, 0)))
    outs = pl.pallas_call(
        functools.partial(_dil_sample_kernel, t_new=t_new),
        grid=(nb,),
        in_specs=[pl.BlockSpec((t_new, QKV_COLS), lambda b: (b, 0))] * N_GROUPS + cache_specs,
        out_specs=[pl.BlockSpec((t_new, DIL_WIDTH), lambda b: (b, 0))] * N_GROUPS
        + [pl.BlockSpec((t_new, LANES), lambda b: (b, 0))] * N_GROUPS,
        out_shape=[jax.ShapeDtypeStruct((nb * t_new, DIL_WIDTH), F32)] * N_GROUPS
        + [jax.ShapeDtypeStruct((nb * t_new, LANES), F32)] * N_GROUPS,
        compiler_params=_params("parallel"),
        name="dil_sample",
    )(*a_g, *cache_args)
    return outs[:N_GROUPS], outs[N_GROUPS:]


def _token_order(ref, scratch_refs):
    if len(ref.shape) == 2:
        return ref[...]
    dil, rows_per_class, _ = ref.shape
    if dil == 1:
        return ref[0]
    scratch = scratch_refs.pop()
    n_slabs = scratch.shape[0]
    for res in range(dil):
        for s in range(n_slabs):
            scratch[s, pl.ds(res, rows_per_class, stride=dil), :] = ref[res, :, s * LANES:(s + 1) * LANES]
    return jnp.concatenate([scratch[s] for s in range(n_slabs)], axis=1) if n_slabs > 1 else scratch[0]


def _final_kernel(ya_ref, o0_ref, o1_ref, o2_ref, l0_ref, l1_ref, l2_ref, gates_ref, x_ref, pe_ref,
                  wa_ref, wb_ref, wo_ref, wpp_ref, wpg_ref, gpost_ref, y_ref, *scratch_refs):
    o_scratch = [s for s in scratch_refs if s.shape[0] == DIL_WIDTH // LANES]
    l_scratch = [s for s in scratch_refs if s.shape[0] == 1]
    lse = [_token_order(r, l_scratch) for r in (l0_ref, l1_ref, l2_ref)]
    o_groups = [_token_order(r, o_scratch) for r in (o0_ref, o1_ref, o2_ref)]
    m = jnp.maximum(jnp.maximum(lse[0], lse[1]), lse[2])
    e = [jnp.exp(x - m) for x in lse]
    inv = 1.0 / (e[0] + e[1] + e[2])
    src = lax.broadcasted_iota(jnp.int32, (LANES, DIL_WIDTH), 0)
    dst_head = lax.broadcasted_iota(jnp.int32, (LANES, DIL_WIDTH), 1) >> _LOG2_DIL_HD
    expand = (src == dst_head * LSE_LANES_PER_HEAD).astype(BF16)
    o_b = None
    for eg, o_g in zip(e, o_groups):
        w = eg * inv
        w_hi = w.astype(BF16)
        w_lo = (w - w_hi.astype(F32)).astype(BF16)
        w_full = _dot(w_hi, expand) + _dot(w_lo, expand)
        term = w_full * o_g
        o_b = term if o_b is None else o_b + term
    dg = gates_ref[:, 0:DIL_WIDTH].astype(F32)
    d = x_ref.shape[1]
    ga = gates_ref[:, DIL_WIDTH:DIL_WIDTH + d].astype(F32)
    gb = gates_ref[:, DIL_WIDTH + d:DIL_WIDTH + 2 * d].astype(F32)
    y_a = _dot(ya_ref[...].astype(BF16), wa_ref[...])
    y_b = _dot((o_b * (dg * _sigmoid(dg))).astype(BF16), wb_ref[...])
    merged = _sigmoid(ga) * y_a + _sigmoid(gb) * y_b
    h = x_ref[...] + _rms(_dot(merged.astype(BF16), wo_ref[...]), gpost_ref[...])
    ple = _dot(pe_ref[...].astype(BF16), wpp_ref[...])
    y_ref[...] = h + ple * _sigmoid(_dot(h.astype(BF16), wpg_ref[...]))


def _final(ya, os_, lses, gates, x2d, pe2d, wa, wb, wo, wpp, wpg, g_post, *, tm):
    n, d = x2d.shape

    def rows(cols):
        return pl.BlockSpec((tm, cols), lambda i: (i, 0))

    scratch = []

    def group_spec(arr):
        if arr.ndim == 2:
            return rows(arr.shape[1])
        _, dil, length, cols = arr.shape
        tiles_per_seq = dil * length // tm
        if dil > 1:
            scratch.append(pltpu.VMEM((cols // LANES, tm, LANES), F32))
        return pl.BlockSpec((None, dil, tm // dil, cols), lambda i: (i // tiles_per_seq, 0, i % tiles_per_seq, 0))

    return pl.pallas_call(
        _final_kernel,
        grid=(n // tm,),
        in_specs=[rows(GLA_VAL)] + [group_spec(a) for a in os_] + [group_spec(a) for a in lses]
        + [rows(gates.shape[1]), rows(d), rows(pe2d.shape[1])]
        + [_const_spec(w.shape) for w in (wa, wb, wo, wpp, wpg, g_post)],
        out_specs=rows(d),
        out_shape=jax.ShapeDtypeStruct((n, d), F32),
        scratch_shapes=scratch,
        compiler_params=_params("parallel"),
        name="final",
    )(ya, *os_, *lses, gates, x2d, pe2d, wa, wb, wo, wpp, wpg, g_post)


def _prep_weights(g_pre, g_post, w_in, w_gla_lr, b_gla_lr, g_gla_norm, w_gla_branch, w_dil_branch, w_out,
                  w_ple_proj, w_ple_gate):
    d = w_in.shape[0]
    c_lr = GLA_COLS
    c_q = c_lr + GLA_LOWRANK
    c_k = c_q + N_GROUPS * DIL_WIDTH
    c_v = c_k + N_GROUPS * DIL_WIDTH
    c_g = c_v + N_GROUPS * DIL_WIDTH
    w_a = w_in[:, :c_lr].astype(BF16)
    w_lr = jnp.pad(w_in[:, c_lr:c_q].astype(BF16), ((0, 0), (0, LANES - GLA_LOWRANK)))
    w_lr2 = jnp.pad(w_gla_lr.astype(BF16), ((0, LANES - GLA_LOWRANK), (0, 0)))
    parts = []
    for g in range(N_GROUPS):
        for c in (c_q, c_k, c_v):
            parts.append(w_in[:, c + g * DIL_WIDTH:c + (g + 1) * DIL_WIDTH])
    parts.append(w_in[:, c_g:])
    w_r = jnp.concatenate(parts, axis=1).astype(BF16)
    return dict(
        g_pre=g_pre.reshape(1, d), g_post=g_post.reshape(1, d), w_a=w_a, w_lr=w_lr, w_lr2=w_lr2, w_r=w_r,
        b_lr=b_gla_lr.reshape(1, GLA_KEY), gn=g_gla_norm.reshape(1, GLA_VAL),
        wa=w_gla_branch.astype(BF16), wb=w_dil_branch.astype(BF16), wo=w_out.astype(BF16),
        wpp=w_ple_proj.astype(BF16), wpg=w_ple_gate.astype(BF16),
    )


PROMPT_TILE = 512


def _prompt_layer(x, pe, s0, w, sample_caches, sample_new_rows):
    nb, seq, d = x.shape
    n = nb * seq
    tm = PROMPT_TILE
    assert seq % tm == 0 and seq % DIL_GROUPS[-1][0] == 0
    x2d = x.reshape(n, d)
    rest = _proj_rest(x2d, w["g_pre"], w["w_r"], tm=tm, seq=seq, prompt=True)
    a_g, gates = rest[:N_GROUPS], rest[N_GROUPS]
    ns = sample_caches[0].shape[0]
    ya, s_new, next_caches = _proj_gla_fused(
        x2d, w["g_pre"], w["w_a"], w["w_lr"], w["w_lr2"], w["b_lr"], s0, w["gn"], tm=tm, seq=seq, chunk=GLA_CHUNK,
        shift_src=tuple(_cache_rows(c) for c in sample_caches),
        shift_tail=tuple(r.reshape(ns, -1, DIL_HD) for r in sample_new_rows))
    new_bufs = [kv.reshape(nb, -1, 2, DIL_HEADS, DIL_HD) for kv in rest[N_GROUPS + 1:]]
    attn = [_dil_prompt(a) for a in a_g]
    y = _final(ya, [o for o, _ in attn], [lse for _, lse in attn], gates, x2d, pe.reshape(n, -1),
               w["wa"], w["wb"], w["wo"], w["wpp"], w["wpg"], w["g_post"], tm=tm)
    next_caches = [c.reshape(old.shape) for c, old in zip(next_caches, sample_caches)]
    return y.reshape(nb, seq, d), s_new, new_bufs, next_caches


def _sample_proj(x, w):
    nb, seq, d = x.shape
    return _proj_rest(x.reshape(nb * seq, d), w["g_pre"], w["w_r"], tm=nb * seq, seq=seq, prompt=False)


def _sample_layer(x, pe, s0, caches, w, rest):
    nb, seq, d = x.shape
    n = nb * seq
    x2d = x.reshape(n, d)
    a_g, gates = rest[:N_GROUPS], rest[N_GROUPS]
    chunk = min(GLA_CHUNK, seq)
    assert seq % chunk == 0
    nseq = 4 if nb % 4 == 0 else 1
    qkvg, la = _proj_gla(x2d, w["g_pre"], w["w_a"], w["w_lr"], w["w_lr2"], w["b_lr"], tm=n, out_dtype=F32)
    ya, s_new = _gla(qkvg, la, s0, w["gn"], nb=nb, seq=seq, chunk=chunk, rows_per_step=seq, nseq=nseq,
                     out_dtype=F32, mm_dtype=F32, precision=lax.Precision.HIGHEST)
    os_, lses = _dil_sample(a_g, caches, t_new=seq)
    y = _final(ya, os_, lses, gates, x2d, pe.reshape(n, -1),
               w["wa"], w["wb"], w["wo"], w["wpp"], w["wpg"], w["g_post"], tm=n)
    return y.reshape(nb, seq, d), s_new


def kernel(x_prompt, x_sample, state_gla, cache_kv_w128, cache_kv_w512, cache_kv_w2048, p_prompt, p_sample, g_pre, g_post, w_in, w_gla_lr, b_gla_lr, g_gla_norm, w_gla_branch, w_dil_branch, w_out, w_ple_proj, w_ple_gate):
    depth = w_in.shape[0]
    hp, hs = x_prompt, x_sample
    gla_p, gla_s = [], []
    kvp = [[] for _ in DIL_GROUPS]
    kvs = [[] for _ in DIL_GROUPS]
    for i in range(depth):
        w = _prep_weights(g_pre[i], g_post[i], w_in[i], w_gla_lr[i], b_gla_lr[i], g_gla_norm[i], w_gla_branch[i],
                          w_dil_branch[i], w_out[i], w_ple_proj[i], w_ple_gate[i])
        s0 = jnp.zeros((x_prompt.shape[0], GLA_HEADS, GLA_DK, GLA_DV), F32)
        caches = (cache_kv_w128[i], cache_kv_w512[i], cache_kv_w2048[i])
        rest_s = _sample_proj(hs, w)
        hp, sp_new, bp_new, bs_new = _prompt_layer(hp, p_prompt[i], s0, w, caches, rest_s[N_GROUPS + 1:])
        hs, ss_new = _sample_layer(hs, p_sample[i], state_gla[i], caches, w, rest_s)
        gla_p.append(sp_new)
        gla_s.append(ss_new)
        for g in range(N_GROUPS):
            kvp[g].append(bp_new[g])
            kvs[g].append(bs_new[g])
    return (hp, hs, jnp.stack(gla_p), jnp.stack(gla_s),
            jnp.stack(kvp[0]), jnp.stack(kvp[1]), jnp.stack(kvp[2]),
            jnp.stack(kvs[0]), jnp.stack(kvs[1]), jnp.stack(kvs[2]))
```

```python
import functools

import jax
import jax.numpy as jnp
from jax import lax
from jax.experimental import pallas as pl
from jax.experimental.pallas import tpu as pltpu

F32 = jnp.float32
BF16 = jnp.bfloat16

NORM_EPS = 1e-6
GLA_HEADS = 4
GLA_DK = 128
GLA_DV = 256
GLA_KEY = GLA_HEADS * GLA_DK
GLA_VAL = GLA_HEADS * GLA_DV
GLA_LOWRANK = 16
GLA_TAU = 16.0
GLA_CHUNK = 64
DIL_GROUPS = ((128, 1), (512, 4), (2048, 16))
DIL_HEADS = 4
DIL_HD = 128
DIL_WIDTH = DIL_HEADS * DIL_HD
DIL_SPAN = 128
N_GROUPS = len(DIL_GROUPS)
GLA_COLS = 2 * GLA_KEY + 2 * GLA_VAL
QKV_COLS = 3 * DIL_WIDTH
KV_COLS = 2 * DIL_WIDTH

LANES = 128
LSE_LANES_PER_HEAD = LANES // DIL_HEADS
_LOG2_LSE_LANES = LSE_LANES_PER_HEAD.bit_length() - 1
_LOG2_DIL_HD = DIL_HD.bit_length() - 1
assert 1 << _LOG2_LSE_LANES == LSE_LANES_PER_HEAD and 1 << _LOG2_DIL_HD == DIL_HD
VMEM_LIMIT_BYTES = 56 * 1024 * 1024

_NT = (((1,), (1,)), ((), ()))
_TN = (((0,), (0,)), ((), ()))


def _dot(a, b, dims=None, precision=None):
    if dims is None:
        return jnp.dot(a, b, preferred_element_type=F32, precision=precision)
    return lax.dot_general(a, b, dims, preferred_element_type=F32, precision=precision)


def _rms(xf, g):
    return xf * lax.rsqrt(jnp.mean(xf * xf, axis=-1, keepdims=True) + NORM_EPS) * g


def _sigmoid(x):
    return 1.0 / (1.0 + jnp.exp(-x))


def _const_spec(shape):
    nd = len(shape)
    return pl.BlockSpec(shape, lambda *_: (0,) * nd, pipeline_mode=pl.Buffered(1))


def _params(*sem):
    return pltpu.CompilerParams(dimension_semantics=sem, vmem_limit_bytes=VMEM_LIMIT_BYTES)


def _proj_gla_kernel(x_ref, g_ref, w_ref, wlr_ref, wlr2_ref, blr_ref, qkvg_ref, la_ref, *, col_chunk):
    xn = _rms(x_ref[...], g_ref[...]).astype(BF16)
    for c0 in range(0, GLA_COLS, col_chunk):
        qkvg_ref[:, c0:c0 + col_chunk] = _dot(xn, w_ref[c0:c0 + col_chunk, :], _NT).astype(qkvg_ref.dtype)
    glr = _dot(xn, wlr_ref[...], _NT)
    z = _dot(glr.astype(BF16), wlr2_ref[...]) + blr_ref[...]
    log_sig = jnp.minimum(z, 0.0) - jnp.log1p(jnp.exp(-jnp.abs(z)))
    la_ref[...] = log_sig * (1.0 / GLA_TAU)


def _proj_gla(x2d, g_pre, w_a, w_lr, w_lr2, b_lr, *, tm, out_dtype):
    n, d = x2d.shape
    return pl.pallas_call(
        functools.partial(_proj_gla_kernel, col_chunk=512),
        grid=(n // tm,),
        in_specs=[
            pl.BlockSpec((tm, d), lambda i: (i, 0)),
            _const_spec(g_pre.shape),
            _const_spec(w_a.shape),
            _const_spec(w_lr.shape),
            _const_spec(w_lr2.shape),
            _const_spec(b_lr.shape),
        ],
        out_specs=[
            pl.BlockSpec((tm, GLA_COLS), lambda i: (i, 0)),
            pl.BlockSpec((tm, GLA_KEY), lambda i: (i, 0)),
        ],
        out_shape=[
            jax.ShapeDtypeStruct((n, GLA_COLS), out_dtype),
            jax.ShapeDtypeStruct((n, GLA_KEY), F32),
        ],
        compiler_params=_params("parallel"),
        name="proj_gla",
    )(x2d, g_pre, w_a, w_lr, w_lr2, b_lr)


def _proj_rest_kernel(x_ref, g_ref, w_ref, *refs, tm, tiles_per_seq, keeps, prompt):
    a_refs = refs[:N_GROUPS]
    gates_ref = refs[N_GROUPS]
    kv_refs = refs[N_GROUPS + 1:2 * N_GROUPS + 1]
    stage_refs = refs[2 * N_GROUPS + 1:]

    def body(with_cache_rows):
        xn = _rms(x_ref[...], g_ref[...]).astype(BF16)
        n_staged = 0
        for g in range(N_GROUPS):
            base = g * QKV_COLS
            dil = DIL_GROUPS[g][1]
            for part in range(3):
                c0 = part * DIL_WIDTH
                r = _dot(xn, w_ref[base + c0:base + c0 + DIL_WIDTH, :], _NT)
                if not prompt:
                    a_refs[g][:, c0:c0 + DIL_WIDTH] = r
                elif dil == 1:
                    a_refs[g][0, :, c0:c0 + DIL_WIDTH] = r.astype(BF16)
                else:
                    stage = stage_refs[n_staged % len(stage_refs)]
                    n_staged += 1
                    for s in range(DIL_WIDTH // LANES):
                        stage[s] = r[:, s * LANES:(s + 1) * LANES]
                    for res in range(dil):
                        for s in range(DIL_WIDTH // LANES):
                            a_refs[g][res, :, c0 + s * LANES:c0 + (s + 1) * LANES] = (
                                stage[s, pl.ds(res, tm // dil, stride=dil), :].astype(BF16))
                if part > 0 and with_cache_rows:
                    rows = min(keeps[g], tm)
                    for h in range(DIL_HEADS):
                        kv_refs[g][pl.ds((part - 1) * DIL_HEADS + h, rows, stride=2 * DIL_HEADS), :] = (
                            r[tm - rows:, h * DIL_HD:(h + 1) * DIL_HD])
        gbase = N_GROUPS * QKV_COLS
        gcols = gates_ref.shape[1]
        for c0 in range(0, gcols, DIL_WIDTH):
            gates_ref[:, c0:c0 + DIL_WIDTH] = _dot(xn, w_ref[gbase + c0:gbase + c0 + DIL_WIDTH, :], _NT).astype(gates_ref.dtype)

    if not prompt:
        body(True)
        return
    t = pl.program_id(0) % tiles_per_seq
    in_kept_range = t >= tiles_per_seq - max(1, max(keeps) // tm)
    pl.when(in_kept_range)(lambda: body(True))
    pl.when(jnp.logical_not(in_kept_range))(lambda: body(False))


def _proj_rest(x2d, g_pre, w_r, *, tm, seq, prompt):
    n, d = x2d.shape
    nb = n // seq
    tiles_per_seq = seq // tm if prompt else None
    gate_cols = w_r.shape[0] - N_GROUPS * QKV_COLS
    keeps = tuple(min(w, seq) for (w, _) in DIL_GROUPS) if prompt else (tm,) * N_GROUPS
    rows_per_token = 2 * DIL_HEADS
    out_specs, out_shape, scratch = [], [], []
    if prompt:
        for _, dil in DIL_GROUPS:
            out_specs.append(pl.BlockSpec((None, dil, tm // dil, QKV_COLS),
                                          lambda i: (i // tiles_per_seq, 0, i % tiles_per_seq, 0)))
            out_shape.append(jax.ShapeDtypeStruct((nb, dil, seq // dil, QKV_COLS), BF16))
    else:
        out_specs += [pl.BlockSpec((tm, QKV_COLS), lambda i: (i, 0)) for _ in range(N_GROUPS)]
        out_shape += [jax.ShapeDtypeStruct((n, QKV_COLS), F32) for _ in range(N_GROUPS)]
    out_specs.append(pl.BlockSpec((tm, gate_cols), lambda i: (i, 0)))
    out_shape.append(jax.ShapeDtypeStruct((n, gate_cols), BF16 if prompt else F32))
    if not prompt:
        for _ in range(N_GROUPS):
            out_specs.append(pl.BlockSpec((tm * rows_per_token, DIL_HD), lambda i: (i, 0)))
            out_shape.append(jax.ShapeDtypeStruct((n * rows_per_token, DIL_HD), F32))
    else:
        for keep in keeps:
            if keep >= tm:
                assert keep % tm == 0
                kt = keep // tm

                def idx(i, kt=kt):
                    b, t = i // tiles_per_seq, i % tiles_per_seq
                    return (b * kt + jnp.maximum(t - (tiles_per_seq - kt), 0), 0)

                out_specs.append(pl.BlockSpec((tm * rows_per_token, DIL_HD), idx))
            else:
                out_specs.append(pl.BlockSpec((keep * rows_per_token, DIL_HD), lambda i: (i // tiles_per_seq, 0)))
            out_shape.append(jax.ShapeDtypeStruct((nb * keep * rows_per_token, DIL_HD), F32))
        n_stage = 3 * sum(1 for _, dil in DIL_GROUPS if dil > 1)
        scratch = [pltpu.VMEM((DIL_WIDTH // LANES, tm, LANES), F32)] * n_stage
    return pl.pallas_call(
        functools.partial(_proj_rest_kernel, tm=tm, tiles_per_seq=tiles_per_seq, keeps=keeps, prompt=prompt),
        grid=(n // tm,),
        in_specs=[
            pl.BlockSpec((tm, d), lambda i: (i, 0)),
            _const_spec(g_pre.shape),
            _const_spec(w_r.shape),
        ],
        out_specs=out_specs,
        out_shape=out_shape,
        scratch_shapes=scratch,
        compiler_params=_params("arbitrary"),
        name="proj_rest",
    )(x2d, g_pre, w_r)


def _gla_kernel(qkvg_ref, la_ref, s0_ref, gn_ref, ya_ref, sfin_ref, st_ref, *, chunk, nchunk, nseq, mm_dtype, precision):
    j = pl.program_id(1)

    @pl.when(j == 0)
    def _():
        for s in range(nseq):
            for h in range(GLA_HEADS):
                st_ref[s * GLA_HEADS + h] = s0_ref[s, h].T

    _gla_compute(lambda r, c: qkvg_ref[r, c], lambda r: la_ref[r, :], st_ref, gn_ref, ya_ref,
                 chunk=chunk, nchunk=nchunk, nseq=nseq, mm_dtype=mm_dtype, precision=precision)

    @pl.when(j == pl.num_programs(1) - 1)
    def _():
        for s in range(nseq):
            for h in range(GLA_HEADS):
                sfin_ref[s, h] = st_ref[s * GLA_HEADS + h].T


def _gla_compute(*args, **kwargs):
    for _ in _gla_phases(*args, **kwargs):
        pass


def _interleave(*generators):
    live = list(generators)
    while live:
        for gen in list(live):
            if next(gen, StopIteration) is StopIteration:
                live.remove(gen)


def _gla_phases(load_qkvg, load_la, st_ref, gn_ref, ya_ref, *, chunk, nchunk, nseq, mm_dtype, precision):
    row = lax.broadcasted_iota(jnp.int32, (chunk, chunk), 0)
    col = lax.broadcasted_iota(jnp.int32, (chunk, chunk), 1)
    causal = row >= col
    tril = causal.astype(F32)
    qscale = GLA_DK ** -0.5

    segs = [(s, c) for s in range(nseq) for c in range(nchunk)]
    units = [(s, c, h) for s, c in segs for h in range(GLA_HEADS)]

    def rows_of(s, c):
        r0 = (s * nchunk + c) * chunk
        return slice(r0, r0 + chunk)

    def key_cols(h):
        return slice(h * GLA_DK, (h + 1) * GLA_DK)

    def v_of(s, c, h):
        return load_qkvg(rows_of(s, c), slice(2 * GLA_KEY + h * GLA_DV, 2 * GLA_KEY + (h + 1) * GLA_DV)).astype(mm_dtype)

    def cumsum(a):
        if mm_dtype != BF16:
            return _dot(tril, a, precision=lax.Precision.HIGHEST)
        hi = a.astype(BF16)
        rest = a - hi.astype(F32)
        mid = rest.astype(BF16)
        lo = (rest - mid.astype(F32)).astype(BF16)
        parts = _dot(tril.astype(BF16), jnp.concatenate([hi, mid, lo], axis=1))
        return parts[:, :GLA_KEY] + parts[:, GLA_KEY:2 * GLA_KEY] + parts[:, 2 * GLA_KEY:]

    cum = {sc: cumsum(load_la(rows_of(*sc))) for sc in segs}
    b_end = {sc: cum[sc][chunk - 1:chunk, :] for sc in segs}
    e_b = {sc: jnp.exp(cum[sc]) for sc in segs}
    e_nb = {sc: jnp.exp(-cum[sc]) for sc in segs}
    e_rest = {sc: jnp.exp(b_end[sc] - cum[sc]) for sc in segs}
    e_end = {sc: jnp.exp(b_end[sc]) for sc in segs}
    yield
    q_in, k_dec, k_end = {}, {}, {}
    for s, c, h in units:
        q = load_qkvg(rows_of(s, c), key_cols(h)).astype(F32) * qscale
        k = load_qkvg(rows_of(s, c), slice(GLA_KEY + h * GLA_DK, GLA_KEY + (h + 1) * GLA_DK)).astype(F32)
        q_in[s, c, h] = (q * e_b[s, c][:, key_cols(h)]).astype(mm_dtype)
        k_dec[s, c, h] = (k * e_nb[s, c][:, key_cols(h)]).astype(mm_dtype)
        k_end[s, c, h] = (k * e_rest[s, c][:, key_cols(h)]).astype(mm_dtype)
    yield
    att = {u: jnp.where(causal, _dot(q_in[u], k_dec[u], _NT, precision), 0.0).astype(mm_dtype) for u in units}
    yield
    o_intra = {u: _dot(att[u], v_of(*u), None, precision) for u in units}
    yield
    d_state = {u: _dot(v_of(*u), k_end[u], _TN, precision) for u in units}
    yield
    o_inter = {}
    for s in range(nseq):
        for h in range(GLA_HEADS):
            st = st_ref[s * GLA_HEADS + h]
            for c in range(nchunk):
                o_inter[s, c, h] = _dot(q_in[s, c, h], st.astype(mm_dtype), _NT, precision)
                st = st * e_end[s, c][:, key_cols(h)] + d_state[s, c, h]
            st_ref[s * GLA_HEADS + h] = st
    yield
    for i, (s, c, h) in enumerate(units):
        vs = slice(h * GLA_DV, (h + 1) * GLA_DV)
        gg = load_qkvg(rows_of(s, c), slice(2 * GLA_KEY + GLA_VAL + h * GLA_DV,
                                            2 * GLA_KEY + GLA_VAL + (h + 1) * GLA_DV)).astype(F32)
        on = _rms(o_intra[s, c, h] + o_inter[s, c, h], gn_ref[:, vs])
        ya_ref[rows_of(s, c), vs] = (on * (gg * _sigmoid(gg))).astype(ya_ref.dtype)
        if i % 8 == 7:
            yield


def _proj_gla_fused_kernel(x_ref, g_ref, w_ref, wlr_ref, wlr2_ref, blr_ref, s0_ref, gn_ref, *refs,
                           steps, chunk, col_chunk, n_shift, pieces, n_chunks):
    old_refs = refs[:n_shift]
    next_refs = refs[n_shift:2 * n_shift]
    tail_refs = refs[2 * n_shift:3 * n_shift]
    ya_ref, sfin_ref = refs[3 * n_shift:3 * n_shift + 2]
    new_refs = refs[3 * n_shift + 2:4 * n_shift + 2]
    qkvg_s0, qkvg_s1, la_s0, la_s1, st_ref = refs[4 * n_shift + 2:]
    j = pl.program_id(0)
    prev = j - 1

    if n_shift:
        last_piece = jnp.minimum(j, n_chunks - 1) % pieces == pieces - 1
        for old, nxt, tail, new in zip(old_refs, next_refs, tail_refs, new_refs):
            p_rows, t_rows = old.shape[0], tail.shape[0]
            new[0:p_rows - t_rows, :] = old[t_rows:p_rows, :]
            new[p_rows - t_rows:p_rows, :] = jnp.where(last_piece, tail[...], nxt[...])

    @pl.when(j == 0)
    def _():
        qkvg_s1[...] = jnp.zeros_like(qkvg_s1)
        la_s1[...] = jnp.zeros_like(la_s1)

    @pl.when(jnp.logical_or(j == 0, prev % steps == 0))
    def _():
        for h in range(GLA_HEADS):
            st_ref[h] = s0_ref[0, h].T

    def project(qkvg_w, la_w):
        xn = _rms(x_ref[...], g_ref[...]).astype(BF16)
        yield
        for c0 in range(0, GLA_COLS, col_chunk):
            qkvg_w[:, c0:c0 + col_chunk] = _dot(xn, w_ref[c0:c0 + col_chunk, :], _NT).astype(qkvg_w.dtype)
            yield
        glr = _dot(xn, wlr_ref[...], _NT)
        z = _dot(glr.astype(BF16), wlr2_ref[...]) + blr_ref[...]
        log_sig = jnp.minimum(z, 0.0) - jnp.log1p(jnp.exp(-jnp.abs(z)))
        la_w[...] = log_sig * (1.0 / GLA_TAU)

    def step(qkvg_w, la_w, qkvg_r, la_r):
        _interleave(
            project(qkvg_w, la_w),
            _gla_phases(lambda r, c: qkvg_r[r, c], lambda r: la_r[r, :], st_ref, gn_ref, ya_ref,
                        chunk=chunk, nchunk=x_ref.shape[0] // chunk, nseq=1, mm_dtype=BF16, precision=None))

    @pl.when(j % 2 == 0)
    def _():
        step(qkvg_s0, la_s0, qkvg_s1, la_s1)

    @pl.when(j % 2 == 1)
    def _():
        step(qkvg_s1, la_s1, qkvg_s0, la_s0)

    @pl.when(jnp.logical_and(j >= 1, prev % steps == steps - 1))
    def _():
        for h in range(GLA_HEADS):
            sfin_ref[0, h] = st_ref[h].T


def _proj_gla_fused(x2d, g_pre, w_a, w_lr, w_lr2, b_lr, s0, gn, *, tm, seq, chunk, shift_src=(), shift_tail=()):
    n, d = x2d.shape
    nb = n // seq
    steps = seq // tm
    n_tiles = n // tm
    n_shift = len(shift_src)
    pieces = n_chunks = 1
    copy_in, copy_out = [], []
    if n_shift:
        entries = shift_src[0].shape[0]
        pieces = max(1, n_tiles // entries)
        n_chunks = entries * pieces
        assert n_chunks <= n_tiles + 1

        def piece_of(j):
            c = jnp.minimum(j, n_chunks - 1)
            return c // pieces, c % pieces

        next_specs, tail_specs = [], []
        for src, tail in zip(shift_src, shift_tail):
            t_rows = tail.shape[1]
            p_rows = src.shape[1] // pieces
            assert src.shape[0] == tail.shape[0] == entries and src.shape[2] == tail.shape[2] == LANES
            assert src.shape[1] % pieces == 0 and p_rows % t_rows == 0 and t_rows % 8 == 0
            piece_spec = pl.BlockSpec((None, p_rows, LANES), lambda j: (*piece_of(j), 0))
            last_t_block = src.shape[1] // t_rows - 1

            def next_idx(j, per_piece=p_rows // t_rows, last=last_t_block):
                entry, piece = piece_of(j)
                return (entry, jnp.minimum((piece + 1) * per_piece, last), 0)

            copy_in.append(piece_spec)
            next_specs.append(pl.BlockSpec((None, t_rows, LANES), next_idx))
            tail_specs.append(pl.BlockSpec((None, t_rows, LANES), lambda j: (piece_of(j)[0], 0, 0)))
            copy_out.append(piece_spec)
        copy_in += next_specs + tail_specs

    def prev_tile(j):
        return jnp.maximum(j - 1, 0)

    outs = pl.pallas_call(
        functools.partial(_proj_gla_fused_kernel, steps=steps, chunk=chunk, col_chunk=512, n_shift=n_shift,
                          pieces=pieces, n_chunks=n_chunks),
        grid=(n_tiles + 1,),
        in_specs=[
            pl.BlockSpec((tm, d), lambda j: (jnp.minimum(j, n_tiles - 1), 0)),
            _const_spec(g_pre.shape),
            _const_spec(w_a.shape),
            _const_spec(w_lr.shape),
            _const_spec(w_lr2.shape),
            _const_spec(b_lr.shape),
            pl.BlockSpec((1, GLA_HEADS, GLA_DK, GLA_DV), lambda j: (prev_tile(j) // steps, 0, 0, 0)),
            _const_spec(gn.shape),
        ] + copy_in,
        out_specs=[
            pl.BlockSpec((tm, GLA_VAL), lambda j: (prev_tile(j), 0)),
            pl.BlockSpec((1, GLA_HEADS, GLA_DK, GLA_DV), lambda j: (prev_tile(j) // steps, 0, 0, 0)),
        ] + copy_out,
        out_shape=[
            jax.ShapeDtypeStruct((n, GLA_VAL), BF16),
            jax.ShapeDtypeStruct((nb, GLA_HEADS, GLA_DK, GLA_DV), F32),
        ] + [jax.ShapeDtypeStruct(src.shape, src.dtype) for src in shift_src],
        scratch_shapes=[
            pltpu.VMEM((tm, GLA_COLS), BF16),
            pltpu.VMEM((tm, GLA_COLS), BF16),
            pltpu.VMEM((tm, GLA_KEY), F32),
            pltpu.VMEM((tm, GLA_KEY), F32),
            pltpu.VMEM((GLA_HEADS, GLA_DV, GLA_DK), F32),
        ],
        compiler_params=_params("arbitrary"),
        name="proj_gla_fused",
    )(x2d, g_pre, w_a, w_lr, w_lr2, b_lr, s0, gn, *shift_src, *shift_src, *shift_tail)
    return outs[0], outs[1], list(outs[2:])


def _gla(qkvg, la, s0, gn, *, nb, seq, chunk, rows_per_step, nseq, out_dtype, mm_dtype, precision):
    steps = seq // rows_per_step
    assert nb % nseq == 0 and (nseq == 1 or steps == 1)
    rows = nseq * rows_per_step
    return pl.pallas_call(
        functools.partial(_gla_kernel, chunk=chunk, nchunk=rows_per_step // chunk, nseq=nseq, mm_dtype=mm_dtype,
                          precision=precision),
        grid=(nb // nseq, steps),
        in_specs=[
            pl.BlockSpec((rows, GLA_COLS), lambda b, j: (b * steps + j, 0)),
            pl.BlockSpec((rows, GLA_KEY), lambda b, j: (b * steps + j, 0)),
            pl.BlockSpec((nseq, GLA_HEADS, GLA_DK, GLA_DV), lambda b, j: (b, 0, 0, 0)),
            pl.BlockSpec((1, GLA_VAL), lambda b, j: (0, 0)),
        ],
        out_specs=[
            pl.BlockSpec((rows, GLA_VAL), lambda b, j: (b * steps + j, 0)),
            pl.BlockSpec((nseq, GLA_HEADS, GLA_DK, GLA_DV), lambda b, j: (b, 0, 0, 0)),
        ],
        out_shape=[
            jax.ShapeDtypeStruct((nb * seq, GLA_VAL), out_dtype),
            jax.ShapeDtypeStruct((nb, GLA_HEADS, GLA_DK, GLA_DV), F32),
        ],
        scratch_shapes=[pltpu.VMEM((nseq * GLA_HEADS, GLA_DV, GLA_DK), F32)],
        compiler_params=_params("parallel", "arbitrary"),
        name="gla",
    )(qkvg, la, s0, gn)


def _pack_lse(lses):
    rows = lses[0].shape[0]
    lane_head = lax.broadcasted_iota(jnp.int32, (rows, LANES), 1) >> _LOG2_LSE_LANES
    packed = jnp.broadcast_to(lses[0], (rows, LANES))
    for h in range(1, DIL_HEADS):
        packed = jnp.where(lane_head == h, lses[h], packed)
    return packed


def _dil_prompt_kernel(a_ref, o_ref, lse_ref, kvp_ref, *, nres, nblk):
    first = pl.program_id(2) == 0

    @pl.when(first)
    def _():
        kvp_ref[...] = jnp.zeros_like(kvp_ref)

    i_idx = lax.broadcasted_iota(jnp.int32, (DIL_SPAN, 2 * DIL_SPAN), 0)
    j_idx = lax.broadcasted_iota(jnp.int32, (DIL_SPAN, 2 * DIL_SPAN), 1)
    diff = j_idx - i_idx
    band = jnp.logical_and(diff >= 0, diff <= DIL_SPAN)
    band_first = jnp.logical_and(diff >= jnp.where(first, DIL_SPAN - i_idx, 0), diff <= DIL_SPAN)
    scale = DIL_HD ** -0.5
    neg_inf = -jnp.inf
    units = [(res, jb, h) for res in range(nres) for jb in range(nblk) for h in range(DIL_HEADS)]

    def window(res, jb, c0):
        if jb == 0:
            prev = kvp_ref[res, :, c0 - DIL_WIDTH:c0 - DIL_WIDTH + DIL_HD]
            return jnp.concatenate([prev, a_ref[res, 0:DIL_SPAN, c0:c0 + DIL_HD]], axis=0)
        return a_ref[res, (jb - 1) * DIL_SPAN:(jb + 1) * DIL_SPAN, c0:c0 + DIL_HD]

    scores = []
    for res, jb, h in units:
        q = a_ref[res, jb * DIL_SPAN:(jb + 1) * DIL_SPAN, h * DIL_HD:(h + 1) * DIL_HD]
        s = _dot(q, window(res, jb, DIL_WIDTH + h * DIL_HD), _NT) * scale
        scores.append(jnp.where(band_first if jb == 0 else band, s, neg_inf))
    maxes = [jnp.max(s, axis=-1, keepdims=True) for s in scores]
    probs = [jnp.exp(s - m) for s, m in zip(scores, maxes)]
    sums = [jnp.sum(p, axis=-1, keepdims=True) for p in probs]
    lses = {}
    for (res, jb, h), p, m, l in zip(units, probs, maxes, sums):
        acc = _dot(p.astype(BF16), window(res, jb, 2 * DIL_WIDTH + h * DIL_HD))
        o_ref[res, jb * DIL_SPAN:(jb + 1) * DIL_SPAN, h * DIL_HD:(h + 1) * DIL_HD] = acc * (1.0 / l)
        lses[res, jb, h] = m + jnp.log(l)
    for res in range(nres):
        for jb in range(nblk):
            lse_ref[res, jb * DIL_SPAN:(jb + 1) * DIL_SPAN, :] = _pack_lse([lses[res, jb, h] for h in range(DIL_HEADS)])
        kvp_ref[res] = a_ref[res, (nblk - 1) * DIL_SPAN:nblk * DIL_SPAN, DIL_WIDTH:QKV_COLS]


def _dil_prompt(a4, *, rows_per_step=1024):
    nb, dil, length, _ = a4.shape
    rows = min(rows_per_step, length)
    nres = min(dil, rows_per_step // rows)

    def spec(cols):
        return pl.BlockSpec((None, nres, rows, cols), lambda b, r, n: (b, r, n, 0))

    return pl.pallas_call(
        functools.partial(_dil_prompt_kernel, nres=nres, nblk=rows // DIL_SPAN),
        grid=(nb, dil // nres, length // rows),
        in_specs=[spec(QKV_COLS)],
        out_specs=[spec(DIL_WIDTH), spec(LANES)],
        out_shape=[
            jax.ShapeDtypeStruct((nb, dil, length, DIL_WIDTH), F32),
            jax.ShapeDtypeStruct((nb, dil, length, LANES), F32),
        ],
        scratch_shapes=[pltpu.VMEM((nres, DIL_SPAN, KV_COLS), BF16)],
        compiler_params=_params("parallel", "parallel", "arbitrary"),
        name=f"dil_prompt_d{dil}",
    )(a4)


def _dil_sample_kernel(*refs, t_new):
    a_refs, cache_refs = refs[:N_GROUPS], refs[N_GROUPS:2 * N_GROUPS]
    o_refs, lse_refs = refs[2 * N_GROUPS:3 * N_GROUPS], refs[3 * N_GROUPS:]
    rpt = 2 * DIL_HEADS
    scale = DIL_HD ** -0.5
    neg_inf = -jnp.inf
    tq_n = lax.broadcasted_iota(jnp.int32, (t_new, t_new), 0)
    j_n = lax.broadcasted_iota(jnp.int32, (t_new, t_new), 1)

    def cache_head(g, c):
        ref = cache_refs[g]
        if len(ref.shape) == 2:
            return ref[pl.ds(c, ref.shape[0] // rpt, stride=rpt), :].astype(BF16)
        x = ref[:, pl.ds(c, t_new, stride=rpt), :]
        return x.reshape(ref.shape[0] * t_new, DIL_HD).astype(BF16)

    valid_c, valid_n = [], []
    for g, (_, dil) in enumerate(DIL_GROUPS):
        ref = cache_refs[g]
        n_keys = ref.shape[0] // rpt if len(ref.shape) == 2 else ref.shape[0] * t_new
        tq = lax.broadcasted_iota(jnp.int32, (t_new, n_keys), 0)
        key = lax.broadcasted_iota(jnp.int32, (t_new, n_keys), 1)
        if len(ref.shape) == 2:
            valid_c.append(jnp.logical_and(((key - tq) & (dil - 1)) == 0, key >= tq))
        else:
            valid_c.append((key & (t_new - 1)) == tq)
        valid_n.append(jnp.logical_and(((tq_n - j_n) & (dil - 1)) == 0, j_n <= tq_n))

    units = [(g, h) for g in range(N_GROUPS) for h in range(DIL_HEADS)]
    scores = []
    for g, h in units:
        q = a_refs[g][:, h * DIL_HD:(h + 1) * DIL_HD].astype(BF16)
        k_new = a_refs[g][:, DIL_WIDTH + h * DIL_HD:DIL_WIDTH + (h + 1) * DIL_HD].astype(BF16)
        s_c = jnp.where(valid_c[g], _dot(q, cache_head(g, h), _NT) * scale, neg_inf)
        s_n = jnp.where(valid_n[g], _dot(q, k_new, _NT) * scale, neg_inf)
        scores.append((s_c, s_n))
    maxes = [jnp.maximum(jnp.max(s_c, axis=-1, keepdims=True), jnp.max(s_n, axis=-1, keepdims=True))
             for s_c, s_n in scores]
    probs = [(jnp.exp(s_c - m), jnp.exp(s_n - m)) for (s_c, s_n), m in zip(scores, maxes)]
    sums = [jnp.sum(p_c, axis=-1, keepdims=True) + jnp.sum(p_n, axis=-1, keepdims=True) for p_c, p_n in probs]
    lses = {}
    for (g, h), (p_c, p_n), m, l in zip(units, probs, maxes, sums):
        v_new = a_refs[g][:, 2 * DIL_WIDTH + h * DIL_HD:2 * DIL_WIDTH + (h + 1) * DIL_HD].astype(BF16)
        acc = _dot(p_c.astype(BF16), cache_head(g, DIL_HEADS + h)) + _dot(p_n.astype(BF16), v_new)
        o_refs[g][:, h * DIL_HD:(h + 1) * DIL_HD] = acc * (1.0 / l)
        lses[g, h] = m + jnp.log(l)
    for g in range(N_GROUPS):
        lse_refs[g][...] = _pack_lse([lses[g, h] for h in range(DIL_HEADS)])


def _cache_rows(cache):
    nb, lb = cache.shape[:2]
    return cache.reshape(nb, lb * 2 * DIL_HEADS, DIL_HD)


def _dil_sample(a_g, caches, *, t_new):
    nb = caches[0].shape[0]
    rpt = 2 * DIL_HEADS
    assert t_new & (t_new - 1) == 0
    cache_args, cache_specs = [], []
    for cache, (_, dil) in zip(caches, DIL_GROUPS):
        lb = cache.shape[1]
        assert lb == dil * DIL_SPAN
        if dil > t_new:
            assert dil % t_new == 0
            cache_args.append(cache.reshape(nb, lb // dil, dil * rpt, DIL_HD))
            cache_specs.append(pl.BlockSpec((None, lb // dil, t_new * rpt, DIL_HD), lambda b: (b, 0, 0, 0)))
        else:
            cache_args.append(_cache_rows(cache))
            cache_specs.append(pl.BlockSpec((None, lb * rpt, DIL_HD), lambda b: (b, 0, 0)))
    outs = pl.pallas_call(
        functools.partial(_dil_sample_kernel, t_new=t_new),
        grid=(nb,),
        in_specs=[pl.BlockSpec((t_new, QKV_COLS), lambda b: (b, 0))] * N_GROUPS + cache_specs,
        out_specs=[pl.BlockSpec((t_new, DIL_WIDTH), lambda b: (b, 0))] * N_GROUPS
        + [pl.BlockSpec((t_new, LANES), lambda b: (b, 0))] * N_GROUPS,
        out_shape=[jax.ShapeDtypeStruct((nb * t_new, DIL_WIDTH), F32)] * N_GROUPS
        + [jax.ShapeDtypeStruct((nb * t_new, LANES), F32)] * N_GROUPS,
        compiler_params=_params("parallel"),
        name="dil_sample",
    )(*a_g, *cache_args)
    return outs[:N_GROUPS], outs[N_GROUPS:]


def _token_order(ref, scratch_refs):
    if len(ref.shape) == 2:
        return ref[...]
    dil, rows_per_class, _ = ref.shape
    if dil == 1:
        return ref[0]
    scratch = scratch_refs.pop()
    n_slabs = scratch.shape[0]
    for res in range(dil):
        for s in range(n_slabs):
            scratch[s, pl.ds(res, rows_per_class, stride=dil), :] = ref[res, :, s * LANES:(s + 1) * LANES]
    return jnp.concatenate([scratch[s] for s in range(n_slabs)], axis=1) if n_slabs > 1 else scratch[0]


def _final_kernel(ya_ref, o0_ref, o1_ref, o2_ref, l0_ref, l1_ref, l2_ref, gates_ref, x_ref, pe_ref,
                  wa_ref, wb_ref, wo_ref, wpp_ref, wpg_ref, gpost_ref, y_ref, *scratch_refs):
    o_scratch = [s for s in scratch_refs if s.shape[0] == DIL_WIDTH // LANES]
    l_scratch = [s for s in scratch_refs if s.shape[0] == 1]
    lse = [_token_order(r, l_scratch) for r in (l0_ref, l1_ref, l2_ref)]
    o_groups = [_token_order(r, o_scratch) for r in (o0_ref, o1_ref, o2_ref)]
    m = jnp.maximum(jnp.maximum(lse[0], lse[1]), lse[2])
    e = [jnp.exp(x - m) for x in lse]
    inv = 1.0 / (e[0] + e[1] + e[2])
    heads = []
    for h in range(DIL_HEADS):
        hs = slice(h * DIL_HD, (h + 1) * DIL_HD)
        lane = h * LSE_LANES_PER_HEAD
        acc = None
        for eg, o_g in zip(e, o_groups):
            term = (eg * inv)[:, lane:lane + 1] * o_g[:, hs]
            acc = term if acc is None else acc + term
        heads.append(acc)
    o_b = jnp.concatenate(heads, axis=1)
    dg = gates_ref[:, 0:DIL_WIDTH].astype(F32)
    d = x_ref.shape[1]
    ga = gates_ref[:, DIL_WIDTH:DIL_WIDTH + d].astype(F32)
    gb = gates_ref[:, DIL_WIDTH + d:DIL_WIDTH + 2 * d].astype(F32)
    y_a = _dot(ya_ref[...].astype(BF16), wa_ref[...])
    y_b = _dot((o_b * (dg * _sigmoid(dg))).astype(BF16), wb_ref[...])
    merged = _sigmoid(ga) * y_a + _sigmoid(gb) * y_b
    h = x_ref[...] + _rms(_dot(merged.astype(BF16), wo_ref[...]), gpost_ref[...])
    ple = _dot(pe_ref[...].astype(BF16), wpp_ref[...])
    y_ref[...] = h + ple * _sigmoid(_dot(h.astype(BF16), wpg_ref[...]))


def _final(ya, os_, lses, gates, x2d, pe2d, wa, wb, wo, wpp, wpg, g_post, *, tm):
    n, d = x2d.shape

    def rows(cols):
        return pl.BlockSpec((tm, cols), lambda i: (i, 0))

    scratch = []

    def group_spec(arr):
        if arr.ndim == 2:
            return rows(arr.shape[1])
        _, dil, length, cols = arr.shape
        tiles_per_seq = dil * length // tm
        if dil > 1:
            scratch.append(pltpu.VMEM((cols // LANES, tm, LANES), F32))
        return pl.BlockSpec((None, dil, tm // dil, cols), lambda i: (i // tiles_per_seq, 0, i % tiles_per_seq, 0))

    return pl.pallas_call(
        _final_kernel,
        grid=(n // tm,),
        in_specs=[rows(GLA_VAL)] + [group_spec(a) for a in os_] + [group_spec(a) for a in lses]
        + [rows(gates.shape[1]), rows(d), rows(pe2d.shape[1])]
        + [_const_spec(w.shape) for w in (wa, wb, wo, wpp, wpg, g_post)],
        out_specs=rows(d),
        out_shape=jax.ShapeDtypeStruct((n, d), F32),
        scratch_shapes=scratch,
        compiler_params=_params("parallel"),
        name="final",
    )(ya, *os_, *lses, gates, x2d, pe2d, wa, wb, wo, wpp, wpg, g_post)


def _prep_weights(g_pre, g_post, w_in, w_gla_lr, b_gla_lr, g_gla_norm, w_gla_branch, w_dil_branch, w_out,
                  w_ple_proj, w_ple_gate):
    d = w_in.shape[0]
    c_lr = GLA_COLS
    c_q = c_lr + GLA_LOWRANK
    c_k = c_q + N_GROUPS * DIL_WIDTH
    c_v = c_k + N_GROUPS * DIL_WIDTH
    c_g = c_v + N_GROUPS * DIL_WIDTH
    w_t = w_in.T
    w_a = w_t[:c_lr].astype(BF16)
    w_lr = jnp.pad(w_t[c_lr:c_q].astype(BF16), ((0, LANES - GLA_LOWRANK), (0, 0)))
    w_lr2 = jnp.pad(w_gla_lr.astype(BF16), ((0, LANES - GLA_LOWRANK), (0, 0)))
    parts = []
    for g in range(N_GROUPS):
        for c in (c_q, c_k, c_v):
            parts.append(w_t[c + g * DIL_WIDTH:c + (g + 1) * DIL_WIDTH])
    parts.append(w_t[c_g:])
    w_r = jnp.concatenate(parts, axis=0).astype(BF16)
    return dict(
        g_pre=g_pre.reshape(1, d), g_post=g_post.reshape(1, d), w_a=w_a, w_lr=w_lr, w_lr2=w_lr2, w_r=w_r,
        b_lr=b_gla_lr.reshape(1, GLA_KEY), gn=g_gla_norm.reshape(1, GLA_VAL),
        wa=w_gla_branch.astype(BF16), wb=w_dil_branch.astype(BF16), wo=w_out.astype(BF16),
        wpp=w_ple_proj.astype(BF16), wpg=w_ple_gate.astype(BF16),
    )


PROMPT_TILE = 512


def _prompt_layer(x, pe, s0, w, sample_caches, sample_new_rows):
    nb, seq, d = x.shape
    n = nb * seq
    tm = PROMPT_TILE
    assert seq % tm == 0 and seq % DIL_GROUPS[-1][0] == 0
    x2d = x.reshape(n, d)
    rest = _proj_rest(x2d, w["g_pre"], w["w_r"], tm=tm, seq=seq, prompt=True)
    a_g, gates = rest[:N_GROUPS], rest[N_GROUPS]
    ns = sample_caches[0].shape[0]
    ya, s_new, next_caches = _proj_gla_fused(
        x2d, w["g_pre"], w["w_a"], w["w_lr"], w["w_lr2"], w["b_lr"], s0, w["gn"], tm=tm, seq=seq, chunk=GLA_CHUNK,
        shift_src=tuple(_cache_rows(c) for c in sample_caches),
        shift_tail=tuple(r.reshape(ns, -1, DIL_HD) for r in sample_new_rows))
    new_bufs = [kv.reshape(nb, -1, 2, DIL_HEADS, DIL_HD) for kv in rest[N_GROUPS + 1:]]
    attn = [_dil_prompt(a) for a in a_g]
    y = _final(ya, [o for o, _ in attn], [lse for _, lse in attn], gates, x2d, pe.reshape(n, -1),
               w["wa"], w["wb"], w["wo"], w["wpp"], w["wpg"], w["g_post"], tm=tm)
    next_caches = [c.reshape(old.shape) for c, old in zip(next_caches, sample_caches)]
    return y.reshape(nb, seq, d), s_new, new_bufs, next_caches


def _sample_proj(x, w):
    nb, seq, d = x.shape
    return _proj_rest(x.reshape(nb * seq, d), w["g_pre"], w["w_r"], tm=nb * seq, seq=seq, prompt=False)


def _sample_layer(x, pe, s0, caches, w, rest):
    nb, seq, d = x.shape
    n = nb * seq
    x2d = x.reshape(n, d)
    a_g, gates = rest[:N_GROUPS], rest[N_GROUPS]
    chunk = min(GLA_CHUNK, seq)
    assert seq % chunk == 0
    nseq = 8 if nb % 8 == 0 else 1
    qkvg, la = _proj_gla(x2d, w["g_pre"], w["w_a"], w["w_lr"], w["w_lr2"], w["b_lr"], tm=n, out_dtype=F32)
    ya, s_new = _gla(qkvg, la, s0, w["gn"], nb=nb, seq=seq, chunk=chunk, rows_per_step=seq, nseq=nseq,
                     out_dtype=F32, mm_dtype=F32, precision=lax.Precision.HIGHEST)
    os_, lses = _dil_sample(a_g, caches, t_new=seq)
    y = _final(ya, os_, lses, gates, x2d, pe.reshape(n, -1),
               w["wa"], w["wb"], w["wo"], w["wpp"], w["wpg"], w["g_post"], tm=n)
    return y.reshape(nb, seq, d), s_new


def kernel(x_prompt, x_sample, state_gla, cache_kv_w128, cache_kv_w512, cache_kv_w2048, p_prompt, p_sample, g_pre, g_post, w_in, w_gla_lr, b_gla_lr, g_gla_norm, w_gla_branch, w_dil_branch, w_out, w_ple_proj, w_ple_gate):
    depth = w_in.shape[0]
    hp, hs = x_prompt, x_sample
    gla_p, gla_s = [], []
    kvp = [[] for _ in DIL_GROUPS]
    kvs = [[] for _ in DIL_GROUPS]
    for i in range(depth):
        w = _prep_weights(g_pre[i], g_post[i], w_in[i], w_gla_lr[i], b_gla_lr[i], g_gla_norm[i], w_gla_branch[i],
                          w_dil_branch[i], w_out[i], w_ple_proj[i], w_ple_gate[i])
        s0 = jnp.zeros((x_prompt.shape[0], GLA_HEADS, GLA_DK, GLA_DV), F32)
        caches = (cache_kv_w128[i], cache_kv_w512[i], cache_kv_w2048[i])
        rest_s = _sample_proj(hs, w)
        hp, sp_new, bp_new, bs_new = _prompt_layer(hp, p_prompt[i], s0, w, caches, rest_s[N_GROUPS + 1:])
        hs, ss_new = _sample_layer(hs, p_sample[i], state_gla[i], caches, w, rest_s)
        gla_p.append(sp_new)
        gla_s.append(ss_new)
        for g in range(N_GROUPS):
            kvp[g].append(bp_new[g])
            kvs[g].append(bs_new[g])
    return (hp, hs, jnp.stack(gla_p), jnp.stack(gla_s),
            jnp.stack(kvp[0]), jnp.stack(kvp[1]), jnp.stack(kvp[2]),
            jnp.stack(kvs[0]), jnp.stack(kvs[1]), jnp.stack(kvs[2]))
```

```python
import functools

import jax
import jax.numpy as jnp
from jax import lax
from jax.experimental import pallas as pl
from jax.experimental.pallas import tpu as pltpu

F32 = jnp.float32
BF16 = jnp.bfloat16

NORM_EPS = 1e-6
GLA_HEADS = 4
GLA_DK = 128
GLA_DV = 256
GLA_KEY = GLA_HEADS * GLA_DK
GLA_VAL = GLA_HEADS * GLA_DV
GLA_LOWRANK = 16
GLA_TAU = 16.0
GLA_CHUNK = 64
DIL_GROUPS = ((128, 1), (512, 4), (2048, 16))
DIL_HEADS = 4
DIL_HD = 128
DIL_WIDTH = DIL_HEADS * DIL_HD
DIL_SPAN = 128
N_GROUPS = len(DIL_GROUPS)
GLA_COLS = 2 * GLA_KEY + 2 * GLA_VAL
QKV_COLS = 3 * DIL_WIDTH
KV_COLS = 2 * DIL_WIDTH

LANES = 128
LSE_LANES_PER_HEAD = LANES // DIL_HEADS
_LOG2_LSE_LANES = LSE_LANES_PER_HEAD.bit_length() - 1
_LOG2_DIL_HD = DIL_HD.bit_length() - 1
assert 1 << _LOG2_LSE_LANES == LSE_LANES_PER_HEAD and 1 << _LOG2_DIL_HD == DIL_HD
VMEM_LIMIT_BYTES = 56 * 1024 * 1024
_CHEAP_ROW_STRIDE = 4

_NT = (((1,), (1,)), ((), ()))
_TN = (((0,), (0,)), ((), ()))


def _dot(a, b, dims=None, precision=None):
    if dims is None:
        return jnp.dot(a, b, preferred_element_type=F32, precision=precision)
    return lax.dot_general(a, b, dims, preferred_element_type=F32, precision=precision)


def _rms(xf, g):
    return xf * lax.rsqrt(jnp.mean(xf * xf, axis=-1, keepdims=True) + NORM_EPS) * g


def _sigmoid(x):
    return 1.0 / (1.0 + jnp.exp(-x))


def _const_spec(shape):
    nd = len(shape)
    return pl.BlockSpec(shape, lambda *_: (0,) * nd, pipeline_mode=pl.Buffered(1))


def _params(*sem):
    return pltpu.CompilerParams(dimension_semantics=sem, vmem_limit_bytes=VMEM_LIMIT_BYTES)


def _proj_gla_kernel(x_ref, g_ref, w_ref, wlr_ref, wlr2_ref, blr_ref, qkvg_ref, la_ref, *, col_chunk):
    xn = _rms(x_ref[...], g_ref[...]).astype(BF16)
    for c0 in range(0, GLA_COLS, col_chunk):
        qkvg_ref[:, c0:c0 + col_chunk] = _dot(xn, w_ref[c0:c0 + col_chunk, :], _NT).astype(qkvg_ref.dtype)
    glr = _dot(xn, wlr_ref[...], _NT)
    z = _dot(glr.astype(BF16), wlr2_ref[...]) + blr_ref[...]
    log_sig = jnp.minimum(z, 0.0) - jnp.log1p(jnp.exp(-jnp.abs(z)))
    la_ref[...] = log_sig * (1.0 / GLA_TAU)


def _proj_gla(x2d, g_pre, w_a, w_lr, w_lr2, b_lr, *, tm, out_dtype):
    n, d = x2d.shape
    return pl.pallas_call(
        functools.partial(_proj_gla_kernel, col_chunk=512),
        grid=(n // tm,),
        in_specs=[
            pl.BlockSpec((tm, d), lambda i: (i, 0)),
            _const_spec(g_pre.shape),
            _const_spec(w_a.shape),
            _const_spec(w_lr.shape),
            _const_spec(w_lr2.shape),
            _const_spec(b_lr.shape),
        ],
        out_specs=[
            pl.BlockSpec((tm, GLA_COLS), lambda i: (i, 0)),
            pl.BlockSpec((tm, GLA_KEY), lambda i: (i, 0)),
        ],
        out_shape=[
            jax.ShapeDtypeStruct((n, GLA_COLS), out_dtype),
            jax.ShapeDtypeStruct((n, GLA_KEY), F32),
        ],
        compiler_params=_params("parallel"),
        name="proj_gla",
    )(x2d, g_pre, w_a, w_lr, w_lr2, b_lr)


def _proj_rest_kernel(x_ref, g_ref, w_ref, *refs, tm, tiles_per_seq, keeps, prompt):
    a_refs = refs[:N_GROUPS]
    gates_ref = refs[N_GROUPS]
    kv_refs = refs[N_GROUPS + 1:2 * N_GROUPS + 1]
    stage_refs = refs[2 * N_GROUPS + 1:]

    def body(with_cache_rows):
        xn = _rms(x_ref[...], g_ref[...]).astype(BF16)
        n_staged = 0
        for g in range(N_GROUPS):
            base = g * QKV_COLS
            dil = DIL_GROUPS[g][1]
            for part in range(3):
                c0 = part * DIL_WIDTH
                r = _dot(xn, w_ref[base + c0:base + c0 + DIL_WIDTH, :], _NT)
                if not prompt:
                    a_refs[g][:, c0:c0 + DIL_WIDTH] = r
                elif dil == 1:
                    a_refs[g][0, :, c0:c0 + DIL_WIDTH] = r.astype(BF16)
                else:
                    stage = stage_refs[n_staged]
                    n_staged += 1
                    for s in range(DIL_WIDTH // LANES):
                        stage[s] = r[:, s * LANES:(s + 1) * LANES]
                    step = dil
                    if dil > _CHEAP_ROW_STRIDE:
                        step = dil // _CHEAP_ROW_STRIDE
                        assert step <= _CHEAP_ROW_STRIDE
                        stage2 = stage_refs[n_staged]
                        n_staged += 1
                        quarter = tm // _CHEAP_ROW_STRIDE
                        for c in range(_CHEAP_ROW_STRIDE):
                            for s in range(DIL_WIDTH // LANES):
                                stage2[s, c * quarter:(c + 1) * quarter, :] = (
                                    stage[s, pl.ds(c, quarter, stride=_CHEAP_ROW_STRIDE), :])
                        stage = stage2
                    for res in range(dil):
                        start = res if step == dil else (res % _CHEAP_ROW_STRIDE) * quarter + res // _CHEAP_ROW_STRIDE
                        for s in range(DIL_WIDTH // LANES):
                            a_refs[g][res, :, c0 + s * LANES:c0 + (s + 1) * LANES] = (
                                stage[s, pl.ds(start, tm // dil, stride=step), :].astype(BF16))
                if part > 0 and with_cache_rows:
                    rows = min(keeps[g], tm)
                    for h in range(DIL_HEADS):
                        kv_refs[g][pl.ds((part - 1) * DIL_HEADS + h, rows, stride=2 * DIL_HEADS), :] = (
                            r[tm - rows:, h * DIL_HD:(h + 1) * DIL_HD])
        gbase = N_GROUPS * QKV_COLS
        gcols = gates_ref.shape[1]
        for c0 in range(0, gcols, DIL_WIDTH):
            gates_ref[:, c0:c0 + DIL_WIDTH] = _dot(xn, w_ref[gbase + c0:gbase + c0 + DIL_WIDTH, :], _NT).astype(gates_ref.dtype)

    if not prompt:
        body(True)
        return
    t = pl.program_id(0) % tiles_per_seq
    in_kept_range = t >= tiles_per_seq - max(1, max(keeps) // tm)
    pl.when(in_kept_range)(lambda: body(True))
    pl.when(jnp.logical_not(in_kept_range))(lambda: body(False))


def _proj_rest(x2d, g_pre, w_r, *, tm, seq, prompt):
    n, d = x2d.shape
    nb = n // seq
    tiles_per_seq = seq // tm if prompt else None
    gate_cols = w_r.shape[0] - N_GROUPS * QKV_COLS
    keeps = tuple(min(w, seq) for (w, _) in DIL_GROUPS) if prompt else (tm,) * N_GROUPS
    rows_per_token = 2 * DIL_HEADS
    out_specs, out_shape, scratch = [], [], []
    if prompt:
        for _, dil in DIL_GROUPS:
            out_specs.append(pl.BlockSpec((None, dil, tm // dil, QKV_COLS),
                                          lambda i: (i // tiles_per_seq, 0, i % tiles_per_seq, 0)))
            out_shape.append(jax.ShapeDtypeStruct((nb, dil, seq // dil, QKV_COLS), BF16))
    else:
        out_specs += [pl.BlockSpec((tm, QKV_COLS), lambda i: (i, 0)) for _ in range(N_GROUPS)]
        out_shape += [jax.ShapeDtypeStruct((n, QKV_COLS), F32) for _ in range(N_GROUPS)]
    out_specs.append(pl.BlockSpec((tm, gate_cols), lambda i: (i, 0)))
    out_shape.append(jax.ShapeDtypeStruct((n, gate_cols), BF16 if prompt else F32))
    if not prompt:
        for _ in range(N_GROUPS):
            out_specs.append(pl.BlockSpec((tm * rows_per_token, DIL_HD), lambda i: (i, 0)))
            out_shape.append(jax.ShapeDtypeStruct((n * rows_per_token, DIL_HD), F32))
    else:
        for keep in keeps:
            if keep >= tm:
                assert keep % tm == 0
                kt = keep // tm

                def idx(i, kt=kt):
                    b, t = i // tiles_per_seq, i % tiles_per_seq
                    return (b * kt + jnp.maximum(t - (tiles_per_seq - kt), 0), 0)

                out_specs.append(pl.BlockSpec((tm * rows_per_token, DIL_HD), idx))
            else:
                out_specs.append(pl.BlockSpec((keep * rows_per_token, DIL_HD), lambda i: (i // tiles_per_seq, 0)))
            out_shape.append(jax.ShapeDtypeStruct((nb * keep * rows_per_token, DIL_HD), F32))
        n_stage = 3 * sum((1 if dil <= _CHEAP_ROW_STRIDE else 2) for _, dil in DIL_GROUPS if dil > 1)
        scratch = [pltpu.VMEM((DIL_WIDTH // LANES, tm, LANES), F32)] * n_stage
    return pl.pallas_call(
        functools.partial(_proj_rest_kernel, tm=tm, tiles_per_seq=tiles_per_seq, keeps=keeps, prompt=prompt),
        grid=(n // tm,),
        in_specs=[
            pl.BlockSpec((tm, d), lambda i: (i, 0)),
            _const_spec(g_pre.shape),
            _const_spec(w_r.shape),
        ],
        out_specs=out_specs,
        out_shape=out_shape,
        scratch_shapes=scratch,
        compiler_params=_params("arbitrary"),
        name="proj_rest",
    )(x2d, g_pre, w_r)


def _gla_kernel(qkvg_ref, la_ref, s0_ref, gn_ref, ya_ref, sfin_ref, st_ref, *, chunk, nchunk, nseq, mm_dtype, precision):
    j = pl.program_id(1)

    @pl.when(j == 0)
    def _():
        for s in range(nseq):
            for h in range(GLA_HEADS):
                st_ref[s * GLA_HEADS + h] = s0_ref[s, h].T

    _gla_compute(lambda r, c: qkvg_ref[r, c], lambda r: la_ref[r, :], st_ref, gn_ref, ya_ref,
                 chunk=chunk, nchunk=nchunk, nseq=nseq, mm_dtype=mm_dtype, precision=precision)

    @pl.when(j == pl.num_programs(1) - 1)
    def _():
        for s in range(nseq):
            for h in range(GLA_HEADS):
                sfin_ref[s, h] = st_ref[s * GLA_HEADS + h].T


def _gla_compute(*args, **kwargs):
    for _ in _gla_phases(*args, **kwargs):
        pass


def _interleave(*generators, phases_per_round=None):
    live = {i: gen for i, gen in enumerate(generators)}
    counts = phases_per_round or (1,) * len(generators)
    while live:
        for i in list(live):
            for _ in range(counts[i]):
                if next(live[i], StopIteration) is StopIteration:
                    del live[i]
                    break


def _gla_phases(load_qkvg, load_la, st_ref, gn_ref, ya_ref, *, chunk, nchunk, nseq, mm_dtype, precision):
    row = lax.broadcasted_iota(jnp.int32, (chunk, chunk), 0)
    col = lax.broadcasted_iota(jnp.int32, (chunk, chunk), 1)
    causal = row >= col
    tril = causal.astype(F32)
    qscale = GLA_DK ** -0.5

    segs = [(s, c) for s in range(nseq) for c in range(nchunk)]
    units = [(s, c, h) for s, c in segs for h in range(GLA_HEADS)]

    def rows_of(s, c):
        r0 = (s * nchunk + c) * chunk
        return slice(r0, r0 + chunk)

    def key_cols(h):
        return slice(h * GLA_DK, (h + 1) * GLA_DK)

    def v_of(s, c, h):
        return load_qkvg(rows_of(s, c), slice(2 * GLA_KEY + h * GLA_DV, 2 * GLA_KEY + (h + 1) * GLA_DV)).astype(mm_dtype)

    def cumsum(a):
        if mm_dtype != BF16:
            return _dot(tril, a, precision=lax.Precision.HIGHEST)
        hi = a.astype(BF16)
        rest = a - hi.astype(F32)
        mid = rest.astype(BF16)
        lo = (rest - mid.astype(F32)).astype(BF16)
        parts = _dot(tril.astype(BF16), jnp.concatenate([hi, mid, lo], axis=1))
        return parts[:, :GLA_KEY] + parts[:, GLA_KEY:2 * GLA_KEY] + parts[:, 2 * GLA_KEY:]

    cum = {sc: cumsum(load_la(rows_of(*sc))) for sc in segs}
    b_end = {sc: cum[sc][chunk - 1:chunk, :] for sc in segs}
    e_b = {sc: jnp.exp(cum[sc]) for sc in segs}
    e_nb = {sc: jnp.exp(-cum[sc]) for sc in segs}
    e_rest = {sc: jnp.exp(b_end[sc] - cum[sc]) for sc in segs}
    e_end = {sc: jnp.exp(b_end[sc]) for sc in segs}
    yield
    q_in, k_dec, k_end = {}, {}, {}
    for s, c, h in units:
        q = load_qkvg(rows_of(s, c), key_cols(h)).astype(F32) * qscale
        k = load_qkvg(rows_of(s, c), slice(GLA_KEY + h * GLA_DK, GLA_KEY + (h + 1) * GLA_DK)).astype(F32)
        q_in[s, c, h] = (q * e_b[s, c][:, key_cols(h)]).astype(mm_dtype)
        k_dec[s, c, h] = (k * e_nb[s, c][:, key_cols(h)]).astype(mm_dtype)
        k_end[s, c, h] = (k * e_rest[s, c][:, key_cols(h)]).astype(mm_dtype)
    yield
    att = {u: jnp.where(causal, _dot(q_in[u], k_dec[u], _NT, precision), 0.0).astype(mm_dtype) for u in units}
    yield
    o_intra = {u: _dot(att[u], v_of(*u), None, precision) for u in units}
    yield
    d_state = {u: _dot(v_of(*u), k_end[u], _TN, precision) for u in units}
    yield
    o_inter = {}
    for s in range(nseq):
        for h in range(GLA_HEADS):
            st = st_ref[s * GLA_HEADS + h]
            for c in range(nchunk):
                o_inter[s, c, h] = _dot(q_in[s, c, h], st.astype(mm_dtype), _NT, precision)
                st = st * e_end[s, c][:, key_cols(h)] + d_state[s, c, h]
            st_ref[s * GLA_HEADS + h] = st
    yield
    for i, (s, c, h) in enumerate(units):
        vs = slice(h * GLA_DV, (h + 1) * GLA_DV)
        gg = load_qkvg(rows_of(s, c), slice(2 * GLA_KEY + GLA_VAL + h * GLA_DV,
                                            2 * GLA_KEY + GLA_VAL + (h + 1) * GLA_DV)).astype(F32)
        on = _rms(o_intra[s, c, h] + o_inter[s, c, h], gn_ref[:, vs])
        ya_ref[rows_of(s, c), vs] = (on * (gg * _sigmoid(gg))).astype(ya_ref.dtype)
        if i % 8 == 7:
            yield


def _proj_gla_fused_kernel(x_ref, g_ref, w_ref, wlr_ref, wlr2_ref, blr_ref, s0_ref, gn_ref, *refs,
                           steps, chunk, col_chunk, n_shift, pieces, n_chunks):
    old_refs = refs[:n_shift]
    next_refs = refs[n_shift:2 * n_shift]
    tail_refs = refs[2 * n_shift:3 * n_shift]
    ya_ref, sfin_ref = refs[3 * n_shift:3 * n_shift + 2]
    new_refs = refs[3 * n_shift + 2:4 * n_shift + 2]
    qkvg_s0, qkvg_s1, la_s0, la_s1, st_ref = refs[4 * n_shift + 2:]
    j = pl.program_id(0)
    prev = j - 1

    if n_shift:
        last_piece = jnp.minimum(j, n_chunks - 1) % pieces == pieces - 1
        for old, nxt, tail, new in zip(old_refs, next_refs, tail_refs, new_refs):
            p_rows, t_rows = old.shape[0], tail.shape[0]
            new[0:p_rows - t_rows, :] = old[t_rows:p_rows, :]
            new[p_rows - t_rows:p_rows, :] = jnp.where(last_piece, tail[...], nxt[...])

    @pl.when(j == 0)
    def _():
        qkvg_s1[...] = jnp.zeros_like(qkvg_s1)
        la_s1[...] = jnp.zeros_like(la_s1)

    @pl.when(jnp.logical_or(j == 0, prev % steps == 0))
    def _():
        for h in range(GLA_HEADS):
            st_ref[h] = s0_ref[0, h].T

    def project(qkvg_w, la_w):
        xn = _rms(x_ref[...], g_ref[...]).astype(BF16)
        yield
        for c0 in range(0, GLA_COLS, col_chunk):
            qkvg_w[:, c0:c0 + col_chunk] = _dot(xn, w_ref[c0:c0 + col_chunk, :], _NT).astype(qkvg_w.dtype)
            yield
        glr = _dot(xn, wlr_ref[...], _NT)
        z = _dot(glr.astype(BF16), wlr2_ref[...]) + blr_ref[...]
        log_sig = jnp.minimum(z, 0.0) - jnp.log1p(jnp.exp(-jnp.abs(z)))
        la_w[...] = log_sig * (1.0 / GLA_TAU)

    def step(qkvg_w, la_w, qkvg_r, la_r):
        _interleave(
            project(qkvg_w, la_w),
            _gla_phases(lambda r, c: qkvg_r[r, c], lambda r: la_r[r, :], st_ref, gn_ref, ya_ref,
                        chunk=chunk, nchunk=x_ref.shape[0] // chunk, nseq=1, mm_dtype=BF16, precision=None),
            phases_per_round=(1, 2))

    @pl.when(j % 2 == 0)
    def _():
        step(qkvg_s0, la_s0, qkvg_s1, la_s1)

    @pl.when(j % 2 == 1)
    def _():
        step(qkvg_s1, la_s1, qkvg_s0, la_s0)

    @pl.when(jnp.logical_and(j >= 1, prev % steps == steps - 1))
    def _():
        for h in range(GLA_HEADS):
            sfin_ref[0, h] = st_ref[h].T


def _proj_gla_fused(x2d, g_pre, w_a, w_lr, w_lr2, b_lr, s0, gn, *, tm, seq, chunk, shift_src=(), shift_tail=()):
    n, d = x2d.shape
    nb = n // seq
    steps = seq // tm
    n_tiles = n // tm
    n_shift = len(shift_src)
    pieces = n_chunks = 1
    copy_in, copy_out = [], []
    if n_shift:
        entries = shift_src[0].shape[0]
        pieces = max(1, n_tiles // entries)
        n_chunks = entries * pieces
        assert n_chunks <= n_tiles + 1

        def piece_of(j):
            c = jnp.minimum(j, n_chunks - 1)
            return c // pieces, c % pieces

        next_specs, tail_specs = [], []
        for src, tail in zip(shift_src, shift_tail):
            t_rows = tail.shape[1]
            p_rows = src.shape[1] // pieces
            assert src.shape[0] == tail.shape[0] == entries and src.shape[2] == tail.shape[2] == LANES
            assert src.shape[1] % pieces == 0 and p_rows % t_rows == 0 and t_rows % 8 == 0
            piece_spec = pl.BlockSpec((None, p_rows, LANES), lambda j: (*piece_of(j), 0))
            last_t_block = src.shape[1] // t_rows - 1

            def next_idx(j, per_piece=p_rows // t_rows, last=last_t_block):
                entry, piece = piece_of(j)
                return (entry, jnp.minimum((piece + 1) * per_piece, last), 0)

            copy_in.append(piece_spec)
            next_specs.append(pl.BlockSpec((None, t_rows, LANES), next_idx))
            tail_specs.append(pl.BlockSpec((None, t_rows, LANES), lambda j: (piece_of(j)[0], 0, 0)))
            copy_out.append(piece_spec)
        copy_in += next_specs + tail_specs

    def prev_tile(j):
        return jnp.maximum(j - 1, 0)

    outs = pl.pallas_call(
        functools.partial(_proj_gla_fused_kernel, steps=steps, chunk=chunk, col_chunk=512, n_shift=n_shift,
                          pieces=pieces, n_chunks=n_chunks),
        grid=(n_tiles + 1,),
        in_specs=[
            pl.BlockSpec((tm, d), lambda j: (jnp.minimum(j, n_tiles - 1), 0)),
            _const_spec(g_pre.shape),
            _const_spec(w_a.shape),
            _const_spec(w_lr.shape),
            _const_spec(w_lr2.shape),
            _const_spec(b_lr.shape),
            pl.BlockSpec((1, GLA_HEADS, GLA_DK, GLA_DV), lambda j: (prev_tile(j) // steps, 0, 0, 0)),
            _const_spec(gn.shape),
        ] + copy_in,
        out_specs=[
            pl.BlockSpec((tm, GLA_VAL), lambda j: (prev_tile(j), 0)),
            pl.BlockSpec((1, GLA_HEADS, GLA_DK, GLA_DV), lambda j: (prev_tile(j) // steps, 0, 0, 0)),
        ] + copy_out,
        out_shape=[
            jax.ShapeDtypeStruct((n, GLA_VAL), BF16),
            jax.ShapeDtypeStruct((nb, GLA_HEADS, GLA_DK, GLA_DV), F32),
        ] + [jax.ShapeDtypeStruct(src.shape, src.dtype) for src in shift_src],
        scratch_shapes=[
            pltpu.VMEM((tm, GLA_COLS), BF16),
            pltpu.VMEM((tm, GLA_COLS), BF16),
            pltpu.VMEM((tm, GLA_KEY), F32),
            pltpu.VMEM((tm, GLA_KEY), F32),
            pltpu.VMEM((GLA_HEADS, GLA_DV, GLA_DK), F32),
        ],
        compiler_params=_params("arbitrary"),
        name="proj_gla_fused",
    )(x2d, g_pre, w_a, w_lr, w_lr2, b_lr, s0, gn, *shift_src, *shift_src, *shift_tail)
    return outs[0], outs[1], list(outs[2:])


def _gla(qkvg, la, s0, gn, *, nb, seq, chunk, rows_per_step, nseq, out_dtype, mm_dtype, precision):
    steps = seq // rows_per_step
    assert nb % nseq == 0 and (nseq == 1 or steps == 1)
    rows = nseq * rows_per_step
    return pl.pallas_call(
        functools.partial(_gla_kernel, chunk=chunk, nchunk=rows_per_step // chunk, nseq=nseq, mm_dtype=mm_dtype,
                          precision=precision),
        grid=(nb // nseq, steps),
        in_specs=[
            pl.BlockSpec((rows, GLA_COLS), lambda b, j: (b * steps + j, 0)),
            pl.BlockSpec((rows, GLA_KEY), lambda b, j: (b * steps + j, 0)),
            pl.BlockSpec((nseq, GLA_HEADS, GLA_DK, GLA_DV), lambda b, j: (b, 0, 0, 0)),
            pl.BlockSpec((1, GLA_VAL), lambda b, j: (0, 0)),
        ],
        out_specs=[
            pl.BlockSpec((rows, GLA_VAL), lambda b, j: (b * steps + j, 0)),
            pl.BlockSpec((nseq, GLA_HEADS, GLA_DK, GLA_DV), lambda b, j: (b, 0, 0, 0)),
        ],
        out_shape=[
            jax.ShapeDtypeStruct((nb * seq, GLA_VAL), out_dtype),
            jax.ShapeDtypeStruct((nb, GLA_HEADS, GLA_DK, GLA_DV), F32),
        ],
        scratch_shapes=[pltpu.VMEM((nseq * GLA_HEADS, GLA_DV, GLA_DK), F32)],
        compiler_params=_params("parallel", "arbitrary"),
        name="gla",
    )(qkvg, la, s0, gn)


def _pack_lse(lses):
    rows = lses[0].shape[0]
    lane_head = lax.broadcasted_iota(jnp.int32, (rows, LANES), 1) >> _LOG2_LSE_LANES
    packed = jnp.broadcast_to(lses[0], (rows, LANES))
    for h in range(1, DIL_HEADS):
        packed = jnp.where(lane_head == h, lses[h], packed)
    return packed


def _dil_prompt_kernel(a_ref, o_ref, lse_ref, kvp_ref, *, nres, nblk):
    first = pl.program_id(2) == 0

    @pl.when(first)
    def _():
        kvp_ref[...] = jnp.zeros_like(kvp_ref)

    i_idx = lax.broadcasted_iota(jnp.int32, (DIL_SPAN, 2 * DIL_SPAN), 0)
    j_idx = lax.broadcasted_iota(jnp.int32, (DIL_SPAN, 2 * DIL_SPAN), 1)
    diff = j_idx - i_idx
    band = jnp.logical_and(diff >= 0, diff <= DIL_SPAN)
    band_first = jnp.logical_and(diff >= jnp.where(first, DIL_SPAN - i_idx, 0), diff <= DIL_SPAN)
    scale = DIL_HD ** -0.5
    neg_inf = -jnp.inf
    units = [(res, jb, h) for res in range(nres) for jb in range(nblk) for h in range(DIL_HEADS)]

    def window(res, jb, c0):
        if jb == 0:
            prev = kvp_ref[res, :, c0 - DIL_WIDTH:c0 - DIL_WIDTH + DIL_HD]
            return jnp.concatenate([prev, a_ref[res, 0:DIL_SPAN, c0:c0 + DIL_HD]], axis=0)
        return a_ref[res, (jb - 1) * DIL_SPAN:(jb + 1) * DIL_SPAN, c0:c0 + DIL_HD]

    scores = []
    for res, jb, h in units:
        q = a_ref[res, jb * DIL_SPAN:(jb + 1) * DIL_SPAN, h * DIL_HD:(h + 1) * DIL_HD]
        s = _dot(q, window(res, jb, DIL_WIDTH + h * DIL_HD), _NT) * scale
        scores.append(jnp.where(band_first if jb == 0 else band, s, neg_inf))
    maxes = [jnp.max(s, axis=-1, keepdims=True) for s in scores]
    probs = [jnp.exp(s - m) for s, m in zip(scores, maxes)]
    sums = [jnp.sum(p, axis=-1, keepdims=True) for p in probs]
    lses = {}
    for (res, jb, h), p, m, l in zip(units, probs, maxes, sums):
        acc = _dot(p.astype(BF16), window(res, jb, 2 * DIL_WIDTH + h * DIL_HD))
        o_ref[res, jb * DIL_SPAN:(jb + 1) * DIL_SPAN, h * DIL_HD:(h + 1) * DIL_HD] = acc * (1.0 / l)
        lses[res, jb, h] = m + jnp.log(l)
    for res in range(nres):
        for jb in range(nblk):
            lse_ref[res, jb * DIL_SPAN:(jb + 1) * DIL_SPAN, :] = _pack_lse([lses[res, jb, h] for h in range(DIL_HEADS)])
        kvp_ref[res] = a_ref[res, (nblk - 1) * DIL_SPAN:nblk * DIL_SPAN, DIL_WIDTH:QKV_COLS]


def _dil_prompt(a4, *, rows_per_step=1024):
    nb, dil, length, _ = a4.shape
    rows = min(rows_per_step, length)
    nres = min(dil, rows_per_step // rows)

    def spec(cols):
        return pl.BlockSpec((None, nres, rows, cols), lambda b, r, n: (b, r, n, 0))

    return pl.pallas_call(
        functools.partial(_dil_prompt_kernel, nres=nres, nblk=rows // DIL_SPAN),
        grid=(nb, dil // nres, length // rows),
        in_specs=[spec(QKV_COLS)],
        out_specs=[spec(DIL_WIDTH), spec(LANES)],
        out_shape=[
            jax.ShapeDtypeStruct((nb, dil, length, DIL_WIDTH), F32),
            jax.ShapeDtypeStruct((nb, dil, length, LANES), F32),
        ],
        scratch_shapes=[pltpu.VMEM((nres, DIL_SPAN, KV_COLS), BF16)],
        compiler_params=_params("parallel", "parallel", "arbitrary"),
        name=f"dil_prompt_d{dil}",
    )(a4)


def _dil_sample_kernel(*refs, t_new):
    a_refs, cache_refs = refs[:N_GROUPS], refs[N_GROUPS:2 * N_GROUPS]
    o_refs, lse_refs = refs[2 * N_GROUPS:3 * N_GROUPS], refs[3 * N_GROUPS:]
    rpt = 2 * DIL_HEADS
    scale = DIL_HD ** -0.5
    neg_inf = -jnp.inf
    tq_n = lax.broadcasted_iota(jnp.int32, (t_new, t_new), 0)
    j_n = lax.broadcasted_iota(jnp.int32, (t_new, t_new), 1)

    def cache_head(g, c):
        ref = cache_refs[g]
        if len(ref.shape) == 2:
            return ref[pl.ds(c, ref.shape[0] // rpt, stride=rpt), :].astype(BF16)
        x = ref[:, pl.ds(c, t_new, stride=rpt), :]
        return x.reshape(ref.shape[0] * t_new, DIL_HD).astype(BF16)

    valid_c, valid_n = [], []
    for g, (_, dil) in enumerate(DIL_GROUPS):
        ref = cache_refs[g]
        n_keys = ref.shape[0] // rpt if len(ref.shape) == 2 else ref.shape[0] * t_new
        tq = lax.broadcasted_iota(jnp.int32, (t_new, n_keys), 0)
        key = lax.broadcasted_iota(jnp.int32, (t_new, n_keys), 1)
        if len(ref.shape) == 2:
            valid_c.append(jnp.logical_and(((key - tq) & (dil - 1)) == 0, key >= tq))
        else:
            valid_c.append((key & (t_new - 1)) == tq)
        valid_n.append(jnp.logical_and(((tq_n - j_n) & (dil - 1)) == 0, j_n <= tq_n))

    units = [(g, h) for g in range(N_GROUPS) for h in range(DIL_HEADS)]
    scores = []
    for g, h in units:
        q = a_refs[g][:, h * DIL_HD:(h + 1) * DIL_HD].astype(BF16)
        k_new = a_refs[g][:, DIL_WIDTH + h * DIL_HD:DIL_WIDTH + (h + 1) * DIL_HD].astype(BF16)
        s_c = jnp.where(valid_c[g], _dot(q, cache_head(g, h), _NT) * scale, neg_inf)
        s_n = jnp.where(valid_n[g], _dot(q, k_new, _NT) * scale, neg_inf)
        scores.append((s_c, s_n))
    maxes = [jnp.maximum(jnp.max(s_c, axis=-1, keepdims=True), jnp.max(s_n, axis=-1, keepdims=True))
             for s_c, s_n in scores]
    probs = [(jnp.exp(s_c - m), jnp.exp(s_n - m)) for (s_c, s_n), m in zip(scores, maxes)]
    sums = [jnp.sum(p_c, axis=-1, keepdims=True) + jnp.sum(p_n, axis=-1, keepdims=True) for p_c, p_n in probs]
    lses = {}
    for (g, h), (p_c, p_n), m, l in zip(units, probs, maxes, sums):
        v_new = a_refs[g][:, 2 * DIL_WIDTH + h * DIL_HD:2 * DIL_WIDTH + (h + 1) * DIL_HD].astype(BF16)
        acc = _dot(p_c.astype(BF16), cache_head(g, DIL_HEADS + h)) + _dot(p_n.astype(BF16), v_new)
        o_refs[g][:, h * DIL_HD:(h + 1) * DIL_HD] = acc * (1.0 / l)
        lses[g, h] = m + jnp.log(l)
    for g in range(N_GROUPS):
        lse_refs[g][...] = _pack_lse([lses[g, h] for h in range(DIL_HEADS)])


def _cache_rows(cache):
    nb, lb = cache.shape[:2]
    return cache.reshape(nb, lb * 2 * DIL_HEADS, DIL_HD)


def _dil_sample(a_g, caches, *, t_new):
    nb = caches[0].shape[0]
    rpt = 2 * DIL_HEADS
    assert t_new & (t_new - 1) == 0
    cache_args, cache_specs = [], []
    for cache, (_, dil) in zip(caches, DIL_GROUPS):
        lb = cache.shape[1]
        assert lb == dil * DIL_SPAN
        if dil > t_new:
            assert dil % t_new == 0
            cache_args.append(cache.reshape(nb, lb // dil, dil * rpt, DIL_HD))
            cache_specs.append(pl.BlockSpec((None, lb // dil, t_new * rpt, DIL_HD), lambda b: (b, 0, 0, 0)))
        else:
            cache_args.append(_cache_rows(cache))
            cache_specs.append(pl.BlockSpec((None, lb * rpt, DIL_HD), lambda b: (b, 0, 0)))
    outs = pl.pallas_call(
        functools.partial(_dil_sample_kernel, t_new=t_new),
        grid=(nb,),
        in_specs=[pl.BlockSpec((t_new, QKV_COLS), lambda b: (b, 0))] * N_GROUPS + cache_specs,
        out_specs=[pl.BlockSpec((t_new, DIL_WIDTH), lambda b: (b, 0))] * N_GROUPS
        + [pl.BlockSpec((t_new, LANES), lambda b: (b, 0))] * N_GROUPS,
        out_shape=[jax.ShapeDtypeStruct((nb * t_new, DIL_WIDTH), F32)] * N_GROUPS
        + [jax.ShapeDtypeStruct((nb * t_new, LANES), F32)] * N_GROUPS,
        compiler_params=_params("parallel"),
        name="dil_sample",
    )(*a_g, *cache_args)
    return outs[:N_GROUPS], outs[N_GROUPS:]


def _token_order(ref, scratch_refs):
    if len(ref.shape) == 2:
        return ref[...]
    dil, rows_per_class, _ = ref.shape
    if dil == 1:
        return ref[0]
    scratch = scratch_refs.pop()
    n_slabs = scratch.shape[0]
    for res in range(dil):
        for s in range(n_slabs):
            scratch[s, pl.ds(res, rows_per_class, stride=dil), :] = ref[res, :, s * LANES:(s + 1) * LANES]
    return jnp.concatenate([scratch[s] for s in range(n_slabs)], axis=1) if n_slabs > 1 else scratch[0]


def _final_kernel(ya_ref, o0_ref, o1_ref, o2_ref, l0_ref, l1_ref, l2_ref, gates_ref, x_ref, pe_ref,
                  wa_ref, wb_ref, wo_ref, wpp_ref, wpg_ref, gpost_ref, y_ref, *scratch_refs):
    o_scratch = [s for s in scratch_refs if s.shape[0] == DIL_WIDTH // LANES]
    l_scratch = [s for s in scratch_refs if s.shape[0] == 1]
    lse = [_token_order(r, l_scratch) for r in (l0_ref, l1_ref, l2_ref)]
    o_groups = [_token_order(r, o_scratch) for r in (o0_ref, o1_ref, o2_ref)]
    m = jnp.maximum(jnp.maximum(lse[0], lse[1]), lse[2])
    e = [jnp.exp(x - m) for x in lse]
    inv = 1.0 / (e[0] + e[1] + e[2])
    heads = []
    for h in range(DIL_HEADS):
        hs = slice(h * DIL_HD, (h + 1) * DIL_HD)
        lane = h * LSE_LANES_PER_HEAD
        acc = None
        for eg, o_g in zip(e, o_groups):
            term = (eg * inv)[:, lane:lane + 1] * o_g[:, hs]
            acc = term if acc is None else acc + term
        heads.append(acc)
    o_b = jnp.concatenate(heads, axis=1)
    dg = gates_ref[:, 0:DIL_WIDTH].astype(F32)
    d = x_ref.shape[1]
    ga = gates_ref[:, DIL_WIDTH:DIL_WIDTH + d].astype(F32)
    gb = gates_ref[:, DIL_WIDTH + d:DIL_WIDTH + 2 * d].astype(F32)
    y_a = _dot(ya_ref[...].astype(BF16), wa_ref[...])
    y_b = _dot((o_b * (dg * _sigmoid(dg))).astype(BF16), wb_ref[...])
    merged = _sigmoid(ga) * y_a + _sigmoid(gb) * y_b
    h = x_ref[...] + _rms(_dot(merged.astype(BF16), wo_ref[...]), gpost_ref[...])
    ple = _dot(pe_ref[...].astype(BF16), wpp_ref[...])
    y_ref[...] = h + ple * _sigmoid(_dot(h.astype(BF16), wpg_ref[...]))


def _final(ya, os_, lses, gates, x2d, pe2d, wa, wb, wo, wpp, wpg, g_post, *, tm):
    n, d = x2d.shape

    def rows(cols):
        return pl.BlockSpec((tm, cols), lambda i: (i, 0))

    scratch = []

    def group_spec(arr):
        if arr.ndim == 2:
            return rows(arr.shape[1])
        _, dil, length, cols = arr.shape
        tiles_per_seq = dil * length // tm
        if dil > 1:
            scratch.append(pltpu.VMEM((cols // LANES, tm, LANES), F32))
        return pl.BlockSpec((None, dil, tm // dil, cols), lambda i: (i // tiles_per_seq, 0, i % tiles_per_seq, 0))

    return pl.pallas_call(
        _final_kernel,
        grid=(n // tm,),
        in_specs=[rows(GLA_VAL)] + [group_spec(a) for a in os_] + [group_spec(a) for a in lses]
        + [rows(gates.shape[1]), rows(d), rows(pe2d.shape[1])]
        + [_const_spec(w.shape) for w in (wa, wb, wo, wpp, wpg, g_post)],
        out_specs=rows(d),
        out_shape=jax.ShapeDtypeStruct((n, d), F32),
        scratch_shapes=scratch,
        compiler_params=_params("parallel"),
        name="final",
    )(ya, *os_, *lses, gates, x2d, pe2d, wa, wb, wo, wpp, wpg, g_post)


def _prep_weights(g_pre, g_post, w_in, w_gla_lr, b_gla_lr, g_gla_norm, w_gla_branch, w_dil_branch, w_out,
                  w_ple_proj, w_ple_gate):
    d = w_in.shape[0]
    c_lr = GLA_COLS
    c_q = c_lr + GLA_LOWRANK
    c_k = c_q + N_GROUPS * DIL_WIDTH
    c_v = c_k + N_GROUPS * DIL_WIDTH
    c_g = c_v + N_GROUPS * DIL_WIDTH
    w_t = w_in.T
    w_a = w_t[:c_lr].astype(BF16)
    w_lr = jnp.pad(w_t[c_lr:c_q].astype(BF16), ((0, LANES - GLA_LOWRANK), (0, 0)))
    w_lr2 = jnp.pad(w_gla_lr.astype(BF16), ((0, LANES - GLA_LOWRANK), (0, 0)))
    parts = []
    for g in range(N_GROUPS):
        for c in (c_q, c_k, c_v):
            parts.append(w_t[c + g * DIL_WIDTH:c + (g + 1) * DIL_WIDTH])
    parts.append(w_t[c_g:])
    w_r = jnp.concatenate(parts, axis=0).astype(BF16)
    return dict(
        g_pre=g_pre.reshape(1, d), g_post=g_post.reshape(1, d), w_a=w_a, w_lr=w_lr, w_lr2=w_lr2, w_r=w_r,
        b_lr=b_gla_lr.reshape(1, GLA_KEY), gn=g_gla_norm.reshape(1, GLA_VAL),
        wa=w_gla_branch.astype(BF16), wb=w_dil_branch.astype(BF16), wo=w_out.astype(BF16),
        wpp=w_ple_proj.astype(BF16), wpg=w_ple_gate.astype(BF16),
    )


PROMPT_TILE = 512


def _prompt_layer(x, pe, s0, w, sample_caches, sample_new_rows):
    nb, seq, d = x.shape
    n = nb * seq
    tm = PROMPT_TILE
    assert seq % tm == 0 and seq % DIL_GROUPS[-1][0] == 0
    x2d = x.reshape(n, d)
    rest = _proj_rest(x2d, w["g_pre"], w["w_r"], tm=tm, seq=seq, prompt=True)
    a_g, gates = rest[:N_GROUPS], rest[N_GROUPS]
    ns = sample_caches[0].shape[0]
    ya, s_new, next_caches = _proj_gla_fused(
        x2d, w["g_pre"], w["w_a"], w["w_lr"], w["w_lr2"], w["b_lr"], s0, w["gn"], tm=tm, seq=seq, chunk=GLA_CHUNK,
        shift_src=tuple(_cache_rows(c) for c in sample_caches),
        shift_tail=tuple(r.reshape(ns, -1, DIL_HD) for r in sample_new_rows))
    new_bufs = [kv.reshape(nb, -1, 2, DIL_HEADS, DIL_HD) for kv in rest[N_GROUPS + 1:]]
    attn = [_dil_prompt(a) for a in a_g]
    y = _final(ya, [o for o, _ in attn], [lse for _, lse in attn], gates, x2d, pe.reshape(n, -1),
               w["wa"], w["wb"], w["wo"], w["wpp"], w["wpg"], w["g_post"], tm=tm)
    next_caches = [c.reshape(old.shape) for c, old in zip(next_caches, sample_caches)]
    return y.reshape(nb, seq, d), s_new, new_bufs, next_caches


def _sample_proj(x, w):
    nb, seq, d = x.shape
    return _proj_rest(x.reshape(nb * seq, d), w["g_pre"], w["w_r"], tm=nb * seq, seq=seq, prompt=False)


def _sample_layer(x, pe, s0, caches, w, rest):
    nb, seq, d = x.shape
    n = nb * seq
    x2d = x.reshape(n, d)
    a_g, gates = rest[:N_GROUPS], rest[N_GROUPS]
    chunk = min(GLA_CHUNK, seq)
    assert seq % chunk == 0
    nseq = 8 if nb % 8 == 0 else 1
    qkvg, la = _proj_gla(x2d, w["g_pre"], w["w_a"], w["w_lr"], w["w_lr2"], w["b_lr"], tm=n, out_dtype=F32)
    ya, s_new = _gla(qkvg, la, s0, w["gn"], nb=nb, seq=seq, chunk=chunk, rows_per_step=seq, nseq=nseq,
                     out_dtype=F32, mm_dtype=BF16, precision=None)
    os_, lses = _dil_sample(a_g, caches, t_new=seq)
    y = _final(ya, os_, lses, gates, x2d, pe.reshape(n, -1),
               w["wa"], w["wb"], w["wo"], w["wpp"], w["wpg"], w["g_post"], tm=n)
    return y.reshape(nb, seq, d), s_new


def kernel(x_prompt, x_sample, state_gla, cache_kv_w128, cache_kv_w512, cache_kv_w2048, p_prompt, p_sample, g_pre, g_post, w_in, w_gla_lr, b_gla_lr, g_gla_norm, w_gla_branch, w_dil_branch, w_out, w_ple_proj, w_ple_gate):
    depth = w_in.shape[0]
    hp, hs = x_prompt, x_sample
    gla_p, gla_s = [], []
    kvp = [[] for _ in DIL_GROUPS]
    kvs = [[] for _ in DIL_GROUPS]
    for i in range(depth):
        w = _prep_weights(g_pre[i], g_post[i], w_in[i], w_gla_lr[i], b_gla_lr[i], g_gla_norm[i], w_gla_branch[i],
                          w_dil_branch[i], w_out[i], w_ple_proj[i], w_ple_gate[i])
        s0 = jnp.zeros((x_prompt.shape[0], GLA_HEADS, GLA_DK, GLA_DV), F32)
        caches = (cache_kv_w128[i], cache_kv_w512[i], cache_kv_w2048[i])
        rest_s = _sample_proj(hs, w)
        hp, sp_new, bp_new, bs_new = _prompt_layer(hp, p_prompt[i], s0, w, caches, rest_s[N_GROUPS + 1:])
        hs, ss_new = _sample_layer(hs, p_sample[i], state_gla[i], caches, w, rest_s)
        gla_p.append(sp_new)
        gla_s.append(ss_new)
        for g in range(N_GROUPS):
            kvp[g].append(bp_new[g])
            kvs[g].append(bs_new[g])
    return (hp, hs, jnp.stack(gla_p), jnp.stack(gla_s),
            jnp.stack(kvp[0]), jnp.stack(kvp[1]), jnp.stack(kvp[2]),
            jnp.stack(kvs[0]), jnp.stack(kvs[1]), jnp.stack(kvs[2]))
```

```python
import functools

import jax
import jax.numpy as jnp
from jax import lax
from jax.experimental import pallas as pl
from jax.experimental.pallas import tpu as pltpu

F32 = jnp.float32
BF16 = jnp.bfloat16

NORM_EPS = 1e-6
GLA_HEADS = 4
GLA_DK = 128
GLA_DV = 256
GLA_KEY = GLA_HEADS * GLA_DK
GLA_VAL = GLA_HEADS * GLA_DV
GLA_LOWRANK = 16
GLA_TAU = 16.0
GLA_CHUNK = 64
DIL_GROUPS = ((128, 1), (512, 4), (2048, 16))
DIL_HEADS = 4
DIL_HD = 128
DIL_WIDTH = DIL_HEADS * DIL_HD
DIL_SPAN = 128
N_GROUPS = len(DIL_GROUPS)
GLA_COLS = 2 * GLA_KEY + 2 * GLA_VAL
QKV_COLS = 3 * DIL_WIDTH
KV_COLS = 2 * DIL_WIDTH

LANES = 128
LSE_LANES_PER_HEAD = LANES // DIL_HEADS
_LOG2_LSE_LANES = LSE_LANES_PER_HEAD.bit_length() - 1
_LOG2_DIL_HD = DIL_HD.bit_length() - 1
assert 1 << _LOG2_LSE_LANES == LSE_LANES_PER_HEAD and 1 << _LOG2_DIL_HD == DIL_HD
VMEM_LIMIT_BYTES = 56 * 1024 * 1024
_CHEAP_ROW_STRIDE = 4

_NT = (((1,), (1,)), ((), ()))
_TN = (((0,), (0,)), ((), ()))


def _dot(a, b, dims=None, precision=None):
    if dims is None:
        return jnp.dot(a, b, preferred_element_type=F32, precision=precision)
    return lax.dot_general(a, b, dims, preferred_element_type=F32, precision=precision)


def _rms(xf, g):
    return xf * lax.rsqrt(jnp.mean(xf * xf, axis=-1, keepdims=True) + NORM_EPS) * g


def _sigmoid(x):
    return 1.0 / (1.0 + jnp.exp(-x))


def _const_spec(shape):
    nd = len(shape)
    return pl.BlockSpec(shape, lambda *_: (0,) * nd, pipeline_mode=pl.Buffered(1))


def _params(*sem):
    return pltpu.CompilerParams(dimension_semantics=sem, vmem_limit_bytes=VMEM_LIMIT_BYTES)


def _proj_gla_kernel(x_ref, g_ref, w_ref, wlr_ref, wlr2_ref, blr_ref, qkvg_ref, la_ref, *, col_chunk):
    xn = _rms(x_ref[...], g_ref[...]).astype(BF16)
    for c0 in range(0, GLA_COLS, col_chunk):
        qkvg_ref[:, c0:c0 + col_chunk] = _dot(xn, w_ref[c0:c0 + col_chunk, :], _NT).astype(qkvg_ref.dtype)
    glr = _dot(xn, wlr_ref[...], _NT)
    z = _dot(glr.astype(BF16), wlr2_ref[...]) + blr_ref[...]
    log_sig = jnp.minimum(z, 0.0) - jnp.log1p(jnp.exp(-jnp.abs(z)))
    la_ref[...] = log_sig * (1.0 / GLA_TAU)


def _proj_gla(x2d, g_pre, w_a, w_lr, w_lr2, b_lr, *, tm, out_dtype):
    n, d = x2d.shape
    return pl.pallas_call(
        functools.partial(_proj_gla_kernel, col_chunk=512),
        grid=(n // tm,),
        in_specs=[
            pl.BlockSpec((tm, d), lambda i: (i, 0)),
            _const_spec(g_pre.shape),
            _const_spec(w_a.shape),
            _const_spec(w_lr.shape),
            _const_spec(w_lr2.shape),
            _const_spec(b_lr.shape),
        ],
        out_specs=[
            pl.BlockSpec((tm, GLA_COLS), lambda i: (i, 0)),
            pl.BlockSpec((tm, GLA_KEY), lambda i: (i, 0)),
        ],
        out_shape=[
            jax.ShapeDtypeStruct((n, GLA_COLS), out_dtype),
            jax.ShapeDtypeStruct((n, GLA_KEY), F32),
        ],
        compiler_params=_params("parallel"),
        name="proj_gla",
    )(x2d, g_pre, w_a, w_lr, w_lr2, b_lr)


def _proj_rest_kernel(x_ref, g_ref, w_ref, *refs, tm, tiles_per_seq, keeps, prompt):
    a_refs = refs[:N_GROUPS]
    gates_ref = refs[N_GROUPS]
    kv_refs = refs[N_GROUPS + 1:2 * N_GROUPS + 1]
    stage_refs = refs[2 * N_GROUPS + 1:]

    def body(with_cache_rows):
        xn = _rms(x_ref[...], g_ref[...]).astype(BF16)
        n_staged = 0
        for g in range(N_GROUPS):
            base = g * QKV_COLS
            dil = DIL_GROUPS[g][1]
            for part in range(3):
                c0 = part * DIL_WIDTH
                r = _dot(xn, w_ref[base + c0:base + c0 + DIL_WIDTH, :], _NT)
                if not prompt:
                    a_refs[g][:, c0:c0 + DIL_WIDTH] = r
                elif dil == 1:
                    a_refs[g][0, :, c0:c0 + DIL_WIDTH] = r.astype(BF16)
                else:
                    stage = stage_refs[n_staged]
                    n_staged += 1
                    for s in range(DIL_WIDTH // LANES):
                        stage[s] = r[:, s * LANES:(s + 1) * LANES]
                    step = dil
                    if dil > _CHEAP_ROW_STRIDE:
                        step = dil // _CHEAP_ROW_STRIDE
                        assert step <= _CHEAP_ROW_STRIDE
                        stage2 = stage_refs[n_staged]
                        n_staged += 1
                        quarter = tm // _CHEAP_ROW_STRIDE
                        for c in range(_CHEAP_ROW_STRIDE):
                            for s in range(DIL_WIDTH // LANES):
                                stage2[s, c * quarter:(c + 1) * quarter, :] = (
                                    stage[s, pl.ds(c, quarter, stride=_CHEAP_ROW_STRIDE), :])
                        stage = stage2
                    for res in range(dil):
                        start = res if step == dil else (res % _CHEAP_ROW_STRIDE) * quarter + res // _CHEAP_ROW_STRIDE
                        for s in range(DIL_WIDTH // LANES):
                            a_refs[g][res, :, c0 + s * LANES:c0 + (s + 1) * LANES] = (
                                stage[s, pl.ds(start, tm // dil, stride=step), :].astype(BF16))
                if part > 0 and with_cache_rows:
                    rows = min(keeps[g], tm)
                    for h in range(DIL_HEADS):
                        kv_refs[g][pl.ds((part - 1) * DIL_HEADS + h, rows, stride=2 * DIL_HEADS), :] = (
                            r[tm - rows:, h * DIL_HD:(h + 1) * DIL_HD])
        gbase = N_GROUPS * QKV_COLS
        gcols = gates_ref.shape[1]
        for c0 in range(0, gcols, DIL_WIDTH):
            gates_ref[:, c0:c0 + DIL_WIDTH] = _dot(xn, w_ref[gbase + c0:gbase + c0 + DIL_WIDTH, :], _NT).astype(gates_ref.dtype)

    if not prompt:
        body(True)
        return
    t = pl.program_id(0) % tiles_per_seq
    in_kept_range = t >= tiles_per_seq - max(1, max(keeps) // tm)
    pl.when(in_kept_range)(lambda: body(True))
    pl.when(jnp.logical_not(in_kept_range))(lambda: body(False))


def _proj_rest(x2d, g_pre, w_r, *, tm, seq, prompt):
    n, d = x2d.shape
    nb = n // seq
    tiles_per_seq = seq // tm if prompt else None
    gate_cols = w_r.shape[0] - N_GROUPS * QKV_COLS
    keeps = tuple(min(w, seq) for (w, _) in DIL_GROUPS) if prompt else (tm,) * N_GROUPS
    rows_per_token = 2 * DIL_HEADS
    out_specs, out_shape, scratch = [], [], []
    if prompt:
        for _, dil in DIL_GROUPS:
            out_specs.append(pl.BlockSpec((None, dil, tm // dil, QKV_COLS),
                                          lambda i: (i // tiles_per_seq, 0, i % tiles_per_seq, 0)))
            out_shape.append(jax.ShapeDtypeStruct((nb, dil, seq // dil, QKV_COLS), BF16))
    else:
        out_specs += [pl.BlockSpec((tm, QKV_COLS), lambda i: (i, 0)) for _ in range(N_GROUPS)]
        out_shape += [jax.ShapeDtypeStruct((n, QKV_COLS), F32) for _ in range(N_GROUPS)]
    out_specs.append(pl.BlockSpec((tm, gate_cols), lambda i: (i, 0)))
    out_shape.append(jax.ShapeDtypeStruct((n, gate_cols), BF16 if prompt else F32))
    if not prompt:
        for _ in range(N_GROUPS):
            out_specs.append(pl.BlockSpec((tm * rows_per_token, DIL_HD), lambda i: (i, 0)))
            out_shape.append(jax.ShapeDtypeStruct((n * rows_per_token, DIL_HD), F32))
    else:
        for keep in keeps:
            if keep >= tm:
                assert keep % tm == 0
                kt = keep // tm

                def idx(i, kt=kt):
                    b, t = i // tiles_per_seq, i % tiles_per_seq
                    return (b * kt + jnp.maximum(t - (tiles_per_seq - kt), 0), 0)

                out_specs.append(pl.BlockSpec((tm * rows_per_token, DIL_HD), idx))
            else:
                out_specs.append(pl.BlockSpec((keep * rows_per_token, DIL_HD), lambda i: (i // tiles_per_seq, 0)))
            out_shape.append(jax.ShapeDtypeStruct((nb * keep * rows_per_token, DIL_HD), F32))
        n_stage = 3 * sum((1 if dil <= _CHEAP_ROW_STRIDE else 2) for _, dil in DIL_GROUPS if dil > 1)
        scratch = [pltpu.VMEM((DIL_WIDTH // LANES, tm, LANES), F32)] * n_stage
    return pl.pallas_call(
        functools.partial(_proj_rest_kernel, tm=tm, tiles_per_seq=tiles_per_seq, keeps=keeps, prompt=prompt),
        grid=(n // tm,),
        in_specs=[
            pl.BlockSpec((tm, d), lambda i: (i, 0)),
            _const_spec(g_pre.shape),
            _const_spec(w_r.shape),
        ],
        out_specs=out_specs,
        out_shape=out_shape,
        scratch_shapes=scratch,
        compiler_params=_params("arbitrary"),
        name="proj_rest",
    )(x2d, g_pre, w_r)


def _gla_kernel(qkvg_ref, la_ref, s0_ref, gn_ref, ya_ref, sfin_ref, st_ref, *, chunk, nchunk, nseq, mm_dtype, precision):
    j = pl.program_id(1)

    @pl.when(j == 0)
    def _():
        for s in range(nseq):
            for h in range(GLA_HEADS):
                st_ref[s * GLA_HEADS + h] = s0_ref[s, h].T

    _gla_compute(lambda r, c: qkvg_ref[r, c], lambda r: la_ref[r, :], st_ref, gn_ref, ya_ref,
                 chunk=chunk, nchunk=nchunk, nseq=nseq, mm_dtype=mm_dtype, precision=precision)

    @pl.when(j == pl.num_programs(1) - 1)
    def _():
        for s in range(nseq):
            for h in range(GLA_HEADS):
                sfin_ref[s, h] = st_ref[s * GLA_HEADS + h].T


def _gla_compute(*args, **kwargs):
    for _ in _gla_phases(*args, **kwargs):
        pass


def _interleave(*generators, phases_per_round=None):
    live = {i: gen for i, gen in enumerate(generators)}
    counts = phases_per_round or (1,) * len(generators)
    while live:
        for i in list(live):
            for _ in range(counts[i]):
                if next(live[i], StopIteration) is StopIteration:
                    del live[i]
                    break


def _gla_phases(load_qkvg, load_la, st_ref, gn_ref, ya_ref, *, chunk, nchunk, nseq, mm_dtype, precision):
    row = lax.broadcasted_iota(jnp.int32, (chunk, chunk), 0)
    col = lax.broadcasted_iota(jnp.int32, (chunk, chunk), 1)
    causal = row >= col
    tril = causal.astype(F32)
    qscale = GLA_DK ** -0.5

    segs = [(s, c) for s in range(nseq) for c in range(nchunk)]
    units = [(s, c, h) for s, c in segs for h in range(GLA_HEADS)]

    def rows_of(s, c):
        r0 = (s * nchunk + c) * chunk
        return slice(r0, r0 + chunk)

    def key_cols(h):
        return slice(h * GLA_DK, (h + 1) * GLA_DK)

    def v_of(s, c, h):
        return load_qkvg(rows_of(s, c), slice(2 * GLA_KEY + h * GLA_DV, 2 * GLA_KEY + (h + 1) * GLA_DV)).astype(mm_dtype)

    def cumsum(a):
        if mm_dtype != BF16:
            return _dot(tril, a, precision=lax.Precision.HIGHEST)
        hi = a.astype(BF16)
        rest = a - hi.astype(F32)
        mid = rest.astype(BF16)
        lo = (rest - mid.astype(F32)).astype(BF16)
        parts = _dot(tril.astype(BF16), jnp.concatenate([hi, mid, lo], axis=1))
        return parts[:, :GLA_KEY] + parts[:, GLA_KEY:2 * GLA_KEY] + parts[:, 2 * GLA_KEY:]

    cum = {sc: cumsum(load_la(rows_of(*sc))) for sc in segs}
    b_end = {sc: cum[sc][chunk - 1:chunk, :] for sc in segs}
    e_b = {sc: jnp.exp(cum[sc]) for sc in segs}
    e_nb = {sc: jnp.exp(-cum[sc]) for sc in segs}
    e_rest = {sc: jnp.exp(b_end[sc] - cum[sc]) for sc in segs}
    e_end = {sc: jnp.exp(b_end[sc]) for sc in segs}
    yield
    q_in, k_dec, k_end = {}, {}, {}
    for s, c, h in units:
        q = load_qkvg(rows_of(s, c), key_cols(h)).astype(F32) * qscale
        k = load_qkvg(rows_of(s, c), slice(GLA_KEY + h * GLA_DK, GLA_KEY + (h + 1) * GLA_DK)).astype(F32)
        q_in[s, c, h] = (q * e_b[s, c][:, key_cols(h)]).astype(mm_dtype)
        k_dec[s, c, h] = (k * e_nb[s, c][:, key_cols(h)]).astype(mm_dtype)
        k_end[s, c, h] = (k * e_rest[s, c][:, key_cols(h)]).astype(mm_dtype)
    yield
    att = {u: jnp.where(causal, _dot(q_in[u], k_dec[u], _NT, precision), 0.0).astype(mm_dtype) for u in units}
    yield
    o_intra = {u: _dot(att[u], v_of(*u), None, precision) for u in units}
    yield
    d_state = {u: _dot(v_of(*u), k_end[u], _TN, precision) for u in units}
    yield
    o_inter = {}
    for s in range(nseq):
        for h in range(GLA_HEADS):
            st = st_ref[s * GLA_HEADS + h]
            for c in range(nchunk):
                o_inter[s, c, h] = _dot(q_in[s, c, h], st.astype(mm_dtype), _NT, precision)
                st = st * e_end[s, c][:, key_cols(h)] + d_state[s, c, h]
            st_ref[s * GLA_HEADS + h] = st
    yield
    for i, (s, c, h) in enumerate(units):
        vs = slice(h * GLA_DV, (h + 1) * GLA_DV)
        gg = load_qkvg(rows_of(s, c), slice(2 * GLA_KEY + GLA_VAL + h * GLA_DV,
                                            2 * GLA_KEY + GLA_VAL + (h + 1) * GLA_DV)).astype(F32)
        on = _rms(o_intra[s, c, h] + o_inter[s, c, h], gn_ref[:, vs])
        ya_ref[rows_of(s, c), vs] = (on * (gg * _sigmoid(gg))).astype(ya_ref.dtype)
        if i % 8 == 7:
            yield


def _proj_gla_fused_kernel(x_ref, g_ref, w_ref, wlr_ref, wlr2_ref, blr_ref, s0_ref, gn_ref, *refs,
                           steps, chunk, col_chunk, n_shift, pieces, n_chunks):
    old_refs = refs[:n_shift]
    next_refs = refs[n_shift:2 * n_shift]
    tail_refs = refs[2 * n_shift:3 * n_shift]
    ya_ref, sfin_ref = refs[3 * n_shift:3 * n_shift + 2]
    new_refs = refs[3 * n_shift + 2:4 * n_shift + 2]
    qkvg_s0, qkvg_s1, la_s0, la_s1, st_ref = refs[4 * n_shift + 2:]
    j = pl.program_id(0)
    prev = j - 1

    def shift_pieces(rows_per_phase=1024):
        last_piece = jnp.minimum(j, n_chunks - 1) % pieces == pieces - 1
        for old, nxt, tail, new in zip(old_refs, next_refs, tail_refs, new_refs):
            p_rows, t_rows = old.shape[0], tail.shape[0]
            for r0 in range(0, p_rows - t_rows, rows_per_phase):
                r1 = min(r0 + rows_per_phase, p_rows - t_rows)
                new[r0:r1, :] = old[t_rows + r0:t_rows + r1, :]
                yield
            new[p_rows - t_rows:p_rows, :] = jnp.where(last_piece, tail[...], nxt[...])
            yield

    @pl.when(j == 0)
    def _():
        qkvg_s1[...] = jnp.zeros_like(qkvg_s1)
        la_s1[...] = jnp.zeros_like(la_s1)

    @pl.when(jnp.logical_or(j == 0, prev % steps == 0))
    def _():
        for h in range(GLA_HEADS):
            st_ref[h] = s0_ref[0, h].T

    def project(qkvg_w, la_w):
        xn = _rms(x_ref[...], g_ref[...]).astype(BF16)
        yield
        for c0 in range(0, GLA_COLS, col_chunk):
            qkvg_w[:, c0:c0 + col_chunk] = _dot(xn, w_ref[c0:c0 + col_chunk, :], _NT).astype(qkvg_w.dtype)
            yield
        glr = _dot(xn, wlr_ref[...], _NT)
        z = _dot(glr.astype(BF16), wlr2_ref[...]) + blr_ref[...]
        log_sig = jnp.minimum(z, 0.0) - jnp.log1p(jnp.exp(-jnp.abs(z)))
        la_w[...] = log_sig * (1.0 / GLA_TAU)

    def step(qkvg_w, la_w, qkvg_r, la_r):
        _interleave(
            project(qkvg_w, la_w),
            _gla_phases(lambda r, c: qkvg_r[r, c], lambda r: la_r[r, :], st_ref, gn_ref, ya_ref,
                        chunk=chunk, nchunk=x_ref.shape[0] // chunk, nseq=1, mm_dtype=BF16, precision=None),
            shift_pieces(),
            phases_per_round=(1, 2, 2))

    @pl.when(j % 2 == 0)
    def _():
        step(qkvg_s0, la_s0, qkvg_s1, la_s1)

    @pl.when(j % 2 == 1)
    def _():
        step(qkvg_s1, la_s1, qkvg_s0, la_s0)

    @pl.when(jnp.logical_and(j >= 1, prev % steps == steps - 1))
    def _():
        for h in range(GLA_HEADS):
            sfin_ref[0, h] = st_ref[h].T


def _proj_gla_fused(x2d, g_pre, w_a, w_lr, w_lr2, b_lr, s0, gn, *, tm, seq, chunk, shift_src=(), shift_tail=()):
    n, d = x2d.shape
    nb = n // seq
    steps = seq // tm
    n_tiles = n // tm
    n_shift = len(shift_src)
    pieces = n_chunks = 1
    copy_in, copy_out = [], []
    if n_shift:
        entries = shift_src[0].shape[0]
        pieces = max(1, n_tiles // entries)
        n_chunks = entries * pieces
        assert n_chunks <= n_tiles + 1

        def piece_of(j):
            c = jnp.minimum(j, n_chunks - 1)
            return c // pieces, c % pieces

        next_specs, tail_specs = [], []
        for src, tail in zip(shift_src, shift_tail):
            t_rows = tail.shape[1]
            p_rows = src.shape[1] // pieces
            assert src.shape[0] == tail.shape[0] == entries and src.shape[2] == tail.shape[2] == LANES
            assert src.shape[1] % pieces == 0 and p_rows % t_rows == 0 and t_rows % 8 == 0
            piece_spec = pl.BlockSpec((None, p_rows, LANES), lambda j: (*piece_of(j), 0))
            last_t_block = src.shape[1] // t_rows - 1

            def next_idx(j, per_piece=p_rows // t_rows, last=last_t_block):
                entry, piece = piece_of(j)
                return (entry, jnp.minimum((piece + 1) * per_piece, last), 0)

            copy_in.append(piece_spec)
            next_specs.append(pl.BlockSpec((None, t_rows, LANES), next_idx))
            tail_specs.append(pl.BlockSpec((None, t_rows, LANES), lambda j: (piece_of(j)[0], 0, 0)))
            copy_out.append(piece_spec)
        copy_in += next_specs + tail_specs

    def prev_tile(j):
        return jnp.maximum(j - 1, 0)

    outs = pl.pallas_call(
        functools.partial(_proj_gla_fused_kernel, steps=steps, chunk=chunk, col_chunk=512, n_shift=n_shift,
                          pieces=pieces, n_chunks=n_chunks),
        grid=(n_tiles + 1,),
        in_specs=[
            pl.BlockSpec((tm, d), lambda j: (jnp.minimum(j, n_tiles - 1), 0)),
            _const_spec(g_pre.shape),
            _const_spec(w_a.shape),
            _const_spec(w_lr.shape),
            _const_spec(w_lr2.shape),
            _const_spec(b_lr.shape),
            pl.BlockSpec((1, GLA_HEADS, GLA_DK, GLA_DV), lambda j: (prev_tile(j) // steps, 0, 0, 0)),
            _const_spec(gn.shape),
        ] + copy_in,
        out_specs=[
            pl.BlockSpec((tm, GLA_VAL), lambda j: (prev_tile(j), 0)),
            pl.BlockSpec((1, GLA_HEADS, GLA_DK, GLA_DV), lambda j: (prev_tile(j) // steps, 0, 0, 0)),
        ] + copy_out,
        out_shape=[
            jax.ShapeDtypeStruct((n, GLA_VAL), BF16),
            jax.ShapeDtypeStruct((nb, GLA_HEADS, GLA_DK, GLA_DV), F32),
        ] + [jax.ShapeDtypeStruct(src.shape, src.dtype) for src in shift_src],
        scratch_shapes=[
            pltpu.VMEM((tm, GLA_COLS), BF16),
            pltpu.VMEM((tm, GLA_COLS), BF16),
            pltpu.VMEM((tm, GLA_KEY), F32),
            pltpu.VMEM((tm, GLA_KEY), F32),
            pltpu.VMEM((GLA_HEADS, GLA_DV, GLA_DK), F32),
        ],
        compiler_params=_params("arbitrary"),
        name="proj_gla_fused",
    )(x2d, g_pre, w_a, w_lr, w_lr2, b_lr, s0, gn, *shift_src, *shift_src, *shift_tail)
    return outs[0], outs[1], list(outs[2:])


def _gla(qkvg, la, s0, gn, *, nb, seq, chunk, rows_per_step, nseq, out_dtype, mm_dtype, precision):
    steps = seq // rows_per_step
    assert nb % nseq == 0 and (nseq == 1 or steps == 1)
    rows = nseq * rows_per_step
    return pl.pallas_call(
        functools.partial(_gla_kernel, chunk=chunk, nchunk=rows_per_step // chunk, nseq=nseq, mm_dtype=mm_dtype,
                          precision=precision),
        grid=(nb // nseq, steps),
        in_specs=[
            pl.BlockSpec((rows, GLA_COLS), lambda b, j: (b * steps + j, 0)),
            pl.BlockSpec((rows, GLA_KEY), lambda b, j: (b * steps + j, 0)),
            pl.BlockSpec((nseq, GLA_HEADS, GLA_DK, GLA_DV), lambda b, j: (b, 0, 0, 0)),
            pl.BlockSpec((1, GLA_VAL), lambda b, j: (0, 0)),
        ],
        out_specs=[
            pl.BlockSpec((rows, GLA_VAL), lambda b, j: (b * steps + j, 0)),
            pl.BlockSpec((nseq, GLA_HEADS, GLA_DK, GLA_DV), lambda b, j: (b, 0, 0, 0)),
        ],
        out_shape=[
            jax.ShapeDtypeStruct((nb * seq, GLA_VAL), out_dtype),
            jax.ShapeDtypeStruct((nb, GLA_HEADS, GLA_DK, GLA_DV), F32),
        ],
        scratch_shapes=[pltpu.VMEM((nseq * GLA_HEADS, GLA_DV, GLA_DK), F32)],
        compiler_params=_params("parallel", "arbitrary"),
        name="gla",
    )(qkvg, la, s0, gn)


def _pack_lse(lses):
    rows = lses[0].shape[0]
    lane_head = lax.broadcasted_iota(jnp.int32, (rows, LANES), 1) >> _LOG2_LSE_LANES
    packed = jnp.broadcast_to(lses[0], (rows, LANES))
    for h in range(1, DIL_HEADS):
        packed = jnp.where(lane_head == h, lses[h], packed)
    return packed


def _dil_prompt_kernel(*refs, len_steps):
    n = N_GROUPS
    a_refs, o_refs, lse_refs, kvp_refs = refs[:n], refs[n:2 * n], refs[2 * n:3 * n], refs[3 * n:]
    step = pl.program_id(1)
    i_idx = lax.broadcasted_iota(jnp.int32, (DIL_SPAN, 2 * DIL_SPAN), 0)
    j_idx = lax.broadcasted_iota(jnp.int32, (DIL_SPAN, 2 * DIL_SPAN), 1)
    diff = j_idx - i_idx
    band = jnp.logical_and(diff >= 0, diff <= DIL_SPAN)
    band_first = []
    for g in range(n):
        first = step % len_steps[g] == 0

        @pl.when(first)
        def _(g=g):
            kvp_refs[g][...] = jnp.zeros_like(kvp_refs[g])

        band_first.append(jnp.logical_and(diff >= jnp.where(first, DIL_SPAN - i_idx, 0), diff <= DIL_SPAN))
    scale = DIL_HD ** -0.5
    neg_inf = -jnp.inf
    units = [(g, res, jb, h) for g in range(n) for res in range(a_refs[g].shape[0])
             for jb in range(a_refs[g].shape[1] // DIL_SPAN) for h in range(DIL_HEADS)]

    def window(g, res, jb, c0):
        if jb == 0:
            prev = kvp_refs[g][res, :, c0 - DIL_WIDTH:c0 - DIL_WIDTH + DIL_HD]
            return jnp.concatenate([prev, a_refs[g][res, 0:DIL_SPAN, c0:c0 + DIL_HD]], axis=0)
        return a_refs[g][res, (jb - 1) * DIL_SPAN:(jb + 1) * DIL_SPAN, c0:c0 + DIL_HD]

    scores = []
    for g, res, jb, h in units:
        q = a_refs[g][res, jb * DIL_SPAN:(jb + 1) * DIL_SPAN, h * DIL_HD:(h + 1) * DIL_HD]
        s = _dot(q, window(g, res, jb, DIL_WIDTH + h * DIL_HD), _NT) * scale
        scores.append(jnp.where(band_first[g] if jb == 0 else band, s, neg_inf))
    maxes = [jnp.max(s, axis=-1, keepdims=True) for s in scores]
    probs = [jnp.exp(s - m) for s, m in zip(scores, maxes)]
    sums = [jnp.sum(p, axis=-1, keepdims=True) for p in probs]
    lses = {}
    for (g, res, jb, h), p, m, l in zip(units, probs, maxes, sums):
        acc = _dot(p.astype(BF16), window(g, res, jb, 2 * DIL_WIDTH + h * DIL_HD))
        o_refs[g][res, jb * DIL_SPAN:(jb + 1) * DIL_SPAN, h * DIL_HD:(h + 1) * DIL_HD] = acc * (1.0 / l)
        lses[g, res, jb, h] = m + jnp.log(l)
    for g in range(n):
        nres, rows = a_refs[g].shape[:2]
        for res in range(nres):
            for jb in range(rows // DIL_SPAN):
                lse_refs[g][res, jb * DIL_SPAN:(jb + 1) * DIL_SPAN, :] = _pack_lse(
                    [lses[g, res, jb, h] for h in range(DIL_HEADS)])
            kvp_refs[g][res] = a_refs[g][res, rows - DIL_SPAN:rows, DIL_WIDTH:QKV_COLS]


def _dil_prompt(a_g, *, rows_per_step=1024):
    nb = a_g[0].shape[0]
    seq = a_g[0].shape[1] * a_g[0].shape[2]
    n_steps = seq // rows_per_step
    in_specs, o_specs, l_specs, o_shapes, l_shapes, scratch, len_steps = [], [], [], [], [], [], []
    for a4 in a_g:
        _, dil, length, _ = a4.shape
        rows = min(rows_per_step, length)
        nres = rows_per_step // rows
        steps_per_class = length // rows
        assert dil % nres == 0 and (dil // nres) * steps_per_class == n_steps

        def spec(cols, nres=nres, rows=rows, spc=steps_per_class):
            return pl.BlockSpec((None, nres, rows, cols), lambda b, s: (b, s // spc, s % spc, 0))

        in_specs.append(spec(QKV_COLS))
        o_specs.append(spec(DIL_WIDTH))
        l_specs.append(spec(LANES))
        o_shapes.append(jax.ShapeDtypeStruct((nb, dil, length, DIL_WIDTH), F32))
        l_shapes.append(jax.ShapeDtypeStruct((nb, dil, length, LANES), F32))
        scratch.append(pltpu.VMEM((nres, DIL_SPAN, KV_COLS), BF16))
        len_steps.append(steps_per_class)
    outs = pl.pallas_call(
        functools.partial(_dil_prompt_kernel, len_steps=tuple(len_steps)),
        grid=(nb, n_steps),
        in_specs=in_specs,
        out_specs=o_specs + l_specs,
        out_shape=o_shapes + l_shapes,
        scratch_shapes=scratch,
        compiler_params=_params("parallel", "arbitrary"),
        name="dil_prompt",
    )(*a_g)
    return outs[:N_GROUPS], outs[N_GROUPS:]


def _dil_sample_kernel(*refs, t_new):
    a_refs, cache_refs = refs[:N_GROUPS], refs[N_GROUPS:2 * N_GROUPS]
    o_refs, lse_refs = refs[2 * N_GROUPS:3 * N_GROUPS], refs[3 * N_GROUPS:]
    rpt = 2 * DIL_HEADS
    scale = DIL_HD ** -0.5
    neg_inf = -jnp.inf
    tq_n = lax.broadcasted_iota(jnp.int32, (t_new, t_new), 0)
    j_n = lax.broadcasted_iota(jnp.int32, (t_new, t_new), 1)

    def cache_head(g, c):
        ref = cache_refs[g]
        if len(ref.shape) == 2:
            return ref[pl.ds(c, ref.shape[0] // rpt, stride=rpt), :].astype(BF16)
        x = ref[:, pl.ds(c, t_new, stride=rpt), :]
        return x.reshape(ref.shape[0] * t_new, DIL_HD).astype(BF16)

    valid_c, valid_n = [], []
    for g, (_, dil) in enumerate(DIL_GROUPS):
        ref = cache_refs[g]
        n_keys = ref.shape[0] // rpt if len(ref.shape) == 2 else ref.shape[0] * t_new
        tq = lax.broadcasted_iota(jnp.int32, (t_new, n_keys), 0)
        key = lax.broadcasted_iota(jnp.int32, (t_new, n_keys), 1)
        if len(ref.shape) == 2:
            valid_c.append(jnp.logical_and(((key - tq) & (dil - 1)) == 0, key >= tq))
        else:
            valid_c.append((key & (t_new - 1)) == tq)
        valid_n.append(jnp.logical_and(((tq_n - j_n) & (dil - 1)) == 0, j_n <= tq_n))

    units = [(g, h) for g in range(N_GROUPS) for h in range(DIL_HEADS)]
    scores = []
    for g, h in units:
        q = a_refs[g][:, h * DIL_HD:(h + 1) * DIL_HD].astype(BF16)
        k_new = a_refs[g][:, DIL_WIDTH + h * DIL_HD:DIL_WIDTH + (h + 1) * DIL_HD].astype(BF16)
        s_c = jnp.where(valid_c[g], _dot(q, cache_head(g, h), _NT) * scale, neg_inf)
        s_n = jnp.where(valid_n[g], _dot(q, k_new, _NT) * scale, neg_inf)
        scores.append((s_c, s_n))
    maxes = [jnp.maximum(jnp.max(s_c, axis=-1, keepdims=True), jnp.max(s_n, axis=-1, keepdims=True))
             for s_c, s_n in scores]
    probs = [(jnp.exp(s_c - m), jnp.exp(s_n - m)) for (s_c, s_n), m in zip(scores, maxes)]
    sums = [jnp.sum(p_c, axis=-1, keepdims=True) + jnp.sum(p_n, axis=-1, keepdims=True) for p_c, p_n in probs]
    lses = {}
    for (g, h), (p_c, p_n), m, l in zip(units, probs, maxes, sums):
        v_new = a_refs[g][:, 2 * DIL_WIDTH + h * DIL_HD:2 * DIL_WIDTH + (h + 1) * DIL_HD].astype(BF16)
        acc = _dot(p_c.astype(BF16), cache_head(g, DIL_HEADS + h)) + _dot(p_n.astype(BF16), v_new)
        o_refs[g][:, h * DIL_HD:(h + 1) * DIL_HD] = acc * (1.0 / l)
        lses[g, h] = m + jnp.log(l)
    for g in range(N_GROUPS):
        lse_refs[g][...] = _pack_lse([lses[g, h] for h in range(DIL_HEADS)])


def _cache_rows(cache):
    nb, lb = cache.shape[:2]
    return cache.reshape(nb, lb * 2 * DIL_HEADS, DIL_HD)


def _dil_sample(a_g, caches, *, t_new):
    nb = caches[0].shape[0]
    rpt = 2 * DIL_HEADS
    assert t_new & (t_new - 1) == 0
    cache_args, cache_specs = [], []
    for cache, (_, dil) in zip(caches, DIL_GROUPS):
        lb = cache.shape[1]
        assert lb == dil * DIL_SPAN
        if dil > t_new:
            assert dil % t_new == 0
            cache_args.append(cache.reshape(nb, lb // dil, dil * rpt, DIL_HD))
            cache_specs.append(pl.BlockSpec((None, lb // dil, t_new * rpt, DIL_HD), lambda b: (b, 0, 0, 0)))
        else:
            cache_args.append(_cache_rows(cache))
            cache_specs.append(pl.BlockSpec((None, lb * rpt, DIL_HD), lambda b: (b, 0, 0)))
    outs = pl.pallas_call(
        functools.partial(_dil_sample_kernel, t_new=t_new),
        grid=(nb,),
        in_specs=[pl.BlockSpec((t_new, QKV_COLS), lambda b: (b, 0))] * N_GROUPS + cache_specs,
        out_specs=[pl.BlockSpec((t_new, DIL_WIDTH), lambda b: (b, 0))] * N_GROUPS
        + [pl.BlockSpec((t_new, LANES), lambda b: (b, 0))] * N_GROUPS,
        out_shape=[jax.ShapeDtypeStruct((nb * t_new, DIL_WIDTH), F32)] * N_GROUPS
        + [jax.ShapeDtypeStruct((nb * t_new, LANES), F32)] * N_GROUPS,
        compiler_params=_params("parallel"),
        name="dil_sample",
    )(*a_g, *cache_args)
    return outs[:N_GROUPS], outs[N_GROUPS:]


def _token_order(ref, scratch_refs):
    if len(ref.shape) == 2:
        return lambda rows: ref[rows, :]
    dil, rows_per_class, _ = ref.shape
    if dil == 1:
        return lambda rows: ref[0, rows, :]
    scratch = scratch_refs.pop()
    n_slabs = scratch.shape[0]
    for res in range(dil):
        for s in range(n_slabs):
            scratch[s, pl.ds(res, rows_per_class, stride=dil), :] = ref[res, :, s * LANES:(s + 1) * LANES]
    if n_slabs == 1:
        return lambda rows: scratch[0, rows, :]
    return lambda rows: jnp.concatenate([scratch[s, rows, :] for s in range(n_slabs)], axis=1)


def _final_kernel(ya_ref, o0_ref, o1_ref, o2_ref, l0_ref, l1_ref, l2_ref, gates_ref, x_ref, pe_ref,
                  wa_ref, wb_ref, wo_ref, wpp_ref, wpg_ref, gpost_ref, y_ref, *scratch_refs, n_parts):
    o_scratch = [s for s in scratch_refs if s.shape[0] == DIL_WIDTH // LANES]
    l_scratch = [s for s in scratch_refs if s.shape[0] == 1]
    lse_of = [_token_order(r, l_scratch) for r in (l0_ref, l1_ref, l2_ref)]
    o_of = [_token_order(r, o_scratch) for r in (o0_ref, o1_ref, o2_ref)]
    tm, d = x_ref.shape

    def part(rows):
        lse = [f(rows) for f in lse_of]
        m = jnp.maximum(jnp.maximum(lse[0], lse[1]), lse[2])
        e = [jnp.exp(x - m) for x in lse]
        inv = 1.0 / (e[0] + e[1] + e[2])
        o_groups = [f(rows) for f in o_of]
        heads = []
        for h in range(DIL_HEADS):
            hs = slice(h * DIL_HD, (h + 1) * DIL_HD)
            lane = h * LSE_LANES_PER_HEAD
            acc = None
            for eg, o_g in zip(e, o_groups):
                term = (eg * inv)[:, lane:lane + 1] * o_g[:, hs]
                acc = term if acc is None else acc + term
            heads.append(acc)
        o_b = jnp.concatenate(heads, axis=1)
        dg = gates_ref[rows, 0:DIL_WIDTH].astype(F32)
        y_a = _dot(ya_ref[rows, :].astype(BF16), wa_ref[...])
        y_b = _dot((o_b * (dg * _sigmoid(dg))).astype(BF16), wb_ref[...])
        yield
        ga = gates_ref[rows, DIL_WIDTH:DIL_WIDTH + d].astype(F32)
        gb = gates_ref[rows, DIL_WIDTH + d:DIL_WIDTH + 2 * d].astype(F32)
        merged = _sigmoid(ga) * y_a + _sigmoid(gb) * y_b
        t = _dot(merged.astype(BF16), wo_ref[...])
        ple = _dot(pe_ref[rows, :].astype(BF16), wpp_ref[...])
        yield
        h = x_ref[rows, :] + _rms(t, gpost_ref[...])
        gate = _dot(h.astype(BF16), wpg_ref[...])
        yield
        y_ref[rows, :] = h + ple * _sigmoid(gate)

    slab = tm // n_parts
    _interleave(*[part(slice(p * slab, (p + 1) * slab)) for p in range(n_parts)])


def _final(ya, os_, lses, gates, x2d, pe2d, wa, wb, wo, wpp, wpg, g_post, *, tm):
    n, d = x2d.shape

    def rows(cols):
        return pl.BlockSpec((tm, cols), lambda i: (i, 0))

    scratch = []

    def group_spec(arr):
        if arr.ndim == 2:
            return rows(arr.shape[1])
        _, dil, length, cols = arr.shape
        tiles_per_seq = dil * length // tm
        if dil > 1:
            scratch.append(pltpu.VMEM((cols // LANES, tm, LANES), F32))
        return pl.BlockSpec((None, dil, tm // dil, cols), lambda i: (i // tiles_per_seq, 0, i % tiles_per_seq, 0))

    return pl.pallas_call(
        functools.partial(_final_kernel, n_parts=1),
        grid=(n // tm,),
        in_specs=[rows(GLA_VAL)] + [group_spec(a) for a in os_] + [group_spec(a) for a in lses]
        + [rows(gates.shape[1]), rows(d), rows(pe2d.shape[1])]
        + [_const_spec(w.shape) for w in (wa, wb, wo, wpp, wpg, g_post)],
        out_specs=rows(d),
        out_shape=jax.ShapeDtypeStruct((n, d), F32),
        scratch_shapes=scratch,
        compiler_params=_params("parallel"),
        name="final",
    )(ya, *os_, *lses, gates, x2d, pe2d, wa, wb, wo, wpp, wpg, g_post)


def _prep_weights(g_pre, g_post, w_in, w_gla_lr, b_gla_lr, g_gla_norm, w_gla_branch, w_dil_branch, w_out,
                  w_ple_proj, w_ple_gate):
    d = w_in.shape[0]
    c_lr = GLA_COLS
    c_q = c_lr + GLA_LOWRANK
    c_k = c_q + N_GROUPS * DIL_WIDTH
    c_v = c_k + N_GROUPS * DIL_WIDTH
    c_g = c_v + N_GROUPS * DIL_WIDTH
    w_t = w_in.T
    w_a = w_t[:c_lr].astype(BF16)
    w_lr = jnp.pad(w_t[c_lr:c_q].astype(BF16), ((0, LANES - GLA_LOWRANK), (0, 0)))
    w_lr2 = jnp.pad(w_gla_lr.astype(BF16), ((0, LANES - GLA_LOWRANK), (0, 0)))
    parts = []
    for g in range(N_GROUPS):
        for c in (c_q, c_k, c_v):
            parts.append(w_t[c + g * DIL_WIDTH:c + (g + 1) * DIL_WIDTH])
    parts.append(w_t[c_g:])
    w_r = jnp.concatenate(parts, axis=0).astype(BF16)
    return dict(
        g_pre=g_pre.reshape(1, d), g_post=g_post.reshape(1, d), w_a=w_a, w_lr=w_lr, w_lr2=w_lr2, w_r=w_r,
        b_lr=b_gla_lr.reshape(1, GLA_KEY), gn=g_gla_norm.reshape(1, GLA_VAL),
        wa=w_gla_branch.astype(BF16), wb=w_dil_branch.astype(BF16), wo=w_out.astype(BF16),
        wpp=w_ple_proj.astype(BF16), wpg=w_ple_gate.astype(BF16),
    )


PROMPT_TILE = 512


def _prompt_layer(x, pe, s0, w, sample_caches, sample_new_rows):
    nb, seq, d = x.shape
    n = nb * seq
    tm = PROMPT_TILE
    assert seq % tm == 0 and seq % DIL_GROUPS[-1][0] == 0
    x2d = x.reshape(n, d)
    rest = _proj_rest(x2d, w["g_pre"], w["w_r"], tm=tm, seq=seq, prompt=True)
    a_g, gates = rest[:N_GROUPS], rest[N_GROUPS]
    ns = sample_caches[0].shape[0]
    ya, s_new, next_caches = _proj_gla_fused(
        x2d, w["g_pre"], w["w_a"], w["w_lr"], w["w_lr2"], w["b_lr"], s0, w["gn"], tm=tm, seq=seq, chunk=GLA_CHUNK,
        shift_src=tuple(_cache_rows(c) for c in sample_caches),
        shift_tail=tuple(r.reshape(ns, -1, DIL_HD) for r in sample_new_rows))
    new_bufs = [kv.reshape(nb, -1, 2, DIL_HEADS, DIL_HD) for kv in rest[N_GROUPS + 1:]]
    os_, lses = _dil_prompt(a_g)
    y = _final(ya, os_, lses, gates, x2d, pe.reshape(n, -1),
               w["wa"], w["wb"], w["wo"], w["wpp"], w["wpg"], w["g_post"], tm=tm)
    next_caches = [c.reshape(old.shape) for c, old in zip(next_caches, sample_caches)]
    return y.reshape(nb, seq, d), s_new, new_bufs, next_caches


def _sample_proj(x, w):
    nb, seq, d = x.shape
    return _proj_rest(x.reshape(nb * seq, d), w["g_pre"], w["w_r"], tm=nb * seq, seq=seq, prompt=False)


def _sample_layer(x, pe, s0, caches, w, rest):
    nb, seq, d = x.shape
    n = nb * seq
    x2d = x.reshape(n, d)
    a_g, gates = rest[:N_GROUPS], rest[N_GROUPS]
    chunk = min(GLA_CHUNK, seq)
    assert seq % chunk == 0
    nseq = 8 if nb % 8 == 0 else 1
    qkvg, la = _proj_gla(x2d, w["g_pre"], w["w_a"], w["w_lr"], w["w_lr2"], w["b_lr"], tm=n, out_dtype=F32)
    ya, s_new = _gla(qkvg, la, s0, w["gn"], nb=nb, seq=seq, chunk=chunk, rows_per_step=seq, nseq=nseq,
                     out_dtype=F32, mm_dtype=BF16, precision=None)
    os_, lses = _dil_sample(a_g, caches, t_new=seq)
    y = _final(ya, os_, lses, gates, x2d, pe.reshape(n, -1),
               w["wa"], w["wb"], w["wo"], w["wpp"], w["wpg"], w["g_post"], tm=n)
    return y.reshape(nb, seq, d), s_new


def kernel(x_prompt, x_sample, state_gla, cache_kv_w128, cache_kv_w512, cache_kv_w2048, p_prompt, p_sample, g_pre, g_post, w_in, w_gla_lr, b_gla_lr, g_gla_norm, w_gla_branch, w_dil_branch, w_out, w_ple_proj, w_ple_gate):
    depth = w_in.shape[0]
    hp, hs = x_prompt, x_sample
    gla_p, gla_s = [], []
    kvp = [[] for _ in DIL_GROUPS]
    kvs = [[] for _ in DIL_GROUPS]
    for i in range(depth):
        w = _prep_weights(g_pre[i], g_post[i], w_in[i], w_gla_lr[i], b_gla_lr[i], g_gla_norm[i], w_gla_branch[i],
                          w_dil_branch[i], w_out[i], w_ple_proj[i], w_ple_gate[i])
        s0 = jnp.zeros((x_prompt.shape[0], GLA_HEADS, GLA_DK, GLA_DV), F32)
        caches = (cache_kv_w128[i], cache_kv_w512[i], cache_kv_w2048[i])
        rest_s = _sample_proj(hs, w)
        hp, sp_new, bp_new, bs_new = _prompt_layer(hp, p_prompt[i], s0, w, caches, rest_s[N_GROUPS + 1:])
        hs, ss_new = _sample_layer(hs, p_sample[i], state_gla[i], caches, w, rest_s)
        gla_p.append(sp_new)
        gla_s.append(ss_new)
        for g in range(N_GROUPS):
            kvp[g].append(bp_new[g])
            kvs[g].append(bs_new[g])
    return (hp, hs, jnp.stack(gla_p), jnp.stack(gla_s),
            jnp.stack(kvp[0]), jnp.stack(kvp[1]), jnp.stack(kvp[2]),
            jnp.stack(kvs[0]), jnp.stack(kvs[1]), jnp.stack(kvs[2]))
```

```python
import functools

import jax
import jax.numpy as jnp
from jax import lax
from jax.experimental import pallas as pl
from jax.experimental.pallas import tpu as pltpu

F32 = jnp.float32
BF16 = jnp.bfloat16

NORM_EPS = 1e-6
GLA_HEADS = 4
GLA_DK = 128
GLA_DV = 256
GLA_KEY = GLA_HEADS * GLA_DK
GLA_VAL = GLA_HEADS * GLA_DV
GLA_LOWRANK = 16
GLA_TAU = 16.0
GLA_CHUNK = 64
DIL_GROUPS = ((128, 1), (512, 4), (2048, 16))
DIL_HEADS = 4
DIL_HD = 128
DIL_WIDTH = DIL_HEADS * DIL_HD
DIL_SPAN = 128
N_GROUPS = len(DIL_GROUPS)
GLA_COLS = 2 * GLA_KEY + 2 * GLA_VAL
QKV_COLS = 3 * DIL_WIDTH
KV_COLS = 2 * DIL_WIDTH

LANES = 128
LSE_LANES_PER_HEAD = LANES // DIL_HEADS
_LOG2_LSE_LANES = LSE_LANES_PER_HEAD.bit_length() - 1
assert 1 << _LOG2_LSE_LANES == LSE_LANES_PER_HEAD
VMEM_LIMIT_BYTES = 56 * 1024 * 1024
_CHEAP_ROW_STRIDE = 4

_NT = (((1,), (1,)), ((), ()))
_TN = (((0,), (0,)), ((), ()))


def _dot(a, b, dims=None, precision=None):
    if dims is None:
        return jnp.dot(a, b, preferred_element_type=F32, precision=precision)
    return lax.dot_general(a, b, dims, preferred_element_type=F32, precision=precision)


def _rms(xf, g):
    return xf * lax.rsqrt(jnp.mean(xf * xf, axis=-1, keepdims=True) + NORM_EPS) * g


def _sigmoid(x):
    return 1.0 / (1.0 + jnp.exp(-x))


def _const_spec(shape):
    nd = len(shape)
    return pl.BlockSpec(shape, lambda *_: (0,) * nd, pipeline_mode=pl.Buffered(1))


def _params(*sem):
    return pltpu.CompilerParams(dimension_semantics=sem, vmem_limit_bytes=VMEM_LIMIT_BYTES)


def _proj_gla_kernel(x_ref, g_ref, w_ref, wlr_ref, wlr2_ref, blr_ref, qkvg_ref, la_ref, *, col_chunk):
    xn = _rms(x_ref[...], g_ref[...]).astype(BF16)
    for c0 in range(0, GLA_COLS, col_chunk):
        qkvg_ref[:, c0:c0 + col_chunk] = _dot(xn, w_ref[c0:c0 + col_chunk, :], _NT).astype(qkvg_ref.dtype)
    glr = _dot(xn, wlr_ref[...], _NT)
    z = _dot(glr.astype(BF16), wlr2_ref[...]) + blr_ref[...]
    log_sig = jnp.minimum(z, 0.0) - jnp.log1p(jnp.exp(-jnp.abs(z)))
    la_ref[...] = log_sig * (1.0 / GLA_TAU)


def _proj_gla(x2d, g_pre, w_a, w_lr, w_lr2, b_lr, *, tm, out_dtype):
    n, d = x2d.shape
    return pl.pallas_call(
        functools.partial(_proj_gla_kernel, col_chunk=512),
        grid=(n // tm,),
        in_specs=[
            pl.BlockSpec((tm, d), lambda i: (i, 0)),
            _const_spec(g_pre.shape),
            _const_spec(w_a.shape),
            _const_spec(w_lr.shape),
            _const_spec(w_lr2.shape),
            _const_spec(b_lr.shape),
        ],
        out_specs=[
            pl.BlockSpec((tm, GLA_COLS), lambda i: (i, 0)),
            pl.BlockSpec((tm, GLA_KEY), lambda i: (i, 0)),
        ],
        out_shape=[
            jax.ShapeDtypeStruct((n, GLA_COLS), out_dtype),
            jax.ShapeDtypeStruct((n, GLA_KEY), F32),
        ],
        compiler_params=_params("parallel"),
        name="proj_gla",
    )(x2d, g_pre, w_a, w_lr, w_lr2, b_lr)


def _proj_rest_kernel(x_ref, g_ref, w_ref, *refs, tm, tiles_per_seq, keeps, prompt):
    a_refs = refs[:N_GROUPS]
    gates_ref = refs[N_GROUPS]
    kv_refs = refs[N_GROUPS + 1:2 * N_GROUPS + 1]
    stage_refs = refs[2 * N_GROUPS + 1:]

    def body(with_cache_rows):
        xn = _rms(x_ref[...], g_ref[...]).astype(BF16)
        n_staged = 0
        for g in range(N_GROUPS):
            base = g * QKV_COLS
            dil = DIL_GROUPS[g][1]
            for part in range(3):
                c0 = part * DIL_WIDTH
                r = _dot(xn, w_ref[base + c0:base + c0 + DIL_WIDTH, :], _NT)
                if not prompt:
                    a_refs[g][:, c0:c0 + DIL_WIDTH] = r
                elif dil == 1:
                    a_refs[g][0, :, c0:c0 + DIL_WIDTH] = r.astype(BF16)
                else:
                    stage = stage_refs[n_staged]
                    n_staged += 1
                    for s in range(DIL_WIDTH // LANES):
                        stage[s] = r[:, s * LANES:(s + 1) * LANES]
                    step = dil
                    if dil > _CHEAP_ROW_STRIDE:
                        step = dil // _CHEAP_ROW_STRIDE
                        assert step <= _CHEAP_ROW_STRIDE
                        stage2 = stage_refs[n_staged]
                        n_staged += 1
                        quarter = tm // _CHEAP_ROW_STRIDE
                        for c in range(_CHEAP_ROW_STRIDE):
                            for s in range(DIL_WIDTH // LANES):
                                stage2[s, c * quarter:(c + 1) * quarter, :] = (
                                    stage[s, pl.ds(c, quarter, stride=_CHEAP_ROW_STRIDE), :])
                        stage = stage2
                    for res in range(dil):
                        start = res if step == dil else (res % _CHEAP_ROW_STRIDE) * quarter + res // _CHEAP_ROW_STRIDE
                        for s in range(DIL_WIDTH // LANES):
                            a_refs[g][res, :, c0 + s * LANES:c0 + (s + 1) * LANES] = (
                                stage[s, pl.ds(start, tm // dil, stride=step), :].astype(BF16))
                if part > 0 and with_cache_rows:
                    rows = min(keeps[g], tm)
                    for h in range(DIL_HEADS):
                        kv_refs[g][pl.ds((part - 1) * DIL_HEADS + h, rows, stride=2 * DIL_HEADS), :] = (
                            r[tm - rows:, h * DIL_HD:(h + 1) * DIL_HD])
        gbase = N_GROUPS * QKV_COLS
        gcols = gates_ref.shape[1]
        for c0 in range(0, gcols, DIL_WIDTH):
            gates_ref[:, c0:c0 + DIL_WIDTH] = _dot(xn, w_ref[gbase + c0:gbase + c0 + DIL_WIDTH, :], _NT).astype(gates_ref.dtype)

    if not prompt:
        body(True)
        return
    t = pl.program_id(0) % tiles_per_seq
    in_kept_range = t >= tiles_per_seq - max(1, max(keeps) // tm)
    pl.when(in_kept_range)(lambda: body(True))
    pl.when(jnp.logical_not(in_kept_range))(lambda: body(False))


def _proj_rest(x2d, g_pre, w_r, *, tm, seq, prompt):
    n, d = x2d.shape
    nb = n // seq
    tiles_per_seq = seq // tm if prompt else None
    gate_cols = w_r.shape[0] - N_GROUPS * QKV_COLS
    keeps = tuple(min(w, seq) for (w, _) in DIL_GROUPS) if prompt else (tm,) * N_GROUPS
    rows_per_token = 2 * DIL_HEADS
    out_specs, out_shape, scratch = [], [], []
    if prompt:
        for _, dil in DIL_GROUPS:
            out_specs.append(pl.BlockSpec((None, dil, tm // dil, QKV_COLS),
                                          lambda i: (i // tiles_per_seq, 0, i % tiles_per_seq, 0)))
            out_shape.append(jax.ShapeDtypeStruct((nb, dil, seq // dil, QKV_COLS), BF16))
    else:
        out_specs += [pl.BlockSpec((tm, QKV_COLS), lambda i: (i, 0)) for _ in range(N_GROUPS)]
        out_shape += [jax.ShapeDtypeStruct((n, QKV_COLS), F32) for _ in range(N_GROUPS)]
    out_specs.append(pl.BlockSpec((tm, gate_cols), lambda i: (i, 0)))
    out_shape.append(jax.ShapeDtypeStruct((n, gate_cols), BF16 if prompt else F32))
    if not prompt:
        for _ in range(N_GROUPS):
            out_specs.append(pl.BlockSpec((tm * rows_per_token, DIL_HD), lambda i: (i, 0)))
            out_shape.append(jax.ShapeDtypeStruct((n * rows_per_token, DIL_HD), F32))
    else:
        for keep in keeps:
            if keep >= tm:
                assert keep % tm == 0
                kt = keep // tm

                def idx(i, kt=kt):
                    b, t = i // tiles_per_seq, i % tiles_per_seq
                    return (b * kt + jnp.maximum(t - (tiles_per_seq - kt), 0), 0)

                out_specs.append(pl.BlockSpec((tm * rows_per_token, DIL_HD), idx))
            else:
                out_specs.append(pl.BlockSpec((keep * rows_per_token, DIL_HD), lambda i: (i // tiles_per_seq, 0)))
            out_shape.append(jax.ShapeDtypeStruct((nb * keep * rows_per_token, DIL_HD), F32))
        n_stage = 3 * sum((1 if dil <= _CHEAP_ROW_STRIDE else 2) for _, dil in DIL_GROUPS if dil > 1)
        scratch = [pltpu.VMEM((DIL_WIDTH // LANES, tm, LANES), F32)] * n_stage
    return pl.pallas_call(
        functools.partial(_proj_rest_kernel, tm=tm, tiles_per_seq=tiles_per_seq, keeps=keeps, prompt=prompt),
        grid=(n // tm,),
        in_specs=[
            pl.BlockSpec((tm, d), lambda i: (i, 0)),
            _const_spec(g_pre.shape),
            _const_spec(w_r.shape),
        ],
        out_specs=out_specs,
        out_shape=out_shape,
        scratch_shapes=scratch,
        compiler_params=_params("arbitrary"),
        name="proj_rest",
    )(x2d, g_pre, w_r)


def _gla_kernel(qkvg_ref, la_ref, s0_ref, gn_ref, ya_ref, sfin_ref, st_ref, *, chunk, nchunk, nseq, mm_dtype, precision):
    j = pl.program_id(1)

    @pl.when(j == 0)
    def _():
        for s in range(nseq):
            for h in range(GLA_HEADS):
                st_ref[s * GLA_HEADS + h] = s0_ref[s, h].T

    _gla_compute(lambda r, c: qkvg_ref[r, c], lambda r: la_ref[r, :], st_ref, gn_ref, ya_ref,
                 chunk=chunk, nchunk=nchunk, nseq=nseq, mm_dtype=mm_dtype, precision=precision)

    @pl.when(j == pl.num_programs(1) - 1)
    def _():
        for s in range(nseq):
            for h in range(GLA_HEADS):
                sfin_ref[s, h] = st_ref[s * GLA_HEADS + h].T


def _gla_compute(*args, **kwargs):
    for _ in _gla_phases(*args, **kwargs):
        pass


def _interleave(*generators, phases_per_round=None):
    live = {i: gen for i, gen in enumerate(generators)}
    counts = phases_per_round or (1,) * len(generators)
    while live:
        for i in list(live):
            for _ in range(counts[i]):
                if next(live[i], StopIteration) is StopIteration:
                    del live[i]
                    break


def _gla_phases(load_qkvg, load_la, st_ref, gn_ref, ya_ref, *, chunk, nchunk, nseq, mm_dtype, precision):
    row = lax.broadcasted_iota(jnp.int32, (chunk, chunk), 0)
    col = lax.broadcasted_iota(jnp.int32, (chunk, chunk), 1)
    causal = row >= col
    tril = causal.astype(F32)
    qscale = GLA_DK ** -0.5

    segs = [(s, c) for s in range(nseq) for c in range(nchunk)]
    units = [(s, c, h) for s, c in segs for h in range(GLA_HEADS)]

    def rows_of(s, c):
        r0 = (s * nchunk + c) * chunk
        return slice(r0, r0 + chunk)

    def key_cols(h):
        return slice(h * GLA_DK, (h + 1) * GLA_DK)

    def v_of(s, c, h):
        return load_qkvg(rows_of(s, c), slice(2 * GLA_KEY + h * GLA_DV, 2 * GLA_KEY + (h + 1) * GLA_DV)).astype(mm_dtype)

    def cumsum(a):
        if mm_dtype != BF16:
            return _dot(tril, a, precision=lax.Precision.HIGHEST)
        hi = a.astype(BF16)
        rest = a - hi.astype(F32)
        mid = rest.astype(BF16)
        lo = (rest - mid.astype(F32)).astype(BF16)
        parts = _dot(tril.astype(BF16), jnp.concatenate([hi, mid, lo], axis=1))
        return parts[:, :GLA_KEY] + parts[:, GLA_KEY:2 * GLA_KEY] + parts[:, 2 * GLA_KEY:]

    cum = {sc: cumsum(load_la(rows_of(*sc))) for sc in segs}
    b_end = {sc: cum[sc][chunk - 1:chunk, :] for sc in segs}
    e_b = {sc: jnp.exp(cum[sc]) for sc in segs}
    e_nb = {sc: jnp.exp(-cum[sc]) for sc in segs}
    e_rest = {sc: jnp.exp(b_end[sc] - cum[sc]) for sc in segs}
    e_end = {sc: jnp.exp(b_end[sc]) for sc in segs}
    yield
    q_in, k_dec, k_end = {}, {}, {}
    for s, c, h in units:
        q = load_qkvg(rows_of(s, c), key_cols(h)).astype(F32) * qscale
        k = load_qkvg(rows_of(s, c), slice(GLA_KEY + h * GLA_DK, GLA_KEY + (h + 1) * GLA_DK)).astype(F32)
        q_in[s, c, h] = (q * e_b[s, c][:, key_cols(h)]).astype(mm_dtype)
        k_dec[s, c, h] = (k * e_nb[s, c][:, key_cols(h)]).astype(mm_dtype)
        k_end[s, c, h] = (k * e_rest[s, c][:, key_cols(h)]).astype(mm_dtype)
    yield
    att = {u: jnp.where(causal, _dot(q_in[u], k_dec[u], _NT, precision), 0.0).astype(mm_dtype) for u in units}
    yield
    o_intra = {u: _dot(att[u], v_of(*u), None, precision) for u in units}
    yield
    d_state = {u: _dot(v_of(*u), k_end[u], _TN, precision) for u in units}
    yield
    o_inter = {}
    for s in range(nseq):
        for h in range(GLA_HEADS):
            st = st_ref[s * GLA_HEADS + h]
            for c in range(nchunk):
                o_inter[s, c, h] = _dot(q_in[s, c, h], st.astype(mm_dtype), _NT, precision)
                st = st * e_end[s, c][:, key_cols(h)] + d_state[s, c, h]
            st_ref[s * GLA_HEADS + h] = st
    yield
    for i, (s, c, h) in enumerate(units):
        vs = slice(h * GLA_DV, (h + 1) * GLA_DV)
        gg = load_qkvg(rows_of(s, c), slice(2 * GLA_KEY + GLA_VAL + h * GLA_DV,
                                            2 * GLA_KEY + GLA_VAL + (h + 1) * GLA_DV)).astype(F32)
        on = _rms(o_intra[s, c, h] + o_inter[s, c, h], gn_ref[:, vs])
        ya_ref[rows_of(s, c), vs] = (on * (gg * _sigmoid(gg))).astype(ya_ref.dtype)
        if i % 8 == 7:
            yield


def _proj_gla_fused_kernel(x_ref, g_ref, w_ref, wlr_ref, wlr2_ref, blr_ref, s0_ref, gn_ref, *refs,
                           steps, chunk, col_chunk, n_shift, pieces, n_chunks):
    old_refs = refs[:n_shift]
    next_refs = refs[n_shift:2 * n_shift]
    tail_refs = refs[2 * n_shift:3 * n_shift]
    ya_ref, sfin_ref = refs[3 * n_shift:3 * n_shift + 2]
    new_refs = refs[3 * n_shift + 2:4 * n_shift + 2]
    qkvg_s0, qkvg_s1, la_s0, la_s1, st_ref = refs[4 * n_shift + 2:]
    j = pl.program_id(0)
    prev = j - 1

    def shift_pieces(rows_per_phase=1024):
        last_piece = jnp.minimum(j, n_chunks - 1) % pieces == pieces - 1
        for old, nxt, tail, new in zip(old_refs, next_refs, tail_refs, new_refs):
            p_rows, t_rows = old.shape[0], tail.shape[0]
            for r0 in range(0, p_rows - t_rows, rows_per_phase):
                r1 = min(r0 + rows_per_phase, p_rows - t_rows)
                new[r0:r1, :] = old[t_rows + r0:t_rows + r1, :]
                yield
            new[p_rows - t_rows:p_rows, :] = jnp.where(last_piece, tail[...], nxt[...])
            yield

    @pl.when(j == 0)
    def _():
        qkvg_s1[...] = jnp.zeros_like(qkvg_s1)
        la_s1[...] = jnp.zeros_like(la_s1)

    @pl.when(jnp.logical_or(j == 0, prev % steps == 0))
    def _():
        for h in range(GLA_HEADS):
            st_ref[h] = s0_ref[0, h].T

    def project(qkvg_w, la_w):
        xn = _rms(x_ref[...], g_ref[...]).astype(BF16)
        yield
        for c0 in range(0, GLA_COLS, col_chunk):
            qkvg_w[:, c0:c0 + col_chunk] = _dot(xn, w_ref[c0:c0 + col_chunk, :], _NT).astype(qkvg_w.dtype)
            yield
        glr = _dot(xn, wlr_ref[...], _NT)
        z = _dot(glr.astype(BF16), wlr2_ref[...]) + blr_ref[...]
        log_sig = jnp.minimum(z, 0.0) - jnp.log1p(jnp.exp(-jnp.abs(z)))
        la_w[...] = log_sig * (1.0 / GLA_TAU)

    def step(qkvg_w, la_w, qkvg_r, la_r):
        _interleave(
            project(qkvg_w, la_w),
            _gla_phases(lambda r, c: qkvg_r[r, c], lambda r: la_r[r, :], st_ref, gn_ref, ya_ref,
                        chunk=chunk, nchunk=x_ref.shape[0] // chunk, nseq=1, mm_dtype=BF16, precision=None),
            shift_pieces(),
            phases_per_round=(1, 2, 2))

    @pl.when(j % 2 == 0)
    def _():
        step(qkvg_s0, la_s0, qkvg_s1, la_s1)

    @pl.when(j % 2 == 1)
    def _():
        step(qkvg_s1, la_s1, qkvg_s0, la_s0)

    @pl.when(jnp.logical_and(j >= 1, prev % steps == steps - 1))
    def _():
        for h in range(GLA_HEADS):
            sfin_ref[0, h] = st_ref[h].T


def _proj_gla_fused(x2d, g_pre, w_a, w_lr, w_lr2, b_lr, s0, gn, *, tm, seq, chunk, shift_src=(), shift_tail=()):
    n, d = x2d.shape
    nb = n // seq
    steps = seq // tm
    n_tiles = n // tm
    n_shift = len(shift_src)
    pieces = n_chunks = 1
    copy_in, copy_out = [], []
    if n_shift:
        entries = shift_src[0].shape[0]
        pieces = max(1, n_tiles // entries)
        n_chunks = entries * pieces
        assert n_chunks <= n_tiles + 1

        def piece_of(j):
            c = jnp.minimum(j, n_chunks - 1)
            return c // pieces, c % pieces

        next_specs, tail_specs = [], []
        for src, tail in zip(shift_src, shift_tail):
            t_rows = tail.shape[1]
            p_rows = src.shape[1] // pieces
            assert src.shape[0] == tail.shape[0] == entries and src.shape[2] == tail.shape[2] == LANES
            assert src.shape[1] % pieces == 0 and p_rows % t_rows == 0 and t_rows % 8 == 0
            piece_spec = pl.BlockSpec((None, p_rows, LANES), lambda j: (*piece_of(j), 0))
            last_t_block = src.shape[1] // t_rows - 1

            def next_idx(j, per_piece=p_rows // t_rows, last=last_t_block):
                entry, piece = piece_of(j)
                return (entry, jnp.minimum((piece + 1) * per_piece, last), 0)

            copy_in.append(piece_spec)
            next_specs.append(pl.BlockSpec((None, t_rows, LANES), next_idx))
            tail_specs.append(pl.BlockSpec((None, t_rows, LANES), lambda j: (piece_of(j)[0], 0, 0)))
            copy_out.append(piece_spec)
        copy_in += next_specs + tail_specs

    def prev_tile(j):
        return jnp.maximum(j - 1, 0)

    outs = pl.pallas_call(
        functools.partial(_proj_gla_fused_kernel, steps=steps, chunk=chunk, col_chunk=512, n_shift=n_shift,
                          pieces=pieces, n_chunks=n_chunks),
        grid=(n_tiles + 1,),
        in_specs=[
            pl.BlockSpec((tm, d), lambda j: (jnp.minimum(j, n_tiles - 1), 0)),
            _const_spec(g_pre.shape),
            _const_spec(w_a.shape),
            _const_spec(w_lr.shape),
            _const_spec(w_lr2.shape),
            _const_spec(b_lr.shape),
            pl.BlockSpec((1, GLA_HEADS, GLA_DK, GLA_DV), lambda j: (prev_tile(j) // steps, 0, 0, 0)),
            _const_spec(gn.shape),
        ] + copy_in,
        out_specs=[
            pl.BlockSpec((tm, GLA_VAL), lambda j: (prev_tile(j), 0)),
            pl.BlockSpec((1, GLA_HEADS, GLA_DK, GLA_DV), lambda j: (prev_tile(j) // steps, 0, 0, 0)),
        ] + copy_out,
        out_shape=[
            jax.ShapeDtypeStruct((n, GLA_VAL), BF16),
            jax.ShapeDtypeStruct((nb, GLA_HEADS, GLA_DK, GLA_DV), F32),
        ] + [jax.ShapeDtypeStruct(src.shape, src.dtype) for src in shift_src],
        scratch_shapes=[
            pltpu.VMEM((tm, GLA_COLS), BF16),
            pltpu.VMEM((tm, GLA_COLS), BF16),
            pltpu.VMEM((tm, GLA_KEY), F32),
            pltpu.VMEM((tm, GLA_KEY), F32),
            pltpu.VMEM((GLA_HEADS, GLA_DV, GLA_DK), F32),
        ],
        compiler_params=_params("arbitrary"),
        name="proj_gla_fused",
    )(x2d, g_pre, w_a, w_lr, w_lr2, b_lr, s0, gn, *shift_src, *shift_src, *shift_tail)
    return outs[0], outs[1], list(outs[2:])


def _gla(qkvg, la, s0, gn, *, nb, seq, chunk, rows_per_step, nseq, out_dtype, mm_dtype, precision):
    steps = seq // rows_per_step
    assert nb % nseq == 0 and (nseq == 1 or steps == 1)
    rows = nseq * rows_per_step
    return pl.pallas_call(
        functools.partial(_gla_kernel, chunk=chunk, nchunk=rows_per_step // chunk, nseq=nseq, mm_dtype=mm_dtype,
                          precision=precision),
        grid=(nb // nseq, steps),
        in_specs=[
            pl.BlockSpec((rows, GLA_COLS), lambda b, j: (b * steps + j, 0)),
            pl.BlockSpec((rows, GLA_KEY), lambda b, j: (b * steps + j, 0)),
            pl.BlockSpec((nseq, GLA_HEADS, GLA_DK, GLA_DV), lambda b, j: (b, 0, 0, 0)),
            pl.BlockSpec((1, GLA_VAL), lambda b, j: (0, 0)),
        ],
        out_specs=[
            pl.BlockSpec((rows, GLA_VAL), lambda b, j: (b * steps + j, 0)),
            pl.BlockSpec((nseq, GLA_HEADS, GLA_DK, GLA_DV), lambda b, j: (b, 0, 0, 0)),
        ],
        out_shape=[
            jax.ShapeDtypeStruct((nb * seq, GLA_VAL), out_dtype),
            jax.ShapeDtypeStruct((nb, GLA_HEADS, GLA_DK, GLA_DV), F32),
        ],
        scratch_shapes=[pltpu.VMEM((nseq * GLA_HEADS, GLA_DV, GLA_DK), F32)],
        compiler_params=_params("parallel", "arbitrary"),
        name="gla",
    )(qkvg, la, s0, gn)


def _pack_lse(lses):
    rows = lses[0].shape[0]
    lane_head = lax.broadcasted_iota(jnp.int32, (rows, LANES), 1) >> _LOG2_LSE_LANES
    packed = jnp.broadcast_to(lses[0], (rows, LANES))
    for h in range(1, DIL_HEADS):
        packed = jnp.where(lane_head == h, lses[h], packed)
    return packed


def _dil_prompt_kernel(*refs, len_steps):
    n = N_GROUPS
    a_refs, o_refs, lse_refs, kvp_refs = refs[:n], refs[n:2 * n], refs[2 * n:3 * n], refs[3 * n:]
    step = pl.program_id(1)
    i_idx = lax.broadcasted_iota(jnp.int32, (DIL_SPAN, 2 * DIL_SPAN), 0)
    j_idx = lax.broadcasted_iota(jnp.int32, (DIL_SPAN, 2 * DIL_SPAN), 1)
    diff = j_idx - i_idx
    band = jnp.logical_and(diff >= 0, diff <= DIL_SPAN)
    band_first = []
    for g in range(n):
        first = step % len_steps[g] == 0

        @pl.when(first)
        def _(g=g):
            kvp_refs[g][...] = jnp.zeros_like(kvp_refs[g])

        band_first.append(jnp.logical_and(diff >= jnp.where(first, DIL_SPAN - i_idx, 0), diff <= DIL_SPAN))
    scale = DIL_HD ** -0.5
    neg_inf = -jnp.inf
    units = [(g, res, jb, h) for g in range(n) for res in range(a_refs[g].shape[0])
             for jb in range(a_refs[g].shape[1] // DIL_SPAN) for h in range(DIL_HEADS)]

    def window(g, res, jb, c0):
        if jb == 0:
            prev = kvp_refs[g][res, :, c0 - DIL_WIDTH:c0 - DIL_WIDTH + DIL_HD]
            return jnp.concatenate([prev, a_refs[g][res, 0:DIL_SPAN, c0:c0 + DIL_HD]], axis=0)
        return a_refs[g][res, (jb - 1) * DIL_SPAN:(jb + 1) * DIL_SPAN, c0:c0 + DIL_HD]

    scores = []
    for g, res, jb, h in units:
        q = a_refs[g][res, jb * DIL_SPAN:(jb + 1) * DIL_SPAN, h * DIL_HD:(h + 1) * DIL_HD]
        s = _dot(q, window(g, res, jb, DIL_WIDTH + h * DIL_HD), _NT) * scale
        scores.append(jnp.where(band_first[g] if jb == 0 else band, s, neg_inf))
    maxes = [jnp.max(s, axis=-1, keepdims=True) for s in scores]
    probs = [jnp.exp(s - m) for s, m in zip(scores, maxes)]
    sums = [jnp.sum(p, axis=-1, keepdims=True) for p in probs]
    lses = {}
    for (g, res, jb, h), p, m, l in zip(units, probs, maxes, sums):
        acc = _dot(p.astype(BF16), window(g, res, jb, 2 * DIL_WIDTH + h * DIL_HD))
        o_refs[g][res, jb * DIL_SPAN:(jb + 1) * DIL_SPAN, h * DIL_HD:(h + 1) * DIL_HD] = acc * (1.0 / l)
        lses[g, res, jb, h] = m + jnp.log(l)
    for g in range(n):
        nres, rows = a_refs[g].shape[:2]
        for res in range(nres):
            for jb in range(rows // DIL_SPAN):
                lse_refs[g][res, jb * DIL_SPAN:(jb + 1) * DIL_SPAN, :] = _pack_lse(
                    [lses[g, res, jb, h] for h in range(DIL_HEADS)])
            kvp_refs[g][res] = a_refs[g][res, rows - DIL_SPAN:rows, DIL_WIDTH:QKV_COLS]


def _dil_prompt(a_g, *, rows_per_step=1024):
    nb = a_g[0].shape[0]
    seq = a_g[0].shape[1] * a_g[0].shape[2]
    n_steps = seq // rows_per_step
    in_specs, o_specs, l_specs, o_shapes, l_shapes, scratch, len_steps = [], [], [], [], [], [], []
    for a4 in a_g:
        _, dil, length, _ = a4.shape
        rows = min(rows_per_step, length)
        nres = rows_per_step // rows
        steps_per_class = length // rows
        assert dil % nres == 0 and (dil // nres) * steps_per_class == n_steps

        def spec(cols, nres=nres, rows=rows, spc=steps_per_class):
            return pl.BlockSpec((None, nres, rows, cols), lambda b, s: (b, s // spc, s % spc, 0))

        in_specs.append(spec(QKV_COLS))
        o_specs.append(spec(DIL_WIDTH))
        l_specs.append(spec(LANES))
        o_shapes.append(jax.ShapeDtypeStruct((nb, dil, length, DIL_WIDTH), F32))
        l_shapes.append(jax.ShapeDtypeStruct((nb, dil, length, LANES), F32))
        scratch.append(pltpu.VMEM((nres, DIL_SPAN, KV_COLS), BF16))
        len_steps.append(steps_per_class)
    outs = pl.pallas_call(
        functools.partial(_dil_prompt_kernel, len_steps=tuple(len_steps)),
        grid=(nb, n_steps),
        in_specs=in_specs,
        out_specs=o_specs + l_specs,
        out_shape=o_shapes + l_shapes,
        scratch_shapes=scratch,
        compiler_params=_params("parallel", "arbitrary"),
        name="dil_prompt",
    )(*a_g)
    return outs[:N_GROUPS], outs[N_GROUPS:]


def _dil_sample_kernel(*refs, t_new):
    a_refs, cache_refs = refs[:N_GROUPS], refs[N_GROUPS:2 * N_GROUPS]
    o_refs, lse_refs = refs[2 * N_GROUPS:3 * N_GROUPS], refs[3 * N_GROUPS:]
    rpt = 2 * DIL_HEADS
    scale = DIL_HD ** -0.5
    neg_inf = -jnp.inf
    tq_n = lax.broadcasted_iota(jnp.int32, (t_new, t_new), 0)
    j_n = lax.broadcasted_iota(jnp.int32, (t_new, t_new), 1)

    def cache_head(g, c):
        ref = cache_refs[g]
        if len(ref.shape) == 2:
            return ref[pl.ds(c, ref.shape[0] // rpt, stride=rpt), :].astype(BF16)
        x = ref[:, pl.ds(c, t_new, stride=rpt), :]
        return x.reshape(ref.shape[0] * t_new, DIL_HD).astype(BF16)

    valid_c, valid_n = [], []
    for g, (_, dil) in enumerate(DIL_GROUPS):
        ref = cache_refs[g]
        n_keys = ref.shape[0] // rpt if len(ref.shape) == 2 else ref.shape[0] * t_new
        tq = lax.broadcasted_iota(jnp.int32, (t_new, n_keys), 0)
        key = lax.broadcasted_iota(jnp.int32, (t_new, n_keys), 1)
        if len(ref.shape) == 2:
            valid_c.append(jnp.logical_and(((key - tq) & (dil - 1)) == 0, key >= tq))
        else:
            valid_c.append((key & (t_new - 1)) == tq)
        valid_n.append(jnp.logical_and(((tq_n - j_n) & (dil - 1)) == 0, j_n <= tq_n))

    units = [(g, h) for g in range(N_GROUPS) for h in range(DIL_HEADS)]
    scores = []
    for g, h in units:
        q = a_refs[g][:, h * DIL_HD:(h + 1) * DIL_HD].astype(BF16)
        k_new = a_refs[g][:, DIL_WIDTH + h * DIL_HD:DIL_WIDTH + (h + 1) * DIL_HD].astype(BF16)
        s_c = jnp.where(valid_c[g], _dot(q, cache_head(g, h), _NT) * scale, neg_inf)
        s_n = jnp.where(valid_n[g], _dot(q, k_new, _NT) * scale, neg_inf)
        scores.append((s_c, s_n))
    maxes = [jnp.maximum(jnp.max(s_c, axis=-1, keepdims=True), jnp.max(s_n, axis=-1, keepdims=True))
             for s_c, s_n in scores]
    probs = [(jnp.exp(s_c - m), jnp.exp(s_n - m)) for (s_c, s_n), m in zip(scores, maxes)]
    sums = [jnp.sum(p_c, axis=-1, keepdims=True) + jnp.sum(p_n, axis=-1, keepdims=True) for p_c, p_n in probs]
    lses = {}
    for (g, h), (p_c, p_n), m, l in zip(units, probs, maxes, sums):
        v_new = a_refs[g][:, 2 * DIL_WIDTH + h * DIL_HD:2 * DIL_WIDTH + (h + 1) * DIL_HD].astype(BF16)
        acc = _dot(p_c.astype(BF16), cache_head(g, DIL_HEADS + h)) + _dot(p_n.astype(BF16), v_new)
        o_refs[g][:, h * DIL_HD:(h + 1) * DIL_HD] = acc * (1.0 / l)
        lses[g, h] = m + jnp.log(l)
    for g in range(N_GROUPS):
        lse_refs[g][...] = _pack_lse([lses[g, h] for h in range(DIL_HEADS)])


def _cache_rows(cache):
    nb, lb = cache.shape[:2]
    return cache.reshape(nb, lb * 2 * DIL_HEADS, DIL_HD)


def _dil_sample(a_g, caches, *, t_new):
    nb = caches[0].shape[0]
    rpt = 2 * DIL_HEADS
    assert t_new & (t_new - 1) == 0
    cache_args, cache_specs = [], []
    for cache, (_, dil) in zip(caches, DIL_GROUPS):
        lb = cache.shape[1]
        assert lb == dil * DIL_SPAN
        if dil > t_new:
            assert dil % t_new == 0
            cache_args.append(cache.reshape(nb, lb // dil, dil * rpt, DIL_HD))
            cache_specs.append(pl.BlockSpec((None, lb // dil, t_new * rpt, DIL_HD), lambda b: (b, 0, 0, 0)))
        else:
            cache_args.append(_cache_rows(cache))
            cache_specs.append(pl.BlockSpec((None, lb * rpt, DIL_HD), lambda b: (b, 0, 0)))
    outs = pl.pallas_call(
        functools.partial(_dil_sample_kernel, t_new=t_new),
        grid=(nb,),
        in_specs=[pl.BlockSpec((t_new, QKV_COLS), lambda b: (b, 0))] * N_GROUPS + cache_specs,
        out_specs=[pl.BlockSpec((t_new, DIL_WIDTH), lambda b: (b, 0))] * N_GROUPS
        + [pl.BlockSpec((t_new, LANES), lambda b: (b, 0))] * N_GROUPS,
        out_shape=[jax.ShapeDtypeStruct((nb * t_new, DIL_WIDTH), F32)] * N_GROUPS
        + [jax.ShapeDtypeStruct((nb * t_new, LANES), F32)] * N_GROUPS,
        compiler_params=_params("parallel"),
        name="dil_sample",
    )(*a_g, *cache_args)
    return outs[:N_GROUPS], outs[N_GROUPS:]


def _token_order(ref, scratch_refs):
    if len(ref.shape) == 2:
        return lambda rows: ref[rows, :]
    dil, rows_per_class, _ = ref.shape
    if dil == 1:
        return lambda rows: ref[0, rows, :]
    scratch = scratch_refs.pop()
    n_slabs = scratch.shape[0]
    for res in range(dil):
        for s in range(n_slabs):
            scratch[s, pl.ds(res, rows_per_class, stride=dil), :] = ref[res, :, s * LANES:(s + 1) * LANES]
    if n_slabs == 1:
        return lambda rows: scratch[0, rows, :]
    return lambda rows: jnp.concatenate([scratch[s, rows, :] for s in range(n_slabs)], axis=1)


def _final_kernel(ya_ref, o0_ref, o1_ref, o2_ref, l0_ref, l1_ref, l2_ref, gates_ref, x_ref, pe_ref,
                  wa_ref, wb_ref, wo_ref, wpp_ref, wpg_ref, gpost_ref, y_ref, *scratch_refs):
    o_scratch = [s for s in scratch_refs if s.shape[0] == DIL_WIDTH // LANES]
    l_scratch = [s for s in scratch_refs if s.shape[0] == 1]
    every_row = slice(None)
    lse = [_token_order(r, l_scratch)(every_row) for r in (l0_ref, l1_ref, l2_ref)]
    m = jnp.maximum(jnp.maximum(lse[0], lse[1]), lse[2])
    e = [jnp.exp(x - m) for x in lse]
    inv = 1.0 / (e[0] + e[1] + e[2])
    o_groups = [_token_order(r, o_scratch)(every_row) for r in (o0_ref, o1_ref, o2_ref)]
    heads = []
    for h in range(DIL_HEADS):
        hs = slice(h * DIL_HD, (h + 1) * DIL_HD)
        lane = h * LSE_LANES_PER_HEAD
        acc = None
        for eg, o_g in zip(e, o_groups):
            term = (eg * inv)[:, lane:lane + 1] * o_g[:, hs]
            acc = term if acc is None else acc + term
        heads.append(acc)
    o_b = jnp.concatenate(heads, axis=1)
    d = x_ref.shape[1]
    dg = gates_ref[:, 0:DIL_WIDTH].astype(F32)
    y_a = _dot(ya_ref[...].astype(BF16), wa_ref[...])
    y_b = _dot((o_b * (dg * _sigmoid(dg))).astype(BF16), wb_ref[...])
    ga = gates_ref[:, DIL_WIDTH:DIL_WIDTH + d].astype(F32)
    gb = gates_ref[:, DIL_WIDTH + d:DIL_WIDTH + 2 * d].astype(F32)
    merged = _sigmoid(ga) * y_a + _sigmoid(gb) * y_b
    t = _dot(merged.astype(BF16), wo_ref[...])
    ple = _dot(pe_ref[...].astype(BF16), wpp_ref[...])
    h = x_ref[...] + _rms(t, gpost_ref[...])
    gate = _dot(h.astype(BF16), wpg_ref[...])
    y_ref[...] = h + ple * _sigmoid(gate)


def _final(ya, os_, lses, gates, x2d, pe2d, wa, wb, wo, wpp, wpg, g_post, *, tm):
    n, d = x2d.shape

    def rows(cols):
        return pl.BlockSpec((tm, cols), lambda i: (i, 0))

    scratch = []

    def group_spec(arr):
        if arr.ndim == 2:
            return rows(arr.shape[1])
        _, dil, length, cols = arr.shape
        tiles_per_seq = dil * length // tm
        if dil > 1:
            scratch.append(pltpu.VMEM((cols // LANES, tm, LANES), F32))
        return pl.BlockSpec((None, dil, tm // dil, cols), lambda i: (i // tiles_per_seq, 0, i % tiles_per_seq, 0))

    return pl.pallas_call(
        _final_kernel,
        grid=(n // tm,),
        in_specs=[rows(GLA_VAL)] + [group_spec(a) for a in os_] + [group_spec(a) for a in lses]
        + [rows(gates.shape[1]), rows(d), rows(pe2d.shape[1])]
        + [_const_spec(w.shape) for w in (wa, wb, wo, wpp, wpg, g_post)],
        out_specs=rows(d),
        out_shape=jax.ShapeDtypeStruct((n, d), F32),
        scratch_shapes=scratch,
        compiler_params=_params("parallel"),
        name="final",
    )(ya, *os_, *lses, gates, x2d, pe2d, wa, wb, wo, wpp, wpg, g_post)


def _prep_weights(g_pre, g_post, w_in, w_gla_lr, b_gla_lr, g_gla_norm, w_gla_branch, w_dil_branch, w_out,
                  w_ple_proj, w_ple_gate):
    d = w_in.shape[0]
    c_lr = GLA_COLS
    c_q = c_lr + GLA_LOWRANK
    c_k = c_q + N_GROUPS * DIL_WIDTH
    c_v = c_k + N_GROUPS * DIL_WIDTH
    c_g = c_v + N_GROUPS * DIL_WIDTH
    w_t = w_in.T
    w_a = w_t[:c_lr].astype(BF16)
    w_lr = jnp.pad(w_t[c_lr:c_q].astype(BF16), ((0, LANES - GLA_LOWRANK), (0, 0)))
    w_lr2 = jnp.pad(w_gla_lr.astype(BF16), ((0, LANES - GLA_LOWRANK), (0, 0)))
    parts = []
    for g in range(N_GROUPS):
        for c in (c_q, c_k, c_v):
            parts.append(w_t[c + g * DIL_WIDTH:c + (g + 1) * DIL_WIDTH])
    parts.append(w_t[c_g:])
    w_r = jnp.concatenate(parts, axis=0).astype(BF16)
    return dict(
        g_pre=g_pre.reshape(1, d), g_post=g_post.reshape(1, d), w_a=w_a, w_lr=w_lr, w_lr2=w_lr2, w_r=w_r,
        b_lr=b_gla_lr.reshape(1, GLA_KEY), gn=g_gla_norm.reshape(1, GLA_VAL),
        wa=w_gla_branch.astype(BF16), wb=w_dil_branch.astype(BF16), wo=w_out.astype(BF16),
        wpp=w_ple_proj.astype(BF16), wpg=w_ple_gate.astype(BF16),
    )


PROMPT_TILE = 512


def _prompt_layer(x, pe, s0, w, sample_caches, sample_new_rows):
    nb, seq, d = x.shape
    n = nb * seq
    tm = PROMPT_TILE
    assert seq % tm == 0 and seq % DIL_GROUPS[-1][0] == 0
    x2d = x.reshape(n, d)
    rest = _proj_rest(x2d, w["g_pre"], w["w_r"], tm=tm, seq=seq, prompt=True)
    a_g, gates = rest[:N_GROUPS], rest[N_GROUPS]
    ns = sample_caches[0].shape[0]
    ya, s_new, next_caches = _proj_gla_fused(
        x2d, w["g_pre"], w["w_a"], w["w_lr"], w["w_lr2"], w["b_lr"], s0, w["gn"], tm=tm, seq=seq, chunk=GLA_CHUNK,
        shift_src=tuple(_cache_rows(c) for c in sample_caches),
        shift_tail=tuple(r.reshape(ns, -1, DIL_HD) for r in sample_new_rows))
    new_bufs = [kv.reshape(nb, -1, 2, DIL_HEADS, DIL_HD) for kv in rest[N_GROUPS + 1:]]
    os_, lses = _dil_prompt(a_g)
    y = _final(ya, os_, lses, gates, x2d, pe.reshape(n, -1),
               w["wa"], w["wb"], w["wo"], w["wpp"], w["wpg"], w["g_post"], tm=tm)
    next_caches = [c.reshape(old.shape) for c, old in zip(next_caches, sample_caches)]
    return y.reshape(nb, seq, d), s_new, new_bufs, next_caches


def _sample_proj(x, w):
    nb, seq, d = x.shape
    return _proj_rest(x.reshape(nb * seq, d), w["g_pre"], w["w_r"], tm=nb * seq, seq=seq, prompt=False)


def _sample_layer(x, pe, s0, caches, w, rest):
    nb, seq, d = x.shape
    n = nb * seq
    x2d = x.reshape(n, d)
    a_g, gates = rest[:N_GROUPS], rest[N_GROUPS]
    chunk = min(GLA_CHUNK, seq)
    assert seq % chunk == 0
    nseq = 8 if nb % 8 == 0 else 1
    qkvg, la = _proj_gla(x2d, w["g_pre"], w["w_a"], w["w_lr"], w["w_lr2"], w["b_lr"], tm=n, out_dtype=F32)
    ya, s_new = _gla(qkvg, la, s0, w["gn"], nb=nb, seq=seq, chunk=chunk, rows_per_step=seq, nseq=nseq,
                     out_dtype=F32, mm_dtype=BF16, precision=None)
    os_, lses = _dil_sample(a_g, caches, t_new=seq)
    y = _final(ya, os_, lses, gates, x2d, pe.reshape(n, -1),
               w["wa"], w["wb"], w["wo"], w["wpp"], w["wpg"], w["g_post"], tm=n)
    return y.reshape(nb, seq, d), s_new


def kernel(x_prompt, x_sample, state_gla, cache_kv_w128, cache_kv_w512, cache_kv_w2048, p_prompt, p_sample, g_pre, g_post, w_in, w_gla_lr, b_gla_lr, g_gla_norm, w_gla_branch, w_dil_branch, w_out, w_ple_proj, w_ple_gate):
    depth = w_in.shape[0]
    hp, hs = x_prompt, x_sample
    gla_p, gla_s = [], []
    kvp = [[] for _ in DIL_GROUPS]
    kvs = [[] for _ in DIL_GROUPS]
    for i in range(depth):
        w = _prep_weights(g_pre[i], g_post[i], w_in[i], w_gla_lr[i], b_gla_lr[i], g_gla_norm[i], w_gla_branch[i],
                          w_dil_branch[i], w_out[i], w_ple_proj[i], w_ple_gate[i])
        s0 = jnp.zeros((x_prompt.shape[0], GLA_HEADS, GLA_DK, GLA_DV), F32)
        caches = (cache_kv_w128[i], cache_kv_w512[i], cache_kv_w2048[i])
        rest_s = _sample_proj(hs, w)
        hp, sp_new, bp_new, bs_new = _prompt_layer(hp, p_prompt[i], s0, w, caches, rest_s[N_GROUPS + 1:])
        hs, ss_new = _sample_layer(hs, p_sample[i], state_gla[i], caches, w, rest_s)
        gla_p.append(sp_new)
        gla_s.append(ss_new)
        for g in range(N_GROUPS):
            kvp[g].append(bp_new[g])
            kvs[g].append(bs_new[g])
    return (hp, hs, jnp.stack(gla_p), jnp.stack(gla_s),
            jnp.stack(kvp[0]), jnp.stack(kvp[1]), jnp.stack(kvp[2]),
            jnp.stack(kvs[0]), jnp.stack(kvs[1]), jnp.stack(kvs[2]))
```

```python
import functools

import jax
import jax.numpy as jnp
from jax import lax
from jax.experimental import pallas as pl
from jax.experimental.pallas import tpu as pltpu

F32 = jnp.float32
BF16 = jnp.bfloat16

NORM_EPS = 1e-6
GLA_HEADS = 4
GLA_DK = 128
GLA_DV = 256
GLA_KEY = GLA_HEADS * GLA_DK
GLA_VAL = GLA_HEADS * GLA_DV
GLA_LOWRANK = 16
GLA_TAU = 16.0
GLA_CHUNK = 64
DIL_GROUPS = ((128, 1), (512, 4), (2048, 16))
DIL_HEADS = 4
DIL_HD = 128
DIL_WIDTH = DIL_HEADS * DIL_HD
DIL_SPAN = 128
N_GROUPS = len(DIL_GROUPS)
GLA_COLS = 2 * GLA_KEY + 2 * GLA_VAL
QKV_COLS = 3 * DIL_WIDTH
KV_COLS = 2 * DIL_WIDTH

LANES = 128
LSE_LANES_PER_HEAD = LANES // DIL_HEADS
_LOG2_LSE_LANES = LSE_LANES_PER_HEAD.bit_length() - 1
assert 1 << _LOG2_LSE_LANES == LSE_LANES_PER_HEAD
VMEM_LIMIT_BYTES = 56 * 1024 * 1024
_CHEAP_ROW_STRIDE = 4

_NT = (((1,), (1,)), ((), ()))
_TN = (((0,), (0,)), ((), ()))


def _dot(a, b, dims=None, precision=None):
    if dims is None:
        return jnp.dot(a, b, preferred_element_type=F32, precision=precision)
    return lax.dot_general(a, b, dims, preferred_element_type=F32, precision=precision)


def _rms(xf, g):
    return xf * lax.rsqrt(jnp.mean(xf * xf, axis=-1, keepdims=True) + NORM_EPS) * g


def _sigmoid(x):
    return 1.0 / (1.0 + jnp.exp(-x))


def _const_spec(shape):
    nd = len(shape)
    return pl.BlockSpec(shape, lambda *_: (0,) * nd, pipeline_mode=pl.Buffered(1))


def _params(*sem):
    return pltpu.CompilerParams(dimension_semantics=sem, vmem_limit_bytes=VMEM_LIMIT_BYTES)


def _proj_gla_kernel(x_ref, g_ref, w_ref, wlr_ref, wlr2_ref, blr_ref, qkvg_ref, la_ref, *, col_chunk):
    xn = _rms(x_ref[...], g_ref[...]).astype(BF16)
    for c0 in range(0, GLA_COLS, col_chunk):
        qkvg_ref[:, c0:c0 + col_chunk] = _dot(xn, w_ref[c0:c0 + col_chunk, :], _NT).astype(qkvg_ref.dtype)
    glr = _dot(xn, wlr_ref[...], _NT)
    z = _dot(glr.astype(BF16), wlr2_ref[...]) + blr_ref[...]
    log_sig = jnp.minimum(z, 0.0) - jnp.log1p(jnp.exp(-jnp.abs(z)))
    la_ref[...] = log_sig * (1.0 / GLA_TAU)


def _proj_gla(x2d, g_pre, w_a, w_lr, w_lr2, b_lr, *, tm, out_dtype):
    n, d = x2d.shape
    return pl.pallas_call(
        functools.partial(_proj_gla_kernel, col_chunk=512),
        grid=(n // tm,),
        in_specs=[
            pl.BlockSpec((tm, d), lambda i: (i, 0)),
            _const_spec(g_pre.shape),
            _const_spec(w_a.shape),
            _const_spec(w_lr.shape),
            _const_spec(w_lr2.shape),
            _const_spec(b_lr.shape),
        ],
        out_specs=[
            pl.BlockSpec((tm, GLA_COLS), lambda i: (i, 0)),
            pl.BlockSpec((tm, GLA_KEY), lambda i: (i, 0)),
        ],
        out_shape=[
            jax.ShapeDtypeStruct((n, GLA_COLS), out_dtype),
            jax.ShapeDtypeStruct((n, GLA_KEY), F32),
        ],
        compiler_params=_params("parallel"),
        name="proj_gla",
    )(x2d, g_pre, w_a, w_lr, w_lr2, b_lr)


def _proj_rest_kernel(x_ref, g_ref, w_ref, *refs, tm, tiles_per_seq, keeps, prompt):
    a_refs = refs[:N_GROUPS]
    gates_ref = refs[N_GROUPS]
    kv_refs = refs[N_GROUPS + 1:2 * N_GROUPS + 1]
    stage_refs = refs[2 * N_GROUPS + 1:]

    def body(cache_groups):
        xn = _rms(x_ref[...], g_ref[...]).astype(BF16)
        n_staged = 0
        for g in range(N_GROUPS):
            base = g * QKV_COLS
            dil = DIL_GROUPS[g][1]
            for part in range(3):
                c0 = part * DIL_WIDTH
                r = _dot(xn, w_ref[base + c0:base + c0 + DIL_WIDTH, :], _NT)
                if not prompt:
                    a_refs[g][:, c0:c0 + DIL_WIDTH] = r
                elif dil == 1:
                    a_refs[g][0, :, c0:c0 + DIL_WIDTH] = r.astype(BF16)
                else:
                    stage = stage_refs[n_staged]
                    n_staged += 1
                    for s in range(DIL_WIDTH // LANES):
                        stage[s] = r[:, s * LANES:(s + 1) * LANES]
                    step = dil
                    if dil > _CHEAP_ROW_STRIDE:
                        step = dil // _CHEAP_ROW_STRIDE
                        assert step <= _CHEAP_ROW_STRIDE
                        stage2 = stage_refs[n_staged]
                        n_staged += 1
                        quarter = tm // _CHEAP_ROW_STRIDE
                        for c in range(_CHEAP_ROW_STRIDE):
                            for s in range(DIL_WIDTH // LANES):
                                stage2[s, c * quarter:(c + 1) * quarter, :] = (
                                    stage[s, pl.ds(c, quarter, stride=_CHEAP_ROW_STRIDE), :])
                        stage = stage2
                    for res in range(dil):
                        start = res if step == dil else (res % _CHEAP_ROW_STRIDE) * quarter + res // _CHEAP_ROW_STRIDE
                        for s in range(DIL_WIDTH // LANES):
                            a_refs[g][res, :, c0 + s * LANES:c0 + (s + 1) * LANES] = (
                                stage[s, pl.ds(start, tm // dil, stride=step), :].astype(BF16))
                if part > 0 and g in cache_groups:
                    rows = min(keeps[g], tm)
                    for h in range(DIL_HEADS):
                        kv_refs[g][pl.ds((part - 1) * DIL_HEADS + h, rows, stride=2 * DIL_HEADS), :] = (
                            r[tm - rows:, h * DIL_HD:(h + 1) * DIL_HD])
        gbase = N_GROUPS * QKV_COLS
        gcols = gates_ref.shape[1]
        for c0 in range(0, gcols, DIL_WIDTH):
            gates_ref[:, c0:c0 + DIL_WIDTH] = _dot(xn, w_ref[gbase + c0:gbase + c0 + DIL_WIDTH, :], _NT).astype(gates_ref.dtype)

    if not prompt:
        body(tuple(range(N_GROUPS)))
        return
    t = pl.program_id(0) % tiles_per_seq
    first_kept = [tiles_per_seq - max(1, keep // tm) for keep in keeps]
    edges = sorted(set(first_kept) | {0}) + [tiles_per_seq]
    for lo, hi in zip(edges[:-1], edges[1:]):
        groups = tuple(g for g in range(N_GROUPS) if first_kept[g] <= lo)
        pl.when(jnp.logical_and(t >= lo, t < hi))(lambda groups=groups: body(groups))


def _proj_rest(x2d, g_pre, w_r, *, tm, seq, prompt):
    n, d = x2d.shape
    nb = n // seq
    tiles_per_seq = seq // tm if prompt else None
    gate_cols = w_r.shape[0] - N_GROUPS * QKV_COLS
    keeps = tuple(min(w, seq) for (w, _) in DIL_GROUPS) if prompt else (tm,) * N_GROUPS
    rows_per_token = 2 * DIL_HEADS
    out_specs, out_shape, scratch = [], [], []
    if prompt:
        for _, dil in DIL_GROUPS:
            out_specs.append(pl.BlockSpec((None, dil, tm // dil, QKV_COLS),
                                          lambda i: (i // tiles_per_seq, 0, i % tiles_per_seq, 0)))
            out_shape.append(jax.ShapeDtypeStruct((nb, dil, seq // dil, QKV_COLS), BF16))
    else:
        out_specs += [pl.BlockSpec((tm, QKV_COLS), lambda i: (i, 0)) for _ in range(N_GROUPS)]
        out_shape += [jax.ShapeDtypeStruct((n, QKV_COLS), F32) for _ in range(N_GROUPS)]
    out_specs.append(pl.BlockSpec((tm, gate_cols), lambda i: (i, 0)))
    out_shape.append(jax.ShapeDtypeStruct((n, gate_cols), BF16 if prompt else F32))
    if not prompt:
        for _ in range(N_GROUPS):
            out_specs.append(pl.BlockSpec((tm * rows_per_token, DIL_HD), lambda i: (i, 0)))
            out_shape.append(jax.ShapeDtypeStruct((n * rows_per_token, DIL_HD), F32))
    else:
        for keep in keeps:
            if keep >= tm:
                assert keep % tm == 0
                kt = keep // tm

                def idx(i, kt=kt):
                    b, t = i // tiles_per_seq, i % tiles_per_seq
                    return (b * kt + jnp.maximum(t - (tiles_per_seq - kt), 0), 0)

                out_specs.append(pl.BlockSpec((tm * rows_per_token, DIL_HD), idx))
            else:
                out_specs.append(pl.BlockSpec((keep * rows_per_token, DIL_HD), lambda i: (i // tiles_per_seq, 0)))
            out_shape.append(jax.ShapeDtypeStruct((nb * keep * rows_per_token, DIL_HD), F32))
        n_stage = 3 * sum((1 if dil <= _CHEAP_ROW_STRIDE else 2) for _, dil in DIL_GROUPS if dil > 1)
        scratch = [pltpu.VMEM((DIL_WIDTH // LANES, tm, LANES), F32)] * n_stage
    return pl.pallas_call(
        functools.partial(_proj_rest_kernel, tm=tm, tiles_per_seq=tiles_per_seq, keeps=keeps, prompt=prompt),
        grid=(n // tm,),
        in_specs=[
            pl.BlockSpec((tm, d), lambda i: (i, 0)),
            _const_spec(g_pre.shape),
            _const_spec(w_r.shape),
        ],
        out_specs=out_specs,
        out_shape=out_shape,
        scratch_shapes=scratch,
        compiler_params=_params("arbitrary"),
        name="proj_rest",
    )(x2d, g_pre, w_r)


def _gla_kernel(qkvg_ref, la_ref, s0_ref, gn_ref, ya_ref, sfin_ref, st_ref, *, chunk, nchunk, nseq, mm_dtype, precision):
    j = pl.program_id(1)

    @pl.when(j == 0)
    def _():
        for s in range(nseq):
            for h in range(GLA_HEADS):
                st_ref[s * GLA_HEADS + h] = s0_ref[s, h].T

    _gla_compute(lambda r, c: qkvg_ref[r, c], lambda r: la_ref[r, :], st_ref, gn_ref, ya_ref,
                 chunk=chunk, nchunk=nchunk, nseq=nseq, mm_dtype=mm_dtype, precision=precision)

    @pl.when(j == pl.num_programs(1) - 1)
    def _():
        for s in range(nseq):
            for h in range(GLA_HEADS):
                sfin_ref[s, h] = st_ref[s * GLA_HEADS + h].T


def _gla_compute(*args, **kwargs):
    for _ in _gla_phases(*args, **kwargs):
        pass


def _interleave(*generators, phases_per_round=None):
    live = {i: gen for i, gen in enumerate(generators)}
    counts = phases_per_round or (1,) * len(generators)
    while live:
        for i in list(live):
            for _ in range(counts[i]):
                if next(live[i], StopIteration) is StopIteration:
                    del live[i]
                    break


def _gla_phases(load_qkvg, load_la, st_ref, gn_ref, ya_ref, *, chunk, nchunk, nseq, mm_dtype, precision):
    row = lax.broadcasted_iota(jnp.int32, (chunk, chunk), 0)
    col = lax.broadcasted_iota(jnp.int32, (chunk, chunk), 1)
    causal = row >= col
    tril = causal.astype(F32)
    qscale = GLA_DK ** -0.5

    segs = [(s, c) for s in range(nseq) for c in range(nchunk)]
    units = [(s, c, h) for s, c in segs for h in range(GLA_HEADS)]

    def rows_of(s, c):
        r0 = (s * nchunk + c) * chunk
        return slice(r0, r0 + chunk)

    def key_cols(h):
        return slice(h * GLA_DK, (h + 1) * GLA_DK)

    def v_of(s, c, h):
        return load_qkvg(rows_of(s, c), slice(2 * GLA_KEY + h * GLA_DV, 2 * GLA_KEY + (h + 1) * GLA_DV)).astype(mm_dtype)

    def cumsum(a):
        if mm_dtype != BF16:
            return _dot(tril, a, precision=lax.Precision.HIGHEST)
        hi = a.astype(BF16)
        rest = a - hi.astype(F32)
        mid = rest.astype(BF16)
        lo = (rest - mid.astype(F32)).astype(BF16)
        parts = _dot(tril.astype(BF16), jnp.concatenate([hi, mid, lo], axis=1))
        return parts[:, :GLA_KEY] + parts[:, GLA_KEY:2 * GLA_KEY] + parts[:, 2 * GLA_KEY:]

    cum = {sc: cumsum(load_la(rows_of(*sc))) for sc in segs}
    b_end = {sc: cum[sc][chunk - 1:chunk, :] for sc in segs}
    e_b = {sc: jnp.exp(cum[sc]) for sc in segs}
    e_nb = {sc: jnp.exp(-cum[sc]) for sc in segs}
    e_rest = {sc: jnp.exp(b_end[sc] - cum[sc]) for sc in segs}
    e_end = {sc: jnp.exp(b_end[sc]) for sc in segs}
    yield
    q_in, k_dec, k_end = {}, {}, {}
    for s, c, h in units:
        q = load_qkvg(rows_of(s, c), key_cols(h)).astype(F32) * qscale
        k = load_qkvg(rows_of(s, c), slice(GLA_KEY + h * GLA_DK, GLA_KEY + (h + 1) * GLA_DK)).astype(F32)
        q_in[s, c, h] = (q * e_b[s, c][:, key_cols(h)]).astype(mm_dtype)
        k_dec[s, c, h] = (k * e_nb[s, c][:, key_cols(h)]).astype(mm_dtype)
        k_end[s, c, h] = (k * e_rest[s, c][:, key_cols(h)]).astype(mm_dtype)
    yield
    att = {u: jnp.where(causal, _dot(q_in[u], k_dec[u], _NT, precision), 0.0).astype(mm_dtype) for u in units}
    yield
    o_intra = {u: _dot(att[u], v_of(*u), None, precision) for u in units}
    yield
    d_state = {u: _dot(v_of(*u), k_end[u], _TN, precision) for u in units}
    yield
    o_inter = {}
    for s in range(nseq):
        for h in range(GLA_HEADS):
            st = st_ref[s * GLA_HEADS + h]
            for c in range(nchunk):
                o_inter[s, c, h] = _dot(q_in[s, c, h], st.astype(mm_dtype), _NT, precision)
                st = st * e_end[s, c][:, key_cols(h)] + d_state[s, c, h]
            st_ref[s * GLA_HEADS + h] = st
    yield
    for i, (s, c, h) in enumerate(units):
        vs = slice(h * GLA_DV, (h + 1) * GLA_DV)
        gg = load_qkvg(rows_of(s, c), slice(2 * GLA_KEY + GLA_VAL + h * GLA_DV,
                                            2 * GLA_KEY + GLA_VAL + (h + 1) * GLA_DV)).astype(F32)
        on = _rms(o_intra[s, c, h] + o_inter[s, c, h], gn_ref[:, vs])
        ya_ref[rows_of(s, c), vs] = (on * (gg * _sigmoid(gg))).astype(ya_ref.dtype)
        if i % 8 == 7:
            yield


def _proj_gla_fused_kernel(x_ref, g_ref, w_ref, wlr_ref, wlr2_ref, blr_ref, s0_ref, gn_ref, *refs,
                           steps, chunk, col_chunk, n_shift, pieces, n_chunks):
    old_refs = refs[:n_shift]
    next_refs = refs[n_shift:2 * n_shift]
    tail_refs = refs[2 * n_shift:3 * n_shift]
    ya_ref, sfin_ref = refs[3 * n_shift:3 * n_shift + 2]
    new_refs = refs[3 * n_shift + 2:4 * n_shift + 2]
    qkvg_s0, qkvg_s1, la_s0, la_s1, st_ref = refs[4 * n_shift + 2:]
    j = pl.program_id(0)
    prev = j - 1

    def shift_pieces(rows_per_phase=1024):
        last_piece = jnp.minimum(j, n_chunks - 1) % pieces == pieces - 1
        for old, nxt, tail, new in zip(old_refs, next_refs, tail_refs, new_refs):
            p_rows, t_rows = old.shape[0], tail.shape[0]
            for r0 in range(0, p_rows - t_rows, rows_per_phase):
                r1 = min(r0 + rows_per_phase, p_rows - t_rows)
                new[r0:r1, :] = old[t_rows + r0:t_rows + r1, :]
                yield
            new[p_rows - t_rows:p_rows, :] = jnp.where(last_piece, tail[...], nxt[...])
            yield

    @pl.when(j == 0)
    def _():
        qkvg_s1[...] = jnp.zeros_like(qkvg_s1)
        la_s1[...] = jnp.zeros_like(la_s1)

    @pl.when(jnp.logical_or(j == 0, prev % steps == 0))
    def _():
        for h in range(GLA_HEADS):
            st_ref[h] = s0_ref[0, h].T

    def project(qkvg_w, la_w):
        xn = _rms(x_ref[...], g_ref[...]).astype(BF16)
        yield
        for c0 in range(0, GLA_COLS, col_chunk):
            qkvg_w[:, c0:c0 + col_chunk] = _dot(xn, w_ref[c0:c0 + col_chunk, :], _NT).astype(qkvg_w.dtype)
            yield
        glr = _dot(xn, wlr_ref[...], _NT)
        z = _dot(glr.astype(BF16), wlr2_ref[...]) + blr_ref[...]
        log_sig = jnp.minimum(z, 0.0) - jnp.log1p(jnp.exp(-jnp.abs(z)))
        la_w[...] = log_sig * (1.0 / GLA_TAU)

    def step(qkvg_w, la_w, qkvg_r, la_r):
        _interleave(
            project(qkvg_w, la_w),
            _gla_phases(lambda r, c: qkvg_r[r, c], lambda r: la_r[r, :], st_ref, gn_ref, ya_ref,
                        chunk=chunk, nchunk=x_ref.shape[0] // chunk, nseq=1, mm_dtype=BF16, precision=None),
            shift_pieces(),
            phases_per_round=(1, 2, 2))

    @pl.when(j % 2 == 0)
    def _():
        step(qkvg_s0, la_s0, qkvg_s1, la_s1)

    @pl.when(j % 2 == 1)
    def _():
        step(qkvg_s1, la_s1, qkvg_s0, la_s0)

    @pl.when(jnp.logical_and(j >= 1, prev % steps == steps - 1))
    def _():
        for h in range(GLA_HEADS):
            sfin_ref[0, h] = st_ref[h].T


def _proj_gla_fused(x2d, g_pre, w_a, w_lr, w_lr2, b_lr, s0, gn, *, tm, seq, chunk, shift_src=(), shift_tail=()):
    n, d = x2d.shape
    nb = n // seq
    steps = seq // tm
    n_tiles = n // tm
    n_shift = len(shift_src)
    pieces = n_chunks = 1
    copy_in, copy_out = [], []
    if n_shift:
        entries = shift_src[0].shape[0]
        pieces = max(1, n_tiles // entries)
        n_chunks = entries * pieces
        assert n_chunks <= n_tiles + 1

        def piece_of(j):
            c = jnp.minimum(j, n_chunks - 1)
            return c // pieces, c % pieces

        next_specs, tail_specs = [], []
        for src, tail in zip(shift_src, shift_tail):
            t_rows = tail.shape[1]
            p_rows = src.shape[1] // pieces
            assert src.shape[0] == tail.shape[0] == entries and src.shape[2] == tail.shape[2] == LANES
            assert src.shape[1] % pieces == 0 and p_rows % t_rows == 0 and t_rows % 8 == 0
            piece_spec = pl.BlockSpec((None, p_rows, LANES), lambda j: (*piece_of(j), 0))
            last_t_block = src.shape[1] // t_rows - 1

            def next_idx(j, per_piece=p_rows // t_rows, last=last_t_block):
                entry, piece = piece_of(j)
                return (entry, jnp.minimum((piece + 1) * per_piece, last), 0)

            copy_in.append(piece_spec)
            next_specs.append(pl.BlockSpec((None, t_rows, LANES), next_idx))
            tail_specs.append(pl.BlockSpec((None, t_rows, LANES), lambda j: (piece_of(j)[0], 0, 0)))
            copy_out.append(piece_spec)
        copy_in += next_specs + tail_specs

    def prev_tile(j):
        return jnp.maximum(j - 1, 0)

    outs = pl.pallas_call(
        functools.partial(_proj_gla_fused_kernel, steps=steps, chunk=chunk, col_chunk=512, n_shift=n_shift,
                          pieces=pieces, n_chunks=n_chunks),
        grid=(n_tiles + 1,),
        in_specs=[
            pl.BlockSpec((tm, d), lambda j: (jnp.minimum(j, n_tiles - 1), 0)),
            _const_spec(g_pre.shape),
            _const_spec(w_a.shape),
            _const_spec(w_lr.shape),
            _const_spec(w_lr2.shape),
            _const_spec(b_lr.shape),
            pl.BlockSpec((1, GLA_HEADS, GLA_DK, GLA_DV), lambda j: (prev_tile(j) // steps, 0, 0, 0)),
            _const_spec(gn.shape),
        ] + copy_in,
        out_specs=[
            pl.BlockSpec((tm, GLA_VAL), lambda j: (prev_tile(j), 0)),
            pl.BlockSpec((1, GLA_HEADS, GLA_DK, GLA_DV), lambda j: (prev_tile(j) // steps, 0, 0, 0)),
        ] + copy_out,
        out_shape=[
            jax.ShapeDtypeStruct((n, GLA_VAL), BF16),
            jax.ShapeDtypeStruct((nb, GLA_HEADS, GLA_DK, GLA_DV), F32),
        ] + [jax.ShapeDtypeStruct(src.shape, src.dtype) for src in shift_src],
        scratch_shapes=[
            pltpu.VMEM((tm, GLA_COLS), BF16),
            pltpu.VMEM((tm, GLA_COLS), BF16),
            pltpu.VMEM((tm, GLA_KEY), F32),
            pltpu.VMEM((tm, GLA_KEY), F32),
            pltpu.VMEM((GLA_HEADS, GLA_DV, GLA_DK), F32),
        ],
        compiler_params=_params("arbitrary"),
        name="proj_gla_fused",
    )(x2d, g_pre, w_a, w_lr, w_lr2, b_lr, s0, gn, *shift_src, *shift_src, *shift_tail)
    return outs[0], outs[1], list(outs[2:])


def _gla(qkvg, la, s0, gn, *, nb, seq, chunk, rows_per_step, nseq, out_dtype, mm_dtype, precision):
    steps = seq // rows_per_step
    assert nb % nseq == 0 and (nseq == 1 or steps == 1)
    rows = nseq * rows_per_step
    return pl.pallas_call(
        functools.partial(_gla_kernel, chunk=chunk, nchunk=rows_per_step // chunk, nseq=nseq, mm_dtype=mm_dtype,
                          precision=precision),
        grid=(nb // nseq, steps),
        in_specs=[
            pl.BlockSpec((rows, GLA_COLS), lambda b, j: (b * steps + j, 0)),
            pl.BlockSpec((rows, GLA_KEY), lambda b, j: (b * steps + j, 0)),
            pl.BlockSpec((nseq, GLA_HEADS, GLA_DK, GLA_DV), lambda b, j: (b, 0, 0, 0)),
            pl.BlockSpec((1, GLA_VAL), lambda b, j: (0, 0)),
        ],
        out_specs=[
            pl.BlockSpec((rows, GLA_VAL), lambda b, j: (b * steps + j, 0)),
            pl.BlockSpec((nseq, GLA_HEADS, GLA_DK, GLA_DV), lambda b, j: (b, 0, 0, 0)),
        ],
        out_shape=[
            jax.ShapeDtypeStruct((nb * seq, GLA_VAL), out_dtype),
            jax.ShapeDtypeStruct((nb, GLA_HEADS, GLA_DK, GLA_DV), F32),
        ],
        scratch_shapes=[pltpu.VMEM((nseq * GLA_HEADS, GLA_DV, GLA_DK), F32)],
        compiler_params=_params("parallel", "arbitrary"),
        name="gla",
    )(qkvg, la, s0, gn)


def _pack_lse(lses):
    rows = lses[0].shape[0]
    lane_head = lax.broadcasted_iota(jnp.int32, (rows, LANES), 1) >> _LOG2_LSE_LANES
    packed = jnp.broadcast_to(lses[0], (rows, LANES))
    for h in range(1, DIL_HEADS):
        packed = jnp.where(lane_head == h, lses[h], packed)
    return packed


def _dil_prompt_kernel(*refs, len_steps):
    n = N_GROUPS
    a_refs, o_refs, lse_refs, kvp_refs = refs[:n], refs[n:2 * n], refs[2 * n:3 * n], refs[3 * n:]
    step = pl.program_id(1)
    i_idx = lax.broadcasted_iota(jnp.int32, (DIL_SPAN, 2 * DIL_SPAN), 0)
    j_idx = lax.broadcasted_iota(jnp.int32, (DIL_SPAN, 2 * DIL_SPAN), 1)
    diff = j_idx - i_idx
    band = jnp.logical_and(diff >= 0, diff <= DIL_SPAN)
    band_first = []
    for g in range(n):
        first = step % len_steps[g] == 0

        @pl.when(first)
        def _(g=g):
            kvp_refs[g][...] = jnp.zeros_like(kvp_refs[g])

        band_first.append(jnp.logical_and(diff >= jnp.where(first, DIL_SPAN - i_idx, 0), diff <= DIL_SPAN))
    scale = DIL_HD ** -0.5
    neg_inf = -jnp.inf
    units = [(g, res, jb, h) for g in range(n) for res in range(a_refs[g].shape[0])
             for jb in range(a_refs[g].shape[1] // DIL_SPAN) for h in range(DIL_HEADS)]

    def window(g, res, jb, c0):
        if jb == 0:
            prev = kvp_refs[g][res, :, c0 - DIL_WIDTH:c0 - DIL_WIDTH + DIL_HD]
            return jnp.concatenate([prev, a_refs[g][res, 0:DIL_SPAN, c0:c0 + DIL_HD]], axis=0)
        return a_refs[g][res, (jb - 1) * DIL_SPAN:(jb + 1) * DIL_SPAN, c0:c0 + DIL_HD]

    scores = []
    for g, res, jb, h in units:
        q = a_refs[g][res, jb * DIL_SPAN:(jb + 1) * DIL_SPAN, h * DIL_HD:(h + 1) * DIL_HD]
        s = _dot(q, window(g, res, jb, DIL_WIDTH + h * DIL_HD), _NT) * scale
        scores.append(jnp.where(band_first[g] if jb == 0 else band, s, neg_inf))
    maxes = [jnp.max(s, axis=-1, keepdims=True) for s in scores]
    probs = [jnp.exp(s - m) for s, m in zip(scores, maxes)]
    sums = [jnp.sum(p, axis=-1, keepdims=True) for p in probs]
    lses = {}
    for (g, res, jb, h), p, m, l in zip(units, probs, maxes, sums):
        acc = _dot(p.astype(BF16), window(g, res, jb, 2 * DIL_WIDTH + h * DIL_HD))
        o_refs[g][res, jb * DIL_SPAN:(jb + 1) * DIL_SPAN, h * DIL_HD:(h + 1) * DIL_HD] = acc * (1.0 / l)
        lses[g, res, jb, h] = m + jnp.log(l)
    for g in range(n):
        nres, rows = a_refs[g].shape[:2]
        for res in range(nres):
            for jb in range(rows // DIL_SPAN):
                lse_refs[g][res, jb * DIL_SPAN:(jb + 1) * DIL_SPAN, :] = _pack_lse(
                    [lses[g, res, jb, h] for h in range(DIL_HEADS)])
            kvp_refs[g][res] = a_refs[g][res, rows - DIL_SPAN:rows, DIL_WIDTH:QKV_COLS]


def _dil_prompt(a_g, *, rows_per_step=1024):
    nb = a_g[0].shape[0]
    seq = a_g[0].shape[1] * a_g[0].shape[2]
    n_steps = seq // rows_per_step
    in_specs, o_specs, l_specs, o_shapes, l_shapes, scratch, len_steps = [], [], [], [], [], [], []
    for a4 in a_g:
        _, dil, length, _ = a4.shape
        rows = min(rows_per_step, length)
        nres = rows_per_step // rows
        steps_per_class = length // rows
        assert dil % nres == 0 and (dil // nres) * steps_per_class == n_steps

        def spec(cols, nres=nres, rows=rows, spc=steps_per_class):
            return pl.BlockSpec((None, nres, rows, cols), lambda b, s: (b, s // spc, s % spc, 0))

        in_specs.append(spec(QKV_COLS))
        o_specs.append(spec(DIL_WIDTH))
        l_specs.append(spec(LANES))
        o_shapes.append(jax.ShapeDtypeStruct((nb, dil, length, DIL_WIDTH), F32))
        l_shapes.append(jax.ShapeDtypeStruct((nb, dil, length, LANES), F32))
        scratch.append(pltpu.VMEM((nres, DIL_SPAN, KV_COLS), BF16))
        len_steps.append(steps_per_class)
    outs = pl.pallas_call(
        functools.partial(_dil_prompt_kernel, len_steps=tuple(len_steps)),
        grid=(nb, n_steps),
        in_specs=in_specs,
        out_specs=o_specs + l_specs,
        out_shape=o_shapes + l_shapes,
        scratch_shapes=scratch,
        compiler_params=_params("parallel", "arbitrary"),
        name="dil_prompt",
    )(*a_g)
    return outs[:N_GROUPS], outs[N_GROUPS:]


def _dil_sample_kernel(*refs, t_new):
    a_refs, cache_refs = refs[:N_GROUPS], refs[N_GROUPS:2 * N_GROUPS]
    o_refs, lse_refs = refs[2 * N_GROUPS:3 * N_GROUPS], refs[3 * N_GROUPS:]
    rpt = 2 * DIL_HEADS
    scale = DIL_HD ** -0.5
    neg_inf = -jnp.inf
    tq_n = lax.broadcasted_iota(jnp.int32, (t_new, t_new), 0)
    j_n = lax.broadcasted_iota(jnp.int32, (t_new, t_new), 1)

    def cache_head(g, c):
        ref = cache_refs[g]
        if len(ref.shape) == 2:
            return ref[pl.ds(c, ref.shape[0] // rpt, stride=rpt), :].astype(BF16)
        x = ref[:, pl.ds(c, t_new, stride=rpt), :]
        return x.reshape(ref.shape[0] * t_new, DIL_HD).astype(BF16)

    valid_c, valid_n = [], []
    for g, (_, dil) in enumerate(DIL_GROUPS):
        ref = cache_refs[g]
        n_keys = ref.shape[0] // rpt if len(ref.shape) == 2 else ref.shape[0] * t_new
        tq = lax.broadcasted_iota(jnp.int32, (t_new, n_keys), 0)
        key = lax.broadcasted_iota(jnp.int32, (t_new, n_keys), 1)
        if len(ref.shape) == 2:
            valid_c.append(jnp.logical_and(((key - tq) & (dil - 1)) == 0, key >= tq))
        else:
            valid_c.append((key & (t_new - 1)) == tq)
        valid_n.append(jnp.logical_and(((tq_n - j_n) & (dil - 1)) == 0, j_n <= tq_n))

    units = [(g, h) for g in range(N_GROUPS) for h in range(DIL_HEADS)]
    scores = []
    for g, h in units:
        q = a_refs[g][:, h * DIL_HD:(h + 1) * DIL_HD].astype(BF16)
        k_new = a_refs[g][:, DIL_WIDTH + h * DIL_HD:DIL_WIDTH + (h + 1) * DIL_HD].astype(BF16)
        s_c = jnp.where(valid_c[g], _dot(q, cache_head(g, h), _NT) * scale, neg_inf)
        s_n = jnp.where(valid_n[g], _dot(q, k_new, _NT) * scale, neg_inf)
        scores.append((s_c, s_n))
    maxes = [jnp.maximum(jnp.max(s_c, axis=-1, keepdims=True), jnp.max(s_n, axis=-1, keepdims=True))
             for s_c, s_n in scores]
    probs = [(jnp.exp(s_c - m), jnp.exp(s_n - m)) for (s_c, s_n), m in zip(scores, maxes)]
    sums = [jnp.sum(p_c, axis=-1, keepdims=True) + jnp.sum(p_n, axis=-1, keepdims=True) for p_c, p_n in probs]
    lses = {}
    for (g, h), (p_c, p_n), m, l in zip(units, probs, maxes, sums):
        v_new = a_refs[g][:, 2 * DIL_WIDTH + h * DIL_HD:2 * DIL_WIDTH + (h + 1) * DIL_HD].astype(BF16)
        acc = _dot(p_c.astype(BF16), cache_head(g, DIL_HEADS + h)) + _dot(p_n.astype(BF16), v_new)
        o_refs[g][:, h * DIL_HD:(h + 1) * DIL_HD] = acc * (1.0 / l)
        lses[g, h] = m + jnp.log(l)
    for g in range(N_GROUPS):
        lse_refs[g][...] = _pack_lse([lses[g, h] for h in range(DIL_HEADS)])


def _cache_rows(cache):
    nb, lb = cache.shape[:2]
    return cache.reshape(nb, lb * 2 * DIL_HEADS, DIL_HD)


def _dil_sample(a_g, caches, *, t_new):
    nb = caches[0].shape[0]
    rpt = 2 * DIL_HEADS
    assert t_new & (t_new - 1) == 0
    cache_args, cache_specs = [], []
    for cache, (_, dil) in zip(caches, DIL_GROUPS):
        lb = cache.shape[1]
        assert lb == dil * DIL_SPAN
        if dil > t_new:
            assert dil % t_new == 0
            cache_args.append(cache.reshape(nb, lb // dil, dil * rpt, DIL_HD))
            cache_specs.append(pl.BlockSpec((None, lb // dil, t_new * rpt, DIL_HD), lambda b: (b, 0, 0, 0)))
        else:
            cache_args.append(_cache_rows(cache))
            cache_specs.append(pl.BlockSpec((None, lb * rpt, DIL_HD), lambda b: (b, 0, 0)))
    outs = pl.pallas_call(
        functools.partial(_dil_sample_kernel, t_new=t_new),
        grid=(nb,),
        in_specs=[pl.BlockSpec((t_new, QKV_COLS), lambda b: (b, 0))] * N_GROUPS + cache_specs,
        out_specs=[pl.BlockSpec((t_new, DIL_WIDTH), lambda b: (b, 0))] * N_GROUPS
        + [pl.BlockSpec((t_new, LANES), lambda b: (b, 0))] * N_GROUPS,
        out_shape=[jax.ShapeDtypeStruct((nb * t_new, DIL_WIDTH), F32)] * N_GROUPS
        + [jax.ShapeDtypeStruct((nb * t_new, LANES), F32)] * N_GROUPS,
        compiler_params=_params("parallel"),
        name="dil_sample",
    )(*a_g, *cache_args)
    return outs[:N_GROUPS], outs[N_GROUPS:]


def _token_order(ref, scratch_refs):
    if len(ref.shape) == 2:
        return lambda rows: ref[rows, :]
    dil, rows_per_class, _ = ref.shape
    if dil == 1:
        return lambda rows: ref[0, rows, :]
    scratch = scratch_refs.pop()
    n_slabs = scratch.shape[0]
    for res in range(dil):
        for s in range(n_slabs):
            scratch[s, pl.ds(res, rows_per_class, stride=dil), :] = ref[res, :, s * LANES:(s + 1) * LANES]
    if n_slabs == 1:
        return lambda rows: scratch[0, rows, :]
    return lambda rows: jnp.concatenate([scratch[s, rows, :] for s in range(n_slabs)], axis=1)


def _final_kernel(ya_ref, o0_ref, o1_ref, o2_ref, l0_ref, l1_ref, l2_ref, gates_ref, x_ref, pe_ref,
                  wa_ref, wb_ref, wo_ref, wpp_ref, wpg_ref, gpost_ref, y_ref, *scratch_refs):
    o_scratch = [s for s in scratch_refs if s.shape[0] == DIL_WIDTH // LANES]
    l_scratch = [s for s in scratch_refs if s.shape[0] == 1]
    every_row = slice(None)
    lse_of = [_token_order(r, l_scratch) for r in (l0_ref, l1_ref, l2_ref)]
    o_of = [_token_order(r, o_scratch) for r in (o0_ref, o1_ref, o2_ref)]
    lse = [read(every_row) for read in lse_of]
    m = jnp.maximum(jnp.maximum(lse[0], lse[1]), lse[2])
    e = [jnp.exp(x - m) for x in lse]
    inv = 1.0 / (e[0] + e[1] + e[2])
    o_groups = [read(every_row) for read in o_of]
    heads = []
    for h in range(DIL_HEADS):
        hs = slice(h * DIL_HD, (h + 1) * DIL_HD)
        lane = h * LSE_LANES_PER_HEAD
        acc = None
        for eg, o_g in zip(e, o_groups):
            term = (eg * inv)[:, lane:lane + 1] * o_g[:, hs]
            acc = term if acc is None else acc + term
        heads.append(acc)
    o_b = jnp.concatenate(heads, axis=1)
    d = x_ref.shape[1]
    dg = gates_ref[:, 0:DIL_WIDTH].astype(F32)
    y_a = _dot(ya_ref[...].astype(BF16), wa_ref[...])
    y_b = _dot((o_b * (dg * _sigmoid(dg))).astype(BF16), wb_ref[...])
    ga = gates_ref[:, DIL_WIDTH:DIL_WIDTH + d].astype(F32)
    gb = gates_ref[:, DIL_WIDTH + d:DIL_WIDTH + 2 * d].astype(F32)
    merged = _sigmoid(ga) * y_a + _sigmoid(gb) * y_b
    t = _dot(merged.astype(BF16), wo_ref[...])
    ple = _dot(pe_ref[...].astype(BF16), wpp_ref[...])
    h = x_ref[...] + _rms(t, gpost_ref[...])
    gate = _dot(h.astype(BF16), wpg_ref[...])
    y_ref[...] = h + ple * _sigmoid(gate)


def _final(ya, os_, lses, gates, x2d, pe2d, wa, wb, wo, wpp, wpg, g_post, *, tm):
    n, d = x2d.shape

    def rows(cols):
        return pl.BlockSpec((tm, cols), lambda i: (i, 0))

    scratch = []

    def group_spec(arr):
        if arr.ndim == 2:
            return rows(arr.shape[1])
        _, dil, length, cols = arr.shape
        tiles_per_seq = dil * length // tm
        if dil > 1:
            scratch.append(pltpu.VMEM((cols // LANES, tm, LANES), F32))
        return pl.BlockSpec((None, dil, tm // dil, cols), lambda i: (i // tiles_per_seq, 0, i % tiles_per_seq, 0))

    return pl.pallas_call(
        _final_kernel,
        grid=(n // tm,),
        in_specs=[rows(GLA_VAL)] + [group_spec(a) for a in os_] + [group_spec(a) for a in lses]
        + [rows(gates.shape[1]), rows(d), rows(pe2d.shape[1])]
        + [_const_spec(w.shape) for w in (wa, wb, wo, wpp, wpg, g_post)],
        out_specs=rows(d),
        out_shape=jax.ShapeDtypeStruct((n, d), F32),
        scratch_shapes=scratch,
        compiler_params=_params("parallel"),
        name="final",
    )(ya, *os_, *lses, gates, x2d, pe2d, wa, wb, wo, wpp, wpg, g_post)


def _prep_weights(g_pre, g_post, w_in, w_gla_lr, b_gla_lr, g_gla_norm, w_gla_branch, w_dil_branch, w_out,
                  w_ple_proj, w_ple_gate):
    d = w_in.shape[0]
    c_lr = GLA_COLS
    c_q = c_lr + GLA_LOWRANK
    c_k = c_q + N_GROUPS * DIL_WIDTH
    c_v = c_k + N_GROUPS * DIL_WIDTH
    c_g = c_v + N_GROUPS * DIL_WIDTH
    w_t = w_in.T
    w_a = w_t[:c_lr].astype(BF16)
    w_lr = jnp.pad(w_t[c_lr:c_q].astype(BF16), ((0, LANES - GLA_LOWRANK), (0, 0)))
    w_lr2 = jnp.pad(w_gla_lr.astype(BF16), ((0, LANES - GLA_LOWRANK), (0, 0)))
    parts = []
    for g in range(N_GROUPS):
        for c in (c_q, c_k, c_v):
            parts.append(w_t[c + g * DIL_WIDTH:c + (g + 1) * DIL_WIDTH])
    parts.append(w_t[c_g:])
    w_r = jnp.concatenate(parts, axis=0).astype(BF16)
    return dict(
        g_pre=g_pre.reshape(1, d), g_post=g_post.reshape(1, d), w_a=w_a, w_lr=w_lr, w_lr2=w_lr2, w_r=w_r,
        b_lr=b_gla_lr.reshape(1, GLA_KEY), gn=g_gla_norm.reshape(1, GLA_VAL),
        wa=w_gla_branch.astype(BF16), wb=w_dil_branch.astype(BF16), wo=w_out.astype(BF16),
        wpp=w_ple_proj.astype(BF16), wpg=w_ple_gate.astype(BF16),
    )


PROMPT_TILE = 512


def _prompt_layer(x, pe, s0, w, sample_caches, sample_new_rows):
    nb, seq, d = x.shape
    n = nb * seq
    tm = PROMPT_TILE
    assert seq % tm == 0 and seq % DIL_GROUPS[-1][0] == 0
    x2d = x.reshape(n, d)
    rest = _proj_rest(x2d, w["g_pre"], w["w_r"], tm=tm, seq=seq, prompt=True)
    a_g, gates = rest[:N_GROUPS], rest[N_GROUPS]
    ns = sample_caches[0].shape[0]
    ya, s_new, next_caches = _proj_gla_fused(
        x2d, w["g_pre"], w["w_a"], w["w_lr"], w["w_lr2"], w["b_lr"], s0, w["gn"], tm=tm, seq=seq, chunk=GLA_CHUNK,
        shift_src=tuple(_cache_rows(c) for c in sample_caches),
        shift_tail=tuple(r.reshape(ns, -1, DIL_HD) for r in sample_new_rows))
    new_bufs = [kv.reshape(nb, -1, 2, DIL_HEADS, DIL_HD) for kv in rest[N_GROUPS + 1:]]
    os_, lses = _dil_prompt(a_g)
    y = _final(ya, os_, lses, gates, x2d, pe.reshape(n, -1),
               w["wa"], w["wb"], w["wo"], w["wpp"], w["wpg"], w["g_post"], tm=tm)
    next_caches = [c.reshape(old.shape) for c, old in zip(next_caches, sample_caches)]
    return y.reshape(nb, seq, d), s_new, new_bufs, next_caches


def _sample_proj(x, w):
    nb, seq, d = x.shape
    return _proj_rest(x.reshape(nb * seq, d), w["g_pre"], w["w_r"], tm=nb * seq, seq=seq, prompt=False)


def _sample_layer(x, pe, s0, caches, w, rest):
    nb, seq, d = x.shape
    n = nb * seq
    x2d = x.reshape(n, d)
    a_g, gates = rest[:N_GROUPS], rest[N_GROUPS]
    chunk = min(GLA_CHUNK, seq)
    assert seq % chunk == 0
    nseq = 8 if nb % 8 == 0 else 1
    qkvg, la = _proj_gla(x2d, w["g_pre"], w["w_a"], w["w_lr"], w["w_lr2"], w["b_lr"], tm=n, out_dtype=F32)
    ya, s_new = _gla(qkvg, la, s0, w["gn"], nb=nb, seq=seq, chunk=chunk, rows_per_step=seq, nseq=nseq,
                     out_dtype=F32, mm_dtype=BF16, precision=None)
    os_, lses = _dil_sample(a_g, caches, t_new=seq)
    y = _final(ya, os_, lses, gates, x2d, pe.reshape(n, -1),
               w["wa"], w["wb"], w["wo"], w["wpp"], w["wpg"], w["g_post"], tm=n)
    return y.reshape(nb, seq, d), s_new


def kernel(x_prompt, x_sample, state_gla, cache_kv_w128, cache_kv_w512, cache_kv_w2048, p_prompt, p_sample, g_pre, g_post, w_in, w_gla_lr, b_gla_lr, g_gla_norm, w_gla_branch, w_dil_branch, w_out, w_ple_proj, w_ple_gate):
    depth = w_in.shape[0]
    hp, hs = x_prompt, x_sample
    gla_p, gla_s = [], []
    kvp = [[] for _ in DIL_GROUPS]
    kvs = [[] for _ in DIL_GROUPS]
    for i in range(depth):
        w = _prep_weights(g_pre[i], g_post[i], w_in[i], w_gla_lr[i], b_gla_lr[i], g_gla_norm[i], w_gla_branch[i],
                          w_dil_branch[i], w_out[i], w_ple_proj[i], w_ple_gate[i])
        s0 = jnp.zeros((x_prompt.shape[0], GLA_HEADS, GLA_DK, GLA_DV), F32)
        caches = (cache_kv_w128[i], cache_kv_w512[i], cache_kv_w2048[i])
        rest_s = _sample_proj(hs, w)
        hp, sp_new, bp_new, bs_new = _prompt_layer(hp, p_prompt[i], s0, w, caches, rest_s[N_GROUPS + 1:])
        hs, ss_new = _sample_layer(hs, p_sample[i], state_gla[i], caches, w, rest_s)
        gla_p.append(sp_new)
        gla_s.append(ss_new)
        for g in range(N_GROUPS):
            kvp[g].append(bp_new[g])
            kvs[g].append(bs_new[g])
    return (hp, hs, jnp.stack(gla_p), jnp.stack(gla_s),
            jnp.stack(kvp[0]), jnp.stack(kvp[1]), jnp.stack(kvp[2]),
            jnp.stack(kvs[0]), jnp.stack(kvs[1]), jnp.stack(kvs[2]))
```

```python
import functools

import jax
import jax.numpy as jnp
from jax import lax
from jax.experimental import pallas as pl
from jax.experimental.pallas import tpu as pltpu

F32 = jnp.float32
BF16 = jnp.bfloat16

NORM_EPS = 1e-6
GLA_HEADS = 4
GLA_DK = 128
GLA_DV = 256
GLA_KEY = GLA_HEADS * GLA_DK
GLA_VAL = GLA_HEADS * GLA_DV
GLA_LOWRANK = 16
GLA_TAU = 16.0
GLA_CHUNK = 64
DIL_GROUPS = ((128, 1), (512, 4), (2048, 16))
DIL_HEADS = 4
DIL_HD = 128
DIL_WIDTH = DIL_HEADS * DIL_HD
DIL_SPAN = 128
N_GROUPS = len(DIL_GROUPS)
GLA_COLS = 2 * GLA_KEY + 2 * GLA_VAL
QKV_COLS = 3 * DIL_WIDTH
KV_COLS = 2 * DIL_WIDTH

LANES = 128
LSE_LANES_PER_HEAD = LANES // DIL_HEADS
_LOG2_LSE_LANES = LSE_LANES_PER_HEAD.bit_length() - 1
assert 1 << _LOG2_LSE_LANES == LSE_LANES_PER_HEAD
VMEM_LIMIT_BYTES = 56 * 1024 * 1024
_CHEAP_ROW_STRIDE = 4

_NT = (((1,), (1,)), ((), ()))
_TN = (((0,), (0,)), ((), ()))


def _dot(a, b, dims=None, precision=None):
    if dims is None:
        return jnp.dot(a, b, preferred_element_type=F32, precision=precision)
    return lax.dot_general(a, b, dims, preferred_element_type=F32, precision=precision)


def _rms(xf, g):
    return xf * lax.rsqrt(jnp.mean(xf * xf, axis=-1, keepdims=True) + NORM_EPS) * g


def _sigmoid(x):
    return 1.0 / (1.0 + jnp.exp(-x))


def _const_spec(shape):
    nd = len(shape)
    return pl.BlockSpec(shape, lambda *_: (0,) * nd, pipeline_mode=pl.Buffered(1))


def _params(*sem):
    return pltpu.CompilerParams(dimension_semantics=sem, vmem_limit_bytes=VMEM_LIMIT_BYTES)


def _proj_gla_kernel(x_ref, g_ref, w_ref, wlr_ref, wlr2_ref, blr_ref, qkvg_ref, la_ref, *, col_chunk):
    xn = _rms(x_ref[...], g_ref[...]).astype(BF16)
    for c0 in range(0, GLA_COLS, col_chunk):
        qkvg_ref[:, c0:c0 + col_chunk] = _dot(xn, w_ref[c0:c0 + col_chunk, :], _NT).astype(qkvg_ref.dtype)
    glr = _dot(xn, wlr_ref[...], _NT)
    z = _dot(glr.astype(BF16), wlr2_ref[...]) + blr_ref[...]
    log_sig = jnp.minimum(z, 0.0) - jnp.log1p(jnp.exp(-jnp.abs(z)))
    la_ref[...] = log_sig * (1.0 / GLA_TAU)


def _proj_gla(x2d, g_pre, w_a, w_lr, w_lr2, b_lr, *, tm, out_dtype):
    n, d = x2d.shape
    return pl.pallas_call(
        functools.partial(_proj_gla_kernel, col_chunk=512),
        grid=(n // tm,),
        in_specs=[
            pl.BlockSpec((tm, d), lambda i: (i, 0)),
            _const_spec(g_pre.shape),
            _const_spec(w_a.shape),
            _const_spec(w_lr.shape),
            _const_spec(w_lr2.shape),
            _const_spec(b_lr.shape),
        ],
        out_specs=[
            pl.BlockSpec((tm, GLA_COLS), lambda i: (i, 0)),
            pl.BlockSpec((tm, GLA_KEY), lambda i: (i, 0)),
        ],
        out_shape=[
            jax.ShapeDtypeStruct((n, GLA_COLS), out_dtype),
            jax.ShapeDtypeStruct((n, GLA_KEY), F32),
        ],
        compiler_params=_params("parallel"),
        name="proj_gla",
    )(x2d, g_pre, w_a, w_lr, w_lr2, b_lr)


def _proj_rest_kernel(x_ref, g_ref, w_ref, *refs, tm, tiles_per_seq, keeps, prompt):
    a_refs = refs[:N_GROUPS]
    gates_ref = refs[N_GROUPS]
    kv_refs = refs[N_GROUPS + 1:2 * N_GROUPS + 1]
    stage_refs = refs[2 * N_GROUPS + 1:]

    def body(cache_groups):
        xn = _rms(x_ref[...], g_ref[...]).astype(BF16)
        n_staged = 0
        for g in reversed(range(N_GROUPS)):
            base = g * QKV_COLS
            dil = DIL_GROUPS[g][1]
            for part in range(3):
                c0 = part * DIL_WIDTH
                r = _dot(xn, w_ref[base + c0:base + c0 + DIL_WIDTH, :], _NT)
                if not prompt:
                    a_refs[g][:, c0:c0 + DIL_WIDTH] = r
                elif dil == 1:
                    a_refs[g][0, :, c0:c0 + DIL_WIDTH] = r.astype(BF16)
                else:
                    stage = stage_refs[n_staged]
                    n_staged += 1
                    for s in range(DIL_WIDTH // LANES):
                        stage[s] = r[:, s * LANES:(s + 1) * LANES]
                    step = dil
                    if dil > _CHEAP_ROW_STRIDE:
                        step = dil // _CHEAP_ROW_STRIDE
                        assert step <= _CHEAP_ROW_STRIDE
                        stage2 = stage_refs[n_staged]
                        n_staged += 1
                        quarter = tm // _CHEAP_ROW_STRIDE
                        for c in range(_CHEAP_ROW_STRIDE):
                            for s in range(DIL_WIDTH // LANES):
                                stage2[s, c * quarter:(c + 1) * quarter, :] = (
                                    stage[s, pl.ds(c, quarter, stride=_CHEAP_ROW_STRIDE), :])
                        stage = stage2
                    for res in range(dil):
                        start = res if step == dil else (res % _CHEAP_ROW_STRIDE) * quarter + res // _CHEAP_ROW_STRIDE
                        for s in range(DIL_WIDTH // LANES):
                            a_refs[g][res, :, c0 + s * LANES:c0 + (s + 1) * LANES] = (
                                stage[s, pl.ds(start, tm // dil, stride=step), :].astype(BF16))
                if part > 0 and g in cache_groups:
                    rows = min(keeps[g], tm)
                    for h in range(DIL_HEADS):
                        kv_refs[g][pl.ds((part - 1) * DIL_HEADS + h, rows, stride=2 * DIL_HEADS), :] = (
                            r[tm - rows:, h * DIL_HD:(h + 1) * DIL_HD])
        gbase = N_GROUPS * QKV_COLS
        gcols = gates_ref.shape[1]
        for c0 in range(0, gcols, DIL_WIDTH):
            gates_ref[:, c0:c0 + DIL_WIDTH] = _dot(xn, w_ref[gbase + c0:gbase + c0 + DIL_WIDTH, :], _NT).astype(gates_ref.dtype)

    if not prompt:
        body(tuple(range(N_GROUPS)))
        return
    t = pl.program_id(0) % tiles_per_seq
    first_kept = [tiles_per_seq - max(1, keep // tm) for keep in keeps]
    edges = sorted(set(first_kept) | {0}) + [tiles_per_seq]
    for lo, hi in zip(edges[:-1], edges[1:]):
        groups = tuple(g for g in range(N_GROUPS) if first_kept[g] <= lo)
        pl.when(jnp.logical_and(t >= lo, t < hi))(lambda groups=groups: body(groups))


def _proj_rest(x2d, g_pre, w_r, *, tm, seq, prompt):
    n, d = x2d.shape
    nb = n // seq
    tiles_per_seq = seq // tm if prompt else None
    gate_cols = w_r.shape[0] - N_GROUPS * QKV_COLS
    keeps = tuple(min(w, seq) for (w, _) in DIL_GROUPS) if prompt else (tm,) * N_GROUPS
    rows_per_token = 2 * DIL_HEADS
    out_specs, out_shape, scratch = [], [], []
    if prompt:
        for _, dil in DIL_GROUPS:
            out_specs.append(pl.BlockSpec((None, dil, tm // dil, QKV_COLS),
                                          lambda i: (i // tiles_per_seq, 0, i % tiles_per_seq, 0)))
            out_shape.append(jax.ShapeDtypeStruct((nb, dil, seq // dil, QKV_COLS), BF16))
    else:
        out_specs += [pl.BlockSpec((tm, QKV_COLS), lambda i: (i, 0)) for _ in range(N_GROUPS)]
        out_shape += [jax.ShapeDtypeStruct((n, QKV_COLS), F32) for _ in range(N_GROUPS)]
    out_specs.append(pl.BlockSpec((tm, gate_cols), lambda i: (i, 0)))
    out_shape.append(jax.ShapeDtypeStruct((n, gate_cols), BF16 if prompt else F32))
    if not prompt:
        for _ in range(N_GROUPS):
            out_specs.append(pl.BlockSpec((tm * rows_per_token, DIL_HD), lambda i: (i, 0)))
            out_shape.append(jax.ShapeDtypeStruct((n * rows_per_token, DIL_HD), F32))
    else:
        for keep in keeps:
            if keep >= tm:
                assert keep % tm == 0
                kt = keep // tm

                def idx(i, kt=kt):
                    b, t = i // tiles_per_seq, i % tiles_per_seq
                    return (b * kt + jnp.maximum(t - (tiles_per_seq - kt), 0), 0)

                out_specs.append(pl.BlockSpec((tm * rows_per_token, DIL_HD), idx))
            else:
                out_specs.append(pl.BlockSpec((keep * rows_per_token, DIL_HD), lambda i: (i // tiles_per_seq, 0)))
            out_shape.append(jax.ShapeDtypeStruct((nb * keep * rows_per_token, DIL_HD), F32))
        n_stage = 3 * sum((1 if dil <= _CHEAP_ROW_STRIDE else 2) for _, dil in DIL_GROUPS if dil > 1)
        scratch = [pltpu.VMEM((DIL_WIDTH // LANES, tm, LANES), F32)] * n_stage
    return pl.pallas_call(
        functools.partial(_proj_rest_kernel, tm=tm, tiles_per_seq=tiles_per_seq, keeps=keeps, prompt=prompt),
        grid=(n // tm,),
        in_specs=[
            pl.BlockSpec((tm, d), lambda i: (i, 0)),
            _const_spec(g_pre.shape),
            _const_spec(w_r.shape),
        ],
        out_specs=out_specs,
        out_shape=out_shape,
        scratch_shapes=scratch,
        compiler_params=_params("arbitrary"),
        name="proj_rest",
    )(x2d, g_pre, w_r)


def _gla_kernel(qkvg_ref, la_ref, s0_ref, gn_ref, ya_ref, sfin_ref, st_ref, *, chunk, nchunk, nseq, mm_dtype, precision):
    j = pl.program_id(1)

    @pl.when(j == 0)
    def _():
        for s in range(nseq):
            for h in range(GLA_HEADS):
                st_ref[s * GLA_HEADS + h] = s0_ref[s, h].T

    _gla_compute(lambda r, c: qkvg_ref[r, c], lambda r: la_ref[r, :], st_ref, gn_ref, ya_ref,
                 chunk=chunk, nchunk=nchunk, nseq=nseq, mm_dtype=mm_dtype, precision=precision)

    @pl.when(j == pl.num_programs(1) - 1)
    def _():
        for s in range(nseq):
            for h in range(GLA_HEADS):
                sfin_ref[s, h] = st_ref[s * GLA_HEADS + h].T


def _gla_compute(*args, **kwargs):
    for _ in _gla_phases(*args, **kwargs):
        pass


def _interleave(*generators, phases_per_round=None):
    live = {i: gen for i, gen in enumerate(generators)}
    counts = phases_per_round or (1,) * len(generators)
    while live:
        for i in list(live):
            for _ in range(counts[i]):
                if next(live[i], StopIteration) is StopIteration:
                    del live[i]
                    break


def _gla_phases(load_qkvg, load_la, st_ref, gn_ref, ya_ref, *, chunk, nchunk, nseq, mm_dtype, precision):
    row = lax.broadcasted_iota(jnp.int32, (chunk, chunk), 0)
    col = lax.broadcasted_iota(jnp.int32, (chunk, chunk), 1)
    causal = row >= col
    tril = causal.astype(F32)
    qscale = GLA_DK ** -0.5

    segs = [(s, c) for s in range(nseq) for c in range(nchunk)]
    units = [(s, c, h) for s, c in segs for h in range(GLA_HEADS)]

    def rows_of(s, c):
        r0 = (s * nchunk + c) * chunk
        return slice(r0, r0 + chunk)

    def key_cols(h):
        return slice(h * GLA_DK, (h + 1) * GLA_DK)

    def v_of(s, c, h):
        return load_qkvg(rows_of(s, c), slice(2 * GLA_KEY + h * GLA_DV, 2 * GLA_KEY + (h + 1) * GLA_DV)).astype(mm_dtype)

    def cumsum(a):
        if mm_dtype != BF16:
            return _dot(tril, a, precision=lax.Precision.HIGHEST)
        hi = a.astype(BF16)
        rest = a - hi.astype(F32)
        mid = rest.astype(BF16)
        lo = (rest - mid.astype(F32)).astype(BF16)
        parts = _dot(tril.astype(BF16), jnp.concatenate([hi, mid, lo], axis=1))
        return parts[:, :GLA_KEY] + parts[:, GLA_KEY:2 * GLA_KEY] + parts[:, 2 * GLA_KEY:]

    cum = {sc: cumsum(load_la(rows_of(*sc))) for sc in segs}
    b_end = {sc: cum[sc][chunk - 1:chunk, :] for sc in segs}
    e_b = {sc: jnp.exp(cum[sc]) for sc in segs}
    e_nb = {sc: jnp.exp(-cum[sc]) for sc in segs}
    e_rest = {sc: jnp.exp(b_end[sc] - cum[sc]) for sc in segs}
    e_end = {sc: jnp.exp(b_end[sc]) for sc in segs}
    yield
    q_in, k_dec, k_end = {}, {}, {}
    for s, c, h in units:
        q = load_qkvg(rows_of(s, c), key_cols(h)).astype(F32) * qscale
        k = load_qkvg(rows_of(s, c), slice(GLA_KEY + h * GLA_DK, GLA_KEY + (h + 1) * GLA_DK)).astype(F32)
        q_in[s, c, h] = (q * e_b[s, c][:, key_cols(h)]).astype(mm_dtype)
        k_dec[s, c, h] = (k * e_nb[s, c][:, key_cols(h)]).astype(mm_dtype)
        k_end[s, c, h] = (k * e_rest[s, c][:, key_cols(h)]).astype(mm_dtype)
    yield
    att = {u: jnp.where(causal, _dot(q_in[u], k_dec[u], _NT, precision), 0.0).astype(mm_dtype) for u in units}
    yield
    o_intra = {u: _dot(att[u], v_of(*u), None, precision) for u in units}
    yield
    d_state = {u: _dot(v_of(*u), k_end[u], _TN, precision) for u in units}
    yield
    o_inter = {}
    for s in range(nseq):
        for h in range(GLA_HEADS):
            st = st_ref[s * GLA_HEADS + h]
            for c in range(nchunk):
                o_inter[s, c, h] = _dot(q_in[s, c, h], st.astype(mm_dtype), _NT, precision)
                st = st * e_end[s, c][:, key_cols(h)] + d_state[s, c, h]
            st_ref[s * GLA_HEADS + h] = st
    yield
    for i, (s, c, h) in enumerate(units):
        vs = slice(h * GLA_DV, (h + 1) * GLA_DV)
        gg = load_qkvg(rows_of(s, c), slice(2 * GLA_KEY + GLA_VAL + h * GLA_DV,
                                            2 * GLA_KEY + GLA_VAL + (h + 1) * GLA_DV)).astype(F32)
        on = _rms(o_intra[s, c, h] + o_inter[s, c, h], gn_ref[:, vs])
        ya_ref[rows_of(s, c), vs] = (on * (gg * _sigmoid(gg))).astype(ya_ref.dtype)
        if i % 8 == 7:
            yield


def _proj_gla_fused_kernel(x_ref, g_ref, w_ref, wlr_ref, wlr2_ref, blr_ref, s0_ref, gn_ref, *refs,
                           steps, chunk, col_chunk, n_shift, pieces, n_chunks):
    old_refs = refs[:n_shift]
    next_refs = refs[n_shift:2 * n_shift]
    tail_refs = refs[2 * n_shift:3 * n_shift]
    ya_ref, sfin_ref = refs[3 * n_shift:3 * n_shift + 2]
    new_refs = refs[3 * n_shift + 2:4 * n_shift + 2]
    qkvg_s0, qkvg_s1, la_s0, la_s1, st_ref = refs[4 * n_shift + 2:]
    j = pl.program_id(0)
    prev = j - 1

    def shift_pieces(rows_per_phase=1024):
        last_piece = jnp.minimum(j, n_chunks - 1) % pieces == pieces - 1
        for old, nxt, tail, new in zip(old_refs, next_refs, tail_refs, new_refs):
            p_rows, t_rows = old.shape[0], tail.shape[0]
            for r0 in range(0, p_rows - t_rows, rows_per_phase):
                r1 = min(r0 + rows_per_phase, p_rows - t_rows)
                new[r0:r1, :] = old[t_rows + r0:t_rows + r1, :]
                yield
            new[p_rows - t_rows:p_rows, :] = jnp.where(last_piece, tail[...], nxt[...])
            yield

    @pl.when(j == 0)
    def _():
        qkvg_s1[...] = jnp.zeros_like(qkvg_s1)
        la_s1[...] = jnp.zeros_like(la_s1)

    @pl.when(jnp.logical_or(j == 0, prev % steps == 0))
    def _():
        for h in range(GLA_HEADS):
            st_ref[h] = s0_ref[0, h].T

    def project(qkvg_w, la_w):
        xn = _rms(x_ref[...], g_ref[...]).astype(BF16)
        yield
        for c0 in range(0, GLA_COLS, col_chunk):
            qkvg_w[:, c0:c0 + col_chunk] = _dot(xn, w_ref[c0:c0 + col_chunk, :], _NT).astype(qkvg_w.dtype)
            yield
        glr = _dot(xn, wlr_ref[...], _NT)
        z = _dot(glr.astype(BF16), wlr2_ref[...]) + blr_ref[...]
        log_sig = jnp.minimum(z, 0.0) - jnp.log1p(jnp.exp(-jnp.abs(z)))
        la_w[...] = log_sig * (1.0 / GLA_TAU)

    def step(qkvg_w, la_w, qkvg_r, la_r):
        _interleave(
            project(qkvg_w, la_w),
            _gla_phases(lambda r, c: qkvg_r[r, c], lambda r: la_r[r, :], st_ref, gn_ref, ya_ref,
                        chunk=chunk, nchunk=x_ref.shape[0] // chunk, nseq=1, mm_dtype=BF16, precision=None),
            shift_pieces(),
            phases_per_round=(1, 2, 2))

    @pl.when(j % 2 == 0)
    def _():
        step(qkvg_s0, la_s0, qkvg_s1, la_s1)

    @pl.when(j % 2 == 1)
    def _():
        step(qkvg_s1, la_s1, qkvg_s0, la_s0)

    @pl.when(jnp.logical_and(j >= 1, prev % steps == steps - 1))
    def _():
        for h in range(GLA_HEADS):
            sfin_ref[0, h] = st_ref[h].T


def _proj_gla_fused(x2d, g_pre, w_a, w_lr, w_lr2, b_lr, s0, gn, *, tm, seq, chunk, shift_src=(), shift_tail=()):
    n, d = x2d.shape
    nb = n // seq
    steps = seq // tm
    n_tiles = n // tm
    n_shift = len(shift_src)
    pieces = n_chunks = 1
    copy_in, copy_out = [], []
    if n_shift:
        entries = shift_src[0].shape[0]
        pieces = max(1, n_tiles // entries)
        n_chunks = entries * pieces
        assert n_chunks <= n_tiles + 1

        def piece_of(j):
            c = jnp.minimum(j, n_chunks - 1)
            return c // pieces, c % pieces

        next_specs, tail_specs = [], []
        for src, tail in zip(shift_src, shift_tail):
            t_rows = tail.shape[1]
            p_rows = src.shape[1] // pieces
            assert src.shape[0] == tail.shape[0] == entries and src.shape[2] == tail.shape[2] == LANES
            assert src.shape[1] % pieces == 0 and p_rows % t_rows == 0 and t_rows % 8 == 0
            piece_spec = pl.BlockSpec((None, p_rows, LANES), lambda j: (*piece_of(j), 0))
            last_t_block = src.shape[1] // t_rows - 1

            def next_idx(j, per_piece=p_rows // t_rows, last=last_t_block):
                entry, piece = piece_of(j)
                return (entry, jnp.minimum((piece + 1) * per_piece, last), 0)

            copy_in.append(piece_spec)
            next_specs.append(pl.BlockSpec((None, t_rows, LANES), next_idx))
            tail_specs.append(pl.BlockSpec((None, t_rows, LANES), lambda j: (piece_of(j)[0], 0, 0)))
            copy_out.append(piece_spec)
        copy_in += next_specs + tail_specs

    def prev_tile(j):
        return jnp.maximum(j - 1, 0)

    outs = pl.pallas_call(
        functools.partial(_proj_gla_fused_kernel, steps=steps, chunk=chunk, col_chunk=512, n_shift=n_shift,
                          pieces=pieces, n_chunks=n_chunks),
        grid=(n_tiles + 1,),
        in_specs=[
            pl.BlockSpec((tm, d), lambda j: (jnp.minimum(j, n_tiles - 1), 0)),
            _const_spec(g_pre.shape),
            _const_spec(w_a.shape),
            _const_spec(w_lr.shape),
            _const_spec(w_lr2.shape),
            _const_spec(b_lr.shape),
            pl.BlockSpec((1, GLA_HEADS, GLA_DK, GLA_DV), lambda j: (prev_tile(j) // steps, 0, 0, 0)),
            _const_spec(gn.shape),
        ] + copy_in,
        out_specs=[
            pl.BlockSpec((tm, GLA_VAL), lambda j: (prev_tile(j), 0)),
            pl.BlockSpec((1, GLA_HEADS, GLA_DK, GLA_DV), lambda j: (prev_tile(j) // steps, 0, 0, 0)),
        ] + copy_out,
        out_shape=[
            jax.ShapeDtypeStruct((n, GLA_VAL), BF16),
            jax.ShapeDtypeStruct((nb, GLA_HEADS, GLA_DK, GLA_DV), F32),
        ] + [jax.ShapeDtypeStruct(src.shape, src.dtype) for src in shift_src],
        scratch_shapes=[
            pltpu.VMEM((tm, GLA_COLS), BF16),
            pltpu.VMEM((tm, GLA_COLS), BF16),
            pltpu.VMEM((tm, GLA_KEY), F32),
            pltpu.VMEM((tm, GLA_KEY), F32),
            pltpu.VMEM((GLA_HEADS, GLA_DV, GLA_DK), F32),
        ],
        compiler_params=_params("arbitrary"),
        name="proj_gla_fused",
    )(x2d, g_pre, w_a, w_lr, w_lr2, b_lr, s0, gn, *shift_src, *shift_src, *shift_tail)
    return outs[0], outs[1], list(outs[2:])


def _gla(qkvg, la, s0, gn, *, nb, seq, chunk, rows_per_step, nseq, out_dtype, mm_dtype, precision):
    steps = seq // rows_per_step
    assert nb % nseq == 0 and (nseq == 1 or steps == 1)
    rows = nseq * rows_per_step
    return pl.pallas_call(
        functools.partial(_gla_kernel, chunk=chunk, nchunk=rows_per_step // chunk, nseq=nseq, mm_dtype=mm_dtype,
                          precision=precision),
        grid=(nb // nseq, steps),
        in_specs=[
            pl.BlockSpec((rows, GLA_COLS), lambda b, j: (b * steps + j, 0)),
            pl.BlockSpec((rows, GLA_KEY), lambda b, j: (b * steps + j, 0)),
            pl.BlockSpec((nseq, GLA_HEADS, GLA_DK, GLA_DV), lambda b, j: (b, 0, 0, 0)),
            pl.BlockSpec((1, GLA_VAL), lambda b, j: (0, 0)),
        ],
        out_specs=[
            pl.BlockSpec((rows, GLA_VAL), lambda b, j: (b * steps + j, 0)),
            pl.BlockSpec((nseq, GLA_HEADS, GLA_DK, GLA_DV), lambda b, j: (b, 0, 0, 0)),
        ],
        out_shape=[
            jax.ShapeDtypeStruct((nb * seq, GLA_VAL), out_dtype),
            jax.ShapeDtypeStruct((nb, GLA_HEADS, GLA_DK, GLA_DV), F32),
        ],
        scratch_shapes=[pltpu.VMEM((nseq * GLA_HEADS, GLA_DV, GLA_DK), F32)],
        compiler_params=_params("parallel", "arbitrary"),
        name="gla",
    )(qkvg, la, s0, gn)


def _pack_lse(lses):
    rows = lses[0].shape[0]
    lane_head = lax.broadcasted_iota(jnp.int32, (rows, LANES), 1) >> _LOG2_LSE_LANES
    packed = jnp.broadcast_to(lses[0], (rows, LANES))
    for h in range(1, DIL_HEADS):
        packed = jnp.where(lane_head == h, lses[h], packed)
    return packed


def _dil_prompt_kernel(*refs, len_steps):
    n = N_GROUPS
    a_refs, o_refs, lse_refs, kvp_refs = refs[:n], refs[n:2 * n], refs[2 * n:3 * n], refs[3 * n:]
    step = pl.program_id(1)
    i_idx = lax.broadcasted_iota(jnp.int32, (DIL_SPAN, 2 * DIL_SPAN), 0)
    j_idx = lax.broadcasted_iota(jnp.int32, (DIL_SPAN, 2 * DIL_SPAN), 1)
    diff = j_idx - i_idx
    band = jnp.logical_and(diff >= 0, diff <= DIL_SPAN)
    band_first = []
    for g in range(n):
        first = step % len_steps[g] == 0

        @pl.when(first)
        def _(g=g):
            kvp_refs[g][...] = jnp.zeros_like(kvp_refs[g])

        band_first.append(jnp.logical_and(diff >= jnp.where(first, DIL_SPAN - i_idx, 0), diff <= DIL_SPAN))
    scale = DIL_HD ** -0.5
    neg_inf = -jnp.inf
    units = [(g, res, jb, h) for g in range(n) for res in range(a_refs[g].shape[0])
             for jb in range(a_refs[g].shape[1] // DIL_SPAN) for h in range(DIL_HEADS)]

    def window(g, res, jb, c0):
        if jb == 0:
            prev = kvp_refs[g][res, :, c0 - DIL_WIDTH:c0 - DIL_WIDTH + DIL_HD]
            return jnp.concatenate([prev, a_refs[g][res, 0:DIL_SPAN, c0:c0 + DIL_HD]], axis=0)
        return a_refs[g][res, (jb - 1) * DIL_SPAN:(jb + 1) * DIL_SPAN, c0:c0 + DIL_HD]

    scores = []
    for g, res, jb, h in units:
        q = a_refs[g][res, jb * DIL_SPAN:(jb + 1) * DIL_SPAN, h * DIL_HD:(h + 1) * DIL_HD]
        s = _dot(q, window(g, res, jb, DIL_WIDTH + h * DIL_HD), _NT) * scale
        scores.append(jnp.where(band_first[g] if jb == 0 else band, s, neg_inf))
    maxes = [jnp.max(s, axis=-1, keepdims=True) for s in scores]
    probs = [jnp.exp(s - m) for s, m in zip(scores, maxes)]
    sums = [jnp.sum(p, axis=-1, keepdims=True) for p in probs]
    lses = {}
    for (g, res, jb, h), p, m, l in zip(units, probs, maxes, sums):
        acc = _dot(p.astype(BF16), window(g, res, jb, 2 * DIL_WIDTH + h * DIL_HD))
        o_refs[g][res, jb * DIL_SPAN:(jb + 1) * DIL_SPAN, h * DIL_HD:(h + 1) * DIL_HD] = acc * (1.0 / l)
        lses[g, res, jb, h] = m + jnp.log(l)
    for g in range(n):
        nres, rows = a_refs[g].shape[:2]
        for res in range(nres):
            for jb in range(rows // DIL_SPAN):
                lse_refs[g][res, jb * DIL_SPAN:(jb + 1) * DIL_SPAN, :] = _pack_lse(
                    [lses[g, res, jb, h] for h in range(DIL_HEADS)])
            kvp_refs[g][res] = a_refs[g][res, rows - DIL_SPAN:rows, DIL_WIDTH:QKV_COLS]


def _dil_prompt(a_g, *, rows_per_step=1024):
    nb = a_g[0].shape[0]
    seq = a_g[0].shape[1] * a_g[0].shape[2]
    n_steps = seq // rows_per_step
    in_specs, o_specs, l_specs, o_shapes, l_shapes, scratch, len_steps = [], [], [], [], [], [], []
    for a4 in a_g:
        _, dil, length, _ = a4.shape
        rows = min(rows_per_step, length)
        nres = rows_per_step // rows
        steps_per_class = length // rows
        assert dil % nres == 0 and (dil // nres) * steps_per_class == n_steps

        def spec(cols, nres=nres, rows=rows, spc=steps_per_class):
            return pl.BlockSpec((None, nres, rows, cols), lambda b, s: (b, s // spc, s % spc, 0))

        in_specs.append(spec(QKV_COLS))
        o_specs.append(spec(DIL_WIDTH))
        l_specs.append(spec(LANES))
        o_shapes.append(jax.ShapeDtypeStruct((nb, dil, length, DIL_WIDTH), F32))
        l_shapes.append(jax.ShapeDtypeStruct((nb, dil, length, LANES), F32))
        scratch.append(pltpu.VMEM((nres, DIL_SPAN, KV_COLS), BF16))
        len_steps.append(steps_per_class)
    outs = pl.pallas_call(
        functools.partial(_dil_prompt_kernel, len_steps=tuple(len_steps)),
        grid=(nb, n_steps),
        in_specs=in_specs,
        out_specs=o_specs + l_specs,
        out_shape=o_shapes + l_shapes,
        scratch_shapes=scratch,
        compiler_params=_params("parallel", "arbitrary"),
        name="dil_prompt",
    )(*a_g)
    return outs[:N_GROUPS], outs[N_GROUPS:]


def _dil_sample_kernel(*refs, t_new):
    a_refs, cache_refs = refs[:N_GROUPS], refs[N_GROUPS:2 * N_GROUPS]
    o_refs, lse_refs = refs[2 * N_GROUPS:3 * N_GROUPS], refs[3 * N_GROUPS:]
    rpt = 2 * DIL_HEADS
    scale = DIL_HD ** -0.5
    neg_inf = -jnp.inf
    tq_n = lax.broadcasted_iota(jnp.int32, (t_new, t_new), 0)
    j_n = lax.broadcasted_iota(jnp.int32, (t_new, t_new), 1)

    def cache_head(g, c):
        ref = cache_refs[g]
        if len(ref.shape) == 2:
            return ref[pl.ds(c, ref.shape[0] // rpt, stride=rpt), :].astype(BF16)
        x = ref[:, pl.ds(c, t_new, stride=rpt), :]
        return x.reshape(ref.shape[0] * t_new, DIL_HD).astype(BF16)

    valid_c, valid_n = [], []
    for g, (_, dil) in enumerate(DIL_GROUPS):
        ref = cache_refs[g]
        n_keys = ref.shape[0] // rpt if len(ref.shape) == 2 else ref.shape[0] * t_new
        tq = lax.broadcasted_iota(jnp.int32, (t_new, n_keys), 0)
        key = lax.broadcasted_iota(jnp.int32, (t_new, n_keys), 1)
        if len(ref.shape) == 2:
            valid_c.append(jnp.logical_and(((key - tq) & (dil - 1)) == 0, key >= tq))
        else:
            valid_c.append((key & (t_new - 1)) == tq)
        valid_n.append(jnp.logical_and(((tq_n - j_n) & (dil - 1)) == 0, j_n <= tq_n))

    units = [(g, h) for g in range(N_GROUPS) for h in range(DIL_HEADS)]
    scores = []
    for g, h in units:
        q = a_refs[g][:, h * DIL_HD:(h + 1) * DIL_HD].astype(BF16)
        k_new = a_refs[g][:, DIL_WIDTH + h * DIL_HD:DIL_WIDTH + (h + 1) * DIL_HD].astype(BF16)
        s_c = jnp.where(valid_c[g], _dot(q, cache_head(g, h), _NT) * scale, neg_inf)
        s_n = jnp.where(valid_n[g], _dot(q, k_new, _NT) * scale, neg_inf)
        scores.append((s_c, s_n))
    maxes = [jnp.maximum(jnp.max(s_c, axis=-1, keepdims=True), jnp.max(s_n, axis=-1, keepdims=True))
             for s_c, s_n in scores]
    probs = [(jnp.exp(s_c - m), jnp.exp(s_n - m)) for (s_c, s_n), m in zip(scores, maxes)]
    sums = [jnp.sum(p_c, axis=-1, keepdims=True) + jnp.sum(p_n, axis=-1, keepdims=True) for p_c, p_n in probs]
    lses = {}
    for (g, h), (p_c, p_n), m, l in zip(units, probs, maxes, sums):
        v_new = a_refs[g][:, 2 * DIL_WIDTH + h * DIL_HD:2 * DIL_WIDTH + (h + 1) * DIL_HD].astype(BF16)
        acc = _dot(p_c.astype(BF16), cache_head(g, DIL_HEADS + h)) + _dot(p_n.astype(BF16), v_new)
        o_refs[g][:, h * DIL_HD:(h + 1) * DIL_HD] = acc * (1.0 / l)
        lses[g, h] = m + jnp.log(l)
    for g in range(N_GROUPS):
        lse_refs[g][...] = _pack_lse([lses[g, h] for h in range(DIL_HEADS)])


def _cache_rows(cache):
    nb, lb = cache.shape[:2]
    return cache.reshape(nb, lb * 2 * DIL_HEADS, DIL_HD)


def _dil_sample(a_g, caches, *, t_new):
    nb = caches[0].shape[0]
    rpt = 2 * DIL_HEADS
    assert t_new & (t_new - 1) == 0
    cache_args, cache_specs = [], []
    for cache, (_, dil) in zip(caches, DIL_GROUPS):
        lb = cache.shape[1]
        assert lb == dil * DIL_SPAN
        if dil > t_new:
            assert dil % t_new == 0
            cache_args.append(cache.reshape(nb, lb // dil, dil * rpt, DIL_HD))
            cache_specs.append(pl.BlockSpec((None, lb // dil, t_new * rpt, DIL_HD), lambda b: (b, 0, 0, 0)))
        else:
            cache_args.append(_cache_rows(cache))
            cache_specs.append(pl.BlockSpec((None, lb * rpt, DIL_HD), lambda b: (b, 0, 0)))
    outs = pl.pallas_call(
        functools.partial(_dil_sample_kernel, t_new=t_new),
        grid=(nb,),
        in_specs=[pl.BlockSpec((t_new, QKV_COLS), lambda b: (b, 0))] * N_GROUPS + cache_specs,
        out_specs=[pl.BlockSpec((t_new, DIL_WIDTH), lambda b: (b, 0))] * N_GROUPS
        + [pl.BlockSpec((t_new, LANES), lambda b: (b, 0))] * N_GROUPS,
        out_shape=[jax.ShapeDtypeStruct((nb * t_new, DIL_WIDTH), F32)] * N_GROUPS
        + [jax.ShapeDtypeStruct((nb * t_new, LANES), F32)] * N_GROUPS,
        compiler_params=_params("parallel"),
        name="dil_sample",
    )(*a_g, *cache_args)
    return outs[:N_GROUPS], outs[N_GROUPS:]


def _token_order(ref, scratch_refs):
    if len(ref.shape) == 2:
        return lambda rows: ref[rows, :]
    dil, rows_per_class, _ = ref.shape
    if dil == 1:
        return lambda rows: ref[0, rows, :]
    scratch = scratch_refs.pop()
    n_slabs = scratch.shape[0]
    for res in range(dil):
        for s in range(n_slabs):
            scratch[s, pl.ds(res, rows_per_class, stride=dil), :] = ref[res, :, s * LANES:(s + 1) * LANES]
    if n_slabs == 1:
        return lambda rows: scratch[0, rows, :]
    return lambda rows: jnp.concatenate([scratch[s, rows, :] for s in range(n_slabs)], axis=1)


def _final_kernel(ya_ref, o0_ref, o1_ref, o2_ref, l0_ref, l1_ref, l2_ref, gates_ref, x_ref, pe_ref,
                  wa_ref, wb_ref, wo_ref, wpp_ref, wpg_ref, gpost_ref, y_ref, *scratch_refs):
    o_scratch = [s for s in scratch_refs if s.shape[0] == DIL_WIDTH // LANES]
    l_scratch = [s for s in scratch_refs if s.shape[0] == 1]
    every_row = slice(None)
    lse_of = [_token_order(r, l_scratch) for r in (l0_ref, l1_ref, l2_ref)]
    o_of = [_token_order(r, o_scratch) for r in (o0_ref, o1_ref, o2_ref)]
    lse = [read(every_row) for read in lse_of]
    m = jnp.maximum(jnp.maximum(lse[0], lse[1]), lse[2])
    e = [jnp.exp(x - m) for x in lse]
    inv = 1.0 / (e[0] + e[1] + e[2])
    o_groups = [read(every_row) for read in o_of]
    heads = []
    for h in range(DIL_HEADS):
        hs = slice(h * DIL_HD, (h + 1) * DIL_HD)
        lane = h * LSE_LANES_PER_HEAD
        acc = None
        for eg, o_g in zip(e, o_groups):
            term = (eg * inv)[:, lane:lane + 1] * o_g[:, hs]
            acc = term if acc is None else acc + term
        heads.append(acc)
    o_b = jnp.concatenate(heads, axis=1)
    d = x_ref.shape[1]
    dg = gates_ref[:, 0:DIL_WIDTH].astype(F32)
    y_a = _dot(ya_ref[...].astype(BF16), wa_ref[...])
    y_b = _dot((o_b * (dg * _sigmoid(dg))).astype(BF16), wb_ref[...])
    ga = gates_ref[:, DIL_WIDTH:DIL_WIDTH + d].astype(F32)
    gb = gates_ref[:, DIL_WIDTH + d:DIL_WIDTH + 2 * d].astype(F32)
    merged = _sigmoid(ga) * y_a + _sigmoid(gb) * y_b
    t = _dot(merged.astype(BF16), wo_ref[...])
    ple = _dot(pe_ref[...].astype(BF16), wpp_ref[...])
    h = x_ref[...] + _rms(t, gpost_ref[...])
    gate = _dot(h.astype(BF16), wpg_ref[...])
    y_ref[...] = h + ple * _sigmoid(gate)


def _final(ya, os_, lses, gates, x2d, pe2d, wa, wb, wo, wpp, wpg, g_post, *, tm):
    n, d = x2d.shape

    def rows(cols):
        return pl.BlockSpec((tm, cols), lambda i: (i, 0))

    scratch = []

    def group_spec(arr):
        if arr.ndim == 2:
            return rows(arr.shape[1])
        _, dil, length, cols = arr.shape
        tiles_per_seq = dil * length // tm
        if dil > 1:
            scratch.append(pltpu.VMEM((cols // LANES, tm, LANES), F32))
        return pl.BlockSpec((None, dil, tm // dil, cols), lambda i: (i // tiles_per_seq, 0, i % tiles_per_seq, 0))

    return pl.pallas_call(
        _final_kernel,
        grid=(n // tm,),
        in_specs=[rows(GLA_VAL)] + [group_spec(a) for a in os_] + [group_spec(a) for a in lses]
        + [rows(gates.shape[1]), rows(d), rows(pe2d.shape[1])]
        + [_const_spec(w.shape) for w in (wa, wb, wo, wpp, wpg, g_post)],
        out_specs=rows(d),
        out_shape=jax.ShapeDtypeStruct((n, d), F32),
        scratch_shapes=scratch,
        compiler_params=_params("parallel"),
        name="final",
    )(ya, *os_, *lses, gates, x2d, pe2d, wa, wb, wo, wpp, wpg, g_post)


def _prep_weights(g_pre, g_post, w_in, w_gla_lr, b_gla_lr, g_gla_norm, w_gla_branch, w_dil_branch, w_out,
                  w_ple_proj, w_ple_gate):
    d = w_in.shape[0]
    c_lr = GLA_COLS
    c_q = c_lr + GLA_LOWRANK
    c_k = c_q + N_GROUPS * DIL_WIDTH
    c_v = c_k + N_GROUPS * DIL_WIDTH
    c_g = c_v + N_GROUPS * DIL_WIDTH
    w_t = w_in.T
    w_a = w_t[:c_lr].astype(BF16)
    w_lr = jnp.pad(w_t[c_lr:c_q].astype(BF16), ((0, LANES - GLA_LOWRANK), (0, 0)))
    w_lr2 = jnp.pad(w_gla_lr.astype(BF16), ((0, LANES - GLA_LOWRANK), (0, 0)))
    parts = []
    for g in range(N_GROUPS):
        for c in (c_q, c_k, c_v):
            parts.append(w_t[c + g * DIL_WIDTH:c + (g + 1) * DIL_WIDTH])
    parts.append(w_t[c_g:])
    w_r = jnp.concatenate(parts, axis=0).astype(BF16)
    return dict(
        g_pre=g_pre.reshape(1, d), g_post=g_post.reshape(1, d), w_a=w_a, w_lr=w_lr, w_lr2=w_lr2, w_r=w_r,
        b_lr=b_gla_lr.reshape(1, GLA_KEY), gn=g_gla_norm.reshape(1, GLA_VAL),
        wa=w_gla_branch.astype(BF16), wb=w_dil_branch.astype(BF16), wo=w_out.astype(BF16),
        wpp=w_ple_proj.astype(BF16), wpg=w_ple_gate.astype(BF16),
    )


PROMPT_TILE = 512


def _prompt_layer(x, pe, s0, w, sample_caches, sample_new_rows):
    nb, seq, d = x.shape
    n = nb * seq
    tm = PROMPT_TILE
    assert seq % tm == 0 and seq % DIL_GROUPS[-1][0] == 0
    x2d = x.reshape(n, d)
    rest = _proj_rest(x2d, w["g_pre"], w["w_r"], tm=tm, seq=seq, prompt=True)
    a_g, gates = rest[:N_GROUPS], rest[N_GROUPS]
    ns = sample_caches[0].shape[0]
    ya, s_new, next_caches = _proj_gla_fused(
        x2d, w["g_pre"], w["w_a"], w["w_lr"], w["w_lr2"], w["b_lr"], s0, w["gn"], tm=tm, seq=seq, chunk=GLA_CHUNK,
        shift_src=tuple(_cache_rows(c) for c in sample_caches),
        shift_tail=tuple(r.reshape(ns, -1, DIL_HD) for r in sample_new_rows))
    new_bufs = [kv.reshape(nb, -1, 2, DIL_HEADS, DIL_HD) for kv in rest[N_GROUPS + 1:]]
    os_, lses = _dil_prompt(a_g)
    y = _final(ya, os_, lses, gates, x2d, pe.reshape(n, -1),
               w["wa"], w["wb"], w["wo"], w["wpp"], w["wpg"], w["g_post"], tm=tm)
    next_caches = [c.reshape(old.shape) for c, old in zip(next_caches, sample_caches)]
    return y.reshape(nb, seq, d), s_new, new_bufs, next_caches


def _sample_proj(x, w):
    nb, seq, d = x.shape
    return _proj_rest(x.reshape(nb * seq, d), w["g_pre"], w["w_r"], tm=nb * seq, seq=seq, prompt=False)


def _sample_layer(x, pe, s0, caches, w, rest):
    nb, seq, d = x.shape
    n = nb * seq
    x2d = x.reshape(n, d)
    a_g, gates = rest[:N_GROUPS], rest[N_GROUPS]
    chunk = min(GLA_CHUNK, seq)
    assert seq % chunk == 0
    nseq = 8 if nb % 8 == 0 else 1
    qkvg, la = _proj_gla(x2d, w["g_pre"], w["w_a"], w["w_lr"], w["w_lr2"], w["b_lr"], tm=n, out_dtype=F32)
    ya, s_new = _gla(qkvg, la, s0, w["gn"], nb=nb, seq=seq, chunk=chunk, rows_per_step=seq, nseq=nseq,
                     out_dtype=F32, mm_dtype=BF16, precision=None)
    os_, lses = _dil_sample(a_g, caches, t_new=seq)
    y = _final(ya, os_, lses, gates, x2d, pe.reshape(n, -1),
               w["wa"], w["wb"], w["wo"], w["wpp"], w["wpg"], w["g_post"], tm=n)
    return y.reshape(nb, seq, d), s_new


def kernel(x_prompt, x_sample, state_gla, cache_kv_w128, cache_kv_w512, cache_kv_w2048, p_prompt, p_sample, g_pre, g_post, w_in, w_gla_lr, b_gla_lr, g_gla_norm, w_gla_branch, w_dil_branch, w_out, w_ple_proj, w_ple_gate):
    depth = w_in.shape[0]
    hp, hs = x_prompt, x_sample
    gla_p, gla_s = [], []
    kvp = [[] for _ in DIL_GROUPS]
    kvs = [[] for _ in DIL_GROUPS]
    for i in range(depth):
        w = _prep_weights(g_pre[i], g_post[i], w_in[i], w_gla_lr[i], b_gla_lr[i], g_gla_norm[i], w_gla_branch[i],
                          w_dil_branch[i], w_out[i], w_ple_proj[i], w_ple_gate[i])
        s0 = jnp.zeros((x_prompt.shape[0], GLA_HEADS, GLA_DK, GLA_DV), F32)
        caches = (cache_kv_w128[i], cache_kv_w512[i], cache_kv_w2048[i])
        rest_s = _sample_proj(hs, w)
        hp, sp_new, bp_new, bs_new = _prompt_layer(hp, p_prompt[i], s0, w, caches, rest_s[N_GROUPS + 1:])
        hs, ss_new = _sample_layer(hs, p_sample[i], state_gla[i], caches, w, rest_s)
        gla_p.append(sp_new)
        gla_s.append(ss_new)
        for g in range(N_GROUPS):
            kvp[g].append(bp_new[g])
            kvs[g].append(bs_new[g])
    return (hp, hs, jnp.stack(gla_p), jnp.stack(gla_s),
            jnp.stack(kvp[0]), jnp.stack(kvp[1]), jnp.stack(kvp[2]),
            jnp.stack(kvs[0]), jnp.stack(kvs[1]), jnp.stack(kvs[2]))
```

```python
import functools

import jax
import jax.numpy as jnp
from jax import lax
from jax.experimental import pallas as pl
from jax.experimental.pallas import tpu as pltpu

F32 = jnp.float32
BF16 = jnp.bfloat16

NORM_EPS = 1e-6
GLA_HEADS = 4
GLA_DK = 128
GLA_DV = 256
GLA_KEY = GLA_HEADS * GLA_DK
GLA_VAL = GLA_HEADS * GLA_DV
GLA_LOWRANK = 16
GLA_TAU = 16.0
GLA_CHUNK = 64
DIL_GROUPS = ((128, 1), (512, 4), (2048, 16))
DIL_HEADS = 4
DIL_HD = 128
DIL_WIDTH = DIL_HEADS * DIL_HD
DIL_SPAN = 128
N_GROUPS = len(DIL_GROUPS)
GLA_COLS = 2 * GLA_KEY + 2 * GLA_VAL
QKV_COLS = 3 * DIL_WIDTH
KV_COLS = 2 * DIL_WIDTH

LANES = 128
LSE_LANES_PER_HEAD = LANES // DIL_HEADS
_LOG2_LSE_LANES = LSE_LANES_PER_HEAD.bit_length() - 1
assert 1 << _LOG2_LSE_LANES == LSE_LANES_PER_HEAD
VMEM_LIMIT_BYTES = 56 * 1024 * 1024
_CHEAP_ROW_STRIDE = 4

_NT = (((1,), (1,)), ((), ()))
_TN = (((0,), (0,)), ((), ()))


def _dot(a, b, dims=None, precision=None):
    if dims is None:
        return jnp.dot(a, b, preferred_element_type=F32, precision=precision)
    return lax.dot_general(a, b, dims, preferred_element_type=F32, precision=precision)


def _rms(xf, g):
    return xf * lax.rsqrt(jnp.mean(xf * xf, axis=-1, keepdims=True) + NORM_EPS) * g


def _sigmoid(x):
    return 1.0 / (1.0 + jnp.exp(-x))


def _const_spec(shape):
    nd = len(shape)
    return pl.BlockSpec(shape, lambda *_: (0,) * nd, pipeline_mode=pl.Buffered(1))


def _params(*sem):
    return pltpu.CompilerParams(dimension_semantics=sem, vmem_limit_bytes=VMEM_LIMIT_BYTES)


def _proj_gla_kernel(x_ref, g_ref, w_ref, wlr_ref, wlr2_ref, blr_ref, qkvg_ref, la_ref, *, col_chunk):
    xn = _rms(x_ref[...], g_ref[...]).astype(BF16)
    for c0 in range(0, GLA_COLS, col_chunk):
        qkvg_ref[:, c0:c0 + col_chunk] = _dot(xn, w_ref[c0:c0 + col_chunk, :], _NT).astype(qkvg_ref.dtype)
    glr = _dot(xn, wlr_ref[...], _NT)
    z = _dot(glr.astype(BF16), wlr2_ref[...]) + blr_ref[...]
    log_sig = jnp.minimum(z, 0.0) - jnp.log1p(jnp.exp(-jnp.abs(z)))
    la_ref[...] = log_sig * (1.0 / GLA_TAU)


def _proj_gla(x2d, g_pre, w_a, w_lr, w_lr2, b_lr, *, tm, out_dtype):
    n, d = x2d.shape
    return pl.pallas_call(
        functools.partial(_proj_gla_kernel, col_chunk=512),
        grid=(n // tm,),
        in_specs=[
            pl.BlockSpec((tm, d), lambda i: (i, 0)),
            _const_spec(g_pre.shape),
            _const_spec(w_a.shape),
            _const_spec(w_lr.shape),
            _const_spec(w_lr2.shape),
            _const_spec(b_lr.shape),
        ],
        out_specs=[
            pl.BlockSpec((tm, GLA_COLS), lambda i: (i, 0)),
            pl.BlockSpec((tm, GLA_KEY), lambda i: (i, 0)),
        ],
        out_shape=[
            jax.ShapeDtypeStruct((n, GLA_COLS), out_dtype),
            jax.ShapeDtypeStruct((n, GLA_KEY), F32),
        ],
        compiler_params=_params("parallel"),
        name="proj_gla",
    )(x2d, g_pre, w_a, w_lr, w_lr2, b_lr)


def _proj_rest_kernel(x_ref, g_ref, w_ref, *refs, tm, tiles_per_seq, keeps, prompt):
    a_refs = refs[:N_GROUPS]
    gates_ref = refs[N_GROUPS]
    kv_refs = refs[N_GROUPS + 1:2 * N_GROUPS + 1]
    stage_refs = refs[2 * N_GROUPS + 1:]

    def body(cache_groups):
        xn = _rms(x_ref[...], g_ref[...]).astype(BF16)
        n_staged = 0
        for g in reversed(range(N_GROUPS)):
            base = g * QKV_COLS
            dil = DIL_GROUPS[g][1]
            for part in range(3):
                c0 = part * DIL_WIDTH
                r = _dot(xn, w_ref[base + c0:base + c0 + DIL_WIDTH, :], _NT)
                if not prompt:
                    a_refs[g][:, c0:c0 + DIL_WIDTH] = r
                elif dil == 1:
                    a_refs[g][0, :, c0:c0 + DIL_WIDTH] = r.astype(BF16)
                else:
                    stage = stage_refs[n_staged]
                    n_staged += 1
                    for s in range(DIL_WIDTH // LANES):
                        stage[s] = r[:, s * LANES:(s + 1) * LANES]
                    step = dil
                    if dil > _CHEAP_ROW_STRIDE:
                        step = dil // _CHEAP_ROW_STRIDE
                        assert step <= _CHEAP_ROW_STRIDE
                        stage2 = stage_refs[n_staged]
                        n_staged += 1
                        quarter = tm // _CHEAP_ROW_STRIDE
                        for c in range(_CHEAP_ROW_STRIDE):
                            for s in range(DIL_WIDTH // LANES):
                                stage2[s, c * quarter:(c + 1) * quarter, :] = (
                                    stage[s, pl.ds(c, quarter, stride=_CHEAP_ROW_STRIDE), :])
                        stage = stage2
                    for res in range(dil):
                        start = res if step == dil else (res % _CHEAP_ROW_STRIDE) * quarter + res // _CHEAP_ROW_STRIDE
                        for s in range(DIL_WIDTH // LANES):
                            a_refs[g][res, :, c0 + s * LANES:c0 + (s + 1) * LANES] = (
                                stage[s, pl.ds(start, tm // dil, stride=step), :].astype(BF16))
                if part > 0 and g in cache_groups:
                    rows = min(keeps[g], tm)
                    for h in range(DIL_HEADS):
                        kv_refs[g][pl.ds((part - 1) * DIL_HEADS + h, rows, stride=2 * DIL_HEADS), :] = (
                            r[tm - rows:, h * DIL_HD:(h + 1) * DIL_HD])
            if g == N_GROUPS - 1:
                gbase = N_GROUPS * QKV_COLS
                for c0 in range(0, gates_ref.shape[1], DIL_WIDTH):
                    gates_ref[:, c0:c0 + DIL_WIDTH] = _dot(
                        xn, w_ref[gbase + c0:gbase + c0 + DIL_WIDTH, :], _NT).astype(gates_ref.dtype)

    if not prompt:
        body(tuple(range(N_GROUPS)))
        return
    t = pl.program_id(0) % tiles_per_seq
    first_kept = [tiles_per_seq - max(1, keep // tm) for keep in keeps]
    edges = sorted(set(first_kept) | {0}) + [tiles_per_seq]
    for lo, hi in zip(edges[:-1], edges[1:]):
        groups = tuple(g for g in range(N_GROUPS) if first_kept[g] <= lo)
        pl.when(jnp.logical_and(t >= lo, t < hi))(lambda groups=groups: body(groups))


def _proj_rest(x2d, g_pre, w_r, *, tm, seq, prompt):
    n, d = x2d.shape
    nb = n // seq
    tiles_per_seq = seq // tm if prompt else None
    gate_cols = w_r.shape[0] - N_GROUPS * QKV_COLS
    keeps = tuple(min(w, seq) for (w, _) in DIL_GROUPS) if prompt else (tm,) * N_GROUPS
    rows_per_token = 2 * DIL_HEADS
    out_specs, out_shape, scratch = [], [], []
    if prompt:
        for _, dil in DIL_GROUPS:
            out_specs.append(pl.BlockSpec((None, dil, tm // dil, QKV_COLS),
                                          lambda i: (i // tiles_per_seq, 0, i % tiles_per_seq, 0)))
            out_shape.append(jax.ShapeDtypeStruct((nb, dil, seq // dil, QKV_COLS), BF16))
    else:
        out_specs += [pl.BlockSpec((tm, QKV_COLS), lambda i: (i, 0)) for _ in range(N_GROUPS)]
        out_shape += [jax.ShapeDtypeStruct((n, QKV_COLS), F32) for _ in range(N_GROUPS)]
    out_specs.append(pl.BlockSpec((tm, gate_cols), lambda i: (i, 0)))
    out_shape.append(jax.ShapeDtypeStruct((n, gate_cols), BF16 if prompt else F32))
    if not prompt:
        for _ in range(N_GROUPS):
            out_specs.append(pl.BlockSpec((tm * rows_per_token, DIL_HD), lambda i: (i, 0)))
            out_shape.append(jax.ShapeDtypeStruct((n * rows_per_token, DIL_HD), F32))
    else:
        for keep in keeps:
            if keep >= tm:
                assert keep % tm == 0
                kt = keep // tm

                def idx(i, kt=kt):
                    b, t = i // tiles_per_seq, i % tiles_per_seq
                    return (b * kt + jnp.maximum(t - (tiles_per_seq - kt), 0), 0)

                out_specs.append(pl.BlockSpec((tm * rows_per_token, DIL_HD), idx))
            else:
                out_specs.append(pl.BlockSpec((keep * rows_per_token, DIL_HD), lambda i: (i // tiles_per_seq, 0)))
            out_shape.append(jax.ShapeDtypeStruct((nb * keep * rows_per_token, DIL_HD), F32))
        n_stage = 3 * sum((1 if dil <= _CHEAP_ROW_STRIDE else 2) for _, dil in DIL_GROUPS if dil > 1)
        scratch = [pltpu.VMEM((DIL_WIDTH // LANES, tm, LANES), F32)] * n_stage
    return pl.pallas_call(
        functools.partial(_proj_rest_kernel, tm=tm, tiles_per_seq=tiles_per_seq, keeps=keeps, prompt=prompt),
        grid=(n // tm,),
        in_specs=[
            pl.BlockSpec((tm, d), lambda i: (i, 0)),
            _const_spec(g_pre.shape),
            _const_spec(w_r.shape),
        ],
        out_specs=out_specs,
        out_shape=out_shape,
        scratch_shapes=scratch,
        compiler_params=_params("arbitrary"),
        name="proj_rest",
    )(x2d, g_pre, w_r)


def _gla_kernel(qkvg_ref, la_ref, s0_ref, gn_ref, ya_ref, sfin_ref, st_ref, *, chunk, nchunk, nseq, mm_dtype, precision):
    j = pl.program_id(1)

    @pl.when(j == 0)
    def _():
        for s in range(nseq):
            for h in range(GLA_HEADS):
                st_ref[s * GLA_HEADS + h] = s0_ref[s, h].T

    _gla_compute(lambda r, c: qkvg_ref[r, c], lambda r: la_ref[r, :], st_ref, gn_ref, ya_ref,
                 chunk=chunk, nchunk=nchunk, nseq=nseq, mm_dtype=mm_dtype, precision=precision)

    @pl.when(j == pl.num_programs(1) - 1)
    def _():
        for s in range(nseq):
            for h in range(GLA_HEADS):
                sfin_ref[s, h] = st_ref[s * GLA_HEADS + h].T


def _gla_compute(*args, **kwargs):
    for _ in _gla_phases(*args, **kwargs):
        pass


def _interleave(*generators, phases_per_round=None):
    live = {i: gen for i, gen in enumerate(generators)}
    counts = phases_per_round or (1,) * len(generators)
    while live:
        for i in list(live):
            for _ in range(counts[i]):
                if next(live[i], StopIteration) is StopIteration:
                    del live[i]
                    break


def _gla_phases(load_qkvg, load_la, st_ref, gn_ref, ya_ref, *, chunk, nchunk, nseq, mm_dtype, precision):
    row = lax.broadcasted_iota(jnp.int32, (chunk, chunk), 0)
    col = lax.broadcasted_iota(jnp.int32, (chunk, chunk), 1)
    causal = row >= col
    tril = causal.astype(F32)
    qscale = GLA_DK ** -0.5

    segs = [(s, c) for s in range(nseq) for c in range(nchunk)]
    units = [(s, c, h) for s, c in segs for h in range(GLA_HEADS)]

    def rows_of(s, c):
        r0 = (s * nchunk + c) * chunk
        return slice(r0, r0 + chunk)

    def key_cols(h):
        return slice(h * GLA_DK, (h + 1) * GLA_DK)

    def v_of(s, c, h):
        return load_qkvg(rows_of(s, c), slice(2 * GLA_KEY + h * GLA_DV, 2 * GLA_KEY + (h + 1) * GLA_DV)).astype(mm_dtype)

    def cumsum(a):
        if mm_dtype != BF16:
            return _dot(tril, a, precision=lax.Precision.HIGHEST)
        hi = a.astype(BF16)
        rest = a - hi.astype(F32)
        mid = rest.astype(BF16)
        lo = (rest - mid.astype(F32)).astype(BF16)
        parts = _dot(tril.astype(BF16), jnp.concatenate([hi, mid, lo], axis=1))
        return parts[:, :GLA_KEY] + parts[:, GLA_KEY:2 * GLA_KEY] + parts[:, 2 * GLA_KEY:]

    cum = {sc: cumsum(load_la(rows_of(*sc))) for sc in segs}
    b_end = {sc: cum[sc][chunk - 1:chunk, :] for sc in segs}
    e_b = {sc: jnp.exp(cum[sc]) for sc in segs}
    e_nb = {sc: jnp.exp(-cum[sc]) for sc in segs}
    e_rest = {sc: jnp.exp(b_end[sc] - cum[sc]) for sc in segs}
    e_end = {sc: jnp.exp(b_end[sc]) for sc in segs}
    yield
    q_in, k_dec, k_end = {}, {}, {}
    for s, c, h in units:
        q = load_qkvg(rows_of(s, c), key_cols(h)).astype(F32) * qscale
        k = load_qkvg(rows_of(s, c), slice(GLA_KEY + h * GLA_DK, GLA_KEY + (h + 1) * GLA_DK)).astype(F32)
        q_in[s, c, h] = (q * e_b[s, c][:, key_cols(h)]).astype(mm_dtype)
        k_dec[s, c, h] = (k * e_nb[s, c][:, key_cols(h)]).astype(mm_dtype)
        k_end[s, c, h] = (k * e_rest[s, c][:, key_cols(h)]).astype(mm_dtype)
    yield
    att = {u: jnp.where(causal, _dot(q_in[u], k_dec[u], _NT, precision), 0.0).astype(mm_dtype) for u in units}
    yield
    o_intra = {u: _dot(att[u], v_of(*u), None, precision) for u in units}
    yield
    d_state = {u: _dot(v_of(*u), k_end[u], _TN, precision) for u in units}
    yield
    o_inter = {}
    for s in range(nseq):
        for h in range(GLA_HEADS):
            st = st_ref[s * GLA_HEADS + h]
            for c in range(nchunk):
                o_inter[s, c, h] = _dot(q_in[s, c, h], st.astype(mm_dtype), _NT, precision)
                st = st * e_end[s, c][:, key_cols(h)] + d_state[s, c, h]
            st_ref[s * GLA_HEADS + h] = st
    yield
    for i, (s, c, h) in enumerate(units):
        vs = slice(h * GLA_DV, (h + 1) * GLA_DV)
        gg = load_qkvg(rows_of(s, c), slice(2 * GLA_KEY + GLA_VAL + h * GLA_DV,
                                            2 * GLA_KEY + GLA_VAL + (h + 1) * GLA_DV)).astype(F32)
        on = _rms(o_intra[s, c, h] + o_inter[s, c, h], gn_ref[:, vs])
        ya_ref[rows_of(s, c), vs] = (on * (gg * _sigmoid(gg))).astype(ya_ref.dtype)
        if i % 8 == 7:
            yield


def _proj_gla_fused_kernel(x_ref, g_ref, w_ref, wlr_ref, wlr2_ref, blr_ref, s0_ref, gn_ref, *refs,
                           steps, chunk, col_chunk, n_shift, pieces, n_chunks):
    old_refs = refs[:n_shift]
    next_refs = refs[n_shift:2 * n_shift]
    tail_refs = refs[2 * n_shift:3 * n_shift]
    ya_ref, sfin_ref = refs[3 * n_shift:3 * n_shift + 2]
    new_refs = refs[3 * n_shift + 2:4 * n_shift + 2]
    qkvg_s0, qkvg_s1, la_s0, la_s1, st_ref = refs[4 * n_shift + 2:]
    j = pl.program_id(0)
    prev = j - 1

    def shift_pieces(rows_per_phase=1024):
        last_piece = jnp.minimum(j, n_chunks - 1) % pieces == pieces - 1
        for old, nxt, tail, new in zip(old_refs, next_refs, tail_refs, new_refs):
            p_rows, t_rows = old.shape[0], tail.shape[0]
            for r0 in range(0, p_rows - t_rows, rows_per_phase):
                r1 = min(r0 + rows_per_phase, p_rows - t_rows)
                new[r0:r1, :] = old[t_rows + r0:t_rows + r1, :]
                yield
            new[p_rows - t_rows:p_rows, :] = jnp.where(last_piece, tail[...], nxt[...])
            yield

    @pl.when(j == 0)
    def _():
        qkvg_s1[...] = jnp.zeros_like(qkvg_s1)
        la_s1[...] = jnp.zeros_like(la_s1)

    @pl.when(jnp.logical_or(j == 0, prev % steps == 0))
    def _():
        for h in range(GLA_HEADS):
            st_ref[h] = s0_ref[0, h].T

    def project(qkvg_w, la_w):
        xn = _rms(x_ref[...], g_ref[...]).astype(BF16)
        yield
        for c0 in range(0, GLA_COLS, col_chunk):
            qkvg_w[:, c0:c0 + col_chunk] = _dot(xn, w_ref[c0:c0 + col_chunk, :], _NT).astype(qkvg_w.dtype)
            yield
        glr = _dot(xn, wlr_ref[...], _NT)
        z = _dot(glr.astype(BF16), wlr2_ref[...]) + blr_ref[...]
        log_sig = jnp.minimum(z, 0.0) - jnp.log1p(jnp.exp(-jnp.abs(z)))
        la_w[...] = log_sig * (1.0 / GLA_TAU)

    def step(qkvg_w, la_w, qkvg_r, la_r):
        _interleave(
            project(qkvg_w, la_w),
            _gla_phases(lambda r, c: qkvg_r[r, c], lambda r: la_r[r, :], st_ref, gn_ref, ya_ref,
                        chunk=chunk, nchunk=x_ref.shape[0] // chunk, nseq=1, mm_dtype=BF16, precision=None),
            shift_pieces(),
            phases_per_round=(1, 2, 2))

    @pl.when(j % 2 == 0)
    def _():
        step(qkvg_s0, la_s0, qkvg_s1, la_s1)

    @pl.when(j % 2 == 1)
    def _():
        step(qkvg_s1, la_s1, qkvg_s0, la_s0)

    @pl.when(jnp.logical_and(j >= 1, prev % steps == steps - 1))
    def _():
        for h in range(GLA_HEADS):
            sfin_ref[0, h] = st_ref[h].T


def _proj_gla_fused(x2d, g_pre, w_a, w_lr, w_lr2, b_lr, s0, gn, *, tm, seq, chunk, shift_src=(), shift_tail=()):
    n, d = x2d.shape
    nb = n // seq
    steps = seq // tm
    n_tiles = n // tm
    n_shift = len(shift_src)
    pieces = n_chunks = 1
    copy_in, copy_out = [], []
    if n_shift:
        entries = shift_src[0].shape[0]
        pieces = max(1, n_tiles // entries)
        n_chunks = entries * pieces
        assert n_chunks <= n_tiles + 1

        def piece_of(j):
            c = jnp.minimum(j, n_chunks - 1)
            return c // pieces, c % pieces

        next_specs, tail_specs = [], []
        for src, tail in zip(shift_src, shift_tail):
            t_rows = tail.shape[1]
            p_rows = src.shape[1] // pieces
            assert src.shape[0] == tail.shape[0] == entries and src.shape[2] == tail.shape[2] == LANES
            assert src.shape[1] % pieces == 0 and p_rows % t_rows == 0 and t_rows % 8 == 0
            piece_spec = pl.BlockSpec((None, p_rows, LANES), lambda j: (*piece_of(j), 0))
            last_t_block = src.shape[1] // t_rows - 1

            def next_idx(j, per_piece=p_rows // t_rows, last=last_t_block):
                entry, piece = piece_of(j)
                return (entry, jnp.minimum((piece + 1) * per_piece, last), 0)

            copy_in.append(piece_spec)
            next_specs.append(pl.BlockSpec((None, t_rows, LANES), next_idx))
            tail_specs.append(pl.BlockSpec((None, t_rows, LANES), lambda j: (piece_of(j)[0], 0, 0)))
            copy_out.append(piece_spec)
        copy_in += next_specs + tail_specs

    def prev_tile(j):
        return jnp.maximum(j - 1, 0)

    outs = pl.pallas_call(
        functools.partial(_proj_gla_fused_kernel, steps=steps, chunk=chunk, col_chunk=512, n_shift=n_shift,
                          pieces=pieces, n_chunks=n_chunks),
        grid=(n_tiles + 1,),
        in_specs=[
            pl.BlockSpec((tm, d), lambda j: (jnp.minimum(j, n_tiles - 1), 0)),
            _const_spec(g_pre.shape),
            _const_spec(w_a.shape),
            _const_spec(w_lr.shape),
            _const_spec(w_lr2.shape),
            _const_spec(b_lr.shape),
            pl.BlockSpec((1, GLA_HEADS, GLA_DK, GLA_DV), lambda j: (prev_tile(j) // steps, 0, 0, 0)),
            _const_spec(gn.shape),
        ] + copy_in,
        out_specs=[
            pl.BlockSpec((tm, GLA_VAL), lambda j: (prev_tile(j), 0)),
            pl.BlockSpec((1, GLA_HEADS, GLA_DK, GLA_DV), lambda j: (prev_tile(j) // steps, 0, 0, 0)),
        ] + copy_out,
        out_shape=[
            jax.ShapeDtypeStruct((n, GLA_VAL), BF16),
            jax.ShapeDtypeStruct((nb, GLA_HEADS, GLA_DK, GLA_DV), F32),
        ] + [jax.ShapeDtypeStruct(src.shape, src.dtype) for src in shift_src],
        scratch_shapes=[
            pltpu.VMEM((tm, GLA_COLS), BF16),
            pltpu.VMEM((tm, GLA_COLS), BF16),
            pltpu.VMEM((tm, GLA_KEY), F32),
            pltpu.VMEM((tm, GLA_KEY), F32),
            pltpu.VMEM((GLA_HEADS, GLA_DV, GLA_DK), F32),
        ],
        compiler_params=_params("arbitrary"),
        name="proj_gla_fused",
    )(x2d, g_pre, w_a, w_lr, w_lr2, b_lr, s0, gn, *shift_src, *shift_src, *shift_tail)
    return outs[0], outs[1], list(outs[2:])


def _gla(qkvg, la, s0, gn, *, nb, seq, chunk, rows_per_step, nseq, out_dtype, mm_dtype, precision):
    steps = seq // rows_per_step
    assert nb % nseq == 0 and (nseq == 1 or steps == 1)
    rows = nseq * rows_per_step
    return pl.pallas_call(
        functools.partial(_gla_kernel, chunk=chunk, nchunk=rows_per_step // chunk, nseq=nseq, mm_dtype=mm_dtype,
                          precision=precision),
        grid=(nb // nseq, steps),
        in_specs=[
            pl.BlockSpec((rows, GLA_COLS), lambda b, j: (b * steps + j, 0)),
            pl.BlockSpec((rows, GLA_KEY), lambda b, j: (b * steps + j, 0)),
            pl.BlockSpec((nseq, GLA_HEADS, GLA_DK, GLA_DV), lambda b, j: (b, 0, 0, 0)),
            pl.BlockSpec((1, GLA_VAL), lambda b, j: (0, 0)),
        ],
        out_specs=[
            pl.BlockSpec((rows, GLA_VAL), lambda b, j: (b * steps + j, 0)),
            pl.BlockSpec((nseq, GLA_HEADS, GLA_DK, GLA_DV), lambda b, j: (b, 0, 0, 0)),
        ],
        out_shape=[
            jax.ShapeDtypeStruct((nb * seq, GLA_VAL), out_dtype),
            jax.ShapeDtypeStruct((nb, GLA_HEADS, GLA_DK, GLA_DV), F32),
        ],
        scratch_shapes=[pltpu.VMEM((nseq * GLA_HEADS, GLA_DV, GLA_DK), F32)],
        compiler_params=_params("parallel", "arbitrary"),
        name="gla",
    )(qkvg, la, s0, gn)


def _pack_lse(lses):
    rows = lses[0].shape[0]
    lane_head = lax.broadcasted_iota(jnp.int32, (rows, LANES), 1) >> _LOG2_LSE_LANES
    packed = jnp.broadcast_to(lses[0], (rows, LANES))
    for h in range(1, DIL_HEADS):
        packed = jnp.where(lane_head == h, lses[h], packed)
    return packed


def _dil_prompt_kernel(*refs, len_steps):
    n = N_GROUPS
    a_refs, o_refs, lse_refs, kvp_refs = refs[:n], refs[n:2 * n], refs[2 * n:3 * n], refs[3 * n:]
    step = pl.program_id(1)
    i_idx = lax.broadcasted_iota(jnp.int32, (DIL_SPAN, 2 * DIL_SPAN), 0)
    j_idx = lax.broadcasted_iota(jnp.int32, (DIL_SPAN, 2 * DIL_SPAN), 1)
    diff = j_idx - i_idx
    band = jnp.logical_and(diff >= 0, diff <= DIL_SPAN)
    band_first = []
    for g in range(n):
        first = step % len_steps[g] == 0

        @pl.when(first)
        def _(g=g):
            kvp_refs[g][...] = jnp.zeros_like(kvp_refs[g])

        band_first.append(jnp.logical_and(diff >= jnp.where(first, DIL_SPAN - i_idx, 0), diff <= DIL_SPAN))
    scale = DIL_HD ** -0.5
    neg_inf = -jnp.inf
    units = [(g, res, jb, h) for g in range(n) for res in range(a_refs[g].shape[0])
             for jb in range(a_refs[g].shape[1] // DIL_SPAN) for h in range(DIL_HEADS)]

    def window(g, res, jb, c0):
        if jb == 0:
            prev = kvp_refs[g][res, :, c0 - DIL_WIDTH:c0 - DIL_WIDTH + DIL_HD]
            return jnp.concatenate([prev, a_refs[g][res, 0:DIL_SPAN, c0:c0 + DIL_HD]], axis=0)
        return a_refs[g][res, (jb - 1) * DIL_SPAN:(jb + 1) * DIL_SPAN, c0:c0 + DIL_HD]

    scores = []
    for g, res, jb, h in units:
        q = a_refs[g][res, jb * DIL_SPAN:(jb + 1) * DIL_SPAN, h * DIL_HD:(h + 1) * DIL_HD]
        s = _dot(q, window(g, res, jb, DIL_WIDTH + h * DIL_HD), _NT) * scale
        scores.append(jnp.where(band_first[g] if jb == 0 else band, s, neg_inf))
    maxes = [jnp.max(s, axis=-1, keepdims=True) for s in scores]
    probs = [jnp.exp(s - m) for s, m in zip(scores, maxes)]
    sums = [jnp.sum(p, axis=-1, keepdims=True) for p in probs]
    lses = {}
    for (g, res, jb, h), p, m, l in zip(units, probs, maxes, sums):
        acc = _dot(p.astype(BF16), window(g, res, jb, 2 * DIL_WIDTH + h * DIL_HD))
        o_refs[g][res, jb * DIL_SPAN:(jb + 1) * DIL_SPAN, h * DIL_HD:(h + 1) * DIL_HD] = acc * (1.0 / l)
        lses[g, res, jb, h] = m + jnp.log(l)
    for g in range(n):
        nres, rows = a_refs[g].shape[:2]
        for res in range(nres):
            for jb in range(rows // DIL_SPAN):
                lse_refs[g][res, jb * DIL_SPAN:(jb + 1) * DIL_SPAN, :] = _pack_lse(
                    [lses[g, res, jb, h] for h in range(DIL_HEADS)])
            kvp_refs[g][res] = a_refs[g][res, rows - DIL_SPAN:rows, DIL_WIDTH:QKV_COLS]


def _dil_prompt(a_g, *, rows_per_step=1024):
    nb = a_g[0].shape[0]
    seq = a_g[0].shape[1] * a_g[0].shape[2]
    n_steps = seq // rows_per_step
    in_specs, o_specs, l_specs, o_shapes, l_shapes, scratch, len_steps = [], [], [], [], [], [], []
    for a4 in a_g:
        _, dil, length, _ = a4.shape
        rows = min(rows_per_step, length)
        nres = rows_per_step // rows
        steps_per_class = length // rows
        assert dil % nres == 0 and (dil // nres) * steps_per_class == n_steps

        def spec(cols, nres=nres, rows=rows, spc=steps_per_class):
            return pl.BlockSpec((None, nres, rows, cols), lambda b, s: (b, s // spc, s % spc, 0))

        in_specs.append(spec(QKV_COLS))
        o_specs.append(spec(DIL_WIDTH))
        l_specs.append(spec(LANES))
        o_shapes.append(jax.ShapeDtypeStruct((nb, dil, length, DIL_WIDTH), F32))
        l_shapes.append(jax.ShapeDtypeStruct((nb, dil, length, LANES), F32))
        scratch.append(pltpu.VMEM((nres, DIL_SPAN, KV_COLS), BF16))
        len_steps.append(steps_per_class)
    outs = pl.pallas_call(
        functools.partial(_dil_prompt_kernel, len_steps=tuple(len_steps)),
        grid=(nb, n_steps),
        in_specs=in_specs,
        out_specs=o_specs + l_specs,
        out_shape=o_shapes + l_shapes,
        scratch_shapes=scratch,
        compiler_params=_params("parallel", "arbitrary"),
        name="dil_prompt",
    )(*a_g)
    return outs[:N_GROUPS], outs[N_GROUPS:]


def _dil_sample_kernel(*refs, t_new):
    a_refs, cache_refs = refs[:N_GROUPS], refs[N_GROUPS:2 * N_GROUPS]
    o_refs, lse_refs = refs[2 * N_GROUPS:3 * N_GROUPS], refs[3 * N_GROUPS:]
    rpt = 2 * DIL_HEADS
    scale = DIL_HD ** -0.5
    neg_inf = -jnp.inf
    tq_n = lax.broadcasted_iota(jnp.int32, (t_new, t_new), 0)
    j_n = lax.broadcasted_iota(jnp.int32, (t_new, t_new), 1)

    def cache_head(g, c):
        ref = cache_refs[g]
        if len(ref.shape) == 2:
            return ref[pl.ds(c, ref.shape[0] // rpt, stride=rpt), :].astype(BF16)
        x = ref[:, pl.ds(c, t_new, stride=rpt), :]
        return x.reshape(ref.shape[0] * t_new, DIL_HD).astype(BF16)

    valid_c, valid_n = [], []
    for g, (_, dil) in enumerate(DIL_GROUPS):
        ref = cache_refs[g]
        n_keys = ref.shape[0] // rpt if len(ref.shape) == 2 else ref.shape[0] * t_new
        tq = lax.broadcasted_iota(jnp.int32, (t_new, n_keys), 0)
        key = lax.broadcasted_iota(jnp.int32, (t_new, n_keys), 1)
        if len(ref.shape) == 2:
            valid_c.append(jnp.logical_and(((key - tq) & (dil - 1)) == 0, key >= tq))
        else:
            valid_c.append((key & (t_new - 1)) == tq)
        valid_n.append(jnp.logical_and(((tq_n - j_n) & (dil - 1)) == 0, j_n <= tq_n))

    units = [(g, h) for g in range(N_GROUPS) for h in range(DIL_HEADS)]
    scores = []
    for g, h in units:
        q = a_refs[g][:, h * DIL_HD:(h + 1) * DIL_HD].astype(BF16)
        k_new = a_refs[g][:, DIL_WIDTH + h * DIL_HD:DIL_WIDTH + (h + 1) * DIL_HD].astype(BF16)
        s_c = jnp.where(valid_c[g], _dot(q, cache_head(g, h), _NT) * scale, neg_inf)
        s_n = jnp.where(valid_n[g], _dot(q, k_new, _NT) * scale, neg_inf)
        scores.append((s_c, s_n))
    maxes = [jnp.maximum(jnp.max(s_c, axis=-1, keepdims=True), jnp.max(s_n, axis=-1, keepdims=True))
             for s_c, s_n in scores]
    probs = [(jnp.exp(s_c - m), jnp.exp(s_n - m)) for (s_c, s_n), m in zip(scores, maxes)]
    sums = [jnp.sum(p_c, axis=-1, keepdims=True) + jnp.sum(p_n, axis=-1, keepdims=True) for p_c, p_n in probs]
    lses = {}
    for (g, h), (p_c, p_n), m, l in zip(units, probs, maxes, sums):
        v_new = a_refs[g][:, 2 * DIL_WIDTH + h * DIL_HD:2 * DIL_WIDTH + (h + 1) * DIL_HD].astype(BF16)
        acc = _dot(p_c.astype(BF16), cache_head(g, DIL_HEADS + h)) + _dot(p_n.astype(BF16), v_new)
        o_refs[g][:, h * DIL_HD:(h + 1) * DIL_HD] = acc * (1.0 / l)
        lses[g, h] = m + jnp.log(l)
    for g in range(N_GROUPS):
        lse_refs[g][...] = _pack_lse([lses[g, h] for h in range(DIL_HEADS)])


def _cache_rows(cache):
    nb, lb = cache.shape[:2]
    return cache.reshape(nb, lb * 2 * DIL_HEADS, DIL_HD)


def _dil_sample(a_g, caches, *, t_new):
    nb = caches[0].shape[0]
    rpt = 2 * DIL_HEADS
    assert t_new & (t_new - 1) == 0
    cache_args, cache_specs = [], []
    for cache, (_, dil) in zip(caches, DIL_GROUPS):
        lb = cache.shape[1]
        assert lb == dil * DIL_SPAN
        if dil > t_new:
            assert dil % t_new == 0
            cache_args.append(cache.reshape(nb, lb // dil, dil * rpt, DIL_HD))
            cache_specs.append(pl.BlockSpec((None, lb // dil, t_new * rpt, DIL_HD), lambda b: (b, 0, 0, 0)))
        else:
            cache_args.append(_cache_rows(cache))
            cache_specs.append(pl.BlockSpec((None, lb * rpt, DIL_HD), lambda b: (b, 0, 0)))
    outs = pl.pallas_call(
        functools.partial(_dil_sample_kernel, t_new=t_new),
        grid=(nb,),
        in_specs=[pl.BlockSpec((t_new, QKV_COLS), lambda b: (b, 0))] * N_GROUPS + cache_specs,
        out_specs=[pl.BlockSpec((t_new, DIL_WIDTH), lambda b: (b, 0))] * N_GROUPS
        + [pl.BlockSpec((t_new, LANES), lambda b: (b, 0))] * N_GROUPS,
        out_shape=[jax.ShapeDtypeStruct((nb * t_new, DIL_WIDTH), F32)] * N_GROUPS
        + [jax.ShapeDtypeStruct((nb * t_new, LANES), F32)] * N_GROUPS,
        compiler_params=_params("parallel"),
        name="dil_sample",
    )(*a_g, *cache_args)
    return outs[:N_GROUPS], outs[N_GROUPS:]


def _token_order(ref, scratch_refs):
    if len(ref.shape) == 2:
        return lambda rows: ref[rows, :]
    dil, rows_per_class, _ = ref.shape
    if dil == 1:
        return lambda rows: ref[0, rows, :]
    scratch = scratch_refs.pop()
    n_slabs = scratch.shape[0]
    for res in range(dil):
        for s in range(n_slabs):
            scratch[s, pl.ds(res, rows_per_class, stride=dil), :] = ref[res, :, s * LANES:(s + 1) * LANES]
    if n_slabs == 1:
        return lambda rows: scratch[0, rows, :]
    return lambda rows: jnp.concatenate([scratch[s, rows, :] for s in range(n_slabs)], axis=1)


def _final_kernel(ya_ref, o0_ref, o1_ref, o2_ref, l0_ref, l1_ref, l2_ref, gates_ref, x_ref, pe_ref,
                  wa_ref, wb_ref, wo_ref, wpp_ref, wpg_ref, gpost_ref, y_ref, *scratch_refs):
    o_scratch = [s for s in scratch_refs if s.shape[0] == DIL_WIDTH // LANES]
    l_scratch = [s for s in scratch_refs if s.shape[0] == 1]
    every_row = slice(None)
    lse_of = [_token_order(r, l_scratch) for r in (l0_ref, l1_ref, l2_ref)]
    o_of = [_token_order(r, o_scratch) for r in (o0_ref, o1_ref, o2_ref)]
    lse = [read(every_row) for read in lse_of]
    m = jnp.maximum(jnp.maximum(lse[0], lse[1]), lse[2])
    e = [jnp.exp(x - m) for x in lse]
    inv = 1.0 / (e[0] + e[1] + e[2])
    o_groups = [read(every_row) for read in o_of]
    heads = []
    for h in range(DIL_HEADS):
        hs = slice(h * DIL_HD, (h + 1) * DIL_HD)
        lane = h * LSE_LANES_PER_HEAD
        acc = None
        for eg, o_g in zip(e, o_groups):
            term = (eg * inv)[:, lane:lane + 1] * o_g[:, hs]
            acc = term if acc is None else acc + term
        heads.append(acc)
    o_b = jnp.concatenate(heads, axis=1)
    d = x_ref.shape[1]
    dg = gates_ref[:, 0:DIL_WIDTH].astype(F32)
    y_a = _dot(ya_ref[...].astype(BF16), wa_ref[...])
    y_b = _dot((o_b * (dg * _sigmoid(dg))).astype(BF16), wb_ref[...])
    ga = gates_ref[:, DIL_WIDTH:DIL_WIDTH + d].astype(F32)
    gb = gates_ref[:, DIL_WIDTH + d:DIL_WIDTH + 2 * d].astype(F32)
    merged = _sigmoid(ga) * y_a + _sigmoid(gb) * y_b
    t = _dot(merged.astype(BF16), wo_ref[...])
    ple = _dot(pe_ref[...].astype(BF16), wpp_ref[...])
    h = x_ref[...] + _rms(t, gpost_ref[...])
    gate = _dot(h.astype(BF16), wpg_ref[...])
    y_ref[...] = h + ple * _sigmoid(gate)


def _final(ya, os_, lses, gates, x2d, pe2d, wa, wb, wo, wpp, wpg, g_post, *, tm):
    n, d = x2d.shape

    def rows(cols):
        return pl.BlockSpec((tm, cols), lambda i: (i, 0))

    scratch = []

    def group_spec(arr):
        if arr.ndim == 2:
            return rows(arr.shape[1])
        _, dil, length, cols = arr.shape
        tiles_per_seq = dil * length // tm
        if dil > 1:
            scratch.append(pltpu.VMEM((cols // LANES, tm, LANES), F32))
        return pl.BlockSpec((None, dil, tm // dil, cols), lambda i: (i // tiles_per_seq, 0, i % tiles_per_seq, 0))

    return pl.pallas_call(
        _final_kernel,
        grid=(n // tm,),
        in_specs=[rows(GLA_VAL)] + [group_spec(a) for a in os_] + [group_spec(a) for a in lses]
        + [rows(gates.shape[1]), rows(d), rows(pe2d.shape[1])]
        + [_const_spec(w.shape) for w in (wa, wb, wo, wpp, wpg, g_post)],
        out_specs=rows(d),
        out_shape=jax.ShapeDtypeStruct((n, d), F32),
        scratch_shapes=scratch,
        compiler_params=_params("parallel"),
        name="final",
    )(ya, *os_, *lses, gates, x2d, pe2d, wa, wb, wo, wpp, wpg, g_post)


def _prep_weights(g_pre, g_post, w_in, w_gla_lr, b_gla_lr, g_gla_norm, w_gla_branch, w_dil_branch, w_out,
                  w_ple_proj, w_ple_gate):
    d = w_in.shape[0]
    c_lr = GLA_COLS
    c_q = c_lr + GLA_LOWRANK
    c_k = c_q + N_GROUPS * DIL_WIDTH
    c_v = c_k + N_GROUPS * DIL_WIDTH
    c_g = c_v + N_GROUPS * DIL_WIDTH
    w_t = w_in.T
    w_a = w_t[:c_lr].astype(BF16)
    w_lr = jnp.pad(w_t[c_lr:c_q].astype(BF16), ((0, LANES - GLA_LOWRANK), (0, 0)))
    w_lr2 = jnp.pad(w_gla_lr.astype(BF16), ((0, LANES - GLA_LOWRANK), (0, 0)))
    parts = []
    for g in range(N_GROUPS):
        for c in (c_q, c_k, c_v):
            parts.append(w_t[c + g * DIL_WIDTH:c + (g + 1) * DIL_WIDTH])
    parts.append(w_t[c_g:])
    w_r = jnp.concatenate(parts, axis=0).astype(BF16)
    return dict(
        g_pre=g_pre.reshape(1, d), g_post=g_post.reshape(1, d), w_a=w_a, w_lr=w_lr, w_lr2=w_lr2, w_r=w_r,
        b_lr=b_gla_lr.reshape(1, GLA_KEY), gn=g_gla_norm.reshape(1, GLA_VAL),
        wa=w_gla_branch.astype(BF16), wb=w_dil_branch.astype(BF16), wo=w_out.astype(BF16),
        wpp=w_ple_proj.astype(BF16), wpg=w_ple_gate.astype(BF16),
    )


PROMPT_TILE = 512


def _prompt_layer(x, pe, s0, w, sample_caches, sample_new_rows):
    nb, seq, d = x.shape
    n = nb * seq
    tm = PROMPT_TILE
    assert seq % tm == 0 and seq % DIL_GROUPS[-1][0] == 0
    x2d = x.reshape(n, d)
    rest = _proj_rest(x2d, w["g_pre"], w["w_r"], tm=tm, seq=seq, prompt=True)
    a_g, gates = rest[:N_GROUPS], rest[N_GROUPS]
    ns = sample_caches[0].shape[0]
    ya, s_new, next_caches = _proj_gla_fused(
        x2d, w["g_pre"], w["w_a"], w["w_lr"], w["w_lr2"], w["b_lr"], s0, w["gn"], tm=tm, seq=seq, chunk=GLA_CHUNK,
        shift_src=tuple(_cache_rows(c) for c in sample_caches),
        shift_tail=tuple(r.reshape(ns, -1, DIL_HD) for r in sample_new_rows))
    new_bufs = [kv.reshape(nb, -1, 2, DIL_HEADS, DIL_HD) for kv in rest[N_GROUPS + 1:]]
    os_, lses = _dil_prompt(a_g)
    y = _final(ya, os_, lses, gates, x2d, pe.reshape(n, -1),
               w["wa"], w["wb"], w["wo"], w["wpp"], w["wpg"], w["g_post"], tm=tm)
    next_caches = [c.reshape(old.shape) for c, old in zip(next_caches, sample_caches)]
    return y.reshape(nb, seq, d), s_new, new_bufs, next_caches


def _sample_proj(x, w):
    nb, seq, d = x.shape
    return _proj_rest(x.reshape(nb * seq, d), w["g_pre"], w["w_r"], tm=nb * seq, seq=seq, prompt=False)


def _sample_layer(x, pe, s0, caches, w, rest):
    nb, seq, d = x.shape
    n = nb * seq
    x2d = x.reshape(n, d)
    a_g, gates = rest[:N_GROUPS], rest[N_GROUPS]
    chunk = min(GLA_CHUNK, seq)
    assert seq % chunk == 0
    nseq = 8 if nb % 8 == 0 else 1
    qkvg, la = _proj_gla(x2d, w["g_pre"], w["w_a"], w["w_lr"], w["w_lr2"], w["b_lr"], tm=n, out_dtype=F32)
    ya, s_new = _gla(qkvg, la, s0, w["gn"], nb=nb, seq=seq, chunk=chunk, rows_per_step=seq, nseq=nseq,
                     out_dtype=F32, mm_dtype=BF16, precision=None)
    os_, lses = _dil_sample(a_g, caches, t_new=seq)
    y = _final(ya, os_, lses, gates, x2d, pe.reshape(n, -1),
               w["wa"], w["wb"], w["wo"], w["wpp"], w["wpg"], w["g_post"], tm=n)
    return y.reshape(nb, seq, d), s_new


def kernel(x_prompt, x_sample, state_gla, cache_kv_w128, cache_kv_w512, cache_kv_w2048, p_prompt, p_sample, g_pre, g_post, w_in, w_gla_lr, b_gla_lr, g_gla_norm, w_gla_branch, w_dil_branch, w_out, w_ple_proj, w_ple_gate):
    depth = w_in.shape[0]
    hp, hs = x_prompt, x_sample
    gla_p, gla_s = [], []
    kvp = [[] for _ in DIL_GROUPS]
    kvs = [[] for _ in DIL_GROUPS]
    for i in range(depth):
        w = _prep_weights(g_pre[i], g_post[i], w_in[i], w_gla_lr[i], b_gla_lr[i], g_gla_norm[i], w_gla_branch[i],
                          w_dil_branch[i], w_out[i], w_ple_proj[i], w_ple_gate[i])
        s0 = jnp.zeros((x_prompt.shape[0], GLA_HEADS, GLA_DK, GLA_DV), F32)
        caches = (cache_kv_w128[i], cache_kv_w512[i], cache_kv_w2048[i])
        rest_s = _sample_proj(hs, w)
        hp, sp_new, bp_new, bs_new = _prompt_layer(hp, p_prompt[i], s0, w, caches, rest_s[N_GROUPS + 1:])
        hs, ss_new = _sample_layer(hs, p_sample[i], state_gla[i], caches, w, rest_s)
        gla_p.append(sp_new)
        gla_s.append(ss_new)
        for g in range(N_GROUPS):
            kvp[g].append(bp_new[g])
            kvs[g].append(bs_new[g])
    return (hp, hs, jnp.stack(gla_p), jnp.stack(gla_s),
            jnp.stack(kvp[0]), jnp.stack(kvp[1]), jnp.stack(kvp[2]),
            jnp.stack(kvs[0]), jnp.stack(kvs[1]), jnp.stack(kvs[2]))
```
